```python
import jax, jax.numpy as jnp
from jax import lax
import numpy as np

D_MODEL = 1024
BATCH = 4
SEQ = 8192
DEPTH = 2

HG_HEADS = 4
HG_HEAD_DIM = 128
HG_WIDTH = HG_HEADS * HG_HEAD_DIM
HG_CHUNK = 64
NSA_HEADS = 8
NSA_GROUPS = 2
NSA_HEAD_DIM = 64
NSA_WIDTH = NSA_HEADS * NSA_HEAD_DIM
NSA_KV = NSA_GROUPS * NSA_HEAD_DIM
CMP_LEN = 32
CMP_STRIDE = 16
CMP_HIDDEN = 64
SEL_LEN = 64
SEL_TOPK = 16
WINDOW = 512
Q_BLOCK = 128
LRU_WIDTH = 512
LRU_BLOCKS = 4
LRU_CONV = 4
LRU_C = 8.0
FFN_DIM = 2816
FFN_CONV = 3
DEEPNORM_ALPHA = (2 * DEPTH) ** 0.25
DEEPNORM_BETA = (8 * DEPTH) ** -0.25
LN_EPS = 1e-5
N_BRANCH = 3
IN_SPLITS = (HG_WIDTH, HG_WIDTH, HG_WIDTH, HG_WIDTH,
             NSA_WIDTH, NSA_KV, NSA_KV, NSA_KV, NSA_KV, NSA_KV, NSA_KV, NSA_HEADS * 3,
             LRU_WIDTH, LRU_WIDTH,
             N_BRANCH * D_MODEL)
IN_DIM = sum(IN_SPLITS)

kernel_name = "hybrid_hgrn2_nsa_rglru_deepnorm"


def _layer_norm(x, g, b):
    xf = x.astype(jnp.float32)
    mu = jnp.mean(xf, -1, keepdims=True)
    var = jnp.mean(jnp.square(xf - mu), -1, keepdims=True)
    return ((xf - mu) * lax.rsqrt(var + LN_EPS) * g + b).astype(x.dtype)


def _split_cols(p):
    offs = [int(o) for o in np.cumsum(IN_SPLITS)[:-1]]
    return jnp.split(p, offs, axis=-1)


def _causal_dwconv(x, w, b):
    K, S = w.shape[0], x.shape[1]
    xp = jnp.pad(x, ((0, 0), (K - 1, 0), (0, 0)))
    out = b + xp[:, 0:S] * w[0]
    for j in range(1, K):
        out = out + xp[:, j:j + S] * w[j]
    return out


def _alibi_slopes(n):
    return 2.0 ** (-(8.0 / n) * jnp.arange(1, n + 1, dtype=jnp.float32))


def _masked_probs(scores, mask):
    s = jnp.where(mask, scores, -jnp.inf)
    m = jnp.max(s, axis=-1, keepdims=True)
    m = jnp.where(jnp.isfinite(m), m, 0.0)
    p = jnp.exp(s - m)
    return p / jnp.maximum(jnp.sum(p, -1, keepdims=True), 1e-30)


def _linear_combine(left, right):
    a_l, u_l = left
    a_r, u_r = right
    return a_l * a_r, a_r * u_l + u_r


def _hgrn2(q, f_logit, i, g, lb, norm_g):
    B, S, _ = q.shape
    f32 = jnp.float32
    H, d, C = HG_HEADS, HG_HEAD_DIM, HG_CHUNK
    lb = lb.astype(f32)
    fl = f_logit.astype(f32)
    log_f = jnp.log(lb + (1.0 - lb) * jax.nn.sigmoid(fl))
    k = (1.0 - lb) * jax.nn.sigmoid(-fl)

    def chunks(t):
        return t.astype(f32).reshape(B, S // C, C, H, d).transpose(1, 0, 3, 2, 4)

    xs = (chunks(q), chunks(log_f), chunks(k), chunks(i))
    causal = jnp.tril(jnp.ones((C, C), bool))[:, :, None]

    def step(state, inp):
        qc, lfc, kc, ic = inp
        b = jnp.cumsum(lfc, axis=2)
        decay = jnp.exp(jnp.where(causal, b[:, :, :, None, :] - b[:, :, None, :, :], -jnp.inf))
        attn = jnp.einsum('bhtk,bhtsk,bhsk->bhts', qc, decay, kc)
        o = attn @ ic + (qc * jnp.exp(b)) @ state
        b_last = b[:, :, -1:, :]
        state = (jnp.exp(b_last[:, :, 0, :, None]) * state
                 + jnp.swapaxes(kc * jnp.exp(b_last - b), -1, -2) @ ic)
        return state, o

    state0 = jnp.zeros((B, H, d, d), f32)
    _, o = lax.scan(step, state0, xs)
    o = o.transpose(1, 0, 3, 2, 4).reshape(B, S, H, d)
    o = o * lax.rsqrt(jnp.mean(o * o, -1, keepdims=True) + 1e-6) * norm_g.astype(f32).reshape(H, d)
    return (o.reshape(B, S, HG_WIDTH) * jax.nn.silu(g.astype(f32))).astype(q.dtype)


def _compress(t, pe, w1, w2):
    B, S, G, hd = t.shape
    tc = t.reshape(B, S // CMP_STRIDE, CMP_STRIDE, G, hd)
    blocks = jnp.concatenate([tc[:, :-1], tc[:, 1:]], axis=2) + pe[None, None, :, None, :]
    flat = blocks.transpose(0, 1, 3, 2, 4).reshape(B, -1, G, CMP_LEN * hd)
    return jax.nn.gelu(flat @ w1) @ w2


def _block_importance(p):
    R = SEL_LEN // CMP_STRIDE
    pp = jnp.pad(p, [(0, 0)] * (p.ndim - 1) + [(1, 1)])
    n_sel = (pp.shape[-1] - 1) // R
    body = pp[..., :-1].reshape(p.shape[:-1] + (n_sel, R))
    return 0.5 * body[..., 0] + jnp.sum(body[..., 1:], -1) + 0.5 * pp[..., R::R]


def _nsa(q, k_cmp, v_cmp, k_sel, v_sel, k_win, v_win, gate_logits,
         pe_k, w1_k, w2_k, pe_v, w1_v, w2_v):
    B, S, _ = q.shape
    f32 = jnp.float32
    H, G, hd = NSA_HEADS, NSA_GROUPS, NSA_HEAD_DIM
    P = H // G
    q = q.astype(f32).reshape(B, S, G, P, hd) * (hd ** -0.5)
    kv = lambda t: t.astype(f32).reshape(B, S, G, hd)
    k_cmp, v_cmp, k_sel, v_sel, k_win, v_win = (kv(t) for t in (k_cmp, v_cmp, k_sel, v_sel, k_win, v_win))
    gates = jax.nn.sigmoid(gate_logits.astype(f32)).reshape(B, S, G, P, 3)

    kc = _compress(k_cmp, pe_k, w1_k, w2_k)
    vc = _compress(v_cmp, pe_v, w1_v, w2_v)
    cmp_end = jnp.arange(kc.shape[1]) * CMP_STRIDE + (CMP_LEN - 1)
    n_sel = S // SEL_LEN
    top = min(SEL_TOPK, n_sel)
    ks_blocks = k_sel.transpose(0, 2, 1, 3).reshape(B, G, n_sel, SEL_LEN * hd)
    vs_blocks = v_sel.transpose(0, 2, 1, 3).reshape(B, G, n_sel, SEL_LEN * hd)
    kw_pad = jnp.pad(k_win, ((0, 0), (WINDOW, 0), (0, 0), (0, 0)))
    vw_pad = jnp.pad(v_win, ((0, 0), (WINDOW, 0), (0, 0), (0, 0)))
    slopes = _alibi_slopes(H).reshape(1, G, P, 1, 1)
    blk = jnp.arange(n_sel)
    in_blk = jnp.arange(SEL_LEN)
    gather = jax.vmap(jax.vmap(lambda a, idx: a[idx]))

    def block(c):
        c0 = c * Q_BLOCK
        qb = lax.dynamic_slice_in_dim(q, c0, Q_BLOCK, axis=1)
        gb = lax.dynamic_slice_in_dim(gates, c0, Q_BLOCK, axis=1)
        t = c0 + jnp.arange(Q_BLOCK)
        s = jnp.einsum('btgpd,bngd->bgptn', qb, kc)
        dist = (t[:, None] - cmp_end[None, :]).astype(f32)
        p_cmp = _masked_probs(s - slopes * dist, cmp_end[None, :] <= t[:, None])
        o_cmp = jnp.einsum('bgptn,bngd->btgpd', p_cmp, vc)
        imp = _block_importance(jnp.sum(p_cmp, axis=2))
        cur = t // SEL_LEN
        forced = (blk[None, :] == 0) | (blk[None, :] == cur[:, None]) | (blk[None, :] == cur[:, None] - 1)
        imp = jnp.where(blk[None, :] > cur[:, None], -jnp.inf, imp)
        imp = jnp.where(forced, jnp.inf, imp)
        _, idx = lax.top_k(imp, top)
        flat = idx.reshape(B, G, Q_BLOCK * top)
        ksb = gather(ks_blocks, flat).reshape(B, G, Q_BLOCK, top * SEL_LEN, hd)
        vsb = gather(vs_blocks, flat).reshape(B, G, Q_BLOCK, top * SEL_LEN, hd)
        spos = (idx[..., None] * SEL_LEN + in_blk).reshape(B, G, Q_BLOCK, top * SEL_LEN)
        s = jnp.einsum('btgpd,bgtkd->bgptk', qb, ksb)
        dist = (t[:, None] - spos[:, :, None]).astype(f32)
        p = _masked_probs(s - slopes * dist, dist >= 0)
        o_sel = jnp.einsum('bgptk,bgtkd->btgpd', p, vsb)
        kwb = lax.dynamic_slice_in_dim(kw_pad, c0, WINDOW + Q_BLOCK, axis=1)
        vwb = lax.dynamic_slice_in_dim(vw_pad, c0, WINDOW + Q_BLOCK, axis=1)
        wpos = c0 - WINDOW + jnp.arange(WINDOW + Q_BLOCK)
        wdist = t[:, None] - wpos[None, :]
        s = jnp.einsum('btgpd,bkgd->bgptk', qb, kwb)
        p = _masked_probs(s - slopes * wdist.astype(f32),
                          (wdist >= 0) & (wdist < WINDOW) & (wpos[None, :] >= 0))
        o_win = jnp.einsum('bgptk,bkgd->btgpd', p, vwb)
        return gb[..., 0:1] * o_cmp + gb[..., 1:2] * o_sel + gb[..., 2:3] * o_win

    out = lax.map(block, jnp.arange(S // Q_BLOCK))
    return out.transpose(1, 0, 2, 3, 4, 5).reshape(B, S, NSA_WIDTH).astype(gate_logits.dtype)


def _rglru(xb, yb, conv_w, conv_b, wa, ba, wx, bx, lam):
    B, S, W = xb.shape
    f32 = jnp.float32
    xc = _causal_dwconv(xb.astype(f32), conv_w.astype(f32), conv_b.astype(f32))
    xg = xc.reshape(B, S, LRU_BLOCKS, W // LRU_BLOCKS)
    r = jax.nn.sigmoid(jnp.einsum('bsgi,gij->bsgj', xg, wa.astype(f32)).reshape(B, S, W) + ba)
    i = jax.nn.sigmoid(jnp.einsum('bsgi,gij->bsgj', xg, wx.astype(f32)).reshape(B, S, W) + bx)
    log_a = -LRU_C * r * jax.nn.softplus(-lam.astype(f32))
    a = jnp.exp(log_a)
    u = jnp.sqrt(-jnp.expm1(2.0 * log_a)) * (i * xc)
    _, h = lax.associative_scan(_linear_combine, (a, u), axis=1)
    return (h * jax.nn.gelu(yb.astype(f32))).astype(yb.dtype)


def _conv_ffn(x, w_up, conv_w, conv_b, w_down):
    u, v = jnp.split(x @ w_up, 2, axis=-1)
    return (jax.nn.gelu(_causal_dwconv(u, conv_w, conv_b)) * v) @ w_down


def setup_inputs(seed: int = 0) -> dict:
    key = jax.random.key(seed)
    keys = jax.random.split(key, 40)
    cnt = iter(range(40))
    nrm = lambda shape, scale: jax.random.normal(keys[next(cnt)], shape, jnp.float32) * scale
    L, D = DEPTH, D_MODEL
    hd = NSA_HEAD_DIM
    bw = LRU_WIDTH // LRU_BLOCKS
    a0 = jax.random.uniform(keys[next(cnt)], (L, LRU_WIDTH), jnp.float32, 0.9, 0.999) ** (1.0 / LRU_C)
    return {
        "x": nrm((BATCH, SEQ, D), 1.0),
        "ln_emb_g": 1.0 + nrm((D,), 0.02),
        "ln_emb_b": nrm((D,), 0.02),
        "w_in": nrm((L, D, IN_DIM), D ** -0.5),
        "b_in": nrm((L, IN_DIM), 0.01),
        "hg_lb_logits": nrm((L, HG_WIDTH), 1.0),
        "hg_norm_g": 1.0 + nrm((L, HG_WIDTH), 0.02),
        "cmp_pe_k": nrm((L, CMP_LEN, hd), 0.1),
        "cmp_w1_k": nrm((L, CMP_LEN * hd, CMP_HIDDEN), (CMP_LEN * hd) ** -0.5),
        "cmp_w2_k": nrm((L, CMP_HIDDEN, hd), CMP_HIDDEN ** -0.5),
        "cmp_pe_v": nrm((L, CMP_LEN, hd), 0.1),
        "cmp_w1_v": nrm((L, CMP_LEN * hd, CMP_HIDDEN), (CMP_LEN * hd) ** -0.5),
        "cmp_w2_v": nrm((L, CMP_HIDDEN, hd), CMP_HIDDEN ** -0.5),
        "lru_conv_w": nrm((L, LRU_CONV, LRU_WIDTH), LRU_CONV ** -0.5),
        "lru_conv_b": nrm((L, LRU_WIDTH), 0.01),
        "lru_wa": nrm((L, LRU_BLOCKS, bw, bw), bw ** -0.5),
        "lru_ba": nrm((L, LRU_WIDTH), 0.01),
        "lru_wx": nrm((L, LRU_BLOCKS, bw, bw), bw ** -0.5),
        "lru_bx": nrm((L, LRU_WIDTH), 0.01),
        "lru_lambda": jnp.log(a0) - jnp.log1p(-a0),
        "w_branch_hg": nrm((L, HG_WIDTH, D), HG_WIDTH ** -0.5 * DEEPNORM_BETA),
        "w_branch_nsa": nrm((L, NSA_WIDTH, D), NSA_WIDTH ** -0.5 * DEEPNORM_BETA),
        "w_branch_lru": nrm((L, LRU_WIDTH, D), LRU_WIDTH ** -0.5 * DEEPNORM_BETA),
        "w_out": nrm((L, D, D), D ** -0.5 * DEEPNORM_BETA),
        "ln1_g": 1.0 + nrm((L, D), 0.02),
        "ln1_b": nrm((L, D), 0.02),
        "ffn_w_up": nrm((L, D, 2 * FFN_DIM), D ** -0.5),
        "ffn_conv_w": nrm((L, FFN_CONV, FFN_DIM), FFN_CONV ** -0.5),
        "ffn_conv_b": nrm((L, FFN_DIM), 0.01),
        "ffn_w_down": nrm((L, FFN_DIM, D), FFN_DIM ** -0.5 * DEEPNORM_BETA),
        "ln2_g": 1.0 + nrm((L, D), 0.02),
        "ln2_b": nrm((L, D), 0.02),
    }


def reference(x, ln_emb_g, ln_emb_b, w_in, b_in, hg_lb_logits, hg_norm_g,
              cmp_pe_k, cmp_w1_k, cmp_w2_k, cmp_pe_v, cmp_w1_v, cmp_w2_v,
              lru_conv_w, lru_conv_b, lru_wa, lru_ba, lru_wx, lru_bx, lru_lambda,
              w_branch_hg, w_branch_nsa, w_branch_lru, w_out, ln1_g, ln1_b,
              ffn_w_up, ffn_conv_w, ffn_conv_b, ffn_w_down, ln2_g, ln2_b):
    B, S, D = x.shape
    gam = jax.nn.softmax(hg_lb_logits.astype(jnp.float32), axis=0)
    lb_all = jnp.cumsum(gam, axis=0) - gam[0]
    x = _layer_norm(x, ln_emb_g, ln_emb_b)
    for l in range(DEPTH):
        proj = x @ w_in[l] + b_in[l]
        (hq, hf, hi, hg, nq, kc, vc, ks, vs, kw, vw, ngate, lx, ly, mgate) = _split_cols(proj)
        y_a = _hgrn2(hq, hf, hi, hg, lb_all[l], hg_norm_g[l])
        y_b = _nsa(nq, kc, vc, ks, vs, kw, vw, ngate,
                   cmp_pe_k[l], cmp_w1_k[l], cmp_w2_k[l], cmp_pe_v[l], cmp_w1_v[l], cmp_w2_v[l])
        y_c = _rglru(lx, ly, lru_conv_w[l], lru_conv_b[l], lru_wa[l], lru_ba[l],
                     lru_wx[l], lru_bx[l], lru_lambda[l])
        mg = jax.nn.sigmoid(mgate.reshape(B, S, N_BRANCH, D))
        merged = (mg[:, :, 0] * (y_a @ w_branch_hg[l])
                  + mg[:, :, 1] * (y_b @ w_branch_nsa[l])
                  + mg[:, :, 2] * (y_c @ w_branch_lru[l]))
        x = _layer_norm(DEEPNORM_ALPHA * x + merged @ w_out[l], ln1_g[l], ln1_b[l])
        x = _layer_norm(DEEPNORM_ALPHA * x + _conv_ffn(x, ffn_w_up[l], ffn_conv_w[l], ffn_conv_b[l], ffn_w_down[l]),
                        ln2_g[l], ln2_b[l])
    return x
```

```python
import functools

import numpy as np
import jax
import jax.numpy as jnp
from jax import lax
from jax.experimental import pallas as pl
from jax.experimental.pallas import tpu as pltpu

F32 = jnp.float32
BF16 = jnp.bfloat16

DEPTH = 2
HG_HEADS = 4
HG_HEAD_DIM = 128
HG_WIDTH = HG_HEADS * HG_HEAD_DIM
NSA_HEADS = 8
NSA_GROUPS = 2
NSA_PER_GROUP = NSA_HEADS // NSA_GROUPS
NSA_HEAD_DIM = 64
NSA_WIDTH = NSA_HEADS * NSA_HEAD_DIM
NSA_KV = NSA_GROUPS * NSA_HEAD_DIM
CMP_LEN = 32
CMP_STRIDE = 16
CMP_HIDDEN = 64
SEL_LEN = 64
SEL_TOPK = 16
WINDOW = 512
Q_BLOCK = 128
LRU_WIDTH = 512
LRU_BLOCKS = 4
LRU_CONV = 4
LRU_C = 8.0
FFN_DIM = 2816
FFN_CONV = 3
DEEPNORM_ALPHA = (2 * DEPTH) ** 0.25
LN_EPS = 1e-5
N_BRANCH = 3

LANES = 128
SUBLANES = 8
VMEM_LIMIT_BYTES = 52 * 1024 * 1024

OFF_MG = 0
OFF_HQ = N_BRANCH * 1024
OFF_HF = OFF_HQ + HG_WIDTH
OFF_HI = OFF_HF + HG_WIDTH
OFF_HG = OFF_HI + HG_WIDTH
OFF_NQ = OFF_HG + HG_WIDTH
OFF_LX = OFF_NQ + NSA_WIDTH
OFF_LY = OFF_LX + LRU_WIDTH
OFF_KC = OFF_LY + LRU_WIDTH
OFF_VC = OFF_KC + NSA_KV
OFF_KS = OFF_VC + NSA_KV
OFF_VS = OFF_KS + NSA_KV
OFF_KW = OFF_VS + NSA_KV
OFF_VW = OFF_KW + NSA_KV
OFF_NG = OFF_VW + NSA_KV
PROJ_TILE_N = 1536
PROJ_COLS = 5 * PROJ_TILE_N

NEG_BIG = -1e30


def _split3(x):
    hi = x.astype(BF16)
    r1 = x - hi.astype(F32)
    mid = r1.astype(BF16)
    lo = (r1 - mid.astype(F32)).astype(BF16)
    return hi, mid, lo


def _dot(a, b):
    return jnp.dot(a, b, preferred_element_type=F32)


def _dot_nt(a, b):
    return lax.dot_general(a, b, (((1,), (1,)), ((), ())), preferred_element_type=F32)


def _dot_exact_lhs(m_bf16, x):
    hi, mid, lo = _split3(x)
    return _dot(m_bf16, hi) + _dot(m_bf16, mid) + _dot(m_bf16, lo)


def _dot_hilo(a, b):
    ah = a.astype(BF16)
    al = (a - ah.astype(F32)).astype(BF16)
    bh = b.astype(BF16)
    bl = (b - bh.astype(F32)).astype(BF16)
    return _dot(ah, bh) + _dot(ah, bl) + _dot(al, bh) + _dot(al, bl)


def _layer_norm_rows(z, g, b):
    mu = jnp.mean(z, -1, keepdims=True)
    zc = z - mu
    var = jnp.mean(zc * zc, -1, keepdims=True)
    return zc * lax.rsqrt(var + LN_EPS) * g + b


def _params(*sem):
    return pltpu.CompilerParams(dimension_semantics=sem, vmem_limit_bytes=VMEM_LIMIT_BYTES)


def _ln_kernel(x_ref, g_ref, b_ref, o_ref):
    o_ref[...] = _layer_norm_rows(x_ref[...], g_ref[...], b_ref[...])


def _layer_norm_call(x2, g, b, tm=512):
    T, D = x2.shape
    return pl.pallas_call(
        _ln_kernel,
        grid=(T // tm,),
        in_specs=[pl.BlockSpec((tm, D), lambda i: (i, 0)),
                  pl.BlockSpec((1, D), lambda i: (0, 0)),
                  pl.BlockSpec((1, D), lambda i: (0, 0))],
        out_specs=pl.BlockSpec((tm, D), lambda i: (i, 0)),
        out_shape=jax.ShapeDtypeStruct((T, D), F32),
        compiler_params=_params("parallel"),
        name="embed_ln",
    )(x2, g.reshape(1, D), b.reshape(1, D))


def _inproj_kernel(x_ref, w_ref, b_ref, o_ref, xb_ref):
    @pl.when(pl.program_id(1) == 0)
    def _():
        xb_ref[...] = x_ref[...].astype(BF16)

    o_ref[...] = _dot(xb_ref[...], w_ref[...]) + b_ref[...]


def _inproj_call(x2, w_bf16, bias, tm=1024):
    T, D = x2.shape
    N = w_bf16.shape[1]
    tm = min(tm, T)
    tn = PROJ_TILE_N
    return pl.pallas_call(
        _inproj_kernel,
        grid=(T // tm, N // tn),
        in_specs=[pl.BlockSpec((tm, D), lambda i, j: (i, 0)),
                  pl.BlockSpec((D, tn), lambda i, j: (0, j)),
                  pl.BlockSpec((1, tn), lambda i, j: (0, j))],
        out_specs=pl.BlockSpec((tm, tn), lambda i, j: (i, j)),
        out_shape=jax.ShapeDtypeStruct((T, N), F32),
        scratch_shapes=[pltpu.VMEM((tm, D), BF16)],
        compiler_params=_params("parallel", "arbitrary"),
        name="in_proj",
    )(x2, w_bf16, bias.reshape(1, N))


HG_CHUNK = 128
HG_DIAG = SUBLANES


def _hgrn_constants(C):
    r = np.arange(C)
    tri = (r[None, :] <= r[:, None]).astype(np.float32)
    mats = [tri]
    masks = []
    w = HG_DIAG
    while 2 * w <= C:
        mid = (r // (2 * w)) * (2 * w) + w
        mats.append(tri - (r[None, :] <= mid[:, None]).astype(np.float32))
        same = (r[:, None] // (2 * w)) == (r[None, :] // (2 * w))
        masks.append((same & ((r[:, None] % (2 * w)) >= w) & ((r[None, :] % (2 * w)) < w)).astype(np.float32))
        w *= 2
    mats.append((r[None, :] > r[:, None]).astype(np.float32))
    return np.concatenate(mats, 0), np.stack(masks, 0)


def _hgrn_kernel(q_ref, f_ref, i_ref, g_ref, lb_ref, ng_ref, m_ref, mask_ref, o_ref, st_ref, *, C, nchunk):
    d = HG_HEAD_DIM
    nlev = mask_ref.shape[0]

    @pl.when(pl.program_id(2) == 0)
    def _():
        st_ref[...] = jnp.zeros_like(st_ref)

    lb = lb_ref[0]
    ng = ng_ref[0]
    nb = C // HG_DIAG
    row3 = lax.broadcasted_iota(jnp.int32, (nb, HG_DIAG, d), 1)

    for c in range(nchunk):
        sl = pl.ds(c * C, C)
        q = q_ref[0, sl, :]
        fl = f_ref[0, sl, :]
        iv = i_ref[0, sl, :]
        g = g_ref[0, sl, :]
        lf = jnp.log(lb + (1.0 - lb) * jax.nn.sigmoid(fl))
        kk = (1.0 - lb) * jax.nn.sigmoid(-fl)
        allm = _dot_exact_lhs(m_ref[...], lf)
        bcs = allm[0:C]
        rem = allm[(nlev + 1) * C:(nlev + 2) * C]
        b_last = bcs[C - 1:C, :]
        iv_b = iv.astype(BF16)

        att = jnp.zeros((C, C), F32)
        for l in range(nlev):
            e = jnp.exp(-jnp.abs(allm[(1 + l) * C:(2 + l) * C]))
            att = att + mask_ref[l] * _dot_nt((q * e).astype(BF16), (kk * e).astype(BF16))
        o = _dot(att.astype(BF16), iv_b)

        q3 = q.reshape(nb, HG_DIAG, d)
        k3 = kk.reshape(nb, HG_DIAG, d)
        b3 = bcs.reshape(nb, HG_DIAG, d)
        i3 = iv.reshape(nb, HG_DIAG, d)
        acc = jnp.zeros((nb, HG_DIAG, d), F32)
        for s in range(HG_DIAG):
            dec = jnp.where(row3 >= s, jnp.exp(b3 - b3[:, s:s + 1, :]), 0.0)
            a = jnp.sum(q3 * dec * k3[:, s:s + 1, :], axis=-1, keepdims=True)
            acc = acc + a * i3[:, s:s + 1, :]
        o = o + acc.reshape(C, d)

        st = st_ref[...]
        o = o + _dot_nt((q * jnp.exp(bcs)).astype(BF16), st.astype(BF16))
        kdec = (kk * jnp.exp(rem)).astype(BF16)
        st_ref[...] = st * jnp.exp(b_last) + _dot(iv.T.astype(BF16), kdec)

        o = o * lax.rsqrt(jnp.mean(o * o, -1, keepdims=True) + 1e-6) * ng
        o_ref[0, sl, :] = (o * (g * jax.nn.sigmoid(g))).astype(o_ref.dtype)


def _hgrn_call(proj3, lb, norm_g, tt=512):
    B, S, _ = proj3.shape
    d = HG_HEAD_DIM
    C = HG_CHUNK
    tt = min(tt, S)
    mstack, masks = _hgrn_constants(C)
    nm = mstack.shape[0]
    col = lambda off: (lambda b, h, t: (b, t, off // d + h))
    blk = (1, tt, d)
    return pl.pallas_call(
        functools.partial(_hgrn_kernel, C=C, nchunk=tt // C),
        grid=(B, HG_HEADS, S // tt),
        in_specs=[pl.BlockSpec(blk, col(OFF_HQ)), pl.BlockSpec(blk, col(OFF_HF)),
                  pl.BlockSpec(blk, col(OFF_HI)), pl.BlockSpec(blk, col(OFF_HG)),
                  pl.BlockSpec((1, 1, d), lambda b, h, t: (h, 0, 0)),
                  pl.BlockSpec((1, 1, d), lambda b, h, t: (h, 0, 0)),
                  pl.BlockSpec((nm, C), lambda b, h, t: (0, 0)),
                  pl.BlockSpec(masks.shape, lambda b, h, t: (0, 0, 0))],
        out_specs=pl.BlockSpec(blk, lambda b, h, t: (b, t, h)),
        out_shape=jax.ShapeDtypeStruct((B, S, HG_WIDTH), BF16),
        scratch_shapes=[pltpu.VMEM((d, d), F32)],
        compiler_params=_params("parallel", "parallel", "arbitrary"),
        name="hgrn2",
    )(proj3, proj3, proj3, proj3, lb.reshape(HG_HEADS, 1, d), norm_g.reshape(HG_HEADS, 1, d),
      jnp.asarray(mstack, BF16), jnp.asarray(masks, F32))


LRU_PAD = SUBLANES
LRU_STEP = 16


def _lru_kernel(x_ref, y_ref, cw_ref, cb_ref, wa_ref, ba_ref, wx_ref, bx_ref, c_ref, o_ref,
                xpad_ref, h_ref, a_ref, u_ref, *, tt):
    W = LRU_WIDTH
    bw = W // LRU_BLOCKS

    @pl.when(pl.program_id(1) == 0)
    def _():
        xpad_ref[0:LRU_PAD, :] = jnp.zeros((LRU_PAD, W), F32)
        h_ref[...] = jnp.zeros_like(h_ref)

    x = x_ref[0]
    xpad_ref[LRU_PAD:LRU_PAD + tt, :] = x
    cw = cw_ref[...]
    xc = cb_ref[...] + cw[LRU_CONV - 1:LRU_CONV, :] * x
    for j in range(LRU_CONV - 1):
        off = LRU_PAD - (LRU_CONV - 1) + j
        xc = xc + cw[j:j + 1, :] * xpad_ref[off:off + tt, :]
    xpad_ref[0:LRU_PAD, :] = x[tt - LRU_PAD:tt, :]

    rs, is_ = [], []
    for gi in range(LRU_BLOCKS):
        xg = xc[:, gi * bw:(gi + 1) * bw].astype(BF16)
        rs.append(_dot(xg, wa_ref[gi]))
        is_.append(_dot(xg, wx_ref[gi]))
    r = jax.nn.sigmoid(jnp.concatenate(rs, -1) + ba_ref[...])
    ig = jax.nn.sigmoid(jnp.concatenate(is_, -1) + bx_ref[...])
    log_a = c_ref[...] * r
    a = jnp.exp(log_a)
    u = jnp.sqrt(-jnp.tanh(log_a) * (a * a + 1.0)) * (ig * xc)

    sub = lax.broadcasted_iota(jnp.int32, (tt, W), 0) % SUBLANES
    for dlt in (1, 2, 4):
        keep = sub >= dlt
        a_sh = jnp.where(keep, pltpu.roll(a, dlt, 0), 1.0)
        u_sh = jnp.where(keep, pltpu.roll(u, dlt, 0), 0.0)
        u = a * u_sh + u
        a = a * a_sh
    a_ref[...] = a
    u_ref[...] = u

    def step(i, h):
        r0 = pl.multiple_of(i * LRU_STEP, LRU_STEP)
        outs = []
        for k in range(LRU_STEP // SUBLANES):
            rows = pl.ds(r0 + k * SUBLANES, SUBLANES)
            hh = u_ref[rows, :] + a_ref[rows, :] * h
            outs.append(hh)
            h = hh[SUBLANES - 1:SUBLANES, :]
        rows = pl.ds(r0, LRU_STEP)
        o_ref[0, rows, :] = (jnp.concatenate(outs, 0) * jax.nn.gelu(y_ref[0, rows, :])).astype(o_ref.dtype)
        return h

    h_ref[...] = lax.fori_loop(0, tt // LRU_STEP, step, h_ref[...])


def _lru_call(proj3, conv_w, conv_b, wa, ba, wx, bx, lam, tt=512):
    B, S, _ = proj3.shape
    W = LRU_WIDTH
    bw = W // LRU_BLOCKS
    tt = min(tt, S)
    c = (-LRU_C * jax.nn.softplus(-lam.astype(F32))).reshape(1, W)
    vec = lambda: pl.BlockSpec((1, W), lambda b, t: (0, 0))
    return pl.pallas_call(
        functools.partial(_lru_kernel, tt=tt),
        grid=(B, S // tt),
        in_specs=[pl.BlockSpec((1, tt, W), lambda b, t: (b, t, OFF_LX // W)),
                  pl.BlockSpec((1, tt, W), lambda b, t: (b, t, OFF_LY // W)),
                  pl.BlockSpec((LRU_CONV, W), lambda b, t: (0, 0)), vec(),
                  pl.BlockSpec((LRU_BLOCKS, bw, bw), lambda b, t: (0, 0, 0)), vec(),
                  pl.BlockSpec((LRU_BLOCKS, bw, bw), lambda b, t: (0, 0, 0)), vec(), vec()],
        out_specs=pl.BlockSpec((1, tt, W), lambda b, t: (b, t, 0)),
        out_shape=jax.ShapeDtypeStruct((B, S, W), BF16),
        scratch_shapes=[pltpu.VMEM((tt + LRU_PAD, W), F32), pltpu.VMEM((1, W), F32),
                        pltpu.VMEM((tt, W), F32), pltpu.VMEM((tt, W), F32)],
        compiler_params=_params("parallel", "arbitrary"),
        name="rglru",
    )(proj3, proj3, conv_w.astype(F32), conv_b.reshape(1, W).astype(F32), wa.astype(BF16),
      ba.reshape(1, W), wx.astype(BF16), bx.reshape(1, W), c)


NSA_AUG = 4
NSA_KW = LANES
SEL_TILE = 256
WIN_TILE = 128
POS_SPLIT = 128


def _compress_kernel(xk_ref, xv_ref, pek_ref, w1k_ref, w2k_ref, pev_ref, w1v_ref, w2v_ref, kc_ref, vc_ref):
    nc = xk_ref.shape[2]
    half = CMP_STRIDE * NSA_HEAD_DIM

    def one(x_ref, pe_ref, w1_ref, w2_ref):
        x = x_ref[0, 0]
        w1 = w1_ref[...]
        u = _dot_hilo(x, w1[0:half])
        v = _dot_hilo(x, w1[half:2 * half])
        cvec = _dot_hilo(pe_ref[...], w1)
        hid = u + pltpu.roll(v, nc - 1, 0) + cvec[0:1, :]
        return _dot_hilo(jax.nn.gelu(hid), w2_ref[...])

    kc_ref[0, 0] = one(xk_ref, pek_ref, w1k_ref, w2k_ref)
    vc_ref[0, 0] = one(xv_ref, pev_ref, w1v_ref, w2v_ref)


def _compress_call(xk, xv, pe_k, w1_k, w2_k, pe_v, w1_v, w2_v):
    B, G, nc, wid = xk.shape
    hd = NSA_HEAD_DIM
    pe8 = lambda pe: jnp.broadcast_to(pe.reshape(1, CMP_LEN * hd), (SUBLANES, CMP_LEN * hd)).astype(F32)
    xs = pl.BlockSpec((1, 1, nc, wid), lambda b, g: (b, g, 0, 0))
    full = lambda shp: pl.BlockSpec(shp, lambda b, g: tuple(0 for _ in shp))
    os_ = pl.BlockSpec((1, 1, nc, hd), lambda b, g: (b, g, 0, 0))
    return pl.pallas_call(
        _compress_kernel,
        grid=(B, G),
        in_specs=[xs, xs, full((SUBLANES, CMP_LEN * hd)), full(w1_k.shape), full(w2_k.shape),
                  full((SUBLANES, CMP_LEN * hd)), full(w1_v.shape), full(w2_v.shape)],
        out_specs=[os_, os_],
        out_shape=[jax.ShapeDtypeStruct((B, G, nc, hd), F32)] * 2,
        compiler_params=_params("parallel", "parallel"),
        name="nsa_compress",
    )(xk, xv, pe8(pe_k), w1_k.astype(F32), w2_k.astype(F32), pe8(pe_v), w1_v.astype(F32), w2_v.astype(F32))


def _importance_matrix(ns, nc):
    ratio = SEL_LEN // CMP_STRIDE
    a = np.zeros((ns, nc), np.float32)
    for j in range(ns):
        for n, wgt in ((ratio * j - 1, 0.5), (ratio * j, 1.0), (ratio * j + 1, 1.0),
                       (ratio * j + 2, 1.0), (ratio * j + 3, 0.5)):
            if 0 <= n < nc - 1:
                a[j, n] = wgt
    return a


def _nsa_kernel(q_ref, gt_ref, slope_ref, kc_ref, vct_ref, ks_ref, vst_ref, kw_ref, vwt_ref, imp_ref, o_ref,
                sel_ref, *, ns, nc):
    hd = NSA_HEAD_DIM
    P = NSA_PER_GROUP
    T = Q_BLOCK
    PT = P * T
    iq = pl.program_id(2)
    c0 = iq * T

    qT = (q_ref[0] * (hd ** -0.5)).T
    qpart = jnp.concatenate([qT[p * hd:(p + 1) * hd, :] for p in range(P)], 1)
    tlane = c0 + lax.broadcasted_iota(jnp.int32, (1, PT), 1) % T
    slope = slope_ref[0]
    t_hi = ((tlane // POS_SPLIT) * POS_SPLIT).astype(F32)
    t_lo = (tlane % POS_SPLIT).astype(F32)
    rowi = lax.broadcasted_iota(jnp.int32, (NSA_KW - hd, PT), 0)
    aug = jnp.where(rowi < 2, slope,
                    jnp.where(rowi == 2, -slope * t_hi, jnp.where(rowi == 3, -slope * t_lo, 0.0)))
    qt = jnp.concatenate([qpart, aug], 0).astype(BF16)

    gt = jax.nn.sigmoid(gt_ref[0]).T

    def gate_row(branch):
        return jnp.concatenate([gt[p * 3 + branch:p * 3 + branch + 1, :] for p in range(P)], 1)

    s = _dot(kc_ref[0, 0], qt)
    cend = lax.broadcasted_iota(jnp.int32, (nc, 1), 0) * CMP_STRIDE + (CMP_LEN - 1)
    cmask = cend <= tlane
    s = jnp.where(cmask, s, NEG_BIG)
    m = jnp.max(s, 0, keepdims=True)
    p_c = jnp.where(cmask, jnp.exp(s - m), 0.0)
    p_c = p_c / jnp.maximum(jnp.sum(p_c, 0, keepdims=True), 1e-30)
    out = gate_row(0) * _dot(vct_ref[0, 0], p_c.astype(BF16))

    psum = p_c[:, 0:T]
    for p in range(1, P):
        psum = psum + p_c[:, p * T:(p + 1) * T]
    imp = _dot_exact_lhs(imp_ref[...], psum)
    blk = lax.broadcasted_iota(jnp.int32, (ns, T), 0)
    cur = (c0 + lax.broadcasted_iota(jnp.int32, (ns, T), 1)) // SEL_LEN
    forced = (blk == 0) | (blk == cur) | (blk == cur - 1)
    val = jnp.where(forced, 3e38, jnp.where(blk > cur, -1.0, imp))
    chosen = jnp.zeros((ns, T), F32)
    for _ in range(min(SEL_TOPK, ns)):
        mx = jnp.max(val, 0, keepdims=True)
        first = jnp.min(jnp.where(val == mx, blk, ns), 0, keepdims=True)
        pick = blk == first
        chosen = jnp.where(pick, 1.0, chosen)
        val = jnp.where(pick, -2.0, val)
    sel_ref[...] = chosen

    def attend(carry, k_tile, vt_tile, mask):
        m_run, l_run, acc = carry
        sc = jnp.where(mask, _dot(k_tile, qt), NEG_BIG)
        m_new = jnp.maximum(m_run, jnp.max(sc, 0, keepdims=True))
        alpha = jnp.exp(m_run - m_new)
        pr = jnp.where(mask, jnp.exp(sc - m_new), 0.0)
        l_new = alpha * l_run + jnp.sum(pr, 0, keepdims=True)
        acc = alpha * acc + _dot(vt_tile, pr.astype(BF16))
        return m_new, l_new, acc

    init = (jnp.full((1, PT), NEG_BIG, F32), jnp.zeros((1, PT), F32), jnp.zeros((hd, PT), F32))

    def finish(carry):
        _, l_run, acc = carry
        return acc / jnp.maximum(l_run, 1e-30)

    per_tile = SEL_TILE // SEL_LEN

    def sel_step(kt, carry):
        k0 = pl.multiple_of(kt * SEL_TILE, SEL_TILE)
        kpos = k0 + lax.broadcasted_iota(jnp.int32, (SEL_TILE, 1), 0)
        rows = []
        for r in range(per_tile):
            srow = sel_ref[pl.ds(kt * per_tile + r, 1), :]
            rows.append(jnp.broadcast_to(srow, (SEL_LEN, T)))
        smask = jnp.concatenate(rows, 0)
        smask = jnp.concatenate([smask] * P, 1) > 0.5
        mask = smask & (kpos <= tlane)
        return attend(carry, ks_ref[0, 0, pl.ds(k0, SEL_TILE), :], vst_ref[0, 0, kt], mask)

    n_sel_tiles = (c0 + T + SEL_TILE - 1) // SEL_TILE
    out = out + gate_row(1) * finish(lax.fori_loop(0, n_sel_tiles, sel_step, init))

    n_win_tiles = (WINDOW + T) // WIN_TILE

    def win_step(j, carry):
        k0 = pl.multiple_of(c0 - WINDOW + j * WIN_TILE, WIN_TILE)
        kpos = k0 + lax.broadcasted_iota(jnp.int32, (WIN_TILE, 1), 0)
        dist = tlane - kpos
        mask = (dist >= 0) & (dist < WINDOW)
        return attend(carry, kw_ref[0, 0, pl.ds(k0, WIN_TILE), :], vwt_ref[0, 0, k0 // WIN_TILE], mask)

    j0 = jnp.maximum(0, (WINDOW - c0) // WIN_TILE)
    out = out + gate_row(2) * finish(lax.fori_loop(j0, n_win_tiles, win_step, init))

    o_ref[0] = jnp.concatenate([out[:, p * T:(p + 1) * T] for p in range(P)], 0).T.astype(o_ref.dtype)


def _augment_keys(k, pos):
    B, G, N, hd = k.shape
    hi = ((pos // POS_SPLIT) * POS_SPLIT).astype(F32)
    lo = (pos % POS_SPLIT).astype(F32)
    aug = jnp.stack([hi, lo, jnp.ones_like(hi), jnp.ones_like(hi)], -1)
    aug = jnp.pad(aug, ((0, 0), (0, NSA_KW - hd - NSA_AUG)))
    aug = jnp.broadcast_to(aug[None, None], (B, G, N, NSA_KW - hd))
    return jnp.concatenate([k, aug], -1).astype(BF16)


def _tile_values_t(v, tile):
    B, G, N, hd = v.shape
    return v.reshape(B, G, N // tile, tile, hd).transpose(0, 1, 2, 4, 3).astype(BF16)


def _nsa_call(proj3, pe_k, w1_k, w2_k, pe_v, w1_v, w2_v):
    B, S, _ = proj3.shape
    G, hd, P, T = NSA_GROUPS, NSA_HEAD_DIM, NSA_PER_GROUP, Q_BLOCK
    nc = S // CMP_STRIDE
    ns = S // SEL_LEN

    def kv(off):
        return proj3[:, :, off:off + NSA_KV].reshape(B, S, G, hd).transpose(0, 2, 1, 3)

    def cmp_rows(off):
        return kv(off).reshape(B, G, nc, CMP_STRIDE * hd)

    kc, vc = _compress_call(cmp_rows(OFF_KC), cmp_rows(OFF_VC), pe_k, w1_k, w2_k, pe_v, w1_v, w2_v)
    pos_c = jnp.arange(nc, dtype=jnp.int32) * CMP_STRIDE + (CMP_LEN - 1)
    pos_t = jnp.arange(S, dtype=jnp.int32)
    kc_a = _augment_keys(kc, pos_c)
    vc_t = vc.transpose(0, 1, 3, 2).astype(BF16)
    ks_a = _augment_keys(kv(OFF_KS), pos_t)
    kw_a = _augment_keys(kv(OFF_KW), pos_t)
    vs_t = _tile_values_t(kv(OFF_VS), SEL_TILE)
    vw_t = _tile_values_t(kv(OFF_VW), WIN_TILE)
    imp_m = jnp.asarray(_importance_matrix(ns, nc), BF16)
    head_slopes = 2.0 ** (-(8.0 / NSA_HEADS) * np.arange(1, NSA_HEADS + 1, dtype=np.float32))
    slopes = jnp.asarray(np.repeat(head_slopes.reshape(G, 1, P), T, axis=-1), F32)

    bg = lambda shp: pl.BlockSpec((1, 1) + shp, lambda b, g, i: (b, g) + tuple(0 for _ in shp))
    return pl.pallas_call(
        functools.partial(_nsa_kernel, ns=ns, nc=nc),
        grid=(B, G, S // T),
        in_specs=[pl.BlockSpec((1, T, P * hd), lambda b, g, i: (b, i, OFF_NQ // (P * hd) + g)),
                  pl.BlockSpec((1, T, LANES), lambda b, g, i: (b, i, OFF_NG // LANES + g)),
                  pl.BlockSpec((1, 1, P * T), lambda b, g, i: (g, 0, 0)),
                  bg((nc, NSA_KW)), bg((hd, nc)),
                  bg((S, NSA_KW)), bg((S // SEL_TILE, hd, SEL_TILE)),
                  bg((S, NSA_KW)), bg((S // WIN_TILE, hd, WIN_TILE)),
                  pl.BlockSpec((ns, nc), lambda b, g, i: (0, 0))],
        out_specs=pl.BlockSpec((1, T, P * hd), lambda b, g, i: (b, i, g)),
        out_shape=jax.ShapeDtypeStruct((B, S, NSA_WIDTH), BF16),
        scratch_shapes=[pltpu.VMEM((ns, T), F32)],
        compiler_params=_params("parallel", "parallel", "arbitrary"),
        name="nsa_attention",
    )(proj3, proj3, slopes, kc_a, vc_t, ks_a, vs_t, kw_a, vw_t, imp_m)


def _merge_kernel(x_ref, ya_ref, yb_ref, yc_ref, g0_ref, g1_ref, g2_ref, wa_ref, wb_ref, wc_ref, wo_ref,
                  lg_ref, lb_ref, o_ref):
    m = jax.nn.sigmoid(g0_ref[...]) * _dot(ya_ref[...], wa_ref[...])
    m = m + jax.nn.sigmoid(g1_ref[...]) * _dot(yb_ref[...], wb_ref[...])
    m = m + jax.nn.sigmoid(g2_ref[...]) * _dot(yc_ref[...], wc_ref[...])
    z = DEEPNORM_ALPHA * x_ref[...] + _dot(m.astype(BF16), wo_ref[...])
    o_ref[...] = _layer_norm_rows(z, lg_ref[...], lb_ref[...])


def _merge_call(x2, ya, yb, yc, proj2, w_hg, w_nsa, w_lru, w_out, ln_g, ln_b, tm=512):
    T, D = x2.shape
    tm = min(tm, T)
    rows = lambda w: pl.BlockSpec((tm, w), lambda i: (i, 0))
    gate = lambda n: pl.BlockSpec((tm, D), lambda i: (i, OFF_MG // D + n))
    full = lambda a: pl.BlockSpec(a.shape, lambda i: (0, 0))
    ws = [w.astype(BF16) for w in (w_hg, w_nsa, w_lru, w_out)]
    return pl.pallas_call(
        _merge_kernel,
        grid=(T // tm,),
        in_specs=[rows(D), rows(HG_WIDTH), rows(NSA_WIDTH), rows(LRU_WIDTH), gate(0), gate(1), gate(2)]
                 + [full(w) for w in ws] + [pl.BlockSpec((1, D), lambda i: (0, 0))] * 2,
        out_specs=rows(D),
        out_shape=jax.ShapeDtypeStruct((T, D), F32),
        compiler_params=_params("parallel"),
        name="merge_out",
    )(x2, ya, yb, yc, proj2, proj2, proj2, *ws, ln_g.reshape(1, D), ln_b.reshape(1, D))


FFN_SPLIT = 2
FFN_PAD = SUBLANES


def _ffn_kernel(x_ref, wu_ref, wv_ref, cw_ref, cb_ref, wd_ref, lg_ref, lb_ref, o_ref,
                xb_ref, acc_ref, upad_ref, carry_ref, *, tm, tiles_per_seq):
    i = pl.program_id(0)
    j = pl.program_id(1)

    @pl.when(j == 0)
    def _():
        xb_ref[...] = x_ref[...].astype(BF16)

    xb = xb_ref[...]
    u = _dot(xb, wu_ref[...])
    v = _dot(xb, wv_ref[...])
    first = (i % tiles_per_seq) == 0

    @pl.when(first)
    def _():
        upad_ref[0:FFN_PAD, :] = jnp.zeros((FFN_PAD, u.shape[1]), F32)

    @pl.when(jnp.logical_not(first))
    def _():
        upad_ref[0:FFN_PAD, :] = carry_ref[j]

    upad_ref[FFN_PAD:FFN_PAD + tm, :] = u
    carry_ref[j] = u[tm - FFN_PAD:tm, :]
    cw = cw_ref[...]
    cv = cb_ref[...] + cw[FFN_CONV - 1:FFN_CONV, :] * u
    for k in range(FFN_CONV - 1):
        off = FFN_PAD - (FFN_CONV - 1) + k
        cv = cv + cw[k:k + 1, :] * upad_ref[off:off + tm, :]
    h = (jax.nn.gelu(cv) * v).astype(BF16)
    part = _dot(h, wd_ref[...])

    @pl.when(j == 0)
    def _():
        acc_ref[...] = part

    @pl.when(j > 0)
    def _():
        acc_ref[...] += part

    @pl.when(j == FFN_SPLIT - 1)
    def _():
        z = DEEPNORM_ALPHA * x_ref[...] + acc_ref[...]
        o_ref[...] = _layer_norm_rows(z, lg_ref[...], lb_ref[...])


def _ffn_call(x2, seq_len, w_up, conv_w, conv_b, w_down, ln_g, ln_b, tm=512):
    T, D = x2.shape
    tm = min(tm, seq_len)
    fc = FFN_DIM // FFN_SPLIT
    wu = w_up.astype(BF16)
    return pl.pallas_call(
        functools.partial(_ffn_kernel, tm=tm, tiles_per_seq=seq_len // tm),
        grid=(T // tm, FFN_SPLIT),
        in_specs=[pl.BlockSpec((tm, D), lambda i, j: (i, 0)),
                  pl.BlockSpec((D, fc), lambda i, j: (0, j)),
                  pl.BlockSpec((D, fc), lambda i, j: (0, FFN_SPLIT + j)),
                  pl.BlockSpec((FFN_CONV, fc), lambda i, j: (0, j)),
                  pl.BlockSpec((1, fc), lambda i, j: (0, j)),
                  pl.BlockSpec((fc, D), lambda i, j: (j, 0)),
                  pl.BlockSpec((1, D), lambda i, j: (0, 0)),
                  pl.BlockSpec((1, D), lambda i, j: (0, 0))],
        out_specs=pl.BlockSpec((tm, D), lambda i, j: (i, 0)),
        out_shape=jax.ShapeDtypeStruct((T, D), F32),
        scratch_shapes=[pltpu.VMEM((tm, D), BF16), pltpu.VMEM((tm, D), F32),
                        pltpu.VMEM((tm + FFN_PAD, fc), F32), pltpu.VMEM((FFN_SPLIT, FFN_PAD, fc), F32)],
        compiler_params=_params("arbitrary", "arbitrary"),
        name="conv_ffn",
    )(x2, wu, wu, conv_w.astype(F32), conv_b.reshape(1, FFN_DIM).astype(F32), w_down.astype(BF16),
      ln_g.reshape(1, D), ln_b.reshape(1, D))


def _permute_in_proj(w, b):
    src = np.cumsum([0, HG_WIDTH, HG_WIDTH, HG_WIDTH, HG_WIDTH, NSA_WIDTH, NSA_KV, NSA_KV, NSA_KV, NSA_KV,
                     NSA_KV, NSA_KV, NSA_HEADS * 3, LRU_WIDTH, LRU_WIDTH, N_BRANCH * 1024])
    names = ["hq", "hf", "hi", "hg", "nq", "kc", "vc", "ks", "vs", "kw", "vw", "ng", "lx", "ly", "mg"]
    seg = {n: (int(src[k]), int(src[k + 1])) for k, n in enumerate(names)}
    order = ["mg", "hq", "hf", "hi", "hg", "nq", "lx", "ly", "kc", "vc", "ks", "vs", "kw", "vw"]
    per_group = NSA_PER_GROUP * 3
    ng0 = seg["ng"][0]
    w_parts = [w[:, seg[n][0]:seg[n][1]] for n in order]
    b_parts = [b[seg[n][0]:seg[n][1]] for n in order]
    for gi in range(NSA_GROUPS):
        lo = ng0 + gi * per_group
        w_parts += [w[:, lo:lo + per_group], jnp.zeros((w.shape[0], LANES - per_group), w.dtype)]
        b_parts += [b[lo:lo + per_group], jnp.zeros((LANES - per_group,), b.dtype)]
    wp = jnp.concatenate(w_parts, 1)
    bp = jnp.concatenate(b_parts)
    assert wp.shape[1] == PROJ_COLS
    return wp.astype(BF16), bp.astype(F32)


def kernel(x, ln_emb_g, ln_emb_b, w_in, b_in, hg_lb_logits, hg_norm_g, cmp_pe_k, cmp_w1_k, cmp_w2_k, cmp_pe_v, cmp_w1_v, cmp_w2_v, lru_conv_w, lru_conv_b, lru_wa, lru_ba, lru_wx, lru_bx, lru_lambda, w_branch_hg, w_branch_nsa, w_branch_lru, w_out, ln1_g, ln1_b, ffn_w_up, ffn_conv_w, ffn_conv_b, ffn_w_down, ln2_g, ln2_b):
    B, S, D = x.shape
    T = B * S
    gam = jax.nn.softmax(hg_lb_logits.astype(F32), axis=0)
    lb_all = jnp.cumsum(gam, axis=0) - gam[0]
    h = _layer_norm_call(x.reshape(T, D), ln_emb_g, ln_emb_b)
    for l in range(DEPTH):
        wp, bp = _permute_in_proj(w_in[l], b_in[l])
        proj2 = _inproj_call(h, wp, bp)
        proj3 = proj2.reshape(B, S, PROJ_COLS)
        y_a = _hgrn_call(proj3, lb_all[l], hg_norm_g[l])
        y_b = _nsa_call(proj3, cmp_pe_k[l], cmp_w1_k[l], cmp_w2_k[l], cmp_pe_v[l], cmp_w1_v[l], cmp_w2_v[l])
        y_c = _lru_call(proj3, lru_conv_w[l], lru_conv_b[l], lru_wa[l], lru_ba[l], lru_wx[l], lru_bx[l],
                        lru_lambda[l])
        h = _merge_call(h, y_a.reshape(T, HG_WIDTH), y_b.reshape(T, NSA_WIDTH), y_c.reshape(T, LRU_WIDTH),
                        proj2, w_branch_hg[l], w_branch_nsa[l], w_branch_lru[l], w_out[l], ln1_g[l], ln1_b[l])
        h = _ffn_call(h, S, ffn_w_up[l], ffn_conv_w[l], ffn_conv_b[l], ffn_w_down[l], ln2_g[l], ln2_b[l])
    return h.reshape(B, S, D)
```

```python
import functools

import numpy as np
import jax
import jax.numpy as jnp
from jax import lax
from jax.experimental import pallas as pl
from jax.experimental.pallas import tpu as pltpu

F32 = jnp.float32
BF16 = jnp.bfloat16

DEPTH = 2
HG_HEADS = 4
HG_HEAD_DIM = 128
HG_WIDTH = HG_HEADS * HG_HEAD_DIM
NSA_HEADS = 8
NSA_GROUPS = 2
NSA_PER_GROUP = NSA_HEADS // NSA_GROUPS
NSA_HEAD_DIM = 64
NSA_WIDTH = NSA_HEADS * NSA_HEAD_DIM
NSA_KV = NSA_GROUPS * NSA_HEAD_DIM
CMP_LEN = 32
CMP_STRIDE = 16
CMP_HIDDEN = 64
SEL_LEN = 64
SEL_TOPK = 16
WINDOW = 512
Q_BLOCK = 128
LRU_WIDTH = 512
LRU_BLOCKS = 4
LRU_CONV = 4
LRU_C = 8.0
FFN_DIM = 2816
FFN_CONV = 3
DEEPNORM_ALPHA = (2 * DEPTH) ** 0.25
LN_EPS = 1e-5
N_BRANCH = 3

LANES = 128
SUBLANES = 8
VMEM_LIMIT_BYTES = 52 * 1024 * 1024

OFF_MG = 0
OFF_HQ = N_BRANCH * 1024
OFF_HF = OFF_HQ + HG_WIDTH
OFF_HI = OFF_HF + HG_WIDTH
OFF_HG = OFF_HI + HG_WIDTH
OFF_NQ = OFF_HG + HG_WIDTH
OFF_LX = OFF_NQ + NSA_WIDTH
OFF_LY = OFF_LX + LRU_WIDTH
OFF_KC = OFF_LY + LRU_WIDTH
OFF_VC = OFF_KC + NSA_KV
OFF_KS = OFF_VC + NSA_KV
OFF_VS = OFF_KS + NSA_KV
OFF_KW = OFF_VS + NSA_KV
OFF_VW = OFF_KW + NSA_KV
OFF_NG = OFF_VW + NSA_KV
PROJ_TILE_N = 1536
PROJ_COLS = 5 * PROJ_TILE_N

NEG_BIG = -1e30


def _split3(x):
    hi = x.astype(BF16)
    r1 = x - hi.astype(F32)
    mid = r1.astype(BF16)
    lo = (r1 - mid.astype(F32)).astype(BF16)
    return hi, mid, lo


def _dot(a, b):
    return jnp.dot(a, b, preferred_element_type=F32)


def _dot_nt(a, b):
    return lax.dot_general(a, b, (((1,), (1,)), ((), ())), preferred_element_type=F32)


def _dot_exact_lhs(m_bf16, x):
    hi, mid, lo = _split3(x)
    return _dot(m_bf16, hi) + _dot(m_bf16, mid) + _dot(m_bf16, lo)


def _dot_hilo(a, b):
    ah = a.astype(BF16)
    al = (a - ah.astype(F32)).astype(BF16)
    bh = b.astype(BF16)
    bl = (b - bh.astype(F32)).astype(BF16)
    return _dot(ah, bh) + _dot(ah, bl) + _dot(al, bh) + _dot(al, bl)


def _layer_norm_rows(z, g, b):
    mu = jnp.mean(z, -1, keepdims=True)
    zc = z - mu
    var = jnp.mean(zc * zc, -1, keepdims=True)
    return zc * lax.rsqrt(var + LN_EPS) * g + b


def _params(*sem):
    return pltpu.CompilerParams(dimension_semantics=sem, vmem_limit_bytes=VMEM_LIMIT_BYTES)


def _ln_kernel(x_ref, g_ref, b_ref, o_ref):
    o_ref[...] = _layer_norm_rows(x_ref[...], g_ref[...], b_ref[...])


def _layer_norm_call(x2, g, b, tm=512):
    T, D = x2.shape
    return pl.pallas_call(
        _ln_kernel,
        grid=(T // tm,),
        in_specs=[pl.BlockSpec((tm, D), lambda i: (i, 0)),
                  pl.BlockSpec((1, D), lambda i: (0, 0)),
                  pl.BlockSpec((1, D), lambda i: (0, 0))],
        out_specs=pl.BlockSpec((tm, D), lambda i: (i, 0)),
        out_shape=jax.ShapeDtypeStruct((T, D), F32),
        compiler_params=_params("parallel"),
        name="embed_ln",
    )(x2, g.reshape(1, D), b.reshape(1, D))


def _inproj_kernel(x_ref, w_ref, b_ref, o_ref, xb_ref):
    @pl.when(pl.program_id(1) == 0)
    def _():
        xb_ref[...] = x_ref[...].astype(BF16)

    o_ref[...] = _dot(xb_ref[...], w_ref[...]) + b_ref[...]


def _inproj_call(x2, w_bf16, bias, tm=1024):
    T, D = x2.shape
    N = w_bf16.shape[1]
    tm = min(tm, T)
    tn = PROJ_TILE_N
    return pl.pallas_call(
        _inproj_kernel,
        grid=(T // tm, N // tn),
        in_specs=[pl.BlockSpec((tm, D), lambda i, j: (i, 0)),
                  pl.BlockSpec((D, tn), lambda i, j: (0, j)),
                  pl.BlockSpec((1, tn), lambda i, j: (0, j))],
        out_specs=pl.BlockSpec((tm, tn), lambda i, j: (i, j)),
        out_shape=jax.ShapeDtypeStruct((T, N), F32),
        scratch_shapes=[pltpu.VMEM((tm, D), BF16)],
        compiler_params=_params("parallel", "arbitrary"),
        name="in_proj",
    )(x2, w_bf16, bias.reshape(1, N))


HG_CHUNK = 128
HG_DIAG = SUBLANES


def _hgrn_constants(C):
    r = np.arange(C)
    tri = (r[None, :] <= r[:, None]).astype(np.float32)
    mats = [tri]
    masks = []
    w = HG_DIAG
    while 2 * w <= C:
        mid = (r // (2 * w)) * (2 * w) + w
        mats.append(tri - (r[None, :] <= mid[:, None]).astype(np.float32))
        same = (r[:, None] // (2 * w)) == (r[None, :] // (2 * w))
        masks.append((same & ((r[:, None] % (2 * w)) >= w) & ((r[None, :] % (2 * w)) < w)).astype(np.float32))
        w *= 2
    mats.append((r[None, :] > r[:, None]).astype(np.float32))
    return np.concatenate(mats, 0), np.stack(masks, 0)


def _hgrn_kernel(q_ref, f_ref, i_ref, g_ref, lb_ref, ng_ref, m_ref, mask_ref, o_ref, st_ref, *, C, nchunk):
    d = HG_HEAD_DIM
    nlev = mask_ref.shape[0]

    @pl.when(pl.program_id(2) == 0)
    def _():
        st_ref[...] = jnp.zeros_like(st_ref)

    lb = lb_ref[0]
    ng = ng_ref[0]
    nb = C // HG_DIAG
    row3 = lax.broadcasted_iota(jnp.int32, (nb, HG_DIAG, d), 1)

    for c in range(nchunk):
        sl = pl.ds(c * C, C)
        q = q_ref[0, sl, :]
        fl = f_ref[0, sl, :]
        iv = i_ref[0, sl, :]
        g = g_ref[0, sl, :]
        lf = jnp.log(lb + (1.0 - lb) * jax.nn.sigmoid(fl))
        kk = (1.0 - lb) * jax.nn.sigmoid(-fl)
        allm = _dot_exact_lhs(m_ref[...], lf)
        bcs = allm[0:C]
        rem = allm[(nlev + 1) * C:(nlev + 2) * C]
        b_last = bcs[C - 1:C, :]
        iv_b = iv.astype(BF16)

        att = jnp.zeros((C, C), F32)
        for l in range(nlev):
            e = jnp.exp(-jnp.abs(allm[(1 + l) * C:(2 + l) * C]))
            att = att + mask_ref[l] * _dot_nt((q * e).astype(BF16), (kk * e).astype(BF16))
        o = _dot(att.astype(BF16), iv_b)

        q3 = q.reshape(nb, HG_DIAG, d)
        k3 = kk.reshape(nb, HG_DIAG, d)
        b3 = bcs.reshape(nb, HG_DIAG, d)
        i3 = iv.reshape(nb, HG_DIAG, d)
        acc = jnp.zeros((nb, HG_DIAG, d), F32)
        for s in range(HG_DIAG):
            dec = jnp.where(row3 >= s, jnp.exp(b3 - b3[:, s:s + 1, :]), 0.0)
            a = jnp.sum(q3 * dec * k3[:, s:s + 1, :], axis=-1, keepdims=True)
            acc = acc + a * i3[:, s:s + 1, :]
        o = o + acc.reshape(C, d)

        st = st_ref[...]
        o = o + _dot_nt((q * jnp.exp(bcs)).astype(BF16), st.astype(BF16))
        kdec = (kk * jnp.exp(rem)).astype(BF16)
        st_ref[...] = st * jnp.exp(b_last) + _dot(iv.T.astype(BF16), kdec)

        o = o * lax.rsqrt(jnp.mean(o * o, -1, keepdims=True) + 1e-6) * ng
        o_ref[0, sl, :] = (o * (g * jax.nn.sigmoid(g))).astype(o_ref.dtype)


def _hgrn_call(proj3, lb, norm_g, tt=512):
    B, S, _ = proj3.shape
    d = HG_HEAD_DIM
    C = HG_CHUNK
    tt = min(tt, S)
    mstack, masks = _hgrn_constants(C)
    nm = mstack.shape[0]
    col = lambda off: (lambda b, h, t: (b, t, off // d + h))
    blk = (1, tt, d)
    return pl.pallas_call(
        functools.partial(_hgrn_kernel, C=C, nchunk=tt // C),
        grid=(B, HG_HEADS, S // tt),
        in_specs=[pl.BlockSpec(blk, col(OFF_HQ)), pl.BlockSpec(blk, col(OFF_HF)),
                  pl.BlockSpec(blk, col(OFF_HI)), pl.BlockSpec(blk, col(OFF_HG)),
                  pl.BlockSpec((1, 1, d), lambda b, h, t: (h, 0, 0)),
                  pl.BlockSpec((1, 1, d), lambda b, h, t: (h, 0, 0)),
                  pl.BlockSpec((nm, C), lambda b, h, t: (0, 0)),
                  pl.BlockSpec(masks.shape, lambda b, h, t: (0, 0, 0))],
        out_specs=pl.BlockSpec(blk, lambda b, h, t: (b, t, h)),
        out_shape=jax.ShapeDtypeStruct((B, S, HG_WIDTH), BF16),
        scratch_shapes=[pltpu.VMEM((d, d), F32)],
        compiler_params=_params("parallel", "parallel", "arbitrary"),
        name="hgrn2",
    )(proj3, proj3, proj3, proj3, lb.reshape(HG_HEADS, 1, d), norm_g.reshape(HG_HEADS, 1, d),
      jnp.asarray(mstack, BF16), jnp.asarray(masks, F32))


LRU_PAD = SUBLANES
LRU_STEP = 16


def _lru_kernel(x_ref, y_ref, cw_ref, cb_ref, wa_ref, ba_ref, wx_ref, bx_ref, c_ref, o_ref,
                xpad_ref, h_ref, a_ref, u_ref, *, tt):
    W = LRU_WIDTH
    bw = W // LRU_BLOCKS

    @pl.when(pl.program_id(1) == 0)
    def _():
        xpad_ref[0:LRU_PAD, :] = jnp.zeros((LRU_PAD, W), F32)
        h_ref[...] = jnp.zeros_like(h_ref)

    x = x_ref[0]
    xpad_ref[LRU_PAD:LRU_PAD + tt, :] = x
    cw = cw_ref[...]
    xc = cb_ref[...] + cw[LRU_CONV - 1:LRU_CONV, :] * x
    for j in range(LRU_CONV - 1):
        off = LRU_PAD - (LRU_CONV - 1) + j
        xc = xc + cw[j:j + 1, :] * xpad_ref[off:off + tt, :]
    xpad_ref[0:LRU_PAD, :] = x[tt - LRU_PAD:tt, :]

    rs, is_ = [], []
    for gi in range(LRU_BLOCKS):
        xg = xc[:, gi * bw:(gi + 1) * bw].astype(BF16)
        rs.append(_dot(xg, wa_ref[gi]))
        is_.append(_dot(xg, wx_ref[gi]))
    r = jax.nn.sigmoid(jnp.concatenate(rs, -1) + ba_ref[...])
    ig = jax.nn.sigmoid(jnp.concatenate(is_, -1) + bx_ref[...])
    log_a = c_ref[...] * r
    a = jnp.exp(log_a)
    u = jnp.sqrt(-jnp.tanh(log_a) * (a * a + 1.0)) * (ig * xc)

    sub = lax.broadcasted_iota(jnp.int32, (tt, W), 0) % SUBLANES
    for dlt in (1, 2, 4):
        keep = sub >= dlt
        a_sh = jnp.where(keep, pltpu.roll(a, dlt, 0), 1.0)
        u_sh = jnp.where(keep, pltpu.roll(u, dlt, 0), 0.0)
        u = a * u_sh + u
        a = a * a_sh
    a_ref[...] = a
    u_ref[...] = u

    def step(i, h):
        r0 = pl.multiple_of(i * LRU_STEP, LRU_STEP)
        outs = []
        for k in range(LRU_STEP // SUBLANES):
            rows = pl.ds(r0 + k * SUBLANES, SUBLANES)
            hh = u_ref[rows, :] + a_ref[rows, :] * h
            outs.append(hh)
            h = hh[SUBLANES - 1:SUBLANES, :]
        rows = pl.ds(r0, LRU_STEP)
        o_ref[0, rows, :] = (jnp.concatenate(outs, 0) * jax.nn.gelu(y_ref[0, rows, :])).astype(o_ref.dtype)
        return h

    h_ref[...] = lax.fori_loop(0, tt // LRU_STEP, step, h_ref[...])


def _lru_call(proj3, conv_w, conv_b, wa, ba, wx, bx, lam, tt=512):
    B, S, _ = proj3.shape
    W = LRU_WIDTH
    bw = W // LRU_BLOCKS
    tt = min(tt, S)
    c = (-LRU_C * jax.nn.softplus(-lam.astype(F32))).reshape(1, W)
    vec = lambda: pl.BlockSpec((1, W), lambda b, t: (0, 0))
    return pl.pallas_call(
        functools.partial(_lru_kernel, tt=tt),
        grid=(B, S // tt),
        in_specs=[pl.BlockSpec((1, tt, W), lambda b, t: (b, t, OFF_LX // W)),
                  pl.BlockSpec((1, tt, W), lambda b, t: (b, t, OFF_LY // W)),
                  pl.BlockSpec((LRU_CONV, W), lambda b, t: (0, 0)), vec(),
                  pl.BlockSpec((LRU_BLOCKS, bw, bw), lambda b, t: (0, 0, 0)), vec(),
                  pl.BlockSpec((LRU_BLOCKS, bw, bw), lambda b, t: (0, 0, 0)), vec(), vec()],
        out_specs=pl.BlockSpec((1, tt, W), lambda b, t: (b, t, 0)),
        out_shape=jax.ShapeDtypeStruct((B, S, W), BF16),
        scratch_shapes=[pltpu.VMEM((tt + LRU_PAD, W), F32), pltpu.VMEM((1, W), F32),
                        pltpu.VMEM((tt, W), F32), pltpu.VMEM((tt, W), F32)],
        compiler_params=_params("parallel", "arbitrary"),
        name="rglru",
    )(proj3, proj3, conv_w.astype(F32), conv_b.reshape(1, W).astype(F32), wa.astype(BF16),
      ba.reshape(1, W), wx.astype(BF16), bx.reshape(1, W), c)


NSA_AUG = 4
NSA_KW = LANES
SEL_TILE = 256
SEL_TILE_BLOCKS = SEL_TILE // SEL_LEN
WIN_TILE = 128
POS_SPLIT = 128
PLAN_HALF = 16
KV_PREP_ROWS = 512


def _key_aug(pos, width):
    n = pos.shape[0]
    col = lax.broadcasted_iota(jnp.int32, (n, width), 1)
    hi = ((pos // POS_SPLIT) * POS_SPLIT).astype(F32)
    lo = (pos % POS_SPLIT).astype(F32)
    return jnp.where(col == 0, hi, jnp.where(col == 1, lo, jnp.where(col < NSA_AUG, 1.0, 0.0)))


def _query_t(q_ref, slope_ref, c0):
    hd, P, T = NSA_HEAD_DIM, NSA_PER_GROUP, Q_BLOCK
    PT = P * T
    qT = (q_ref[0] * (hd ** -0.5)).T
    qpart = jnp.concatenate([qT[p * hd:(p + 1) * hd, :] for p in range(P)], 1)
    tlane = c0 + lax.broadcasted_iota(jnp.int32, (1, PT), 1) % T
    slope = slope_ref[0]
    t_hi = ((tlane // POS_SPLIT) * POS_SPLIT).astype(F32)
    t_lo = (tlane % POS_SPLIT).astype(F32)
    rowi = lax.broadcasted_iota(jnp.int32, (NSA_KW - hd, PT), 0)
    aug = jnp.where(rowi < 2, slope,
                    jnp.where(rowi == 2, -slope * t_hi, jnp.where(rowi == 3, -slope * t_lo, 0.0)))
    return jnp.concatenate([qpart, aug], 0).astype(BF16), tlane


def _compress_kernel(xk_ref, xv_ref, w1k_ref, pek_ref, w1fk_ref, w2k_ref, w1v_ref, pev_ref, w1fv_ref, w2v_ref,
                     kc_ref, vct_ref, *, nc):
    hd = NSA_HEAD_DIM
    pos = lax.broadcasted_iota(jnp.int32, (nc, 1), 0) * CMP_STRIDE + (CMP_LEN - 1)
    aug = _key_aug(pos, NSA_KW - hd)

    def one(x_ref, w1_ref, pe_ref, w1f_ref, w2_ref, g):
        uv = jnp.zeros((nc, 2 * CMP_HIDDEN), F32)
        for r in range(CMP_STRIDE):
            xr = x_ref[0, pl.ds(r, nc, stride=CMP_STRIDE), :][:, g * hd:(g + 1) * hd]
            uv = uv + _dot_hilo(xr, w1_ref[r])
        cvec = _dot_hilo(pe_ref[...], w1f_ref[...])[0:1, :]
        hid = uv[:, 0:CMP_HIDDEN] + pltpu.roll(uv[:, CMP_HIDDEN:], nc - 1, 0) + cvec
        return _dot_hilo(jax.nn.gelu(hid), w2_ref[...])

    for g in range(NSA_GROUPS):
        kc = one(xk_ref, w1k_ref, pek_ref, w1fk_ref, w2k_ref, g)
        kc_ref[0, g] = jnp.concatenate([kc, aug], 1).astype(BF16)
        vct_ref[0, g] = one(xv_ref, w1v_ref, pev_ref, w1fv_ref, w2v_ref, g).T.astype(BF16)


def _compress_call(proj3, pe_k, w1_k, w2_k, pe_v, w1_v, w2_v):
    B, S, _ = proj3.shape
    G, hd = NSA_GROUPS, NSA_HEAD_DIM
    nc = S // CMP_STRIDE
    half = CMP_STRIDE * hd

    def prep(pe, w1):
        w1 = w1.astype(F32)
        wr = jnp.concatenate([w1[:half].reshape(CMP_STRIDE, hd, CMP_HIDDEN),
                              w1[half:].reshape(CMP_STRIDE, hd, CMP_HIDDEN)], -1)
        pe8 = jnp.broadcast_to(pe.reshape(1, CMP_LEN * hd).astype(F32), (SUBLANES, CMP_LEN * hd))
        return wr, pe8, w1

    full = lambda a: pl.BlockSpec(a.shape, lambda b: tuple(0 for _ in a.shape))
    col = lambda off: pl.BlockSpec((1, S, LANES), lambda b: (b, 0, off // LANES))
    args = prep(pe_k, w1_k) + (w2_k.astype(F32),) + prep(pe_v, w1_v) + (w2_v.astype(F32),)
    return pl.pallas_call(
        functools.partial(_compress_kernel, nc=nc),
        grid=(B,),
        in_specs=[col(OFF_KC), col(OFF_VC)] + [full(a) for a in args],
        out_specs=[pl.BlockSpec((1, G, nc, NSA_KW), lambda b: (b, 0, 0, 0)),
                   pl.BlockSpec((1, G, hd, nc), lambda b: (b, 0, 0, 0))],
        out_shape=[jax.ShapeDtypeStruct((B, G, nc, NSA_KW), BF16),
                   jax.ShapeDtypeStruct((B, G, hd, nc), BF16)],
        compiler_params=_params("parallel"),
        name="nsa_compress",
    )(proj3, proj3, *args)


def _kvprep_kernel(ks_ref, vs_ref, kw_ref, vw_ref, ksa_ref, vst_ref, kwa_ref, vwt_ref, *, tt):
    hd = NSA_HEAD_DIM
    pos = pl.program_id(1) * tt + lax.broadcasted_iota(jnp.int32, (tt, 1), 0)
    aug = _key_aug(pos, NSA_KW - hd)
    for g in range(NSA_GROUPS):
        lanes = slice(g * hd, (g + 1) * hd)
        ksa_ref[0, g] = jnp.concatenate([ks_ref[0][:, lanes], aug], 1).astype(BF16)
        kwa_ref[0, g] = jnp.concatenate([kw_ref[0][:, lanes], aug], 1).astype(BF16)
        vs_t = vs_ref[0][:, lanes].T.astype(BF16)
        vw_t = vw_ref[0][:, lanes].T.astype(BF16)
        for c in range(tt // SEL_TILE):
            vst_ref[0, g, c] = vs_t[:, c * SEL_TILE:(c + 1) * SEL_TILE]
        for c in range(tt // WIN_TILE):
            vwt_ref[0, g, c] = vw_t[:, c * WIN_TILE:(c + 1) * WIN_TILE]


def _kvprep_call(proj3):
    B, S, _ = proj3.shape
    G, hd = NSA_GROUPS, NSA_HEAD_DIM
    tt = min(KV_PREP_ROWS, S)
    col = lambda off: pl.BlockSpec((1, tt, LANES), lambda b, t: (b, t, off // LANES))
    return pl.pallas_call(
        functools.partial(_kvprep_kernel, tt=tt),
        grid=(B, S // tt),
        in_specs=[col(OFF_KS), col(OFF_VS), col(OFF_KW), col(OFF_VW)],
        out_specs=[pl.BlockSpec((1, G, tt, NSA_KW), lambda b, t: (b, 0, t, 0)),
                   pl.BlockSpec((1, G, tt // SEL_TILE, hd, SEL_TILE), lambda b, t: (b, 0, t, 0, 0)),
                   pl.BlockSpec((1, G, tt, NSA_KW), lambda b, t: (b, 0, t, 0)),
                   pl.BlockSpec((1, G, tt // WIN_TILE, hd, WIN_TILE), lambda b, t: (b, 0, t, 0, 0))],
        out_shape=[jax.ShapeDtypeStruct((B, G, S, NSA_KW), BF16),
                   jax.ShapeDtypeStruct((B, G, S // SEL_TILE, hd, SEL_TILE), BF16),
                   jax.ShapeDtypeStruct((B, G, S, NSA_KW), BF16),
                   jax.ShapeDtypeStruct((B, G, S // WIN_TILE, hd, WIN_TILE), BF16)],
        compiler_params=_params("parallel", "parallel"),
        name="nsa_kvprep",
    )(proj3, proj3, proj3, proj3)


def _importance_matrix(ns, nc):
    ratio = SEL_LEN // CMP_STRIDE
    a = np.zeros((ns, nc), np.float32)
    for j in range(ns):
        for n, wgt in ((ratio * j - 1, 0.5), (ratio * j, 1.0), (ratio * j + 1, 1.0),
                       (ratio * j + 2, 1.0), (ratio * j + 3, 0.5)):
            if 0 <= n < nc - 1:
                a[j, n] = wgt
    return a


def _plan_constants(ns):
    nt = ns // SEL_TILE_BLOCKS
    grp = (np.arange(ns)[None, :] // SEL_TILE_BLOCKS == np.arange(nt)[:, None]).astype(np.float32)
    k = np.arange(nt)
    w_lo = np.where(k < PLAN_HALF, 2.0 ** np.minimum(k, PLAN_HALF - 1), 0.0)
    w_hi = np.where(k >= PLAN_HALF, 2.0 ** np.maximum(k - PLAN_HALF, 0), 0.0)
    wts = np.stack([np.repeat(w_lo[:, None], LANES, 1), np.repeat(w_hi[:, None], LANES, 1)]).astype(np.float32)
    return grp, wts


def _select_kernel(q_ref, slope_ref, kc_ref, vct_ref, imp_ref, grp_ref, wts_ref, ocmp_ref, bias_ref, bits_ref,
                   *, ns, nc):
    P, T = NSA_PER_GROUP, Q_BLOCK
    c0 = pl.program_id(2) * T
    qt, tlane = _query_t(q_ref, slope_ref, c0)

    s = _dot(kc_ref[0, 0], qt)
    cend = lax.broadcasted_iota(jnp.int32, (nc, 1), 0) * CMP_STRIDE + (CMP_LEN - 1)
    cmask = cend <= tlane
    s = jnp.where(cmask, s, NEG_BIG)
    m = jnp.max(s, 0, keepdims=True)
    p_c = jnp.where(cmask, jnp.exp(s - m), 0.0)
    p_c = p_c / jnp.maximum(jnp.sum(p_c, 0, keepdims=True), 1e-30)
    ocmp_ref[0, 0, 0] = _dot(vct_ref[0, 0], p_c.astype(BF16))

    psum = p_c[:, 0:T]
    for p in range(1, P):
        psum = psum + p_c[:, p * T:(p + 1) * T]
    imp = _dot_exact_lhs(imp_ref[...], psum)
    blk = lax.broadcasted_iota(jnp.int32, (ns, T), 0)
    cur = (c0 + lax.broadcasted_iota(jnp.int32, (ns, T), 1)) // SEL_LEN
    forced = (blk == 0) | (blk == cur) | (blk == cur - 1)
    val = jnp.where(forced, 3e38, jnp.where(blk > cur, -1.0, imp))
    chosen = jnp.zeros((ns, T), F32)
    for _ in range(min(SEL_TOPK, ns)):
        mx = jnp.max(val, 0, keepdims=True)
        first = jnp.min(jnp.where(val == mx, blk, ns), 0, keepdims=True)
        pick = blk == first
        chosen = jnp.where(pick, 1.0, chosen)
        val = jnp.where(pick, -2.0, val)
    bias_ref[0, 0, 0] = jnp.where(chosen > 0.5, 0.0, NEG_BIG)

    cnt = _dot(grp_ref[...], chosen.astype(BF16))
    act = jnp.where(jnp.max(cnt, 1, keepdims=True) > 0.5, 1.0, 0.0)
    lo = jnp.sum(act * wts_ref[0], 0, keepdims=True).astype(jnp.int32)
    hi = jnp.sum(act * wts_ref[1], 0, keepdims=True).astype(jnp.int32)
    bits_ref[0] = jnp.broadcast_to(lo | (hi << PLAN_HALF), (SUBLANES, LANES))


def _attend_kernel(bits_sref, q_ref, gt_ref, slope_ref, ocmp_ref, bias_ref, ks_ref, vst_ref, kw_ref, vwt_ref,
                   o_ref, list_ref, *, n_tiles_total):
    hd, P, T = NSA_HEAD_DIM, NSA_PER_GROUP, Q_BLOCK
    PT = P * T
    b, g, iq = pl.program_id(0), pl.program_id(1), pl.program_id(2)
    c0 = iq * T
    qt, tlane = _query_t(q_ref, slope_ref, c0)
    tl_row = lax.broadcasted_iota(jnp.int32, (1, PT), 1) % T

    gt = jax.nn.sigmoid(gt_ref[0]).T

    def gate_row(branch):
        return jnp.concatenate([gt[p * 3 + branch:p * 3 + branch + 1, :] for p in range(P)], 1)

    bits = bits_sref[(b * pl.num_programs(1) + g) * pl.num_programs(2) + iq]
    last_tile = (c0 + T - 1) // SEL_TILE

    def compact(k, n):
        active = (lax.shift_right_logical(bits, k) & 1) == 1
        take = jnp.logical_and(active, k < last_tile)

        @pl.when(take)
        def _():
            list_ref[n] = k

        return n + take.astype(jnp.int32)

    n_prev = lax.fori_loop(0, n_tiles_total, compact, 0)

    def tile_scores(kt, dead):
        k0 = pl.multiple_of(kt * SEL_TILE, SEL_TILE)
        rows = []
        for r in range(SEL_TILE_BLOCKS):
            srow = bias_ref[0, 0, 0, pl.ds(kt * SEL_TILE_BLOCKS + r, 1), :]
            rows.append(jnp.broadcast_to(srow, (SEL_LEN, T)))
        sb = jnp.concatenate(rows, 0)
        sb = jnp.where(dead, NEG_BIG, sb)
        return _dot(ks_ref[0, 0, pl.ds(k0, SEL_TILE), :], qt) + jnp.concatenate([sb] * P, 1)

    def softmax_update(carry, scores, vts):
        m_run, l_run, acc = carry
        m_new = m_run
        for sc in scores:
            m_new = jnp.maximum(m_new, jnp.max(sc, 0, keepdims=True))
        alpha = jnp.exp(m_run - m_new)
        l_new = alpha * l_run
        acc = alpha * acc
        for sc, vt in zip(scores, vts):
            pr = jnp.exp(sc - m_new)
            l_new = l_new + jnp.sum(pr, 0, keepdims=True)
            acc = acc + _dot(vt, pr.astype(BF16))
        return m_new, l_new, acc

    init = (jnp.full((1, PT), NEG_BIG, F32), jnp.zeros((1, PT), F32), jnp.zeros((hd, PT), F32))

    def pair_step(i, carry):
        ka = list_ref[2 * i]
        has_b = 2 * i + 1 < n_prev
        kb = list_ref[jnp.where(has_b, 2 * i + 1, 2 * i)]
        sa = tile_scores(ka, False)
        sb = tile_scores(kb, jnp.logical_not(has_b))
        return softmax_update(carry, [sa, sb], [vst_ref[0, 0, ka], vst_ref[0, 0, kb]])

    carry = lax.fori_loop(0, (n_prev + 1) // 2, pair_step, init)
    kpos = last_tile * SEL_TILE + lax.broadcasted_iota(jnp.int32, (SEL_TILE, 1), 0)
    s_last = jnp.where(kpos <= tlane, tile_scores(last_tile, False), NEG_BIG)
    _, l_sel, acc_sel = softmax_update(carry, [s_last], [vst_ref[0, 0, last_tile]])
    out = gate_row(0) * ocmp_ref[0, 0, 0] + gate_row(1) * (acc_sel / l_sel)

    n_win = (WINDOW + T) // WIN_TILE
    roff = lax.broadcasted_iota(jnp.int32, (WIN_TILE, 1), 0)
    scores, vts = [], []
    for j in range(n_win):
        start = c0 - WINDOW + j * WIN_TILE
        valid = start >= 0
        k0 = pl.multiple_of(jnp.maximum(start, 0), WIN_TILE)
        sc = _dot(kw_ref[0, 0, pl.ds(k0, WIN_TILE), :], qt)
        if j == 0:
            sc = jnp.where(roff > tl_row, sc, NEG_BIG)
        if j == n_win - 1:
            sc = jnp.where(roff <= tl_row, sc, NEG_BIG)
        else:
            sc = jnp.where(valid, sc, NEG_BIG)
        scores.append(sc)
        vts.append(vwt_ref[0, 0, k0 // WIN_TILE])
    _, l_win, acc_win = softmax_update(init, scores, vts)
    out = out + gate_row(2) * (acc_win / l_win)

    o_ref[0] = jnp.concatenate([out[:, p * T:(p + 1) * T] for p in range(P)], 0).T.astype(o_ref.dtype)


def _nsa_call(proj3, pe_k, w1_k, w2_k, pe_v, w1_v, w2_v):
    B, S, _ = proj3.shape
    G, hd, P, T = NSA_GROUPS, NSA_HEAD_DIM, NSA_PER_GROUP, Q_BLOCK
    PT = P * T
    nc = S // CMP_STRIDE
    ns = S // SEL_LEN
    nq = S // T
    nt = ns // SEL_TILE_BLOCKS
    assert nt <= 2 * PLAN_HALF and S % KV_PREP_ROWS == 0

    kc_a, vc_t = _compress_call(proj3, pe_k, w1_k, w2_k, pe_v, w1_v, w2_v)
    ks_a, vs_t, kw_a, vw_t = _kvprep_call(proj3)
    imp_m = jnp.asarray(_importance_matrix(ns, nc), BF16)
    grp, wts = _plan_constants(ns)
    head_slopes = 2.0 ** (-(8.0 / NSA_HEADS) * np.arange(1, NSA_HEADS + 1, dtype=np.float32))
    slopes = jnp.asarray(np.repeat(head_slopes.reshape(G, 1, P), T, axis=-1), F32)

    q_spec = lambda im: pl.BlockSpec((1, T, P * hd), im)
    bg = lambda shp: pl.BlockSpec((1, 1) + shp, lambda b, g, i: (b, g) + tuple(0 for _ in shp))
    const = lambda a: pl.BlockSpec(a.shape, lambda b, g, i: tuple(0 for _ in a.shape))
    ocmp, sel_bias, bits = pl.pallas_call(
        functools.partial(_select_kernel, ns=ns, nc=nc),
        grid=(B, G, nq),
        in_specs=[q_spec(lambda b, g, i: (b, i, OFF_NQ // (P * hd) + g)),
                  pl.BlockSpec((1, 1, PT), lambda b, g, i: (g, 0, 0)),
                  bg((nc, NSA_KW)), bg((hd, nc)), const(imp_m),
                  pl.BlockSpec(grp.shape, lambda b, g, i: (0, 0)),
                  pl.BlockSpec(wts.shape, lambda b, g, i: (0, 0, 0))],
        out_specs=[pl.BlockSpec((1, 1, 1, hd, PT), lambda b, g, i: (b, g, i, 0, 0)),
                   pl.BlockSpec((1, 1, 1, ns, T), lambda b, g, i: (b, g, i, 0, 0)),
                   pl.BlockSpec((1, SUBLANES, LANES), lambda b, g, i: ((b * G + g) * nq + i, 0, 0))],
        out_shape=[jax.ShapeDtypeStruct((B, G, nq, hd, PT), F32),
                   jax.ShapeDtypeStruct((B, G, nq, ns, T), F32),
                   jax.ShapeDtypeStruct((B * G * nq, SUBLANES, LANES), jnp.int32)],
        compiler_params=_params("parallel", "parallel", "parallel"),
        name="nsa_select",
    )(proj3, slopes, kc_a, vc_t, imp_m, jnp.asarray(grp, BF16), jnp.asarray(wts, F32))

    bg2 = lambda shp: pl.BlockSpec((1, 1) + shp, lambda b, g, i, s: (b, g) + tuple(0 for _ in shp))
    return pl.pallas_call(
        functools.partial(_attend_kernel, n_tiles_total=nt),
        grid_spec=pltpu.PrefetchScalarGridSpec(
            num_scalar_prefetch=1,
            grid=(B, G, nq),
            in_specs=[pl.BlockSpec((1, T, P * hd), lambda b, g, i, s: (b, i, OFF_NQ // (P * hd) + g)),
                      pl.BlockSpec((1, T, LANES), lambda b, g, i, s: (b, i, OFF_NG // LANES + g)),
                      pl.BlockSpec((1, 1, PT), lambda b, g, i, s: (g, 0, 0)),
                      pl.BlockSpec((1, 1, 1, hd, PT), lambda b, g, i, s: (b, g, i, 0, 0)),
                      pl.BlockSpec((1, 1, 1, ns, T), lambda b, g, i, s: (b, g, i, 0, 0)),
                      bg2((S, NSA_KW)), bg2((S // SEL_TILE, hd, SEL_TILE)),
                      bg2((S, NSA_KW)), bg2((S // WIN_TILE, hd, WIN_TILE))],
            out_specs=pl.BlockSpec((1, T, P * hd), lambda b, g, i, s: (b, i, g)),
            scratch_shapes=[pltpu.SMEM((nt,), jnp.int32)]),
        out_shape=jax.ShapeDtypeStruct((B, S, NSA_WIDTH), BF16),
        compiler_params=_params("parallel", "parallel", "arbitrary"),
        name="nsa_attend",
    )(bits[:, 0, 0], proj3, proj3, slopes, ocmp, sel_bias, ks_a, vs_t, kw_a, vw_t)


def _merge_kernel(x_ref, ya_ref, yb_ref, yc_ref, g0_ref, g1_ref, g2_ref, wa_ref, wb_ref, wc_ref, wo_ref,
                  lg_ref, lb_ref, o_ref):
    m = jax.nn.sigmoid(g0_ref[...]) * _dot(ya_ref[...], wa_ref[...])
    m = m + jax.nn.sigmoid(g1_ref[...]) * _dot(yb_ref[...], wb_ref[...])
    m = m + jax.nn.sigmoid(g2_ref[...]) * _dot(yc_ref[...], wc_ref[...])
    z = DEEPNORM_ALPHA * x_ref[...] + _dot(m.astype(BF16), wo_ref[...])
    o_ref[...] = _layer_norm_rows(z, lg_ref[...], lb_ref[...])


def _merge_call(x2, ya, yb, yc, proj2, w_hg, w_nsa, w_lru, w_out, ln_g, ln_b, tm=512):
    T, D = x2.shape
    tm = min(tm, T)
    rows = lambda w: pl.BlockSpec((tm, w), lambda i: (i, 0))
    gate = lambda n: pl.BlockSpec((tm, D), lambda i: (i, OFF_MG // D + n))
    full = lambda a: pl.BlockSpec(a.shape, lambda i: (0, 0))
    ws = [w.astype(BF16) for w in (w_hg, w_nsa, w_lru, w_out)]
    return pl.pallas_call(
        _merge_kernel,
        grid=(T // tm,),
        in_specs=[rows(D), rows(HG_WIDTH), rows(NSA_WIDTH), rows(LRU_WIDTH), gate(0), gate(1), gate(2)]
                 + [full(w) for w in ws] + [pl.BlockSpec((1, D), lambda i: (0, 0))] * 2,
        out_specs=rows(D),
        out_shape=jax.ShapeDtypeStruct((T, D), F32),
        compiler_params=_params("parallel"),
        name="merge_out",
    )(x2, ya, yb, yc, proj2, proj2, proj2, *ws, ln_g.reshape(1, D), ln_b.reshape(1, D))


FFN_SPLIT = 2
FFN_PAD = SUBLANES


def _ffn_kernel(x_ref, wu_ref, wv_ref, cw_ref, cb_ref, wd_ref, lg_ref, lb_ref, o_ref,
                xb_ref, acc_ref, upad_ref, carry_ref, *, tm, tiles_per_seq):
    i = pl.program_id(0)
    j = pl.program_id(1)

    @pl.when(j == 0)
    def _():
        xb_ref[...] = x_ref[...].astype(BF16)

    xb = xb_ref[...]
    u = _dot(xb, wu_ref[...])
    v = _dot(xb, wv_ref[...])
    first = (i % tiles_per_seq) == 0

    @pl.when(first)
    def _():
        upad_ref[0:FFN_PAD, :] = jnp.zeros((FFN_PAD, u.shape[1]), F32)

    @pl.when(jnp.logical_not(first))
    def _():
        upad_ref[0:FFN_PAD, :] = carry_ref[j]

    upad_ref[FFN_PAD:FFN_PAD + tm, :] = u
    carry_ref[j] = u[tm - FFN_PAD:tm, :]
    cw = cw_ref[...]
    cv = cb_ref[...] + cw[FFN_CONV - 1:FFN_CONV, :] * u
    for k in range(FFN_CONV - 1):
        off = FFN_PAD - (FFN_CONV - 1) + k
        cv = cv + cw[k:k + 1, :] * upad_ref[off:off + tm, :]
    h = (jax.nn.gelu(cv) * v).astype(BF16)
    part = _dot(h, wd_ref[...])

    @pl.when(j == 0)
    def _():
        acc_ref[...] = part

    @pl.when(j > 0)
    def _():
        acc_ref[...] += part

    @pl.when(j == FFN_SPLIT - 1)
    def _():
        z = DEEPNORM_ALPHA * x_ref[...] + acc_ref[...]
        o_ref[...] = _layer_norm_rows(z, lg_ref[...], lb_ref[...])


def _ffn_call(x2, seq_len, w_up, conv_w, conv_b, w_down, ln_g, ln_b, tm=512):
    T, D = x2.shape
    tm = min(tm, seq_len)
    fc = FFN_DIM // FFN_SPLIT
    wu = w_up.astype(BF16)
    return pl.pallas_call(
        functools.partial(_ffn_kernel, tm=tm, tiles_per_seq=seq_len // tm),
        grid=(T // tm, FFN_SPLIT),
        in_specs=[pl.BlockSpec((tm, D), lambda i, j: (i, 0)),
                  pl.BlockSpec((D, fc), lambda i, j: (0, j)),
                  pl.BlockSpec((D, fc), lambda i, j: (0, FFN_SPLIT + j)),
                  pl.BlockSpec((FFN_CONV, fc), lambda i, j: (0, j)),
                  pl.BlockSpec((1, fc), lambda i, j: (0, j)),
                  pl.BlockSpec((fc, D), lambda i, j: (j, 0)),
                  pl.BlockSpec((1, D), lambda i, j: (0, 0)),
                  pl.BlockSpec((1, D), lambda i, j: (0, 0))],
        out_specs=pl.BlockSpec((tm, D), lambda i, j: (i, 0)),
        out_shape=jax.ShapeDtypeStruct((T, D), F32),
        scratch_shapes=[pltpu.VMEM((tm, D), BF16), pltpu.VMEM((tm, D), F32),
                        pltpu.VMEM((tm + FFN_PAD, fc), F32), pltpu.VMEM((FFN_SPLIT, FFN_PAD, fc), F32)],
        compiler_params=_params("arbitrary", "arbitrary"),
        name="conv_ffn",
    )(x2, wu, wu, conv_w.astype(F32), conv_b.reshape(1, FFN_DIM).astype(F32), w_down.astype(BF16),
      ln_g.reshape(1, D), ln_b.reshape(1, D))


def _permute_in_proj(w, b):
    src = np.cumsum([0, HG_WIDTH, HG_WIDTH, HG_WIDTH, HG_WIDTH, NSA_WIDTH, NSA_KV, NSA_KV, NSA_KV, NSA_KV,
                     NSA_KV, NSA_KV, NSA_HEADS * 3, LRU_WIDTH, LRU_WIDTH, N_BRANCH * 1024])
    names = ["hq", "hf", "hi", "hg", "nq", "kc", "vc", "ks", "vs", "kw", "vw", "ng", "lx", "ly", "mg"]
    seg = {n: (int(src[k]), int(src[k + 1])) for k, n in enumerate(names)}
    order = ["mg", "hq", "hf", "hi", "hg", "nq", "lx", "ly", "kc", "vc", "ks", "vs", "kw", "vw"]
    per_group = NSA_PER_GROUP * 3
    ng0 = seg["ng"][0]
    w_parts = [w[:, seg[n][0]:seg[n][1]] for n in order]
    b_parts = [b[seg[n][0]:seg[n][1]] for n in order]
    for gi in range(NSA_GROUPS):
        lo = ng0 + gi * per_group
        w_parts += [w[:, lo:lo + per_group], jnp.zeros((w.shape[0], LANES - per_group), w.dtype)]
        b_parts += [b[lo:lo + per_group], jnp.zeros((LANES - per_group,), b.dtype)]
    wp = jnp.concatenate(w_parts, 1)
    bp = jnp.concatenate(b_parts)
    assert wp.shape[1] == PROJ_COLS
    return wp.astype(BF16), bp.astype(F32)


def kernel(x, ln_emb_g, ln_emb_b, w_in, b_in, hg_lb_logits, hg_norm_g, cmp_pe_k, cmp_w1_k, cmp_w2_k, cmp_pe_v, cmp_w1_v, cmp_w2_v, lru_conv_w, lru_conv_b, lru_wa, lru_ba, lru_wx, lru_bx, lru_lambda, w_branch_hg, w_branch_nsa, w_branch_lru, w_out, ln1_g, ln1_b, ffn_w_up, ffn_conv_w, ffn_conv_b, ffn_w_down, ln2_g, ln2_b):
    B, S, D = x.shape
    T = B * S
    gam = jax.nn.softmax(hg_lb_logits.astype(F32), axis=0)
    lb_all = jnp.cumsum(gam, axis=0) - gam[0]
    h = _layer_norm_call(x.reshape(T, D), ln_emb_g, ln_emb_b)
    for l in range(DEPTH):
        wp, bp = _permute_in_proj(w_in[l], b_in[l])
        proj2 = _inproj_call(h, wp, bp)
        proj3 = proj2.reshape(B, S, PROJ_COLS)
        y_a = _hgrn_call(proj3, lb_all[l], hg_norm_g[l])
        y_b = _nsa_call(proj3, cmp_pe_k[l], cmp_w1_k[l], cmp_w2_k[l], cmp_pe_v[l], cmp_w1_v[l], cmp_w2_v[l])
        y_c = _lru_call(proj3, lru_conv_w[l], lru_conv_b[l], lru_wa[l], lru_ba[l], lru_wx[l], lru_bx[l],
                        lru_lambda[l])
        h = _merge_call(h, y_a.reshape(T, HG_WIDTH), y_b.reshape(T, NSA_WIDTH), y_c.reshape(T, LRU_WIDTH),
                        proj2, w_branch_hg[l], w_branch_nsa[l], w_branch_lru[l], w_out[l], ln1_g[l], ln1_b[l])
        h = _ffn_call(h, S, ffn_w_up[l], ffn_conv_w[l], ffn_conv_b[l], ffn_w_down[l], ln2_g[l], ln2_b[l])
    return h.reshape(B, S, D)
```

```python
import functools

import numpy as np
import jax
import jax.numpy as jnp
from jax import lax
from jax.experimental import pallas as pl
from jax.experimental.pallas import tpu as pltpu

F32 = jnp.float32
BF16 = jnp.bfloat16

DEPTH = 2
HG_HEADS = 4
HG_HEAD_DIM = 128
HG_WIDTH = HG_HEADS * HG_HEAD_DIM
NSA_HEADS = 8
NSA_GROUPS = 2
NSA_PER_GROUP = NSA_HEADS // NSA_GROUPS
NSA_HEAD_DIM = 64
NSA_WIDTH = NSA_HEADS * NSA_HEAD_DIM
NSA_KV = NSA_GROUPS * NSA_HEAD_DIM
CMP_LEN = 32
CMP_STRIDE = 16
CMP_HIDDEN = 64
SEL_LEN = 64
SEL_TOPK = 16
WINDOW = 512
Q_BLOCK = 128
LRU_WIDTH = 512
LRU_BLOCKS = 4
LRU_CONV = 4
LRU_C = 8.0
FFN_DIM = 2816
FFN_CONV = 3
DEEPNORM_ALPHA = (2 * DEPTH) ** 0.25
LN_EPS = 1e-5
N_BRANCH = 3

LANES = 128
SUBLANES = 8
VMEM_LIMIT_BYTES = 52 * 1024 * 1024

OFF_MG = 0
OFF_HQ = N_BRANCH * 1024
OFF_HF = OFF_HQ + HG_WIDTH
OFF_HI = OFF_HF + HG_WIDTH
OFF_HG = OFF_HI + HG_WIDTH
OFF_NQ = OFF_HG + HG_WIDTH
OFF_LX = OFF_NQ + NSA_WIDTH
OFF_LY = OFF_LX + LRU_WIDTH
OFF_KC = OFF_LY + LRU_WIDTH
OFF_VC = OFF_KC + NSA_KV
OFF_KS = OFF_VC + NSA_KV
OFF_VS = OFF_KS + NSA_KV
OFF_KW = OFF_VS + NSA_KV
OFF_VW = OFF_KW + NSA_KV
OFF_NG = OFF_VW + NSA_KV
PROJ_TILE_N = 1536
PROJ_COLS = 5 * PROJ_TILE_N

NEG_BIG = -1e30


def _split3(x):
    hi = x.astype(BF16)
    r1 = x - hi.astype(F32)
    mid = r1.astype(BF16)
    lo = (r1 - mid.astype(F32)).astype(BF16)
    return hi, mid, lo


def _dot(a, b):
    return jnp.dot(a, b, preferred_element_type=F32)


def _dot_nt(a, b):
    return lax.dot_general(a, b, (((1,), (1,)), ((), ())), preferred_element_type=F32)


def _dot_exact_lhs(m_bf16, x):
    hi, mid, lo = _split3(x)
    return _dot(m_bf16, hi) + _dot(m_bf16, mid) + _dot(m_bf16, lo)


def _dot_hilo(a, b):
    ah = a.astype(BF16)
    al = (a - ah.astype(F32)).astype(BF16)
    bh = b.astype(BF16)
    bl = (b - bh.astype(F32)).astype(BF16)
    return _dot(ah, bh) + _dot(ah, bl) + _dot(al, bh) + _dot(al, bl)


def _layer_norm_rows(z, g, b):
    mu = jnp.mean(z, -1, keepdims=True)
    zc = z - mu
    var = jnp.mean(zc * zc, -1, keepdims=True)
    return zc * lax.rsqrt(var + LN_EPS) * g + b


def _params(*sem):
    return pltpu.CompilerParams(dimension_semantics=sem, vmem_limit_bytes=VMEM_LIMIT_BYTES)


def _ln_kernel(x_ref, g_ref, b_ref, o_ref):
    o_ref[...] = _layer_norm_rows(x_ref[...], g_ref[...], b_ref[...])


def _layer_norm_call(x2, g, b, tm=512):
    T, D = x2.shape
    return pl.pallas_call(
        _ln_kernel,
        grid=(T // tm,),
        in_specs=[pl.BlockSpec((tm, D), lambda i: (i, 0)),
                  pl.BlockSpec((1, D), lambda i: (0, 0)),
                  pl.BlockSpec((1, D), lambda i: (0, 0))],
        out_specs=pl.BlockSpec((tm, D), lambda i: (i, 0)),
        out_shape=jax.ShapeDtypeStruct((T, D), F32),
        compiler_params=_params("parallel"),
        name="embed_ln",
    )(x2, g.reshape(1, D), b.reshape(1, D))


def _inproj_kernel(x_ref, w_ref, b_ref, o_ref, xb_ref):
    @pl.when(pl.program_id(1) == 0)
    def _():
        xb_ref[...] = x_ref[...].astype(BF16)

    o_ref[...] = _dot(xb_ref[...], w_ref[...]) + b_ref[...]


def _inproj_call(x2, w_bf16, bias, tm=1024):
    T, D = x2.shape
    N = w_bf16.shape[1]
    tm = min(tm, T)
    tn = PROJ_TILE_N
    return pl.pallas_call(
        _inproj_kernel,
        grid=(T // tm, N // tn),
        in_specs=[pl.BlockSpec((tm, D), lambda i, j: (i, 0)),
                  pl.BlockSpec((D, tn), lambda i, j: (0, j)),
                  pl.BlockSpec((1, tn), lambda i, j: (0, j))],
        out_specs=pl.BlockSpec((tm, tn), lambda i, j: (i, j)),
        out_shape=jax.ShapeDtypeStruct((T, N), F32),
        scratch_shapes=[pltpu.VMEM((tm, D), BF16)],
        compiler_params=_params("parallel", "arbitrary"),
        name="in_proj",
    )(x2, w_bf16, bias.reshape(1, N))


HG_CHUNK = 128
HG_DIAG = SUBLANES


def _hgrn_constants(C):
    r = np.arange(C)
    tri = (r[None, :] <= r[:, None]).astype(np.float32)
    mats = [tri]
    masks = []
    w = HG_DIAG
    while 2 * w <= C:
        mid = (r // (2 * w)) * (2 * w) + w
        mats.append(tri - (r[None, :] <= mid[:, None]).astype(np.float32))
        same = (r[:, None] // (2 * w)) == (r[None, :] // (2 * w))
        masks.append((same & ((r[:, None] % (2 * w)) >= w) & ((r[None, :] % (2 * w)) < w)).astype(np.float32))
        w *= 2
    mats.append((r[None, :] > r[:, None]).astype(np.float32))
    return np.concatenate(mats, 0), np.stack(masks, 0)


def _hgrn_kernel(q_ref, f_ref, i_ref, g_ref, lb_ref, ng_ref, m_ref, mask_ref, o_ref, st_ref, *, C, nchunk):
    d = HG_HEAD_DIM
    nlev = mask_ref.shape[0]

    @pl.when(pl.program_id(2) == 0)
    def _():
        st_ref[...] = jnp.zeros_like(st_ref)

    lb = lb_ref[0]
    ng = ng_ref[0]
    nb = C // HG_DIAG
    row3 = lax.broadcasted_iota(jnp.int32, (nb, HG_DIAG, d), 1)

    fl_all = f_ref[0]
    lf_all = jnp.log(lb + (1.0 - lb) * jax.nn.sigmoid(fl_all))
    kk_all = (1.0 - lb) * jax.nn.sigmoid(-fl_all)
    allm_all = _dot_exact_lhs(m_ref[...], jnp.concatenate([lf_all[c * C:(c + 1) * C] for c in range(nchunk)], 1))

    for c in range(nchunk):
        sl = pl.ds(c * C, C)
        q = q_ref[0, sl, :]
        iv = i_ref[0, sl, :]
        g = g_ref[0, sl, :]
        kk = kk_all[c * C:(c + 1) * C]
        allm = allm_all[:, c * d:(c + 1) * d]
        bcs = allm[0:C]
        rem = allm[(nlev + 1) * C:(nlev + 2) * C]
        b_last = bcs[C - 1:C, :]
        iv_b = iv.astype(BF16)

        att = jnp.zeros((C, C), F32)
        for l in range(nlev):
            e = jnp.exp(-jnp.abs(allm[(1 + l) * C:(2 + l) * C]))
            att = att + mask_ref[l] * _dot_nt((q * e).astype(BF16), (kk * e).astype(BF16))
        o = _dot(att.astype(BF16), iv_b)

        q3 = q.reshape(nb, HG_DIAG, d)
        k3 = kk.reshape(nb, HG_DIAG, d)
        b3 = bcs.reshape(nb, HG_DIAG, d)
        i3 = iv.reshape(nb, HG_DIAG, d)
        acc = jnp.zeros((nb, HG_DIAG, d), F32)
        for s in range(HG_DIAG):
            dec = jnp.where(row3 >= s, jnp.exp(b3 - b3[:, s:s + 1, :]), 0.0)
            a = jnp.sum(q3 * dec * k3[:, s:s + 1, :], axis=-1, keepdims=True)
            acc = acc + a * i3[:, s:s + 1, :]
        o = o + acc.reshape(C, d)

        st = st_ref[...]
        o = o + _dot_nt((q * jnp.exp(bcs)).astype(BF16), st.astype(BF16))
        kdec = (kk * jnp.exp(rem)).astype(BF16)
        st_ref[...] = st * jnp.exp(b_last) + _dot(iv.T.astype(BF16), kdec)

        o = o * lax.rsqrt(jnp.mean(o * o, -1, keepdims=True) + 1e-6) * ng
        o_ref[0, sl, :] = (o * (g * jax.nn.sigmoid(g))).astype(o_ref.dtype)


def _hgrn_call(proj3, lb, norm_g, tt=512):
    B, S, _ = proj3.shape
    d = HG_HEAD_DIM
    C = HG_CHUNK
    tt = min(tt, S)
    mstack, masks = _hgrn_constants(C)
    nm = mstack.shape[0]
    col = lambda off: (lambda b, h, t: (b, t, off // d + h))
    blk = (1, tt, d)
    return pl.pallas_call(
        functools.partial(_hgrn_kernel, C=C, nchunk=tt // C),
        grid=(B, HG_HEADS, S // tt),
        in_specs=[pl.BlockSpec(blk, col(OFF_HQ)), pl.BlockSpec(blk, col(OFF_HF)),
                  pl.BlockSpec(blk, col(OFF_HI)), pl.BlockSpec(blk, col(OFF_HG)),
                  pl.BlockSpec((1, 1, d), lambda b, h, t: (h, 0, 0)),
                  pl.BlockSpec((1, 1, d), lambda b, h, t: (h, 0, 0)),
                  pl.BlockSpec((nm, C), lambda b, h, t: (0, 0)),
                  pl.BlockSpec(masks.shape, lambda b, h, t: (0, 0, 0))],
        out_specs=pl.BlockSpec(blk, lambda b, h, t: (b, t, h)),
        out_shape=jax.ShapeDtypeStruct((B, S, HG_WIDTH), BF16),
        scratch_shapes=[pltpu.VMEM((d, d), F32)],
        compiler_params=_params("parallel", "parallel", "arbitrary"),
        name="hgrn2",
    )(proj3, proj3, proj3, proj3, lb.reshape(HG_HEADS, 1, d), norm_g.reshape(HG_HEADS, 1, d),
      jnp.asarray(mstack, BF16), jnp.asarray(masks, F32))


LRU_PAD = SUBLANES
LRU_STEP = 16


def _lru_kernel(x_ref, y_ref, cw_ref, cb_ref, wa_ref, ba_ref, wx_ref, bx_ref, c_ref, o_ref,
                xpad_ref, h_ref, a_ref, u_ref, *, tt):
    W = LRU_WIDTH
    bw = W // LRU_BLOCKS

    @pl.when(pl.program_id(1) == 0)
    def _():
        xpad_ref[0:LRU_PAD, :] = jnp.zeros((LRU_PAD, W), F32)
        h_ref[...] = jnp.zeros_like(h_ref)

    x = x_ref[0]
    xpad_ref[LRU_PAD:LRU_PAD + tt, :] = x
    cw = cw_ref[...]
    xc = cb_ref[...] + cw[LRU_CONV - 1:LRU_CONV, :] * x
    for j in range(LRU_CONV - 1):
        off = LRU_PAD - (LRU_CONV - 1) + j
        xc = xc + cw[j:j + 1, :] * xpad_ref[off:off + tt, :]
    xpad_ref[0:LRU_PAD, :] = x[tt - LRU_PAD:tt, :]

    rs, is_ = [], []
    for gi in range(LRU_BLOCKS):
        xg = xc[:, gi * bw:(gi + 1) * bw].astype(BF16)
        rs.append(_dot(xg, wa_ref[gi]))
        is_.append(_dot(xg, wx_ref[gi]))
    r = jax.nn.sigmoid(jnp.concatenate(rs, -1) + ba_ref[...])
    ig = jax.nn.sigmoid(jnp.concatenate(is_, -1) + bx_ref[...])
    log_a = c_ref[...] * r
    a = jnp.exp(log_a)
    u = jnp.sqrt(-jnp.tanh(log_a) * (a * a + 1.0)) * (ig * xc)

    sub = lax.broadcasted_iota(jnp.int32, (tt, W), 0) % SUBLANES
    for dlt in (1, 2, 4):
        keep = sub >= dlt
        a_sh = jnp.where(keep, pltpu.roll(a, dlt, 0), 1.0)
        u_sh = jnp.where(keep, pltpu.roll(u, dlt, 0), 0.0)
        u = a * u_sh + u
        a = a * a_sh
    a_ref[...] = a
    u_ref[...] = u

    def step(i, h):
        r0 = pl.multiple_of(i * LRU_STEP, LRU_STEP)
        outs = []
        for k in range(LRU_STEP // SUBLANES):
            rows = pl.ds(r0 + k * SUBLANES, SUBLANES)
            hh = u_ref[rows, :] + a_ref[rows, :] * h
            outs.append(hh)
            h = hh[SUBLANES - 1:SUBLANES, :]
        rows = pl.ds(r0, LRU_STEP)
        o_ref[0, rows, :] = (jnp.concatenate(outs, 0) * jax.nn.gelu(y_ref[0, rows, :])).astype(o_ref.dtype)
        return h

    h_ref[...] = lax.fori_loop(0, tt // LRU_STEP, step, h_ref[...])


def _lru_call(proj3, conv_w, conv_b, wa, ba, wx, bx, lam, tt=512):
    B, S, _ = proj3.shape
    W = LRU_WIDTH
    bw = W // LRU_BLOCKS
    tt = min(tt, S)
    c = (-LRU_C * jax.nn.softplus(-lam.astype(F32))).reshape(1, W)
    vec = lambda: pl.BlockSpec((1, W), lambda b, t: (0, 0))
    return pl.pallas_call(
        functools.partial(_lru_kernel, tt=tt),
        grid=(B, S // tt),
        in_specs=[pl.BlockSpec((1, tt, W), lambda b, t: (b, t, OFF_LX // W)),
                  pl.BlockSpec((1, tt, W), lambda b, t: (b, t, OFF_LY // W)),
                  pl.BlockSpec((LRU_CONV, W), lambda b, t: (0, 0)), vec(),
                  pl.BlockSpec((LRU_BLOCKS, bw, bw), lambda b, t: (0, 0, 0)), vec(),
                  pl.BlockSpec((LRU_BLOCKS, bw, bw), lambda b, t: (0, 0, 0)), vec(), vec()],
        out_specs=pl.BlockSpec((1, tt, W), lambda b, t: (b, t, 0)),
        out_shape=jax.ShapeDtypeStruct((B, S, W), BF16),
        scratch_shapes=[pltpu.VMEM((tt + LRU_PAD, W), F32), pltpu.VMEM((1, W), F32),
                        pltpu.VMEM((tt, W), F32), pltpu.VMEM((tt, W), F32)],
        compiler_params=_params("parallel", "arbitrary"),
        name="rglru",
    )(proj3, proj3, conv_w.astype(F32), conv_b.reshape(1, W).astype(F32), wa.astype(BF16),
      ba.reshape(1, W), wx.astype(BF16), bx.reshape(1, W), c)


NSA_AUG = 4
NSA_KW = LANES
SEL_TILE = 256
SEL_TILE_BLOCKS = SEL_TILE // SEL_LEN
WIN_TILE = 128
POS_SPLIT = 128
SEL_ONEHOT = LANES
PLAN_HALF = 16
SELECT_PARTS = 4
KV_PREP_ROWS = 512


def _key_aug(pos, width, dead=False):
    n = pos.shape[0]
    col = lax.broadcasted_iota(jnp.int32, (n, width), 1)
    hi = ((pos // POS_SPLIT) * POS_SPLIT).astype(F32)
    lo = (pos % POS_SPLIT).astype(F32)
    live = jnp.where(col == 0, hi, jnp.where(col == 1, lo, jnp.where(col < NSA_AUG, 1.0, 0.0)))
    return jnp.where(dead, jnp.where(col == NSA_AUG, 1.0, 0.0), live)


def _query_t(q_ref, slope_ref, c0):
    hd, P, T = NSA_HEAD_DIM, NSA_PER_GROUP, Q_BLOCK
    PT = P * T
    qT = (q_ref[0] * (hd ** -0.5)).T
    qpart = jnp.concatenate([qT[p * hd:(p + 1) * hd, :] for p in range(P)], 1)
    tlane = c0 + lax.broadcasted_iota(jnp.int32, (1, PT), 1) % T
    slope = slope_ref[0]
    t_hi = ((tlane // POS_SPLIT) * POS_SPLIT).astype(F32)
    t_lo = (tlane % POS_SPLIT).astype(F32)
    rowi = lax.broadcasted_iota(jnp.int32, (NSA_KW - hd, PT), 0)
    aug = jnp.where(rowi < 2, slope,
                    jnp.where(rowi == 2, -slope * t_hi,
                              jnp.where(rowi == 3, -slope * t_lo, jnp.where(rowi == NSA_AUG, NEG_BIG, 0.0))))
    return jnp.concatenate([qpart, aug], 0).astype(BF16), tlane


def _compress_kernel(xk_ref, xv_ref, w1k_ref, pek_ref, w1fk_ref, w2k_ref, w1v_ref, pev_ref, w1fv_ref, w2v_ref,
                     kc_ref, vct_ref, *, nc):
    hd = NSA_HEAD_DIM
    pos = lax.broadcasted_iota(jnp.int32, (nc, 1), 0) * CMP_STRIDE + (CMP_LEN - 1)
    aug = _key_aug(pos, NSA_KW - hd)

    def one(x_ref, w1_ref, pe_ref, w1f_ref, w2_ref, g):
        uv = jnp.zeros((nc, 2 * CMP_HIDDEN), F32)
        for r in range(CMP_STRIDE):
            xr = x_ref[0, pl.ds(r, nc, stride=CMP_STRIDE), :][:, g * hd:(g + 1) * hd]
            uv = uv + _dot_hilo(xr, w1_ref[r])
        cvec = _dot_hilo(pe_ref[...], w1f_ref[...])[0:1, :]
        hid = uv[:, 0:CMP_HIDDEN] + pltpu.roll(uv[:, CMP_HIDDEN:], nc - 1, 0) + cvec
        return _dot_hilo(jax.nn.gelu(hid), w2_ref[...])

    for g in range(NSA_GROUPS):
        kc = one(xk_ref, w1k_ref, pek_ref, w1fk_ref, w2k_ref, g)
        kc_ref[0, g] = jnp.concatenate([kc, aug], 1).astype(BF16)
        vct_ref[0, g] = one(xv_ref, w1v_ref, pev_ref, w1fv_ref, w2v_ref, g).T.astype(BF16)


def _compress_call(proj3, pe_k, w1_k, w2_k, pe_v, w1_v, w2_v):
    B, S, _ = proj3.shape
    G, hd = NSA_GROUPS, NSA_HEAD_DIM
    nc = S // CMP_STRIDE
    half = CMP_STRIDE * hd

    def prep(pe, w1):
        w1 = w1.astype(F32)
        wr = jnp.concatenate([w1[:half].reshape(CMP_STRIDE, hd, CMP_HIDDEN),
                              w1[half:].reshape(CMP_STRIDE, hd, CMP_HIDDEN)], -1)
        pe8 = jnp.broadcast_to(pe.reshape(1, CMP_LEN * hd).astype(F32), (SUBLANES, CMP_LEN * hd))
        return wr, pe8, w1

    full = lambda a: pl.BlockSpec(a.shape, lambda b: tuple(0 for _ in a.shape))
    col = lambda off: pl.BlockSpec((1, S, LANES), lambda b: (b, 0, off // LANES))
    args = prep(pe_k, w1_k) + (w2_k.astype(F32),) + prep(pe_v, w1_v) + (w2_v.astype(F32),)
    return pl.pallas_call(
        functools.partial(_compress_kernel, nc=nc),
        grid=(B,),
        in_specs=[col(OFF_KC), col(OFF_VC)] + [full(a) for a in args],
        out_specs=[pl.BlockSpec((1, G, nc, NSA_KW), lambda b: (b, 0, 0, 0)),
                   pl.BlockSpec((1, G, hd, nc), lambda b: (b, 0, 0, 0))],
        out_shape=[jax.ShapeDtypeStruct((B, G, nc, NSA_KW), BF16),
                   jax.ShapeDtypeStruct((B, G, hd, nc), BF16)],
        compiler_params=_params("parallel"),
        name="nsa_compress",
    )(proj3, proj3, *args)


def _kvprep_kernel(ks_ref, vs_ref, kw_ref, vw_ref, ksa_ref, vst_ref, kwa_ref, vwt_ref, *, tt, n_live):
    hd = NSA_HEAD_DIM
    t = pl.program_id(1)
    dead = t >= n_live
    pos = t * tt + lax.broadcasted_iota(jnp.int32, (tt, 1), 0)
    aug = _key_aug(pos, NSA_KW - hd, dead)
    blk_col = lax.broadcasted_iota(jnp.int32, (tt, SEL_ONEHOT), 1)
    onehot = jnp.where(jnp.logical_and(blk_col == pos // SEL_LEN, jnp.logical_not(dead)), 1.0, 0.0)
    for g in range(NSA_GROUPS):
        lanes = slice(g * hd, (g + 1) * hd)
        live = lambda ref: jnp.where(dead, 0.0, ref[0][:, lanes])
        ksa_ref[0, g] = jnp.concatenate([live(ks_ref), aug, onehot], 1).astype(BF16)
        kwa_ref[0, g] = jnp.concatenate([live(kw_ref), aug], 1).astype(BF16)
        vs_t = live(vs_ref).T.astype(BF16)
        vw_t = live(vw_ref).T.astype(BF16)
        for c in range(tt // SEL_TILE):
            vst_ref[0, g, c] = vs_t[:, c * SEL_TILE:(c + 1) * SEL_TILE]
        for c in range(tt // WIN_TILE):
            vwt_ref[0, g, c] = vw_t[:, c * WIN_TILE:(c + 1) * WIN_TILE]


def _kvprep_call(proj3):
    B, S, _ = proj3.shape
    G, hd = NSA_GROUPS, NSA_HEAD_DIM
    tt = min(KV_PREP_ROWS, S)
    n_live = S // tt
    sp = S + tt
    col = lambda off: pl.BlockSpec((1, tt, LANES), lambda b, t: (b, jnp.minimum(t, n_live - 1), off // LANES))
    return pl.pallas_call(
        functools.partial(_kvprep_kernel, tt=tt, n_live=n_live),
        grid=(B, n_live + 1),
        in_specs=[col(OFF_KS), col(OFF_VS), col(OFF_KW), col(OFF_VW)],
        out_specs=[pl.BlockSpec((1, G, tt, NSA_KW + SEL_ONEHOT), lambda b, t: (b, 0, t, 0)),
                   pl.BlockSpec((1, G, tt // SEL_TILE, hd, SEL_TILE), lambda b, t: (b, 0, t, 0, 0)),
                   pl.BlockSpec((1, G, tt, NSA_KW), lambda b, t: (b, 0, t, 0)),
                   pl.BlockSpec((1, G, tt // WIN_TILE, hd, WIN_TILE), lambda b, t: (b, 0, t, 0, 0))],
        out_shape=[jax.ShapeDtypeStruct((B, G, sp, NSA_KW + SEL_ONEHOT), BF16),
                   jax.ShapeDtypeStruct((B, G, sp // SEL_TILE, hd, SEL_TILE), BF16),
                   jax.ShapeDtypeStruct((B, G, sp, NSA_KW), BF16),
                   jax.ShapeDtypeStruct((B, G, sp // WIN_TILE, hd, WIN_TILE), BF16)],
        compiler_params=_params("parallel", "parallel"),
        name="nsa_kvprep",
    )(proj3, proj3, proj3, proj3)


def _importance_matrix(ns, nc):
    ratio = SEL_LEN // CMP_STRIDE
    a = np.zeros((ns, nc), np.float32)
    for j in range(ns):
        for n, wgt in ((ratio * j - 1, 0.5), (ratio * j, 1.0), (ratio * j + 1, 1.0),
                       (ratio * j + 2, 1.0), (ratio * j + 3, 0.5)):
            if 0 <= n < nc - 1:
                a[j, n] = wgt
    return a


def _plan_constants(ns):
    nt = ns // SEL_TILE_BLOCKS
    grp = (np.arange(ns)[None, :] // SEL_TILE_BLOCKS == np.arange(nt)[:, None]).astype(np.float32)
    k = np.arange(nt)
    w_lo = np.where(k < PLAN_HALF, 2.0 ** np.minimum(k, PLAN_HALF - 1), 0.0)
    w_hi = np.where(k >= PLAN_HALF, 2.0 ** np.maximum(k - PLAN_HALF, 0), 0.0)
    wts = np.stack([np.repeat(w_lo[:, None], LANES, 1), np.repeat(w_hi[:, None], LANES, 1)]).astype(np.float32)
    return grp, wts


def _select_kernel(q_ref, slope_ref, kc_ref, vct_ref, imp_ref, grp_ref, wts_ref, ocmp_ref, bias_ref, bits_ref,
                   *, ns, nc):
    P, T = NSA_PER_GROUP, Q_BLOCK
    iq = pl.program_id(2)
    nq = pl.num_programs(2)
    c0 = iq * T
    qt, tlane = _query_t(q_ref, slope_ref, c0)

    def body(rows, nblk):
        s = _dot(kc_ref[0, 0, 0:rows, :], qt)
        cend = lax.broadcasted_iota(jnp.int32, (rows, 1), 0) * CMP_STRIDE + (CMP_LEN - 1)
        cmask = cend <= tlane
        s = jnp.where(cmask, s, NEG_BIG)
        m = jnp.max(s, 0, keepdims=True)
        p_c = jnp.where(cmask, jnp.exp(s - m), 0.0)
        p_c = p_c / jnp.maximum(jnp.sum(p_c, 0, keepdims=True), 1e-30)
        ocmp_ref[0, 0, 0] = _dot(vct_ref[0, 0][:, 0:rows], p_c.astype(BF16))

        psum = p_c[:, 0:T]
        for p in range(1, P):
            psum = psum + p_c[:, p * T:(p + 1) * T]
        imp = _dot_exact_lhs(imp_ref[0:nblk, 0:rows], psum)
        blk = lax.broadcasted_iota(jnp.int32, (nblk, T), 0)
        cur = (c0 + lax.broadcasted_iota(jnp.int32, (nblk, T), 1)) // SEL_LEN
        forced = (blk == 0) | (blk == cur) | (blk == cur - 1)
        val = jnp.where(forced, 3e38, jnp.where(blk > cur, -1.0, imp))
        chosen = jnp.zeros((nblk, T), F32)
        for _ in range(min(SEL_TOPK, nblk)):
            mx = jnp.max(val, 0, keepdims=True)
            first = jnp.min(jnp.where(val == mx, blk, ns), 0, keepdims=True)
            pick = blk == first
            chosen = jnp.where(pick, 1.0, chosen)
            val = jnp.where(pick, -2.0, val)
        bias_ref[0, 0, 0, 0:nblk, :] = jnp.where(chosen > 0.5, 0.0, NEG_BIG)
        if nblk < ns:
            bias_ref[0, 0, 0, nblk:ns, :] = jnp.full((ns - nblk, T), NEG_BIG, F32)

        if nblk < ns:
            chosen = jnp.concatenate([chosen, jnp.zeros((ns - nblk, T), F32)], 0)
        cnt = _dot(grp_ref[...], chosen.astype(BF16))
        act = jnp.where(jnp.max(cnt, 1, keepdims=True) > 0.5, 1.0, 0.0)
        lo = jnp.sum(act * wts_ref[0], 0, keepdims=True).astype(jnp.int32)
        hi = jnp.sum(act * wts_ref[1], 0, keepdims=True).astype(jnp.int32)
        bits_ref[0] = jnp.broadcast_to(lo | (hi << PLAN_HALF), (SUBLANES, LANES))

    parts = max(p for p in (SELECT_PARTS, 2, 1) if nc % (p * LANES) == 0 and ns % (p * 2 * SUBLANES) == 0)
    part = iq * parts // nq
    for v in range(parts):
        pl.when(part == v)(functools.partial(body, nc * (v + 1) // parts, ns * (v + 1) // parts))


def _attend_kernel(bits_sref, q_ref, gt_ref, slope_ref, ocmp_ref, bias_ref, ks_ref, vst_ref, kw_ref, vwt_ref,
                   o_ref, list_ref, *, n_tiles_total, ns):
    hd, P, T = NSA_HEAD_DIM, NSA_PER_GROUP, Q_BLOCK
    PT = P * T
    b, g, iq = pl.program_id(0), pl.program_id(1), pl.program_id(2)
    c0 = iq * T
    qt, tlane = _query_t(q_ref, slope_ref, c0)
    tl_row = lax.broadcasted_iota(jnp.int32, (1, PT), 1) % T

    gt = jax.nn.sigmoid(gt_ref[0]).T

    def gate_row(branch):
        return jnp.concatenate([gt[p * 3 + branch:p * 3 + branch + 1, :] for p in range(P)], 1)

    bits = bits_sref[(b * pl.num_programs(1) + g) * pl.num_programs(2) + iq]
    last_tile = (c0 + T - 1) // SEL_TILE

    def compact(k, n):
        active = (lax.shift_right_logical(bits, k) & 1) == 1
        take = jnp.logical_and(active, k < last_tile)

        @pl.when(take)
        def _():
            list_ref[n] = k

        return n + take.astype(jnp.int32)

    n_prev = lax.fori_loop(0, n_tiles_total, compact, 0)

    sel_b = bias_ref[0, 0, 0]
    if ns < SEL_ONEHOT:
        sel_b = jnp.concatenate([sel_b, jnp.zeros((SEL_ONEHOT - ns, T), F32)], 0)
    qt_sel = jnp.concatenate([qt, jnp.concatenate([sel_b.astype(BF16)] * P, 1)], 0)

    def tile_scores(kt):
        k0 = pl.multiple_of(kt * SEL_TILE, SEL_TILE)
        return _dot(ks_ref[0, 0, pl.ds(k0, SEL_TILE), :], qt_sel)

    def softmax_update(carry, scores, vts):
        m_run, l_run, acc = carry
        m_new = m_run
        for sc in scores:
            m_new = jnp.maximum(m_new, jnp.max(sc, 0, keepdims=True))
        alpha = jnp.exp(m_run - m_new)
        l_new = alpha * l_run
        acc = alpha * acc
        for sc, vt in zip(scores, vts):
            pr = jnp.exp(sc - m_new)
            l_new = l_new + jnp.sum(pr, 0, keepdims=True)
            acc = acc + _dot(vt, pr.astype(BF16))
        return m_new, l_new, acc

    init = (jnp.full((1, PT), NEG_BIG, F32), jnp.zeros((1, PT), F32), jnp.zeros((hd, PT), F32))

    def pair_step(i, carry):
        ka = list_ref[2 * i]
        has_b = 2 * i + 1 < n_prev
        kb = jnp.where(has_b, list_ref[jnp.where(has_b, 2 * i + 1, 2 * i)], n_tiles_total)
        return softmax_update(carry, [tile_scores(ka), tile_scores(kb)], [vst_ref[0, 0, ka], vst_ref[0, 0, kb]])

    carry = lax.fori_loop(0, (n_prev + 1) // 2, pair_step, init)
    kpos = last_tile * SEL_TILE + lax.broadcasted_iota(jnp.int32, (SEL_TILE, 1), 0)
    s_last = jnp.where(kpos <= tlane, tile_scores(last_tile), NEG_BIG)
    _, l_sel, acc_sel = softmax_update(carry, [s_last], [vst_ref[0, 0, last_tile]])
    out = gate_row(0) * ocmp_ref[0, 0, 0] + gate_row(1) * (acc_sel / l_sel)

    n_win = (WINDOW + T) // WIN_TILE
    roff = lax.broadcasted_iota(jnp.int32, (WIN_TILE, 1), 0)
    scores, vts = [], []
    for j in range(n_win):
        start = c0 - WINDOW + j * WIN_TILE
        k0 = pl.multiple_of(jnp.where(start >= 0, start, n_tiles_total * SEL_TILE), WIN_TILE)
        sc = _dot(kw_ref[0, 0, pl.ds(k0, WIN_TILE), :], qt)
        if j == 0:
            sc = jnp.where(roff > tl_row, sc, NEG_BIG)
        if j == n_win - 1:
            sc = jnp.where(roff <= tl_row, sc, NEG_BIG)
        scores.append(sc)
        vts.append(vwt_ref[0, 0, k0 // WIN_TILE])
    _, l_win, acc_win = softmax_update(init, scores, vts)
    out = out + gate_row(2) * (acc_win / l_win)

    o_ref[0] = jnp.concatenate([out[:, p * T:(p + 1) * T] for p in range(P)], 0).T.astype(o_ref.dtype)


def _nsa_call(proj3, pe_k, w1_k, w2_k, pe_v, w1_v, w2_v):
    B, S, _ = proj3.shape
    G, hd, P, T = NSA_GROUPS, NSA_HEAD_DIM, NSA_PER_GROUP, Q_BLOCK
    PT = P * T
    nc = S // CMP_STRIDE
    ns = S // SEL_LEN
    nq = S // T
    nt = ns // SEL_TILE_BLOCKS
    assert nt <= 2 * PLAN_HALF and S % KV_PREP_ROWS == 0

    kc_a, vc_t = _compress_call(proj3, pe_k, w1_k, w2_k, pe_v, w1_v, w2_v)
    ks_a, vs_t, kw_a, vw_t = _kvprep_call(proj3)
    imp_m = jnp.asarray(_importance_matrix(ns, nc), BF16)
    grp, wts = _plan_constants(ns)
    head_slopes = 2.0 ** (-(8.0 / NSA_HEADS) * np.arange(1, NSA_HEADS + 1, dtype=np.float32))
    slopes = jnp.asarray(np.repeat(head_slopes.reshape(G, 1, P), T, axis=-1), F32)

    q_spec = lambda im: pl.BlockSpec((1, T, P * hd), im)
    bg = lambda shp: pl.BlockSpec((1, 1) + shp, lambda b, g, i: (b, g) + tuple(0 for _ in shp))
    const = lambda a: pl.BlockSpec(a.shape, lambda b, g, i: tuple(0 for _ in a.shape))
    ocmp, sel_bias, bits = pl.pallas_call(
        functools.partial(_select_kernel, ns=ns, nc=nc),
        grid=(B, G, nq),
        in_specs=[q_spec(lambda b, g, i: (b, i, OFF_NQ // (P * hd) + g)),
                  pl.BlockSpec((1, 1, PT), lambda b, g, i: (g, 0, 0)),
                  bg((nc, NSA_KW)), bg((hd, nc)), const(imp_m),
                  pl.BlockSpec(grp.shape, lambda b, g, i: (0, 0)),
                  pl.BlockSpec(wts.shape, lambda b, g, i: (0, 0, 0))],
        out_specs=[pl.BlockSpec((1, 1, 1, hd, PT), lambda b, g, i: (b, g, i, 0, 0)),
                   pl.BlockSpec((1, 1, 1, ns, T), lambda b, g, i: (b, g, i, 0, 0)),
                   pl.BlockSpec((1, SUBLANES, LANES), lambda b, g, i: ((b * G + g) * nq + i, 0, 0))],
        out_shape=[jax.ShapeDtypeStruct((B, G, nq, hd, PT), F32),
                   jax.ShapeDtypeStruct((B, G, nq, ns, T), F32),
                   jax.ShapeDtypeStruct((B * G * nq, SUBLANES, LANES), jnp.int32)],
        compiler_params=_params("parallel", "parallel", "parallel"),
        name="nsa_select",
    )(proj3, slopes, kc_a, vc_t, imp_m, jnp.asarray(grp, BF16), jnp.asarray(wts, F32))

    bg2 = lambda shp: pl.BlockSpec((1, 1) + shp, lambda b, g, i, s: (b, g) + tuple(0 for _ in shp))
    return pl.pallas_call(
        functools.partial(_attend_kernel, n_tiles_total=nt, ns=ns),
        grid_spec=pltpu.PrefetchScalarGridSpec(
            num_scalar_prefetch=1,
            grid=(B, G, nq),
            in_specs=[pl.BlockSpec((1, T, P * hd), lambda b, g, i, s: (b, i, OFF_NQ // (P * hd) + g)),
                      pl.BlockSpec((1, T, LANES), lambda b, g, i, s: (b, i, OFF_NG // LANES + g)),
                      pl.BlockSpec((1, 1, PT), lambda b, g, i, s: (g, 0, 0)),
                      pl.BlockSpec((1, 1, 1, hd, PT), lambda b, g, i, s: (b, g, i, 0, 0)),
                      pl.BlockSpec((1, 1, 1, ns, T), lambda b, g, i, s: (b, g, i, 0, 0)),
                      bg2(ks_a.shape[2:]), bg2(vs_t.shape[2:]), bg2(kw_a.shape[2:]), bg2(vw_t.shape[2:])],
            out_specs=pl.BlockSpec((1, T, P * hd), lambda b, g, i, s: (b, i, g)),
            scratch_shapes=[pltpu.SMEM((nt,), jnp.int32)]),
        out_shape=jax.ShapeDtypeStruct((B, S, NSA_WIDTH), BF16),
        compiler_params=_params("parallel", "parallel", "arbitrary"),
        name="nsa_attend",
    )(bits[:, 0, 0], proj3, proj3, slopes, ocmp, sel_bias, ks_a, vs_t, kw_a, vw_t)


def _merge_kernel(x_ref, ya_ref, yb_ref, yc_ref, g0_ref, g1_ref, g2_ref, wa_ref, wb_ref, wc_ref, wo_ref,
                  lg_ref, lb_ref, o_ref):
    m = jax.nn.sigmoid(g0_ref[...]) * _dot(ya_ref[...], wa_ref[...])
    m = m + jax.nn.sigmoid(g1_ref[...]) * _dot(yb_ref[...], wb_ref[...])
    m = m + jax.nn.sigmoid(g2_ref[...]) * _dot(yc_ref[...], wc_ref[...])
    z = DEEPNORM_ALPHA * x_ref[...] + _dot(m.astype(BF16), wo_ref[...])
    o_ref[...] = _layer_norm_rows(z, lg_ref[...], lb_ref[...])


def _merge_call(x2, ya, yb, yc, proj2, w_hg, w_nsa, w_lru, w_out, ln_g, ln_b, tm=512):
    T, D = x2.shape
    tm = min(tm, T)
    rows = lambda w: pl.BlockSpec((tm, w), lambda i: (i, 0))
    gate = lambda n: pl.BlockSpec((tm, D), lambda i: (i, OFF_MG // D + n))
    full = lambda a: pl.BlockSpec(a.shape, lambda i: (0, 0))
    ws = [w.astype(BF16) for w in (w_hg, w_nsa, w_lru, w_out)]
    return pl.pallas_call(
        _merge_kernel,
        grid=(T // tm,),
        in_specs=[rows(D), rows(HG_WIDTH), rows(NSA_WIDTH), rows(LRU_WIDTH), gate(0), gate(1), gate(2)]
                 + [full(w) for w in ws] + [pl.BlockSpec((1, D), lambda i: (0, 0))] * 2,
        out_specs=rows(D),
        out_shape=jax.ShapeDtypeStruct((T, D), F32),
        compiler_params=_params("parallel"),
        name="merge_out",
    )(x2, ya, yb, yc, proj2, proj2, proj2, *ws, ln_g.reshape(1, D), ln_b.reshape(1, D))


FFN_SPLIT = 2
FFN_PAD = SUBLANES
FFN_SUB = 256


def _ffn_kernel(x_ref, wu_ref, wv_ref, cw_ref, cb_ref, wd_ref, lg_ref, lb_ref, o_ref,
                xb_ref, acc_ref, upad_ref, carry_ref, *, tm, tiles_per_seq):
    i = pl.program_id(0)
    j = pl.program_id(1)

    @pl.when(j == 0)
    def _():
        xb_ref[...] = x_ref[...].astype(BF16)

    xb = xb_ref[...]
    fc = wu_ref.shape[1]
    first = (i % tiles_per_seq) == 0

    @pl.when(first)
    def _():
        upad_ref[0:FFN_PAD, :] = jnp.zeros((FFN_PAD, fc), F32)

    @pl.when(jnp.logical_not(first))
    def _():
        upad_ref[0:FFN_PAD, :] = carry_ref[j]

    @pl.when(j == 0)
    def _():
        acc_ref[...] = jnp.zeros_like(acc_ref)

    for c0 in range(0, fc, FFN_SUB):
        cs = slice(c0, min(c0 + FFN_SUB, fc))
        u = _dot(xb, wu_ref[:, cs])
        v = _dot(xb, wv_ref[:, cs])
        upad_ref[FFN_PAD:FFN_PAD + tm, cs] = u
        cw = cw_ref[:, cs]
        cv = cb_ref[:, cs] + cw[FFN_CONV - 1:FFN_CONV, :] * u
        for k in range(FFN_CONV - 1):
            off = FFN_PAD - (FFN_CONV - 1) + k
            cv = cv + cw[k:k + 1, :] * upad_ref[off:off + tm, cs]
        carry_ref[j, :, cs] = u[tm - FFN_PAD:tm, :]
        h = (jax.nn.gelu(cv) * v).astype(BF16)
        acc_ref[...] += _dot(h, wd_ref[cs, :])

    @pl.when(j == FFN_SPLIT - 1)
    def _():
        z = DEEPNORM_ALPHA * x_ref[...] + acc_ref[...]
        o_ref[...] = _layer_norm_rows(z, lg_ref[...], lb_ref[...])


def _ffn_call(x2, seq_len, w_up, conv_w, conv_b, w_down, ln_g, ln_b, tm=512):
    T, D = x2.shape
    tm = min(tm, seq_len)
    fc = FFN_DIM // FFN_SPLIT
    wu = w_up.astype(BF16)
    return pl.pallas_call(
        functools.partial(_ffn_kernel, tm=tm, tiles_per_seq=seq_len // tm),
        grid=(T // tm, FFN_SPLIT),
        in_specs=[pl.BlockSpec((tm, D), lambda i, j: (i, 0)),
                  pl.BlockSpec((D, fc), lambda i, j: (0, j)),
                  pl.BlockSpec((D, fc), lambda i, j: (0, FFN_SPLIT + j)),
                  pl.BlockSpec((FFN_CONV, fc), lambda i, j: (0, j)),
                  pl.BlockSpec((1, fc), lambda i, j: (0, j)),
                  pl.BlockSpec((fc, D), lambda i, j: (j, 0)),
                  pl.BlockSpec((1, D), lambda i, j: (0, 0)),
                  pl.BlockSpec((1, D), lambda i, j: (0, 0))],
        out_specs=pl.BlockSpec((tm, D), lambda i, j: (i, 0)),
        out_shape=jax.ShapeDtypeStruct((T, D), F32),
        scratch_shapes=[pltpu.VMEM((tm, D), BF16), pltpu.VMEM((tm, D), F32),
                        pltpu.VMEM((tm + FFN_PAD, fc), F32), pltpu.VMEM((FFN_SPLIT, FFN_PAD, fc), F32)],
        compiler_params=_params("arbitrary", "arbitrary"),
        name="conv_ffn",
    )(x2, wu, wu, conv_w.astype(F32), conv_b.reshape(1, FFN_DIM).astype(F32), w_down.astype(BF16),
      ln_g.reshape(1, D), ln_b.reshape(1, D))


def _permute_in_proj(w, b):
    src = np.cumsum([0, HG_WIDTH, HG_WIDTH, HG_WIDTH, HG_WIDTH, NSA_WIDTH, NSA_KV, NSA_KV, NSA_KV, NSA_KV,
                     NSA_KV, NSA_KV, NSA_HEADS * 3, LRU_WIDTH, LRU_WIDTH, N_BRANCH * 1024])
    names = ["hq", "hf", "hi", "hg", "nq", "kc", "vc", "ks", "vs", "kw", "vw", "ng", "lx", "ly", "mg"]
    seg = {n: (int(src[k]), int(src[k + 1])) for k, n in enumerate(names)}
    order = ["mg", "hq", "hf", "hi", "hg", "nq", "lx", "ly", "kc", "vc", "ks", "vs", "kw", "vw"]
    per_group = NSA_PER_GROUP * 3
    ng0 = seg["ng"][0]
    w_parts = [w[:, seg[n][0]:seg[n][1]] for n in order]
    b_parts = [b[seg[n][0]:seg[n][1]] for n in order]
    for gi in range(NSA_GROUPS):
        lo = ng0 + gi * per_group
        w_parts += [w[:, lo:lo + per_group], jnp.zeros((w.shape[0], LANES - per_group), w.dtype)]
        b_parts += [b[lo:lo + per_group], jnp.zeros((LANES - per_group,), b.dtype)]
    wp = jnp.concatenate(w_parts, 1)
    bp = jnp.concatenate(b_parts)
    assert wp.shape[1] == PROJ_COLS
    return wp.astype(BF16), bp.astype(F32)


def kernel(x, ln_emb_g, ln_emb_b, w_in, b_in, hg_lb_logits, hg_norm_g, cmp_pe_k, cmp_w1_k, cmp_w2_k, cmp_pe_v, cmp_w1_v, cmp_w2_v, lru_conv_w, lru_conv_b, lru_wa, lru_ba, lru_wx, lru_bx, lru_lambda, w_branch_hg, w_branch_nsa, w_branch_lru, w_out, ln1_g, ln1_b, ffn_w_up, ffn_conv_w, ffn_conv_b, ffn_w_down, ln2_g, ln2_b):
    B, S, D = x.shape
    T = B * S
    gam = jax.nn.softmax(hg_lb_logits.astype(F32), axis=0)
    lb_all = jnp.cumsum(gam, axis=0) - gam[0]
    h = _layer_norm_call(x.reshape(T, D), ln_emb_g, ln_emb_b)
    for l in range(DEPTH):
        wp, bp = _permute_in_proj(w_in[l], b_in[l])
        proj2 = _inproj_call(h, wp, bp)
        proj3 = proj2.reshape(B, S, PROJ_COLS)
        y_a = _hgrn_call(proj3, lb_all[l], hg_norm_g[l])
        y_b = _nsa_call(proj3, cmp_pe_k[l], cmp_w1_k[l], cmp_w2_k[l], cmp_pe_v[l], cmp_w1_v[l], cmp_w2_v[l])
        y_c = _lru_call(proj3, lru_conv_w[l], lru_conv_b[l], lru_wa[l], lru_ba[l], lru_wx[l], lru_bx[l],
                        lru_lambda[l])
        h = _merge_call(h, y_a.reshape(T, HG_WIDTH), y_b.reshape(T, NSA_WIDTH), y_c.reshape(T, LRU_WIDTH),
                        proj2, w_branch_hg[l], w_branch_nsa[l], w_branch_lru[l], w_out[l], ln1_g[l], ln1_b[l])
        h = _ffn_call(h, S, ffn_w_up[l], ffn_conv_w[l], ffn_conv_b[l], ffn_w_down[l], ln2_g[l], ln2_b[l])
    return h.reshape(B, S, D)
```

```python
import functools

import numpy as np
import jax
import jax.numpy as jnp
from jax import lax
from jax.experimental import pallas as pl
from jax.experimental.pallas import tpu as pltpu

F32 = jnp.float32
BF16 = jnp.bfloat16

DEPTH = 2
HG_HEADS = 4
HG_HEAD_DIM = 128
HG_WIDTH = HG_HEADS * HG_HEAD_DIM
NSA_HEADS = 8
NSA_GROUPS = 2
NSA_PER_GROUP = NSA_HEADS // NSA_GROUPS
NSA_HEAD_DIM = 64
NSA_WIDTH = NSA_HEADS * NSA_HEAD_DIM
NSA_KV = NSA_GROUPS * NSA_HEAD_DIM
CMP_LEN = 32
CMP_STRIDE = 16
CMP_HIDDEN = 64
SEL_LEN = 64
SEL_TOPK = 16
WINDOW = 512
Q_BLOCK = 128
LRU_WIDTH = 512
LRU_BLOCKS = 4
LRU_CONV = 4
LRU_C = 8.0
FFN_DIM = 2816
FFN_CONV = 3
DEEPNORM_ALPHA = (2 * DEPTH) ** 0.25
LN_EPS = 1e-5
N_BRANCH = 3

LANES = 128
SUBLANES = 8
VMEM_LIMIT_BYTES = 52 * 1024 * 1024

OFF_MG = 0
OFF_HQ = N_BRANCH * 1024
OFF_HF = OFF_HQ + HG_WIDTH
OFF_HI = OFF_HF + HG_WIDTH
OFF_HG = OFF_HI + HG_WIDTH
OFF_NQ = OFF_HG + HG_WIDTH
OFF_LX = OFF_NQ + NSA_WIDTH
OFF_LY = OFF_LX + LRU_WIDTH
OFF_KC = OFF_LY + LRU_WIDTH
OFF_VC = OFF_KC + NSA_KV
OFF_KS = OFF_VC + NSA_KV
OFF_VS = OFF_KS + NSA_KV
OFF_KW = OFF_VS + NSA_KV
OFF_VW = OFF_KW + NSA_KV
OFF_NG = OFF_VW + NSA_KV
PROJ_TILE_N = 1536
PROJ_COLS = 5 * PROJ_TILE_N

NEG_BIG = -1e30


def _split3(x):
    hi = x.astype(BF16)
    r1 = x - hi.astype(F32)
    mid = r1.astype(BF16)
    lo = (r1 - mid.astype(F32)).astype(BF16)
    return hi, mid, lo


def _dot(a, b):
    return jnp.dot(a, b, preferred_element_type=F32)


def _dot_nt(a, b):
    return lax.dot_general(a, b, (((1,), (1,)), ((), ())), preferred_element_type=F32)


def _dot_exact_lhs(m_bf16, x):
    hi, mid, lo = _split3(x)
    return _dot(m_bf16, hi) + _dot(m_bf16, mid) + _dot(m_bf16, lo)


def _dot_hilo(a, b):
    ah = a.astype(BF16)
    al = (a - ah.astype(F32)).astype(BF16)
    bh = b.astype(BF16)
    bl = (b - bh.astype(F32)).astype(BF16)
    return _dot(ah, bh) + _dot(ah, bl) + _dot(al, bh) + _dot(al, bl)


def _layer_norm_rows(z, g, b):
    mu = jnp.mean(z, -1, keepdims=True)
    zc = z - mu
    var = jnp.mean(zc * zc, -1, keepdims=True)
    return zc * lax.rsqrt(var + LN_EPS) * g + b


def _params(*sem):
    return pltpu.CompilerParams(dimension_semantics=sem, vmem_limit_bytes=VMEM_LIMIT_BYTES)


def _ln_kernel(x_ref, g_ref, b_ref, o_ref):
    o_ref[...] = _layer_norm_rows(x_ref[...], g_ref[...], b_ref[...])


def _layer_norm_call(x2, g, b, tm=512):
    T, D = x2.shape
    return pl.pallas_call(
        _ln_kernel,
        grid=(T // tm,),
        in_specs=[pl.BlockSpec((tm, D), lambda i: (i, 0)),
                  pl.BlockSpec((1, D), lambda i: (0, 0)),
                  pl.BlockSpec((1, D), lambda i: (0, 0))],
        out_specs=pl.BlockSpec((tm, D), lambda i: (i, 0)),
        out_shape=jax.ShapeDtypeStruct((T, D), F32),
        compiler_params=_params("parallel"),
        name="embed_ln",
    )(x2, g.reshape(1, D), b.reshape(1, D))


def _inproj_kernel(x_ref, w_ref, b_ref, o_ref, xb_ref):
    @pl.when(pl.program_id(1) == 0)
    def _():
        xb_ref[...] = x_ref[...].astype(BF16)

    o_ref[...] = _dot(xb_ref[...], w_ref[...]) + b_ref[...]


def _inproj_call(x2, w_bf16, bias, tm=1024):
    T, D = x2.shape
    N = w_bf16.shape[1]
    tm = min(tm, T)
    tn = PROJ_TILE_N
    return pl.pallas_call(
        _inproj_kernel,
        grid=(T // tm, N // tn),
        in_specs=[pl.BlockSpec((tm, D), lambda i, j: (i, 0)),
                  pl.BlockSpec((D, tn), lambda i, j: (0, j)),
                  pl.BlockSpec((1, tn), lambda i, j: (0, j))],
        out_specs=pl.BlockSpec((tm, tn), lambda i, j: (i, j)),
        out_shape=jax.ShapeDtypeStruct((T, N), F32),
        scratch_shapes=[pltpu.VMEM((tm, D), BF16)],
        compiler_params=_params("parallel", "arbitrary"),
        name="in_proj",
    )(x2, w_bf16, bias.reshape(1, N))


HG_CHUNK = 128
HG_DIAG = SUBLANES


def _hgrn_constants(C):
    r = np.arange(C)
    tri = (r[None, :] <= r[:, None]).astype(np.float32)
    mats = [tri]
    masks = []
    w = HG_DIAG
    while 2 * w <= C:
        mid = (r // (2 * w)) * (2 * w) + w
        mats.append(tri - (r[None, :] <= mid[:, None]).astype(np.float32))
        same = (r[:, None] // (2 * w)) == (r[None, :] // (2 * w))
        masks.append((same & ((r[:, None] % (2 * w)) >= w) & ((r[None, :] % (2 * w)) < w)).astype(np.float32))
        w *= 2
    mats.append((r[None, :] > r[:, None]).astype(np.float32))
    return np.concatenate(mats, 0), np.stack(masks, 0)


def _hgrn_kernel(q_ref, f_ref, i_ref, g_ref, lb_ref, ng_ref, m_ref, mask_ref, o_ref, st_ref, *, C, nchunk):
    d = HG_HEAD_DIM
    nlev = mask_ref.shape[0]

    @pl.when(pl.program_id(2) == 0)
    def _():
        st_ref[...] = jnp.zeros_like(st_ref)

    lb = lb_ref[0]
    ng = ng_ref[0]
    nb = C // HG_DIAG
    row3 = lax.broadcasted_iota(jnp.int32, (nb, HG_DIAG, d), 1)

    fl_all = f_ref[0]
    lf_all = jnp.log(lb + (1.0 - lb) * jax.nn.sigmoid(fl_all))
    kk_all = (1.0 - lb) * jax.nn.sigmoid(-fl_all)
    allm_all = _dot_exact_lhs(m_ref[...], jnp.concatenate([lf_all[c * C:(c + 1) * C] for c in range(nchunk)], 1))

    for c in range(nchunk):
        sl = pl.ds(c * C, C)
        q = q_ref[0, sl, :]
        iv = i_ref[0, sl, :]
        g = g_ref[0, sl, :]
        kk = kk_all[c * C:(c + 1) * C]
        allm = allm_all[:, c * d:(c + 1) * d]
        bcs = allm[0:C]
        rem = allm[(nlev + 1) * C:(nlev + 2) * C]
        b_last = bcs[C - 1:C, :]
        iv_b = iv.astype(BF16)

        att = jnp.zeros((C, C), F32)
        for l in range(nlev):
            e = jnp.exp(-jnp.abs(allm[(1 + l) * C:(2 + l) * C]))
            att = att + mask_ref[l] * _dot_nt((q * e).astype(BF16), (kk * e).astype(BF16))
        o = _dot(att.astype(BF16), iv_b)

        q3 = q.reshape(nb, HG_DIAG, d)
        k3 = kk.reshape(nb, HG_DIAG, d)
        b3 = bcs.reshape(nb, HG_DIAG, d)
        i3 = iv.reshape(nb, HG_DIAG, d)
        acc = jnp.zeros((nb, HG_DIAG, d), F32)
        for s in range(HG_DIAG):
            dec = jnp.where(row3 >= s, jnp.exp(b3 - b3[:, s:s + 1, :]), 0.0)
            a = jnp.sum(q3 * dec * k3[:, s:s + 1, :], axis=-1, keepdims=True)
            acc = acc + a * i3[:, s:s + 1, :]
        o = o + acc.reshape(C, d)

        st = st_ref[...]
        o = o + _dot_nt((q * jnp.exp(bcs)).astype(BF16), st.astype(BF16))
        kdec = (kk * jnp.exp(rem)).astype(BF16)
        st_ref[...] = st * jnp.exp(b_last) + _dot(iv.T.astype(BF16), kdec)

        o = o * lax.rsqrt(jnp.mean(o * o, -1, keepdims=True) + 1e-6) * ng
        o_ref[0, sl, :] = (o * (g * jax.nn.sigmoid(g))).astype(o_ref.dtype)


def _hgrn_call(proj3, lb, norm_g, tt=512):
    B, S, _ = proj3.shape
    d = HG_HEAD_DIM
    C = HG_CHUNK
    tt = min(tt, S)
    mstack, masks = _hgrn_constants(C)
    nm = mstack.shape[0]
    col = lambda off: (lambda b, h, t: (b, t, off // d + h))
    blk = (1, tt, d)
    return pl.pallas_call(
        functools.partial(_hgrn_kernel, C=C, nchunk=tt // C),
        grid=(B, HG_HEADS, S // tt),
        in_specs=[pl.BlockSpec(blk, col(OFF_HQ)), pl.BlockSpec(blk, col(OFF_HF)),
                  pl.BlockSpec(blk, col(OFF_HI)), pl.BlockSpec(blk, col(OFF_HG)),
                  pl.BlockSpec((1, 1, d), lambda b, h, t: (h, 0, 0)),
                  pl.BlockSpec((1, 1, d), lambda b, h, t: (h, 0, 0)),
                  pl.BlockSpec((nm, C), lambda b, h, t: (0, 0)),
                  pl.BlockSpec(masks.shape, lambda b, h, t: (0, 0, 0))],
        out_specs=pl.BlockSpec(blk, lambda b, h, t: (b, t, h)),
        out_shape=jax.ShapeDtypeStruct((B, S, HG_WIDTH), BF16),
        scratch_shapes=[pltpu.VMEM((d, d), F32)],
        compiler_params=_params("parallel", "parallel", "arbitrary"),
        name="hgrn2",
    )(proj3, proj3, proj3, proj3, lb.reshape(HG_HEADS, 1, d), norm_g.reshape(HG_HEADS, 1, d),
      jnp.asarray(mstack, BF16), jnp.asarray(masks, F32))


LRU_PAD = SUBLANES
LRU_STEP = 16


def _lru_kernel(x_ref, y_ref, cw_ref, cb_ref, wa_ref, ba_ref, wx_ref, bx_ref, c_ref, o_ref,
                xpad_ref, h_ref, a_ref, u_ref, *, tt):
    W = LRU_WIDTH
    bw = W // LRU_BLOCKS

    @pl.when(pl.program_id(1) == 0)
    def _():
        xpad_ref[0:LRU_PAD, :] = jnp.zeros((LRU_PAD, W), F32)
        h_ref[...] = jnp.zeros_like(h_ref)

    x = x_ref[0]
    xpad_ref[LRU_PAD:LRU_PAD + tt, :] = x
    cw = cw_ref[...]
    xc = cb_ref[...] + cw[LRU_CONV - 1:LRU_CONV, :] * x
    for j in range(LRU_CONV - 1):
        off = LRU_PAD - (LRU_CONV - 1) + j
        xc = xc + cw[j:j + 1, :] * xpad_ref[off:off + tt, :]
    xpad_ref[0:LRU_PAD, :] = x[tt - LRU_PAD:tt, :]

    rs, is_ = [], []
    for gi in range(LRU_BLOCKS):
        xg = xc[:, gi * bw:(gi + 1) * bw].astype(BF16)
        rs.append(_dot(xg, wa_ref[gi]))
        is_.append(_dot(xg, wx_ref[gi]))
    r = jax.nn.sigmoid(jnp.concatenate(rs, -1) + ba_ref[...])
    ig = jax.nn.sigmoid(jnp.concatenate(is_, -1) + bx_ref[...])
    log_a = c_ref[...] * r
    a = jnp.exp(log_a)
    u = jnp.sqrt(-jnp.tanh(log_a) * (a * a + 1.0)) * (ig * xc)

    sub = lax.broadcasted_iota(jnp.int32, (tt, W), 0) % SUBLANES
    for dlt in (1, 2, 4):
        keep = sub >= dlt
        a_sh = jnp.where(keep, pltpu.roll(a, dlt, 0), 1.0)
        u_sh = jnp.where(keep, pltpu.roll(u, dlt, 0), 0.0)
        u = a * u_sh + u
        a = a * a_sh
    a_ref[...] = a
    u_ref[...] = u

    def step(i, h):
        r0 = pl.multiple_of(i * LRU_STEP, LRU_STEP)
        outs = []
        for k in range(LRU_STEP // SUBLANES):
            rows = pl.ds(r0 + k * SUBLANES, SUBLANES)
            hh = u_ref[rows, :] + a_ref[rows, :] * h
            outs.append(hh)
            h = hh[SUBLANES - 1:SUBLANES, :]
        rows = pl.ds(r0, LRU_STEP)
        o_ref[0, rows, :] = (jnp.concatenate(outs, 0) * jax.nn.gelu(y_ref[0, rows, :])).astype(o_ref.dtype)
        return h

    h_ref[...] = lax.fori_loop(0, tt // LRU_STEP, step, h_ref[...])


def _lru_call(proj3, conv_w, conv_b, wa, ba, wx, bx, lam, tt=512):
    B, S, _ = proj3.shape
    W = LRU_WIDTH
    bw = W // LRU_BLOCKS
    tt = min(tt, S)
    c = (-LRU_C * jax.nn.softplus(-lam.astype(F32))).reshape(1, W)
    vec = lambda: pl.BlockSpec((1, W), lambda b, t: (0, 0))
    return pl.pallas_call(
        functools.partial(_lru_kernel, tt=tt),
        grid=(B, S // tt),
        in_specs=[pl.BlockSpec((1, tt, W), lambda b, t: (b, t, OFF_LX // W)),
                  pl.BlockSpec((1, tt, W), lambda b, t: (b, t, OFF_LY // W)),
                  pl.BlockSpec((LRU_CONV, W), lambda b, t: (0, 0)), vec(),
                  pl.BlockSpec((LRU_BLOCKS, bw, bw), lambda b, t: (0, 0, 0)), vec(),
                  pl.BlockSpec((LRU_BLOCKS, bw, bw), lambda b, t: (0, 0, 0)), vec(), vec()],
        out_specs=pl.BlockSpec((1, tt, W), lambda b, t: (b, t, 0)),
        out_shape=jax.ShapeDtypeStruct((B, S, W), BF16),
        scratch_shapes=[pltpu.VMEM((tt + LRU_PAD, W), F32), pltpu.VMEM((1, W), F32),
                        pltpu.VMEM((tt, W), F32), pltpu.VMEM((tt, W), F32)],
        compiler_params=_params("parallel", "arbitrary"),
        name="rglru",
    )(proj3, proj3, conv_w.astype(F32), conv_b.reshape(1, W).astype(F32), wa.astype(BF16),
      ba.reshape(1, W), wx.astype(BF16), bx.reshape(1, W), c)


NSA_AUG = 4
NSA_KW = LANES
SEL_TILE = 256
SEL_TILE_BLOCKS = SEL_TILE // SEL_LEN
WIN_TILE = 128
POS_SPLIT = 128
SEL_ONEHOT = LANES
PLAN_HALF = 16
SELECT_PARTS = 4
NSA_VROWS = NSA_HEAD_DIM + 16
KV_PREP_ROWS = 512


def _key_aug(pos, width, dead=False):
    n = pos.shape[0]
    col = lax.broadcasted_iota(jnp.int32, (n, width), 1)
    hi = ((pos // POS_SPLIT) * POS_SPLIT).astype(F32)
    lo = (pos % POS_SPLIT).astype(F32)
    live = jnp.where(col == 0, hi, jnp.where(col == 1, lo, jnp.where(col < NSA_AUG, 1.0, 0.0)))
    return jnp.where(dead, jnp.where(col == NSA_AUG, 1.0, 0.0), live)


def _query_t(q_ref, slope_ref, c0):
    hd, P, T = NSA_HEAD_DIM, NSA_PER_GROUP, Q_BLOCK
    PT = P * T
    qT = (q_ref[0] * (hd ** -0.5)).T
    qpart = jnp.concatenate([qT[p * hd:(p + 1) * hd, :] for p in range(P)], 1)
    tlane = c0 + lax.broadcasted_iota(jnp.int32, (1, PT), 1) % T
    slope = slope_ref[0]
    t_hi = ((tlane // POS_SPLIT) * POS_SPLIT).astype(F32)
    t_lo = (tlane % POS_SPLIT).astype(F32)
    rowi = lax.broadcasted_iota(jnp.int32, (NSA_KW - hd, PT), 0)
    aug = jnp.where(rowi < 2, slope,
                    jnp.where(rowi == 2, -slope * t_hi,
                              jnp.where(rowi == 3, -slope * t_lo, jnp.where(rowi == NSA_AUG, NEG_BIG, 0.0))))
    return jnp.concatenate([qpart, aug], 0).astype(BF16), tlane


def _compress_kernel(xk_ref, xv_ref, w1k_ref, pek_ref, w1fk_ref, w2k_ref, w1v_ref, pev_ref, w1fv_ref, w2v_ref,
                     kc_ref, vct_ref, *, nc):
    hd = NSA_HEAD_DIM
    pos = lax.broadcasted_iota(jnp.int32, (nc, 1), 0) * CMP_STRIDE + (CMP_LEN - 1)
    aug = _key_aug(pos, NSA_KW - hd)

    def one(x_ref, w1_ref, pe_ref, w1f_ref, w2_ref, g):
        uv = jnp.zeros((nc, 2 * CMP_HIDDEN), F32)
        for r in range(CMP_STRIDE):
            xr = x_ref[0, pl.ds(r, nc, stride=CMP_STRIDE), :][:, g * hd:(g + 1) * hd]
            uv = uv + _dot_hilo(xr, w1_ref[r])
        cvec = _dot_hilo(pe_ref[...], w1f_ref[...])[0:1, :]
        hid = uv[:, 0:CMP_HIDDEN] + pltpu.roll(uv[:, CMP_HIDDEN:], nc - 1, 0) + cvec
        return _dot_hilo(jax.nn.gelu(hid), w2_ref[...])

    for g in range(NSA_GROUPS):
        kc = one(xk_ref, w1k_ref, pek_ref, w1fk_ref, w2k_ref, g)
        kc_ref[0, g] = jnp.concatenate([kc, aug], 1).astype(BF16)
        vct_ref[0, g] = one(xv_ref, w1v_ref, pev_ref, w1fv_ref, w2v_ref, g).T.astype(BF16)


def _compress_call(proj3, pe_k, w1_k, w2_k, pe_v, w1_v, w2_v):
    B, S, _ = proj3.shape
    G, hd = NSA_GROUPS, NSA_HEAD_DIM
    nc = S // CMP_STRIDE
    half = CMP_STRIDE * hd

    def prep(pe, w1):
        w1 = w1.astype(F32)
        wr = jnp.concatenate([w1[:half].reshape(CMP_STRIDE, hd, CMP_HIDDEN),
                              w1[half:].reshape(CMP_STRIDE, hd, CMP_HIDDEN)], -1)
        pe8 = jnp.broadcast_to(pe.reshape(1, CMP_LEN * hd).astype(F32), (SUBLANES, CMP_LEN * hd))
        return wr, pe8, w1

    full = lambda a: pl.BlockSpec(a.shape, lambda b: tuple(0 for _ in a.shape))
    col = lambda off: pl.BlockSpec((1, S, LANES), lambda b: (b, 0, off // LANES))
    args = prep(pe_k, w1_k) + (w2_k.astype(F32),) + prep(pe_v, w1_v) + (w2_v.astype(F32),)
    return pl.pallas_call(
        functools.partial(_compress_kernel, nc=nc),
        grid=(B,),
        in_specs=[col(OFF_KC), col(OFF_VC)] + [full(a) for a in args],
        out_specs=[pl.BlockSpec((1, G, nc, NSA_KW), lambda b: (b, 0, 0, 0)),
                   pl.BlockSpec((1, G, hd, nc), lambda b: (b, 0, 0, 0))],
        out_shape=[jax.ShapeDtypeStruct((B, G, nc, NSA_KW), BF16),
                   jax.ShapeDtypeStruct((B, G, hd, nc), BF16)],
        compiler_params=_params("parallel"),
        name="nsa_compress",
    )(proj3, proj3, *args)


def _kvprep_kernel(ks_ref, vs_ref, kw_ref, vw_ref, ksa_ref, vst_ref, kwa_ref, vwt_ref, *, tt, n_live):
    hd = NSA_HEAD_DIM
    t = pl.program_id(1)
    dead = t >= n_live
    pos = t * tt + lax.broadcasted_iota(jnp.int32, (tt, 1), 0)
    aug = _key_aug(pos, NSA_KW - hd, dead)
    blk_col = lax.broadcasted_iota(jnp.int32, (tt, SEL_ONEHOT), 1)
    onehot = jnp.where(jnp.logical_and(blk_col == pos // SEL_LEN, jnp.logical_not(dead)), 1.0, 0.0)
    vrow = lax.broadcasted_iota(jnp.int32, (NSA_VROWS - hd, tt), 0)
    ones_rows = jnp.where(jnp.logical_and(vrow == 0, jnp.logical_not(dead)), 1.0, 0.0)
    for g in range(NSA_GROUPS):
        lanes = slice(g * hd, (g + 1) * hd)
        live = lambda ref: jnp.where(dead, 0.0, ref[0][:, lanes])
        ksa_ref[0, g] = jnp.concatenate([live(ks_ref), aug, onehot], 1).astype(BF16)
        kwa_ref[0, g] = jnp.concatenate([live(kw_ref), aug], 1).astype(BF16)
        vs_t = jnp.concatenate([live(vs_ref).T, ones_rows], 0).astype(BF16)
        vw_t = jnp.concatenate([live(vw_ref).T, ones_rows], 0).astype(BF16)
        for c in range(tt // SEL_TILE):
            vst_ref[0, g, c] = vs_t[:, c * SEL_TILE:(c + 1) * SEL_TILE]
        for c in range(tt // WIN_TILE):
            vwt_ref[0, g, c] = vw_t[:, c * WIN_TILE:(c + 1) * WIN_TILE]


def _kvprep_call(proj3):
    B, S, _ = proj3.shape
    G, hd = NSA_GROUPS, NSA_HEAD_DIM
    tt = min(KV_PREP_ROWS, S)
    n_live = S // tt
    assert tt == WINDOW
    sp = S + tt
    col = lambda off: pl.BlockSpec((1, tt, LANES), lambda b, t: (b, jnp.minimum(t, n_live - 1), off // LANES))
    front = lambda t: (t + 1) % (n_live + 1)
    return pl.pallas_call(
        functools.partial(_kvprep_kernel, tt=tt, n_live=n_live),
        grid=(B, n_live + 1),
        in_specs=[col(OFF_KS), col(OFF_VS), col(OFF_KW), col(OFF_VW)],
        out_specs=[pl.BlockSpec((1, G, tt, NSA_KW + SEL_ONEHOT), lambda b, t: (b, 0, t, 0)),
                   pl.BlockSpec((1, G, tt // SEL_TILE, NSA_VROWS, SEL_TILE), lambda b, t: (b, 0, t, 0, 0)),
                   pl.BlockSpec((1, G, tt, NSA_KW), lambda b, t: (b, 0, front(t), 0)),
                   pl.BlockSpec((1, G, tt // WIN_TILE, NSA_VROWS, WIN_TILE), lambda b, t: (b, 0, front(t), 0, 0))],
        out_shape=[jax.ShapeDtypeStruct((B, G, sp, NSA_KW + SEL_ONEHOT), BF16),
                   jax.ShapeDtypeStruct((B, G, sp // SEL_TILE, NSA_VROWS, SEL_TILE), BF16),
                   jax.ShapeDtypeStruct((B, G, sp, NSA_KW), BF16),
                   jax.ShapeDtypeStruct((B, G, sp // WIN_TILE, NSA_VROWS, WIN_TILE), BF16)],
        compiler_params=_params("parallel", "parallel"),
        name="nsa_kvprep",
    )(proj3, proj3, proj3, proj3)


def _importance_matrix(ns, nc):
    ratio = SEL_LEN // CMP_STRIDE
    a = np.zeros((ns, nc), np.float32)
    for j in range(ns):
        for n, wgt in ((ratio * j - 1, 0.5), (ratio * j, 1.0), (ratio * j + 1, 1.0),
                       (ratio * j + 2, 1.0), (ratio * j + 3, 0.5)):
            if 0 <= n < nc - 1:
                a[j, n] = wgt
    return a


def _plan_constants(ns):
    nt = ns // SEL_TILE_BLOCKS
    grp = (np.arange(ns)[None, :] // SEL_TILE_BLOCKS == np.arange(nt)[:, None]).astype(np.float32)
    k = np.arange(nt)
    w_lo = np.where(k < PLAN_HALF, 2.0 ** np.minimum(k, PLAN_HALF - 1), 0.0)
    w_hi = np.where(k >= PLAN_HALF, 2.0 ** np.maximum(k - PLAN_HALF, 0), 0.0)
    wts = np.stack([np.repeat(w_lo[:, None], LANES, 1), np.repeat(w_hi[:, None], LANES, 1)]).astype(np.float32)
    return grp, wts


def _select_kernel(q_ref, slope_ref, kc_ref, vct_ref, imp_ref, grp_ref, wts_ref, ocmp_ref, bias_ref, bits_ref,
                   *, ns, nc):
    P, T = NSA_PER_GROUP, Q_BLOCK
    iq = pl.program_id(2)
    nq = pl.num_programs(2)
    c0 = iq * T
    qt, tlane = _query_t(q_ref, slope_ref, c0)

    def body(rows, nblk):
        s = _dot(kc_ref[0, 0, 0:rows, :], qt)
        cend = lax.broadcasted_iota(jnp.int32, (rows, 1), 0) * CMP_STRIDE + (CMP_LEN - 1)
        cmask = cend <= tlane
        s = jnp.where(cmask, s, NEG_BIG)
        m = jnp.max(s, 0, keepdims=True)
        p_c = jnp.where(cmask, jnp.exp(s - m), 0.0)
        p_c = p_c / jnp.maximum(jnp.sum(p_c, 0, keepdims=True), 1e-30)
        ocmp_ref[0, 0, 0] = _dot(vct_ref[0, 0][:, 0:rows], p_c.astype(BF16))

        psum = p_c[:, 0:T]
        for p in range(1, P):
            psum = psum + p_c[:, p * T:(p + 1) * T]
        imp = _dot_exact_lhs(imp_ref[0:nblk, 0:rows], psum)
        blk = lax.broadcasted_iota(jnp.int32, (nblk, T), 0)
        cur = (c0 + lax.broadcasted_iota(jnp.int32, (nblk, T), 1)) // SEL_LEN
        forced = (blk == 0) | (blk == cur) | (blk == cur - 1)
        val = jnp.where(forced, 3e38, jnp.where(blk > cur, -1.0, imp))
        chosen = jnp.zeros((nblk, T), F32)
        for _ in range(min(SEL_TOPK, nblk)):
            mx = jnp.max(val, 0, keepdims=True)
            first = jnp.min(jnp.where(val == mx, blk, ns), 0, keepdims=True)
            pick = blk == first
            chosen = jnp.where(pick, 1.0, chosen)
            val = jnp.where(pick, -2.0, val)
        bias_ref[0, 0, 0, 0:nblk, :] = jnp.where(chosen > 0.5, 0.0, NEG_BIG)
        if nblk < ns:
            bias_ref[0, 0, 0, nblk:ns, :] = jnp.full((ns - nblk, T), NEG_BIG, F32)

        if nblk < ns:
            chosen = jnp.concatenate([chosen, jnp.zeros((ns - nblk, T), F32)], 0)
        cnt = _dot(grp_ref[...], chosen.astype(BF16))
        act = jnp.where(jnp.max(cnt, 1, keepdims=True) > 0.5, 1.0, 0.0)
        lo = jnp.sum(act * wts_ref[0], 0, keepdims=True).astype(jnp.int32)
        hi = jnp.sum(act * wts_ref[1], 0, keepdims=True).astype(jnp.int32)
        bits_ref[0] = jnp.broadcast_to(lo | (hi << PLAN_HALF), (SUBLANES, LANES))

    parts = max(p for p in (SELECT_PARTS, 2, 1) if nc % (p * LANES) == 0 and ns % (p * 2 * SUBLANES) == 0)
    part = iq * parts // nq
    for v in range(parts):
        pl.when(part == v)(functools.partial(body, nc * (v + 1) // parts, ns * (v + 1) // parts))


def _attend_kernel(bits_sref, q_ref, gt_ref, slope_ref, ocmp_ref, bias_ref, ks_ref, vst_ref, kw_ref, vwt_ref,
                   o_ref, list_ref, s_ref, m_ref, *, n_tiles_total, ns):
    hd, P, T = NSA_HEAD_DIM, NSA_PER_GROUP, Q_BLOCK
    PT = P * T
    b, g, iq = pl.program_id(0), pl.program_id(1), pl.program_id(2)
    c0 = iq * T
    qt, tlane = _query_t(q_ref, slope_ref, c0)
    tl_row = lax.broadcasted_iota(jnp.int32, (1, PT), 1) % T

    gt = jax.nn.sigmoid(gt_ref[0]).T

    def gate_row(branch):
        return jnp.concatenate([gt[p * 3 + branch:p * 3 + branch + 1, :] for p in range(P)], 1)

    bits = bits_sref[(b * pl.num_programs(1) + g) * pl.num_programs(2) + iq]
    last_tile = (c0 + T - 1) // SEL_TILE

    prev_bits = bits & (lax.shift_left(jnp.int32(1), last_tile) - 1)
    n_prev = jnp.int32(0)
    for k in range(n_tiles_total):
        list_ref[n_prev] = k
        n_prev = n_prev + (lax.shift_right_logical(prev_bits, k) & 1)

    sel_b = bias_ref[0, 0, 0]
    if ns < SEL_ONEHOT:
        sel_b = jnp.concatenate([sel_b, jnp.zeros((SEL_ONEHOT - ns, T), F32)], 0)
    qt_sel = jnp.concatenate([qt, jnp.concatenate([sel_b.astype(BF16)] * P, 1)], 0)

    def tile_scores(kt):
        k0 = pl.multiple_of(kt * SEL_TILE, SEL_TILE)
        return _dot(ks_ref[0, 0, pl.ds(k0, SEL_TILE), :], qt_sel)

    def col_max(scores):
        m = jnp.max(scores[0], 0, keepdims=True)
        for sc in scores[1:]:
            m = jnp.maximum(m, jnp.max(sc, 0, keepdims=True))
        return m

    def softmax_update(carry, scores, vts, m_tile):
        m_run, acc = carry
        m_new = jnp.maximum(m_run, m_tile)
        acc = jnp.exp(m_run - m_new) * acc
        for sc, vt in zip(scores, vts):
            acc = acc + _dot(vt, jnp.exp(sc - m_new).astype(BF16))
        return m_new, acc

    init = (jnp.full((1, PT), NEG_BIG, F32), jnp.zeros((NSA_VROWS, PT), F32))
    dead_tile = n_tiles_total
    n_pairs = jnp.maximum((n_prev + 1) // 2, 1)

    def pair_tiles(i):
        has_a, has_b = 2 * i < n_prev, 2 * i + 1 < n_prev
        ka = jnp.where(has_a, list_ref[jnp.where(has_a, 2 * i, 0)], dead_tile)
        kb = jnp.where(has_b, list_ref[jnp.where(has_b, 2 * i + 1, 0)], dead_tile)
        return ka, kb

    def pair_scores(i):
        ka, kb = pair_tiles(i)
        keys = jnp.concatenate([ks_ref[0, 0, pl.ds(pl.multiple_of(k * SEL_TILE, SEL_TILE), SEL_TILE), :]
                                for k in (ka, kb)], 0)
        return _dot(keys, qt_sel)

    def put_scores(i, sc):
        slot = i % 2
        s_ref[slot] = sc
        m_ref[slot] = col_max([sc])

    def pair_probs(i, m_run):
        slot = i % 2
        m_new = jnp.maximum(m_run, m_ref[slot])
        return m_new, jnp.exp(s_ref[slot] - m_new).astype(BF16)

    def add_values(i, carry, m_new, pr):
        ka, kb = pair_tiles(i)
        m_run, acc = carry
        vt = jnp.concatenate([vst_ref[0, 0, ka], vst_ref[0, 0, kb]], 1)
        return m_new, jnp.exp(m_run - m_new) * acc + _dot(vt, pr)

    put_scores(0, pair_scores(0))

    n_win = (WINDOW + T) // WIN_TILE
    roff = lax.broadcasted_iota(jnp.int32, (WIN_TILE, 1), 0)
    w0 = pl.multiple_of(c0, WIN_TILE)
    sc = _dot(kw_ref[0, 0, pl.ds(w0, WINDOW + T), :], qt)
    scores = [jnp.where(roff > tl_row, sc[0:WIN_TILE], NEG_BIG),
              sc[WIN_TILE:WINDOW],
              jnp.where(roff <= tl_row, sc[WINDOW:WINDOW + T], NEG_BIG)]
    sc = jnp.concatenate(scores, 0)
    vt = jnp.concatenate([vwt_ref[0, 0, iq + j] for j in range(n_win)], 1)
    _, acc_win = softmax_update(init, [sc], [vt], col_max(scores))
    out_cw = gate_row(0) * ocmp_ref[0, 0, 0] + gate_row(2) * (acc_win[0:hd] / acc_win[hd:hd + 1])

    def pipe_step(i, carry):
        m_new, pr = pair_probs(i, carry[0])
        sc_next = pair_scores(i + 1)
        carry = add_values(i, carry, m_new, pr)
        put_scores(i + 1, sc_next)
        return carry

    carry = lax.fori_loop(0, n_pairs - 1, pipe_step, init)
    m_new, pr = pair_probs(n_pairs - 1, carry[0])
    kpos = last_tile * SEL_TILE + lax.broadcasted_iota(jnp.int32, (SEL_TILE, 1), 0)
    s_last = jnp.where(kpos <= tlane, tile_scores(last_tile), NEG_BIG)
    carry = add_values(n_pairs - 1, carry, m_new, pr)
    _, acc_sel = softmax_update(carry, [s_last], [vst_ref[0, 0, last_tile]], col_max([s_last]))
    out = out_cw + gate_row(1) * (acc_sel[0:hd] / acc_sel[hd:hd + 1])

    o_ref[0] = jnp.concatenate([out[:, p * T:(p + 1) * T] for p in range(P)], 0).T.astype(o_ref.dtype)


def _nsa_call(proj3, pe_k, w1_k, w2_k, pe_v, w1_v, w2_v):
    B, S, _ = proj3.shape
    G, hd, P, T = NSA_GROUPS, NSA_HEAD_DIM, NSA_PER_GROUP, Q_BLOCK
    PT = P * T
    nc = S // CMP_STRIDE
    ns = S // SEL_LEN
    nq = S // T
    nt = ns // SEL_TILE_BLOCKS
    assert nt <= 2 * PLAN_HALF and S % KV_PREP_ROWS == 0

    kc_a, vc_t = _compress_call(proj3, pe_k, w1_k, w2_k, pe_v, w1_v, w2_v)
    ks_a, vs_t, kw_a, vw_t = _kvprep_call(proj3)
    imp_m = jnp.asarray(_importance_matrix(ns, nc), BF16)
    grp, wts = _plan_constants(ns)
    head_slopes = 2.0 ** (-(8.0 / NSA_HEADS) * np.arange(1, NSA_HEADS + 1, dtype=np.float32))
    slopes = jnp.asarray(np.repeat(head_slopes.reshape(G, 1, P), T, axis=-1), F32)

    q_spec = lambda im: pl.BlockSpec((1, T, P * hd), im)
    bg = lambda shp: pl.BlockSpec((1, 1) + shp, lambda b, g, i: (b, g) + tuple(0 for _ in shp))
    const = lambda a: pl.BlockSpec(a.shape, lambda b, g, i: tuple(0 for _ in a.shape))
    ocmp, sel_bias, bits = pl.pallas_call(
        functools.partial(_select_kernel, ns=ns, nc=nc),
        grid=(B, G, nq),
        in_specs=[q_spec(lambda b, g, i: (b, i, OFF_NQ // (P * hd) + g)),
                  pl.BlockSpec((1, 1, PT), lambda b, g, i: (g, 0, 0)),
                  bg((nc, NSA_KW)), bg((hd, nc)), const(imp_m),
                  pl.BlockSpec(grp.shape, lambda b, g, i: (0, 0)),
                  pl.BlockSpec(wts.shape, lambda b, g, i: (0, 0, 0))],
        out_specs=[pl.BlockSpec((1, 1, 1, hd, PT), lambda b, g, i: (b, g, i, 0, 0)),
                   pl.BlockSpec((1, 1, 1, ns, T), lambda b, g, i: (b, g, i, 0, 0)),
                   pl.BlockSpec((1, SUBLANES, LANES), lambda b, g, i: ((b * G + g) * nq + i, 0, 0))],
        out_shape=[jax.ShapeDtypeStruct((B, G, nq, hd, PT), F32),
                   jax.ShapeDtypeStruct((B, G, nq, ns, T), F32),
                   jax.ShapeDtypeStruct((B * G * nq, SUBLANES, LANES), jnp.int32)],
        compiler_params=_params("parallel", "parallel", "parallel"),
        name="nsa_select",
    )(proj3, slopes, kc_a, vc_t, imp_m, jnp.asarray(grp, BF16), jnp.asarray(wts, F32))

    bg2 = lambda shp: pl.BlockSpec((1, 1) + shp, lambda b, g, i, s: (b, g) + tuple(0 for _ in shp))
    return pl.pallas_call(
        functools.partial(_attend_kernel, n_tiles_total=nt, ns=ns),
        grid_spec=pltpu.PrefetchScalarGridSpec(
            num_scalar_prefetch=1,
            grid=(B, G, nq),
            in_specs=[pl.BlockSpec((1, T, P * hd), lambda b, g, i, s: (b, i, OFF_NQ // (P * hd) + g)),
                      pl.BlockSpec((1, T, LANES), lambda b, g, i, s: (b, i, OFF_NG // LANES + g)),
                      pl.BlockSpec((1, 1, PT), lambda b, g, i, s: (g, 0, 0)),
                      pl.BlockSpec((1, 1, 1, hd, PT), lambda b, g, i, s: (b, g, i, 0, 0)),
                      pl.BlockSpec((1, 1, 1, ns, T), lambda b, g, i, s: (b, g, i, 0, 0)),
                      bg2(ks_a.shape[2:]), bg2(vs_t.shape[2:]), bg2(kw_a.shape[2:]), bg2(vw_t.shape[2:])],
            out_specs=pl.BlockSpec((1, T, P * hd), lambda b, g, i, s: (b, i, g)),
            scratch_shapes=[pltpu.SMEM((nt + 1,), jnp.int32),
                            pltpu.VMEM((2, 2 * SEL_TILE, PT), F32), pltpu.VMEM((2, 1, PT), F32)]),
        out_shape=jax.ShapeDtypeStruct((B, S, NSA_WIDTH), BF16),
        compiler_params=_params("parallel", "parallel", "arbitrary"),
        name="nsa_attend",
    )(bits[:, 0, 0], proj3, proj3, slopes, ocmp, sel_bias, ks_a, vs_t, kw_a, vw_t)


def _merge_kernel(x_ref, ya_ref, yb_ref, yc_ref, g0_ref, g1_ref, g2_ref, wa_ref, wb_ref, wc_ref, wo_ref,
                  lg_ref, lb_ref, o_ref):
    m = jax.nn.sigmoid(g0_ref[...]) * _dot(ya_ref[...], wa_ref[...])
    m = m + jax.nn.sigmoid(g1_ref[...]) * _dot(yb_ref[...], wb_ref[...])
    m = m + jax.nn.sigmoid(g2_ref[...]) * _dot(yc_ref[...], wc_ref[...])
    z = DEEPNORM_ALPHA * x_ref[...] + _dot(m.astype(BF16), wo_ref[...])
    o_ref[...] = _layer_norm_rows(z, lg_ref[...], lb_ref[...])


def _merge_call(x2, ya, yb, yc, proj2, w_hg, w_nsa, w_lru, w_out, ln_g, ln_b, tm=512):
    T, D = x2.shape
    tm = min(tm, T)
    rows = lambda w: pl.BlockSpec((tm, w), lambda i: (i, 0))
    gate = lambda n: pl.BlockSpec((tm, D), lambda i: (i, OFF_MG // D + n))
    full = lambda a: pl.BlockSpec(a.shape, lambda i: (0, 0))
    ws = [w.astype(BF16) for w in (w_hg, w_nsa, w_lru, w_out)]
    return pl.pallas_call(
        _merge_kernel,
        grid=(T // tm,),
        in_specs=[rows(D), rows(HG_WIDTH), rows(NSA_WIDTH), rows(LRU_WIDTH), gate(0), gate(1), gate(2)]
                 + [full(w) for w in ws] + [pl.BlockSpec((1, D), lambda i: (0, 0))] * 2,
        out_specs=rows(D),
        out_shape=jax.ShapeDtypeStruct((T, D), F32),
        compiler_params=_params("parallel"),
        name="merge_out",
    )(x2, ya, yb, yc, proj2, proj2, proj2, *ws, ln_g.reshape(1, D), ln_b.reshape(1, D))


FFN_SPLIT = 2
FFN_PAD = SUBLANES
FFN_SUB = 256


def _ffn_kernel(x_ref, wu_ref, wv_ref, cw_ref, cb_ref, wd_ref, lg_ref, lb_ref, o_ref,
                xb_ref, acc_ref, upad_ref, carry_ref, *, tm, tiles_per_seq):
    i = pl.program_id(0)
    j = pl.program_id(1)

    @pl.when(j == 0)
    def _():
        xb_ref[...] = x_ref[...].astype(BF16)

    xb = xb_ref[...]
    fc = wu_ref.shape[1]
    first = (i % tiles_per_seq) == 0

    @pl.when(first)
    def _():
        upad_ref[0:FFN_PAD, :] = jnp.zeros((FFN_PAD, fc), F32)

    @pl.when(jnp.logical_not(first))
    def _():
        upad_ref[0:FFN_PAD, :] = carry_ref[j]

    @pl.when(j == 0)
    def _():
        acc_ref[...] = jnp.zeros_like(acc_ref)

    for c0 in range(0, fc, FFN_SUB):
        cs = slice(c0, min(c0 + FFN_SUB, fc))
        u = _dot(xb, wu_ref[:, cs])
        v = _dot(xb, wv_ref[:, cs])
        upad_ref[FFN_PAD:FFN_PAD + tm, cs] = u
        cw = cw_ref[:, cs]
        cv = cb_ref[:, cs] + cw[FFN_CONV - 1:FFN_CONV, :] * u
        for k in range(FFN_CONV - 1):
            off = FFN_PAD - (FFN_CONV - 1) + k
            cv = cv + cw[k:k + 1, :] * upad_ref[off:off + tm, cs]
        carry_ref[j, :, cs] = u[tm - FFN_PAD:tm, :]
        h = (jax.nn.gelu(cv) * v).astype(BF16)
        acc_ref[...] += _dot(h, wd_ref[cs, :])

    @pl.when(j == FFN_SPLIT - 1)
    def _():
        z = DEEPNORM_ALPHA * x_ref[...] + acc_ref[...]
        o_ref[...] = _layer_norm_rows(z, lg_ref[...], lb_ref[...])


def _ffn_call(x2, seq_len, w_up, conv_w, conv_b, w_down, ln_g, ln_b, tm=512):
    T, D = x2.shape
    tm = min(tm, seq_len)
    fc = FFN_DIM // FFN_SPLIT
    wu = w_up.astype(BF16)
    return pl.pallas_call(
        functools.partial(_ffn_kernel, tm=tm, tiles_per_seq=seq_len // tm),
        grid=(T // tm, FFN_SPLIT),
        in_specs=[pl.BlockSpec((tm, D), lambda i, j: (i, 0)),
                  pl.BlockSpec((D, fc), lambda i, j: (0, j)),
                  pl.BlockSpec((D, fc), lambda i, j: (0, FFN_SPLIT + j)),
                  pl.BlockSpec((FFN_CONV, fc), lambda i, j: (0, j)),
                  pl.BlockSpec((1, fc), lambda i, j: (0, j)),
                  pl.BlockSpec((fc, D), lambda i, j: (j, 0)),
                  pl.BlockSpec((1, D), lambda i, j: (0, 0)),
                  pl.BlockSpec((1, D), lambda i, j: (0, 0))],
        out_specs=pl.BlockSpec((tm, D), lambda i, j: (i, 0)),
        out_shape=jax.ShapeDtypeStruct((T, D), F32),
        scratch_shapes=[pltpu.VMEM((tm, D), BF16), pltpu.VMEM((tm, D), F32),
                        pltpu.VMEM((tm + FFN_PAD, fc), F32), pltpu.VMEM((FFN_SPLIT, FFN_PAD, fc), F32)],
        compiler_params=_params("arbitrary", "arbitrary"),
        name="conv_ffn",
    )(x2, wu, wu, conv_w.astype(F32), conv_b.reshape(1, FFN_DIM).astype(F32), w_down.astype(BF16),
      ln_g.reshape(1, D), ln_b.reshape(1, D))


def _permute_in_proj(w, b):
    src = np.cumsum([0, HG_WIDTH, HG_WIDTH, HG_WIDTH, HG_WIDTH, NSA_WIDTH, NSA_KV, NSA_KV, NSA_KV, NSA_KV,
                     NSA_KV, NSA_KV, NSA_HEADS * 3, LRU_WIDTH, LRU_WIDTH, N_BRANCH * 1024])
    names = ["hq", "hf", "hi", "hg", "nq", "kc", "vc", "ks", "vs", "kw", "vw", "ng", "lx", "ly", "mg"]
    seg = {n: (int(src[k]), int(src[k + 1])) for k, n in enumerate(names)}
    order = ["mg", "hq", "hf", "hi", "hg", "nq", "lx", "ly", "kc", "vc", "ks", "vs", "kw", "vw"]
    per_group = NSA_PER_GROUP * 3
    ng0 = seg["ng"][0]
    w_parts = [w[:, seg[n][0]:seg[n][1]] for n in order]
    b_parts = [b[seg[n][0]:seg[n][1]] for n in order]
    for gi in range(NSA_GROUPS):
        lo = ng0 + gi * per_group
        w_parts += [w[:, lo:lo + per_group], jnp.zeros((w.shape[0], LANES - per_group), w.dtype)]
        b_parts += [b[lo:lo + per_group], jnp.zeros((LANES - per_group,), b.dtype)]
    wp = jnp.concatenate(w_parts, 1)
    bp = jnp.concatenate(b_parts)
    assert wp.shape[1] == PROJ_COLS
    return wp.astype(BF16), bp.astype(F32)


def kernel(x, ln_emb_g, ln_emb_b, w_in, b_in, hg_lb_logits, hg_norm_g, cmp_pe_k, cmp_w1_k, cmp_w2_k, cmp_pe_v, cmp_w1_v, cmp_w2_v, lru_conv_w, lru_conv_b, lru_wa, lru_ba, lru_wx, lru_bx, lru_lambda, w_branch_hg, w_branch_nsa, w_branch_lru, w_out, ln1_g, ln1_b, ffn_w_up, ffn_conv_w, ffn_conv_b, ffn_w_down, ln2_g, ln2_b):
    B, S, D = x.shape
    T = B * S
    gam = jax.nn.softmax(hg_lb_logits.astype(F32), axis=0)
    lb_all = jnp.cumsum(gam, axis=0) - gam[0]
    h = _layer_norm_call(x.reshape(T, D), ln_emb_g, ln_emb_b)
    for l in range(DEPTH):
        wp, bp = _permute_in_proj(w_in[l], b_in[l])
        proj2 = _inproj_call(h, wp, bp)
        proj3 = proj2.reshape(B, S, PROJ_COLS)
        y_a = _hgrn_call(proj3, lb_all[l], hg_norm_g[l])
        y_b = _nsa_call(proj3, cmp_pe_k[l], cmp_w1_k[l], cmp_w2_k[l], cmp_pe_v[l], cmp_w1_v[l], cmp_w2_v[l])
        y_c = _lru_call(proj3, lru_conv_w[l], lru_conv_b[l], lru_wa[l], lru_ba[l], lru_wx[l], lru_bx[l],
                        lru_lambda[l])
        h = _merge_call(h, y_a.reshape(T, HG_WIDTH), y_b.reshape(T, NSA_WIDTH), y_c.reshape(T, LRU_WIDTH),
                        proj2, w_branch_hg[l], w_branch_nsa[l], w_branch_lru[l], w_out[l], ln1_g[l], ln1_b[l])
        h = _ffn_call(h, S, ffn_w_up[l], ffn_conv_w[l], ffn_conv_b[l], ffn_w_down[l], ln2_g[l], ln2_b[l])
    return h.reshape(B, S, D)
```

```python
import functools

import numpy as np
import jax
import jax.numpy as jnp
from jax import lax
from jax.experimental import pallas as pl
from jax.experimental.pallas import tpu as pltpu

F32 = jnp.float32
BF16 = jnp.bfloat16

DEPTH = 2
HG_HEADS = 4
HG_HEAD_DIM = 128
HG_WIDTH = HG_HEADS * HG_HEAD_DIM
NSA_HEADS = 8
NSA_GROUPS = 2
NSA_PER_GROUP = NSA_HEADS // NSA_GROUPS
NSA_HEAD_DIM = 64
NSA_WIDTH = NSA_HEADS * NSA_HEAD_DIM
NSA_KV = NSA_GROUPS * NSA_HEAD_DIM
CMP_LEN = 32
CMP_STRIDE = 16
CMP_HIDDEN = 64
SEL_LEN = 64
SEL_TOPK = 16
WINDOW = 512
Q_BLOCK = 128
LRU_WIDTH = 512
LRU_BLOCKS = 4
LRU_CONV = 4
LRU_C = 8.0
FFN_DIM = 2816
FFN_CONV = 3
DEEPNORM_ALPHA = (2 * DEPTH) ** 0.25
LN_EPS = 1e-5
N_BRANCH = 3

LANES = 128
SUBLANES = 8
VMEM_LIMIT_BYTES = 52 * 1024 * 1024

OFF_MG = 0
OFF_HQ = N_BRANCH * 1024
OFF_HF = OFF_HQ + HG_WIDTH
OFF_HI = OFF_HF + HG_WIDTH
OFF_HG = OFF_HI + HG_WIDTH
OFF_NQ = OFF_HG + HG_WIDTH
OFF_LX = OFF_NQ + NSA_WIDTH
OFF_LY = OFF_LX + LRU_WIDTH
OFF_KC = OFF_LY + LRU_WIDTH
OFF_VC = OFF_KC + NSA_KV
OFF_KS = OFF_VC + NSA_KV
OFF_VS = OFF_KS + NSA_KV
OFF_KW = OFF_VS + NSA_KV
OFF_VW = OFF_KW + NSA_KV
OFF_NG = OFF_VW + NSA_KV
PROJ_TILE_N = 1536
PROJ_COLS = 5 * PROJ_TILE_N

NEG_BIG = -1e30


def _split3(x):
    hi = x.astype(BF16)
    r1 = x - hi.astype(F32)
    mid = r1.astype(BF16)
    lo = (r1 - mid.astype(F32)).astype(BF16)
    return hi, mid, lo


def _dot(a, b):
    return jnp.dot(a, b, preferred_element_type=F32)


def _dot_nt(a, b):
    return lax.dot_general(a, b, (((1,), (1,)), ((), ())), preferred_element_type=F32)


def _dot_exact_lhs(m_bf16, x):
    hi, mid, lo = _split3(x)
    return _dot(m_bf16, hi) + _dot(m_bf16, mid) + _dot(m_bf16, lo)


def _dot_hilo(a, b):
    ah = a.astype(BF16)
    al = (a - ah.astype(F32)).astype(BF16)
    bh = b.astype(BF16)
    bl = (b - bh.astype(F32)).astype(BF16)
    return _dot(ah, bh) + _dot(ah, bl) + _dot(al, bh) + _dot(al, bl)


def _layer_norm_rows(z, g, b):
    mu = jnp.mean(z, -1, keepdims=True)
    zc = z - mu
    var = jnp.mean(zc * zc, -1, keepdims=True)
    return zc * lax.rsqrt(var + LN_EPS) * g + b


def _params(*sem):
    return pltpu.CompilerParams(dimension_semantics=sem, vmem_limit_bytes=VMEM_LIMIT_BYTES)


def _ln_kernel(x_ref, g_ref, b_ref, o_ref):
    o_ref[...] = _layer_norm_rows(x_ref[...], g_ref[...], b_ref[...])


def _layer_norm_call(x2, g, b, tm=512):
    T, D = x2.shape
    return pl.pallas_call(
        _ln_kernel,
        grid=(T // tm,),
        in_specs=[pl.BlockSpec((tm, D), lambda i: (i, 0)),
                  pl.BlockSpec((1, D), lambda i: (0, 0)),
                  pl.BlockSpec((1, D), lambda i: (0, 0))],
        out_specs=pl.BlockSpec((tm, D), lambda i: (i, 0)),
        out_shape=jax.ShapeDtypeStruct((T, D), F32),
        compiler_params=_params("parallel"),
        name="embed_ln",
    )(x2, g.reshape(1, D), b.reshape(1, D))


def _inproj_kernel(x_ref, w_ref, b_ref, o_ref, xb_ref):
    @pl.when(pl.program_id(1) == 0)
    def _():
        xb_ref[...] = x_ref[...].astype(BF16)

    o_ref[...] = _dot(xb_ref[...], w_ref[...]) + b_ref[...]


def _inproj_call(x2, w_bf16, bias, tm=1024):
    T, D = x2.shape
    N = w_bf16.shape[1]
    tm = min(tm, T)
    tn = PROJ_TILE_N
    return pl.pallas_call(
        _inproj_kernel,
        grid=(T // tm, N // tn),
        in_specs=[pl.BlockSpec((tm, D), lambda i, j: (i, 0)),
                  pl.BlockSpec((D, tn), lambda i, j: (0, j)),
                  pl.BlockSpec((1, tn), lambda i, j: (0, j))],
        out_specs=pl.BlockSpec((tm, tn), lambda i, j: (i, j)),
        out_shape=jax.ShapeDtypeStruct((T, N), F32),
        scratch_shapes=[pltpu.VMEM((tm, D), BF16)],
        compiler_params=_params("parallel", "arbitrary"),
        name="in_proj",
    )(x2, w_bf16, bias.reshape(1, N))


HG_CHUNK = 128
HG_DIAG = SUBLANES


def _hgrn_constants(C):
    r = np.arange(C)
    tri = (r[None, :] <= r[:, None]).astype(np.float32)
    mats = [tri]
    masks = []
    w = HG_DIAG
    while 2 * w <= C:
        mid = (r // (2 * w)) * (2 * w) + w
        mats.append(tri - (r[None, :] <= mid[:, None]).astype(np.float32))
        same = (r[:, None] // (2 * w)) == (r[None, :] // (2 * w))
        masks.append((same & ((r[:, None] % (2 * w)) >= w) & ((r[None, :] % (2 * w)) < w)).astype(np.float32))
        w *= 2
    mats.append((r[None, :] > r[:, None]).astype(np.float32))
    return np.concatenate(mats, 0), np.stack(masks, 0)


def _hgrn_kernel(q_ref, f_ref, i_ref, g_ref, lb_ref, ng_ref, m_ref, mask_ref, o_ref, st_ref, *, C, nchunk):
    d = HG_HEAD_DIM
    nlev = mask_ref.shape[0]

    @pl.when(pl.program_id(2) == 0)
    def _():
        st_ref[...] = jnp.zeros_like(st_ref)

    lb = lb_ref[0]
    ng = ng_ref[0]
    nb = C // HG_DIAG
    row3 = lax.broadcasted_iota(jnp.int32, (nb, HG_DIAG, d), 1)

    fl_all = f_ref[0]
    lf_all = jnp.log(lb + (1.0 - lb) * jax.nn.sigmoid(fl_all))
    kk_all = (1.0 - lb) * jax.nn.sigmoid(-fl_all)
    allm_all = _dot_exact_lhs(m_ref[...], jnp.concatenate([lf_all[c * C:(c + 1) * C] for c in range(nchunk)], 1))

    for c in range(nchunk):
        sl = pl.ds(c * C, C)
        q = q_ref[0, sl, :]
        iv = i_ref[0, sl, :]
        g = g_ref[0, sl, :]
        kk = kk_all[c * C:(c + 1) * C]
        allm = allm_all[:, c * d:(c + 1) * d]
        bcs = allm[0:C]
        rem = allm[(nlev + 1) * C:(nlev + 2) * C]
        b_last = bcs[C - 1:C, :]
        iv_b = iv.astype(BF16)

        att = jnp.zeros((C, C), F32)
        for l in range(nlev):
            e = jnp.exp(-jnp.abs(allm[(1 + l) * C:(2 + l) * C]))
            att = att + mask_ref[l] * _dot_nt((q * e).astype(BF16), (kk * e).astype(BF16))
        o = _dot(att.astype(BF16), iv_b)

        q3 = q.reshape(nb, HG_DIAG, d)
        k3 = kk.reshape(nb, HG_DIAG, d)
        b3 = bcs.reshape(nb, HG_DIAG, d)
        i3 = iv.reshape(nb, HG_DIAG, d)
        acc = jnp.zeros((nb, HG_DIAG, d), F32)
        for s in range(HG_DIAG):
            dec = jnp.where(row3 >= s, jnp.exp(b3 - b3[:, s:s + 1, :]), 0.0)
            a = jnp.sum(q3 * dec * k3[:, s:s + 1, :], axis=-1, keepdims=True)
            acc = acc + a * i3[:, s:s + 1, :]
        o = o + acc.reshape(C, d)

        st = st_ref[...]
        o = o + _dot_nt((q * jnp.exp(bcs)).astype(BF16), st.astype(BF16))
        kdec = (kk * jnp.exp(rem)).astype(BF16)
        st_ref[...] = st * jnp.exp(b_last) + _dot(iv.T.astype(BF16), kdec)

        o = o * lax.rsqrt(jnp.mean(o * o, -1, keepdims=True) + 1e-6) * ng
        o_ref[0, sl, :] = (o * (g * jax.nn.sigmoid(g))).astype(o_ref.dtype)


def _hgrn_call(proj3, lb, norm_g, tt=512):
    B, S, _ = proj3.shape
    d = HG_HEAD_DIM
    C = HG_CHUNK
    tt = min(tt, S)
    mstack, masks = _hgrn_constants(C)
    nm = mstack.shape[0]
    col = lambda off: (lambda b, h, t: (b, t, off // d + h))
    blk = (1, tt, d)
    return pl.pallas_call(
        functools.partial(_hgrn_kernel, C=C, nchunk=tt // C),
        grid=(B, HG_HEADS, S // tt),
        in_specs=[pl.BlockSpec(blk, col(OFF_HQ)), pl.BlockSpec(blk, col(OFF_HF)),
                  pl.BlockSpec(blk, col(OFF_HI)), pl.BlockSpec(blk, col(OFF_HG)),
                  pl.BlockSpec((1, 1, d), lambda b, h, t: (h, 0, 0)),
                  pl.BlockSpec((1, 1, d), lambda b, h, t: (h, 0, 0)),
                  pl.BlockSpec((nm, C), lambda b, h, t: (0, 0)),
                  pl.BlockSpec(masks.shape, lambda b, h, t: (0, 0, 0))],
        out_specs=pl.BlockSpec(blk, lambda b, h, t: (b, t, h)),
        out_shape=jax.ShapeDtypeStruct((B, S, HG_WIDTH), BF16),
        scratch_shapes=[pltpu.VMEM((d, d), F32)],
        compiler_params=_params("parallel", "parallel", "arbitrary"),
        name="hgrn2",
    )(proj3, proj3, proj3, proj3, lb.reshape(HG_HEADS, 1, d), norm_g.reshape(HG_HEADS, 1, d),
      jnp.asarray(mstack, BF16), jnp.asarray(masks, F32))


LRU_PAD = SUBLANES
LRU_STEP = 16


def _lru_kernel(x_ref, y_ref, cw_ref, cb_ref, wa_ref, ba_ref, wx_ref, bx_ref, c_ref, o_ref,
                xpad_ref, h_ref, a_ref, u_ref, *, tt):
    W = LRU_WIDTH
    bw = W // LRU_BLOCKS

    @pl.when(pl.program_id(1) == 0)
    def _():
        xpad_ref[0:LRU_PAD, :] = jnp.zeros((LRU_PAD, W), F32)
        h_ref[...] = jnp.zeros_like(h_ref)

    x = x_ref[0]
    xpad_ref[LRU_PAD:LRU_PAD + tt, :] = x
    cw = cw_ref[...]
    xc = cb_ref[...] + cw[LRU_CONV - 1:LRU_CONV, :] * x
    for j in range(LRU_CONV - 1):
        off = LRU_PAD - (LRU_CONV - 1) + j
        xc = xc + cw[j:j + 1, :] * xpad_ref[off:off + tt, :]
    xpad_ref[0:LRU_PAD, :] = x[tt - LRU_PAD:tt, :]

    rs, is_ = [], []
    for gi in range(LRU_BLOCKS):
        xg = xc[:, gi * bw:(gi + 1) * bw].astype(BF16)
        rs.append(_dot(xg, wa_ref[gi]))
        is_.append(_dot(xg, wx_ref[gi]))
    r = jax.nn.sigmoid(jnp.concatenate(rs, -1) + ba_ref[...])
    ig = jax.nn.sigmoid(jnp.concatenate(is_, -1) + bx_ref[...])
    log_a = c_ref[...] * r
    a = jnp.exp(log_a)
    u = jnp.sqrt(-jnp.tanh(log_a) * (a * a + 1.0)) * (ig * xc)

    sub = lax.broadcasted_iota(jnp.int32, (tt, W), 0) % SUBLANES
    for dlt in (1, 2, 4):
        keep = sub >= dlt
        a_sh = jnp.where(keep, pltpu.roll(a, dlt, 0), 1.0)
        u_sh = jnp.where(keep, pltpu.roll(u, dlt, 0), 0.0)
        u = a * u_sh + u
        a = a * a_sh
    a_ref[...] = a
    u_ref[...] = u

    def step(i, h):
        r0 = pl.multiple_of(i * LRU_STEP, LRU_STEP)
        outs = []
        for k in range(LRU_STEP // SUBLANES):
            rows = pl.ds(r0 + k * SUBLANES, SUBLANES)
            hh = u_ref[rows, :] + a_ref[rows, :] * h
            outs.append(hh)
            h = hh[SUBLANES - 1:SUBLANES, :]
        rows = pl.ds(r0, LRU_STEP)
        o_ref[0, rows, :] = (jnp.concatenate(outs, 0) * jax.nn.gelu(y_ref[0, rows, :])).astype(o_ref.dtype)
        return h

    h_ref[...] = lax.fori_loop(0, tt // LRU_STEP, step, h_ref[...])


def _lru_call(proj3, conv_w, conv_b, wa, ba, wx, bx, lam, tt=512):
    B, S, _ = proj3.shape
    W = LRU_WIDTH
    bw = W // LRU_BLOCKS
    tt = min(tt, S)
    c = (-LRU_C * jax.nn.softplus(-lam.astype(F32))).reshape(1, W)
    vec = lambda: pl.BlockSpec((1, W), lambda b, t: (0, 0))
    return pl.pallas_call(
        functools.partial(_lru_kernel, tt=tt),
        grid=(B, S // tt),
        in_specs=[pl.BlockSpec((1, tt, W), lambda b, t: (b, t, OFF_LX // W)),
                  pl.BlockSpec((1, tt, W), lambda b, t: (b, t, OFF_LY // W)),
                  pl.BlockSpec((LRU_CONV, W), lambda b, t: (0, 0)), vec(),
                  pl.BlockSpec((LRU_BLOCKS, bw, bw), lambda b, t: (0, 0, 0)), vec(),
                  pl.BlockSpec((LRU_BLOCKS, bw, bw), lambda b, t: (0, 0, 0)), vec(), vec()],
        out_specs=pl.BlockSpec((1, tt, W), lambda b, t: (b, t, 0)),
        out_shape=jax.ShapeDtypeStruct((B, S, W), BF16),
        scratch_shapes=[pltpu.VMEM((tt + LRU_PAD, W), F32), pltpu.VMEM((1, W), F32),
                        pltpu.VMEM((tt, W), F32), pltpu.VMEM((tt, W), F32)],
        compiler_params=_params("parallel", "arbitrary"),
        name="rglru",
    )(proj3, proj3, conv_w.astype(F32), conv_b.reshape(1, W).astype(F32), wa.astype(BF16),
      ba.reshape(1, W), wx.astype(BF16), bx.reshape(1, W), c)


NSA_AUG = 4
NSA_KW = LANES
SEL_TILE = 256
SEL_TILE_BLOCKS = SEL_TILE // SEL_LEN
WIN_TILE = 128
POS_SPLIT = 128
SEL_ONEHOT = LANES
PLAN_HALF = 16
SELECT_PARTS = 4
SELECT_QB = 2
NSA_VROWS = NSA_HEAD_DIM + 16
KV_PREP_ROWS = 512


def _key_aug(pos, width, dead=False):
    n = pos.shape[0]
    col = lax.broadcasted_iota(jnp.int32, (n, width), 1)
    hi = ((pos // POS_SPLIT) * POS_SPLIT).astype(F32)
    lo = (pos % POS_SPLIT).astype(F32)
    live = jnp.where(col == 0, hi, jnp.where(col == 1, lo, jnp.where(col < NSA_AUG, 1.0, 0.0)))
    return jnp.where(dead, jnp.where(col == NSA_AUG, 1.0, 0.0), live)


def _query_t(q_ref, slope_ref, c0, row0=0):
    hd, P, T = NSA_HEAD_DIM, NSA_PER_GROUP, Q_BLOCK
    PT = P * T
    qT = (q_ref[0, row0:row0 + T, :] * (hd ** -0.5)).T
    qpart = jnp.concatenate([qT[p * hd:(p + 1) * hd, :] for p in range(P)], 1)
    tlane = c0 + lax.broadcasted_iota(jnp.int32, (1, PT), 1) % T
    slope = slope_ref[0]
    t_hi = ((tlane // POS_SPLIT) * POS_SPLIT).astype(F32)
    t_lo = (tlane % POS_SPLIT).astype(F32)
    rowi = lax.broadcasted_iota(jnp.int32, (NSA_KW - hd, PT), 0)
    aug = jnp.where(rowi < 2, slope,
                    jnp.where(rowi == 2, -slope * t_hi,
                              jnp.where(rowi == 3, -slope * t_lo, jnp.where(rowi == NSA_AUG, NEG_BIG, 0.0))))
    return jnp.concatenate([qpart, aug], 0).astype(BF16), tlane


def _compress_kernel(xk_ref, xv_ref, w1k_ref, pek_ref, w1fk_ref, w2k_ref, w1v_ref, pev_ref, w1fv_ref, w2v_ref,
                     kc_ref, vct_ref, *, nc):
    hd = NSA_HEAD_DIM
    pos = lax.broadcasted_iota(jnp.int32, (nc, 1), 0) * CMP_STRIDE + (CMP_LEN - 1)
    aug = _key_aug(pos, NSA_KW - hd)

    def one(x_ref, w1_ref, pe_ref, w1f_ref, w2_ref, g):
        uv = jnp.zeros((nc, 2 * CMP_HIDDEN), F32)
        for r in range(CMP_STRIDE):
            xr = x_ref[0, pl.ds(r, nc, stride=CMP_STRIDE), :][:, g * hd:(g + 1) * hd]
            uv = uv + _dot_hilo(xr, w1_ref[r])
        cvec = _dot_hilo(pe_ref[...], w1f_ref[...])[0:1, :]
        hid = uv[:, 0:CMP_HIDDEN] + pltpu.roll(uv[:, CMP_HIDDEN:], nc - 1, 0) + cvec
        return _dot_hilo(jax.nn.gelu(hid), w2_ref[...])

    for g in range(NSA_GROUPS):
        kc = one(xk_ref, w1k_ref, pek_ref, w1fk_ref, w2k_ref, g)
        kc_ref[0, g] = jnp.concatenate([kc, aug], 1).astype(BF16)
        vct_ref[0, g] = one(xv_ref, w1v_ref, pev_ref, w1fv_ref, w2v_ref, g).T.astype(BF16)


def _compress_call(proj3, pe_k, w1_k, w2_k, pe_v, w1_v, w2_v):
    B, S, _ = proj3.shape
    G, hd = NSA_GROUPS, NSA_HEAD_DIM
    nc = S // CMP_STRIDE
    half = CMP_STRIDE * hd

    def prep(pe, w1):
        w1 = w1.astype(F32)
        wr = jnp.concatenate([w1[:half].reshape(CMP_STRIDE, hd, CMP_HIDDEN),
                              w1[half:].reshape(CMP_STRIDE, hd, CMP_HIDDEN)], -1)
        pe8 = jnp.broadcast_to(pe.reshape(1, CMP_LEN * hd).astype(F32), (SUBLANES, CMP_LEN * hd))
        return wr, pe8, w1

    full = lambda a: pl.BlockSpec(a.shape, lambda b: tuple(0 for _ in a.shape))
    col = lambda off: pl.BlockSpec((1, S, LANES), lambda b: (b, 0, off // LANES))
    args = prep(pe_k, w1_k) + (w2_k.astype(F32),) + prep(pe_v, w1_v) + (w2_v.astype(F32),)
    return pl.pallas_call(
        functools.partial(_compress_kernel, nc=nc),
        grid=(B,),
        in_specs=[col(OFF_KC), col(OFF_VC)] + [full(a) for a in args],
        out_specs=[pl.BlockSpec((1, G, nc, NSA_KW), lambda b: (b, 0, 0, 0)),
                   pl.BlockSpec((1, G, hd, nc), lambda b: (b, 0, 0, 0))],
        out_shape=[jax.ShapeDtypeStruct((B, G, nc, NSA_KW), BF16),
                   jax.ShapeDtypeStruct((B, G, hd, nc), BF16)],
        compiler_params=_params("parallel"),
        name="nsa_compress",
    )(proj3, proj3, *args)


def _kvprep_kernel(ks_ref, vs_ref, kw_ref, vw_ref, ksa_ref, vst_ref, kwa_ref, vwt_ref, *, tt, n_live):
    hd = NSA_HEAD_DIM
    t = pl.program_id(1)
    dead = t >= n_live
    pos = t * tt + lax.broadcasted_iota(jnp.int32, (tt, 1), 0)
    aug = _key_aug(pos, NSA_KW - hd, dead)
    blk_col = lax.broadcasted_iota(jnp.int32, (tt, SEL_ONEHOT), 1)
    onehot = jnp.where(jnp.logical_and(blk_col == pos // SEL_LEN, jnp.logical_not(dead)), 1.0, 0.0)
    vrow = lax.broadcasted_iota(jnp.int32, (NSA_VROWS - hd, tt), 0)
    ones_rows = jnp.where(jnp.logical_and(vrow == 0, jnp.logical_not(dead)), 1.0, 0.0)
    for g in range(NSA_GROUPS):
        lanes = slice(g * hd, (g + 1) * hd)
        live = lambda ref: jnp.where(dead, 0.0, ref[0][:, lanes])
        ksa_ref[0, g] = jnp.concatenate([live(ks_ref), aug, onehot], 1).astype(BF16)
        kwa_ref[0, g] = jnp.concatenate([live(kw_ref), aug], 1).astype(BF16)
        vs_t = jnp.concatenate([live(vs_ref).T, ones_rows], 0).astype(BF16)
        vw_t = jnp.concatenate([live(vw_ref).T, ones_rows], 0).astype(BF16)
        for c in range(tt // SEL_TILE):
            vst_ref[0, g, c] = vs_t[:, c * SEL_TILE:(c + 1) * SEL_TILE]
        for c in range(tt // WIN_TILE):
            vwt_ref[0, g, c] = vw_t[:, c * WIN_TILE:(c + 1) * WIN_TILE]


def _kvprep_call(proj3):
    B, S, _ = proj3.shape
    G, hd = NSA_GROUPS, NSA_HEAD_DIM
    tt = min(KV_PREP_ROWS, S)
    n_live = S // tt
    assert tt == WINDOW
    sp = S + tt
    col = lambda off: pl.BlockSpec((1, tt, LANES), lambda b, t: (b, jnp.minimum(t, n_live - 1), off // LANES))
    front = lambda t: (t + 1) % (n_live + 1)
    return pl.pallas_call(
        functools.partial(_kvprep_kernel, tt=tt, n_live=n_live),
        grid=(B, n_live + 1),
        in_specs=[col(OFF_KS), col(OFF_VS), col(OFF_KW), col(OFF_VW)],
        out_specs=[pl.BlockSpec((1, G, tt, NSA_KW + SEL_ONEHOT), lambda b, t: (b, 0, t, 0)),
                   pl.BlockSpec((1, G, tt // SEL_TILE, NSA_VROWS, SEL_TILE), lambda b, t: (b, 0, t, 0, 0)),
                   pl.BlockSpec((1, G, tt, NSA_KW), lambda b, t: (b, 0, front(t), 0)),
                   pl.BlockSpec((1, G, tt // WIN_TILE, NSA_VROWS, WIN_TILE), lambda b, t: (b, 0, front(t), 0, 0))],
        out_shape=[jax.ShapeDtypeStruct((B, G, sp, NSA_KW + SEL_ONEHOT), BF16),
                   jax.ShapeDtypeStruct((B, G, sp // SEL_TILE, NSA_VROWS, SEL_TILE), BF16),
                   jax.ShapeDtypeStruct((B, G, sp, NSA_KW), BF16),
                   jax.ShapeDtypeStruct((B, G, sp // WIN_TILE, NSA_VROWS, WIN_TILE), BF16)],
        compiler_params=_params("parallel", "parallel"),
        name="nsa_kvprep",
    )(proj3, proj3, proj3, proj3)


def _importance_matrix(ns, nc):
    ratio = SEL_LEN // CMP_STRIDE
    a = np.zeros((ns, nc), np.float32)
    for j in range(ns):
        for n, wgt in ((ratio * j - 1, 0.5), (ratio * j, 1.0), (ratio * j + 1, 1.0),
                       (ratio * j + 2, 1.0), (ratio * j + 3, 0.5)):
            if 0 <= n < nc - 1:
                a[j, n] = wgt
    return a


def _plan_constants(ns):
    nt = ns // SEL_TILE_BLOCKS
    grp = (np.arange(ns)[None, :] // SEL_TILE_BLOCKS == np.arange(nt)[:, None]).astype(np.float32)
    k = np.arange(nt)
    w_lo = np.where(k < PLAN_HALF, 2.0 ** np.minimum(k, PLAN_HALF - 1), 0.0)
    w_hi = np.where(k >= PLAN_HALF, 2.0 ** np.maximum(k - PLAN_HALF, 0), 0.0)
    wts = np.stack([np.repeat(w_lo[:, None], LANES, 1), np.repeat(w_hi[:, None], LANES, 1)]).astype(np.float32)
    return grp, wts


def _select_kernel(q_ref, slope_ref, kc_ref, vct_ref, imp_ref, grp_ref, wts_ref, ocmp_ref, bias_ref, bits_ref,
                   *, ns, nc, qb):
    P, T = NSA_PER_GROUP, Q_BLOCK
    step = pl.program_id(2)
    nsteps = pl.num_programs(2)
    subs = range(qb)
    c0s = [(step * qb + u) * T for u in subs]
    qts, tlanes = zip(*[_query_t(q_ref, slope_ref, c0s[u], u * T) for u in subs])

    def body(rows, nblk):
        cend = lax.broadcasted_iota(jnp.int32, (rows, 1), 0) * CMP_STRIDE + (CMP_LEN - 1)
        cmask = [cend <= tlanes[u] for u in subs]
        s = [jnp.where(cmask[u], _dot(kc_ref[0, 0, 0:rows, :], qts[u]), NEG_BIG) for u in subs]
        m = [jnp.max(s[u], 0, keepdims=True) for u in subs]
        p_c = [jnp.where(cmask[u], jnp.exp(s[u] - m[u]), 0.0) for u in subs]
        p_c = [p_c[u] * (1.0 / jnp.maximum(jnp.sum(p_c[u], 0, keepdims=True), 1e-30)) for u in subs]
        for u in subs:
            ocmp_ref[0, 0, u] = _dot(vct_ref[0, 0][:, 0:rows], p_c[u].astype(BF16))

        psum = [p_c[u][:, 0:T] for u in subs]
        for p in range(1, P):
            psum = [psum[u] + p_c[u][:, p * T:(p + 1) * T] for u in subs]
        imp = [_dot_exact_lhs(imp_ref[0:nblk, 0:rows], psum[u]) for u in subs]
        blk = lax.broadcasted_iota(jnp.int32, (nblk, T), 0)
        val, chosen = [], []
        for u in subs:
            cur = (c0s[u] + lax.broadcasted_iota(jnp.int32, (nblk, T), 1)) // SEL_LEN
            forced = (blk == 0) | (blk == cur) | (blk == cur - 1)
            val.append(jnp.where(forced, 3e38, jnp.where(blk > cur, -1.0, imp[u])))
            chosen.append(jnp.zeros((nblk, T), F32))
        for _ in range(min(SEL_TOPK, nblk)):
            for u in subs:
                mx = jnp.max(val[u], 0, keepdims=True)
                first = jnp.min(jnp.where(val[u] == mx, blk, ns), 0, keepdims=True)
                pick = blk == first
                chosen[u] = jnp.where(pick, 1.0, chosen[u])
                val[u] = jnp.where(pick, -2.0, val[u])
        for u in subs:
            bias_ref[0, 0, u, 0:nblk, :] = jnp.where(chosen[u] > 0.5, 0.0, NEG_BIG)
            if nblk < ns:
                bias_ref[0, 0, u, nblk:ns, :] = jnp.full((ns - nblk, T), NEG_BIG, F32)
            ch = chosen[u]
            if nblk < ns:
                ch = jnp.concatenate([ch, jnp.zeros((ns - nblk, T), F32)], 0)
            cnt = _dot(grp_ref[...], ch.astype(BF16))
            act = jnp.where(jnp.max(cnt, 1, keepdims=True) > 0.5, 1.0, 0.0)
            lo = jnp.sum(act * wts_ref[0], 0, keepdims=True).astype(jnp.int32)
            hi = jnp.sum(act * wts_ref[1], 0, keepdims=True).astype(jnp.int32)
            bits_ref[u] = jnp.broadcast_to(lo | (hi << PLAN_HALF), (SUBLANES, LANES))

    parts = max(p for p in (SELECT_PARTS, 2, 1) if nc % (p * LANES) == 0 and ns % (p * 2 * SUBLANES) == 0)
    part = step * parts // nsteps
    for v in range(parts):
        pl.when(part == v)(functools.partial(body, nc * (v + 1) // parts, ns * (v + 1) // parts))


def _attend_kernel(bits_sref, q_ref, gt_ref, slope_ref, ocmp_ref, bias_ref, ks_ref, vst_ref, kw_ref, vwt_ref,
                   o_ref, list_ref, s_ref, m_ref, *, n_tiles_total, ns):
    hd, P, T = NSA_HEAD_DIM, NSA_PER_GROUP, Q_BLOCK
    PT = P * T
    b, g, iq = pl.program_id(0), pl.program_id(1), pl.program_id(2)
    c0 = iq * T
    qt, tlane = _query_t(q_ref, slope_ref, c0)
    tl_row = lax.broadcasted_iota(jnp.int32, (1, PT), 1) % T

    gt = jax.nn.sigmoid(gt_ref[0]).T

    def gate_row(branch):
        return jnp.concatenate([gt[p * 3 + branch:p * 3 + branch + 1, :] for p in range(P)], 1)

    bits = bits_sref[(b * pl.num_programs(1) + g) * pl.num_programs(2) + iq]
    last_tile = (c0 + T - 1) // SEL_TILE

    prev_bits = bits & (lax.shift_left(jnp.int32(1), last_tile) - 1)
    n_prev = jnp.int32(0)
    for k in range(n_tiles_total):
        list_ref[n_prev] = k
        n_prev = n_prev + (lax.shift_right_logical(prev_bits, k) & 1)

    sel_b = bias_ref[0, 0, 0]
    if ns < SEL_ONEHOT:
        sel_b = jnp.concatenate([sel_b, jnp.zeros((SEL_ONEHOT - ns, T), F32)], 0)
    qt_sel = jnp.concatenate([qt, jnp.concatenate([sel_b.astype(BF16)] * P, 1)], 0)

    def tile_scores(kt):
        k0 = pl.multiple_of(kt * SEL_TILE, SEL_TILE)
        return _dot(ks_ref[0, 0, pl.ds(k0, SEL_TILE), :], qt_sel)

    def col_max(scores):
        m = jnp.max(scores[0], 0, keepdims=True)
        for sc in scores[1:]:
            m = jnp.maximum(m, jnp.max(sc, 0, keepdims=True))
        return m

    def softmax_update(carry, scores, vts, m_tile):
        m_run, acc = carry
        m_new = jnp.maximum(m_run, m_tile)
        acc = jnp.exp(m_run - m_new) * acc
        for sc, vt in zip(scores, vts):
            acc = acc + _dot(vt, jnp.exp(sc - m_new).astype(BF16))
        return m_new, acc

    init = (jnp.full((1, PT), NEG_BIG, F32), jnp.zeros((NSA_VROWS, PT), F32))
    dead_tile = n_tiles_total
    n_pairs = jnp.maximum((n_prev + 1) // 2, 1)

    def pair_tiles(i):
        has_a, has_b = 2 * i < n_prev, 2 * i + 1 < n_prev
        ka = jnp.where(has_a, list_ref[jnp.where(has_a, 2 * i, 0)], dead_tile)
        kb = jnp.where(has_b, list_ref[jnp.where(has_b, 2 * i + 1, 0)], dead_tile)
        return ka, kb

    def pair_scores(i):
        ka, kb = pair_tiles(i)
        keys = jnp.concatenate([ks_ref[0, 0, pl.ds(pl.multiple_of(k * SEL_TILE, SEL_TILE), SEL_TILE), :]
                                for k in (ka, kb)], 0)
        return _dot(keys, qt_sel)

    def put_scores(i, sc):
        slot = i % 2
        s_ref[slot] = sc
        m_ref[slot] = col_max([sc])

    def pair_probs(i, m_run):
        slot = i % 2
        m_new = jnp.maximum(m_run, m_ref[slot])
        return m_new, jnp.exp(s_ref[slot] - m_new).astype(BF16)

    def add_values(i, carry, m_new, pr):
        ka, kb = pair_tiles(i)
        m_run, acc = carry
        vt = jnp.concatenate([vst_ref[0, 0, ka], vst_ref[0, 0, kb]], 1)
        return m_new, jnp.exp(m_run - m_new) * acc + _dot(vt, pr)

    put_scores(0, pair_scores(0))

    n_win = (WINDOW + T) // WIN_TILE
    roff = lax.broadcasted_iota(jnp.int32, (WIN_TILE, 1), 0)
    w0 = pl.multiple_of(c0, WIN_TILE)
    sc = _dot(kw_ref[0, 0, pl.ds(w0, WINDOW + T), :], qt)
    scores = [jnp.where(roff > tl_row, sc[0:WIN_TILE], NEG_BIG),
              sc[WIN_TILE:WINDOW],
              jnp.where(roff <= tl_row, sc[WINDOW:WINDOW + T], NEG_BIG)]
    sc = jnp.concatenate(scores, 0)
    vt = jnp.concatenate([vwt_ref[0, 0, iq + j] for j in range(n_win)], 1)
    _, acc_win = softmax_update(init, [sc], [vt], col_max(scores))
    out_cw = gate_row(0) * ocmp_ref[0, 0, 0] + gate_row(2) * (acc_win[0:hd] / acc_win[hd:hd + 1])

    def pipe_step(i, carry):
        m_new, pr = pair_probs(i, carry[0])
        sc_next = pair_scores(i + 1)
        carry = add_values(i, carry, m_new, pr)
        put_scores(i + 1, sc_next)
        return carry

    carry = lax.fori_loop(0, n_pairs - 1, pipe_step, init)
    m_new, pr = pair_probs(n_pairs - 1, carry[0])
    kpos = last_tile * SEL_TILE + lax.broadcasted_iota(jnp.int32, (SEL_TILE, 1), 0)
    s_last = jnp.where(kpos <= tlane, tile_scores(last_tile), NEG_BIG)
    carry = add_values(n_pairs - 1, carry, m_new, pr)
    _, acc_sel = softmax_update(carry, [s_last], [vst_ref[0, 0, last_tile]], col_max([s_last]))
    out = out_cw + gate_row(1) * (acc_sel[0:hd] / acc_sel[hd:hd + 1])

    o_ref[0] = jnp.concatenate([out[:, p * T:(p + 1) * T] for p in range(P)], 0).T.astype(o_ref.dtype)


def _nsa_call(proj3, pe_k, w1_k, w2_k, pe_v, w1_v, w2_v):
    B, S, _ = proj3.shape
    G, hd, P, T = NSA_GROUPS, NSA_HEAD_DIM, NSA_PER_GROUP, Q_BLOCK
    PT = P * T
    nc = S // CMP_STRIDE
    ns = S // SEL_LEN
    nq = S // T
    nt = ns // SEL_TILE_BLOCKS
    assert nt <= 2 * PLAN_HALF and S % KV_PREP_ROWS == 0

    kc_a, vc_t = _compress_call(proj3, pe_k, w1_k, w2_k, pe_v, w1_v, w2_v)
    ks_a, vs_t, kw_a, vw_t = _kvprep_call(proj3)
    imp_m = jnp.asarray(_importance_matrix(ns, nc), BF16)
    grp, wts = _plan_constants(ns)
    head_slopes = 2.0 ** (-(8.0 / NSA_HEADS) * np.arange(1, NSA_HEADS + 1, dtype=np.float32))
    slopes = jnp.asarray(np.repeat(head_slopes.reshape(G, 1, P), T, axis=-1), F32)

    q_spec = lambda im: pl.BlockSpec((1, T, P * hd), im)
    bg = lambda shp: pl.BlockSpec((1, 1) + shp, lambda b, g, i: (b, g) + tuple(0 for _ in shp))
    const = lambda a: pl.BlockSpec(a.shape, lambda b, g, i: tuple(0 for _ in a.shape))
    qb = SELECT_QB if nq % (SELECT_QB * SELECT_PARTS) == 0 else 1
    nsteps = nq // qb
    ocmp, sel_bias, bits = pl.pallas_call(
        functools.partial(_select_kernel, ns=ns, nc=nc, qb=qb),
        grid=(B, G, nsteps),
        in_specs=[pl.BlockSpec((1, qb * T, P * hd), lambda b, g, i: (b, i, OFF_NQ // (P * hd) + g)),
                  pl.BlockSpec((1, 1, PT), lambda b, g, i: (g, 0, 0)),
                  bg((nc, NSA_KW)), bg((hd, nc)), const(imp_m),
                  pl.BlockSpec(grp.shape, lambda b, g, i: (0, 0)),
                  pl.BlockSpec(wts.shape, lambda b, g, i: (0, 0, 0))],
        out_specs=[pl.BlockSpec((1, 1, qb, hd, PT), lambda b, g, i: (b, g, i, 0, 0)),
                   pl.BlockSpec((1, 1, qb, ns, T), lambda b, g, i: (b, g, i, 0, 0)),
                   pl.BlockSpec((qb, SUBLANES, LANES), lambda b, g, i: ((b * G + g) * nsteps + i, 0, 0))],
        out_shape=[jax.ShapeDtypeStruct((B, G, nq, hd, PT), F32),
                   jax.ShapeDtypeStruct((B, G, nq, ns, T), F32),
                   jax.ShapeDtypeStruct((B * G * nq, SUBLANES, LANES), jnp.int32)],
        compiler_params=_params("parallel", "parallel", "parallel"),
        name="nsa_select",
    )(proj3, slopes, kc_a, vc_t, imp_m, jnp.asarray(grp, BF16), jnp.asarray(wts, F32))

    bg2 = lambda shp: pl.BlockSpec((1, 1) + shp, lambda b, g, i, s: (b, g) + tuple(0 for _ in shp))
    return pl.pallas_call(
        functools.partial(_attend_kernel, n_tiles_total=nt, ns=ns),
        grid_spec=pltpu.PrefetchScalarGridSpec(
            num_scalar_prefetch=1,
            grid=(B, G, nq),
            in_specs=[pl.BlockSpec((1, T, P * hd), lambda b, g, i, s: (b, i, OFF_NQ // (P * hd) + g)),
                      pl.BlockSpec((1, T, LANES), lambda b, g, i, s: (b, i, OFF_NG // LANES + g)),
                      pl.BlockSpec((1, 1, PT), lambda b, g, i, s: (g, 0, 0)),
                      pl.BlockSpec((1, 1, 1, hd, PT), lambda b, g, i, s: (b, g, i, 0, 0)),
                      pl.BlockSpec((1, 1, 1, ns, T), lambda b, g, i, s: (b, g, i, 0, 0)),
                      bg2(ks_a.shape[2:]), bg2(vs_t.shape[2:]), bg2(kw_a.shape[2:]), bg2(vw_t.shape[2:])],
            out_specs=pl.BlockSpec((1, T, P * hd), lambda b, g, i, s: (b, i, g)),
            scratch_shapes=[pltpu.SMEM((nt + 1,), jnp.int32),
                            pltpu.VMEM((2, 2 * SEL_TILE, PT), F32), pltpu.VMEM((2, 1, PT), F32)]),
        out_shape=jax.ShapeDtypeStruct((B, S, NSA_WIDTH), BF16),
        compiler_params=_params("parallel", "parallel", "arbitrary"),
        name="nsa_attend",
    )(bits[:, 0, 0], proj3, proj3, slopes, ocmp, sel_bias, ks_a, vs_t, kw_a, vw_t)


def _merge_kernel(x_ref, ya_ref, yb_ref, yc_ref, g0_ref, g1_ref, g2_ref, wa_ref, wb_ref, wc_ref, wo_ref,
                  lg_ref, lb_ref, o_ref):
    m = jax.nn.sigmoid(g0_ref[...]) * _dot(ya_ref[...], wa_ref[...])
    m = m + jax.nn.sigmoid(g1_ref[...]) * _dot(yb_ref[...], wb_ref[...])
    m = m + jax.nn.sigmoid(g2_ref[...]) * _dot(yc_ref[...], wc_ref[...])
    z = DEEPNORM_ALPHA * x_ref[...] + _dot(m.astype(BF16), wo_ref[...])
    o_ref[...] = _layer_norm_rows(z, lg_ref[...], lb_ref[...])


def _merge_call(x2, ya, yb, yc, proj2, w_hg, w_nsa, w_lru, w_out, ln_g, ln_b, tm=512):
    T, D = x2.shape
    tm = min(tm, T)
    rows = lambda w: pl.BlockSpec((tm, w), lambda i: (i, 0))
    gate = lambda n: pl.BlockSpec((tm, D), lambda i: (i, OFF_MG // D + n))
    full = lambda a: pl.BlockSpec(a.shape, lambda i: (0, 0))
    ws = [w.astype(BF16) for w in (w_hg, w_nsa, w_lru, w_out)]
    return pl.pallas_call(
        _merge_kernel,
        grid=(T // tm,),
        in_specs=[rows(D), rows(HG_WIDTH), rows(NSA_WIDTH), rows(LRU_WIDTH), gate(0), gate(1), gate(2)]
                 + [full(w) for w in ws] + [pl.BlockSpec((1, D), lambda i: (0, 0))] * 2,
        out_specs=rows(D),
        out_shape=jax.ShapeDtypeStruct((T, D), F32),
        compiler_params=_params("parallel"),
        name="merge_out",
    )(x2, ya, yb, yc, proj2, proj2, proj2, *ws, ln_g.reshape(1, D), ln_b.reshape(1, D))


FFN_SPLIT = 2
FFN_PAD = SUBLANES
FFN_SUB = 256


def _ffn_kernel(x_ref, wu_ref, wv_ref, cw_ref, cb_ref, wd_ref, lg_ref, lb_ref, o_ref,
                xb_ref, acc_ref, upad_ref, carry_ref, *, tm, tiles_per_seq):
    i = pl.program_id(0)
    j = pl.program_id(1)

    @pl.when(j == 0)
    def _():
        xb_ref[...] = x_ref[...].astype(BF16)

    xb = xb_ref[...]
    fc = wu_ref.shape[1]
    first = (i % tiles_per_seq) == 0

    @pl.when(first)
    def _():
        upad_ref[0:FFN_PAD, :] = jnp.zeros((FFN_PAD, fc), F32)

    @pl.when(jnp.logical_not(first))
    def _():
        upad_ref[0:FFN_PAD, :] = carry_ref[j]

    @pl.when(j == 0)
    def _():
        acc_ref[...] = jnp.zeros_like(acc_ref)

    chunks = [slice(c0, min(c0 + FFN_SUB, fc)) for c0 in range(0, fc, FFN_SUB)]

    up = lambda cs: (_dot(xb, wu_ref[:, cs]), _dot(xb, wv_ref[:, cs]))
    nxt = up(chunks[0])
    for ci, cs in enumerate(chunks):
        u, v = nxt
        if ci + 1 < len(chunks):
            nxt = up(chunks[ci + 1])
        upad_ref[FFN_PAD:FFN_PAD + tm, cs] = u
        cw = cw_ref[:, cs]
        cv = cb_ref[:, cs] + cw[FFN_CONV - 1:FFN_CONV, :] * u
        for k in range(FFN_CONV - 1):
            off = FFN_PAD - (FFN_CONV - 1) + k
            cv = cv + cw[k:k + 1, :] * upad_ref[off:off + tm, cs]
        carry_ref[j, :, cs] = u[tm - FFN_PAD:tm, :]
        h = (jax.nn.gelu(cv) * v).astype(BF16)
        acc_ref[...] += _dot(h, wd_ref[cs, :])

    @pl.when(j == FFN_SPLIT - 1)
    def _():
        z = DEEPNORM_ALPHA * x_ref[...] + acc_ref[...]
        o_ref[...] = _layer_norm_rows(z, lg_ref[...], lb_ref[...])


def _ffn_call(x2, seq_len, w_up, conv_w, conv_b, w_down, ln_g, ln_b, tm=512):
    T, D = x2.shape
    tm = min(tm, seq_len)
    fc = FFN_DIM // FFN_SPLIT
    wu = w_up.astype(BF16)
    return pl.pallas_call(
        functools.partial(_ffn_kernel, tm=tm, tiles_per_seq=seq_len // tm),
        grid=(T // tm, FFN_SPLIT),
        in_specs=[pl.BlockSpec((tm, D), lambda i, j: (i, 0)),
                  pl.BlockSpec((D, fc), lambda i, j: (0, j)),
                  pl.BlockSpec((D, fc), lambda i, j: (0, FFN_SPLIT + j)),
                  pl.BlockSpec((FFN_CONV, fc), lambda i, j: (0, j)),
                  pl.BlockSpec((1, fc), lambda i, j: (0, j)),
                  pl.BlockSpec((fc, D), lambda i, j: (j, 0)),
                  pl.BlockSpec((1, D), lambda i, j: (0, 0)),
                  pl.BlockSpec((1, D), lambda i, j: (0, 0))],
        out_specs=pl.BlockSpec((tm, D), lambda i, j: (i, 0)),
        out_shape=jax.ShapeDtypeStruct((T, D), F32),
        scratch_shapes=[pltpu.VMEM((tm, D), BF16), pltpu.VMEM((tm, D), F32),
                        pltpu.VMEM((tm + FFN_PAD, fc), F32), pltpu.VMEM((FFN_SPLIT, FFN_PAD, fc), F32)],
        compiler_params=_params("arbitrary", "arbitrary"),
        name="conv_ffn",
    )(x2, wu, wu, conv_w.astype(F32), conv_b.reshape(1, FFN_DIM).astype(F32), w_down.astype(BF16),
      ln_g.reshape(1, D), ln_b.reshape(1, D))


def _permute_in_proj(w, b):
    src = np.cumsum([0, HG_WIDTH, HG_WIDTH, HG_WIDTH, HG_WIDTH, NSA_WIDTH, NSA_KV, NSA_KV, NSA_KV, NSA_KV,
                     NSA_KV, NSA_KV, NSA_HEADS * 3, LRU_WIDTH, LRU_WIDTH, N_BRANCH * 1024])
    names = ["hq", "hf", "hi", "hg", "nq", "kc", "vc", "ks", "vs", "kw", "vw", "ng", "lx", "ly", "mg"]
    seg = {n: (int(src[k]), int(src[k + 1])) for k, n in enumerate(names)}
    order = ["mg", "hq", "hf", "hi", "hg", "nq", "lx", "ly", "kc", "vc", "ks", "vs", "kw", "vw"]
    per_group = NSA_PER_GROUP * 3
    ng0 = seg["ng"][0]
    w_parts = [w[:, seg[n][0]:seg[n][1]] for n in order]
    b_parts = [b[seg[n][0]:seg[n][1]] for n in order]
    for gi in range(NSA_GROUPS):
        lo = ng0 + gi * per_group
        w_parts += [w[:, lo:lo + per_group], jnp.zeros((w.shape[0], LANES - per_group), w.dtype)]
        b_parts += [b[lo:lo + per_group], jnp.zeros((LANES - per_group,), b.dtype)]
    wp = jnp.concatenate(w_parts, 1)
    bp = jnp.concatenate(b_parts)
    assert wp.shape[1] == PROJ_COLS
    return wp.astype(BF16), bp.astype(F32)


def kernel(x, ln_emb_g, ln_emb_b, w_in, b_in, hg_lb_logits, hg_norm_g, cmp_pe_k, cmp_w1_k, cmp_w2_k, cmp_pe_v, cmp_w1_v, cmp_w2_v, lru_conv_w, lru_conv_b, lru_wa, lru_ba, lru_wx, lru_bx, lru_lambda, w_branch_hg, w_branch_nsa, w_branch_lru, w_out, ln1_g, ln1_b, ffn_w_up, ffn_conv_w, ffn_conv_b, ffn_w_down, ln2_g, ln2_b):
    B, S, D = x.shape
    T = B * S
    gam = jax.nn.softmax(hg_lb_logits.astype(F32), axis=0)
    lb_all = jnp.cumsum(gam, axis=0) - gam[0]
    h = _layer_norm_call(x.reshape(T, D), ln_emb_g, ln_emb_b)
    for l in range(DEPTH):
        wp, bp = _permute_in_proj(w_in[l], b_in[l])
        proj2 = _inproj_call(h, wp, bp)
        proj3 = proj2.reshape(B, S, PROJ_COLS)
        y_a = _hgrn_call(proj3, lb_all[l], hg_norm_g[l])
        y_b = _nsa_call(proj3, cmp_pe_k[l], cmp_w1_k[l], cmp_w2_k[l], cmp_pe_v[l], cmp_w1_v[l], cmp_w2_v[l])
        y_c = _lru_call(proj3, lru_conv_w[l], lru_conv_b[l], lru_wa[l], lru_ba[l], lru_wx[l], lru_bx[l],
                        lru_lambda[l])
        h = _merge_call(h, y_a.reshape(T, HG_WIDTH), y_b.reshape(T, NSA_WIDTH), y_c.reshape(T, LRU_WIDTH),
                        proj2, w_branch_hg[l], w_branch_nsa[l], w_branch_lru[l], w_out[l], ln1_g[l], ln1_b[l])
        h = _ffn_call(h, S, ffn_w_up[l], ffn_conv_w[l], ffn_conv_b[l], ffn_w_down[l], ln2_g[l], ln2_b[l])
    return h.reshape(B, S, D)
```

```python
import functools

import numpy as np
import jax
import jax.numpy as jnp
from jax import lax
from jax.experimental import pallas as pl
from jax.experimental.pallas import tpu as pltpu

F32 = jnp.float32
BF16 = jnp.bfloat16

DEPTH = 2
HG_HEADS = 4
HG_HEAD_DIM = 128
HG_WIDTH = HG_HEADS * HG_HEAD_DIM
NSA_HEADS = 8
NSA_GROUPS = 2
NSA_PER_GROUP = NSA_HEADS // NSA_GROUPS
NSA_HEAD_DIM = 64
NSA_WIDTH = NSA_HEADS * NSA_HEAD_DIM
NSA_KV = NSA_GROUPS * NSA_HEAD_DIM
CMP_LEN = 32
CMP_STRIDE = 16
CMP_HIDDEN = 64
SEL_LEN = 64
SEL_TOPK = 16
WINDOW = 512
Q_BLOCK = 128
LRU_WIDTH = 512
LRU_BLOCKS = 4
LRU_CONV = 4
LRU_C = 8.0
FFN_DIM = 2816
FFN_CONV = 3
DEEPNORM_ALPHA = (2 * DEPTH) ** 0.25
LN_EPS = 1e-5
N_BRANCH = 3

LANES = 128
SUBLANES = 8
VMEM_LIMIT_BYTES = 52 * 1024 * 1024

OFF_MG = 0
OFF_HQ = N_BRANCH * 1024
OFF_HF = OFF_HQ + HG_WIDTH
OFF_HI = OFF_HF + HG_WIDTH
OFF_HG = OFF_HI + HG_WIDTH
OFF_NQ = OFF_HG + HG_WIDTH
OFF_LX = OFF_NQ + NSA_WIDTH
OFF_LY = OFF_LX + LRU_WIDTH
OFF_KC = OFF_LY + LRU_WIDTH
OFF_VC = OFF_KC + NSA_KV
OFF_KS = OFF_VC + NSA_KV
OFF_VS = OFF_KS + NSA_KV
OFF_KW = OFF_VS + NSA_KV
OFF_VW = OFF_KW + NSA_KV
OFF_NG = OFF_VW + NSA_KV
PROJ_TILE_N = 1536
PROJ_COLS = 5 * PROJ_TILE_N

NEG_BIG = -1e30


def _split3(x):
    hi = x.astype(BF16)
    r1 = x - hi.astype(F32)
    mid = r1.astype(BF16)
    lo = (r1 - mid.astype(F32)).astype(BF16)
    return hi, mid, lo


def _dot(a, b):
    return jnp.dot(a, b, preferred_element_type=F32)


def _dot_nt(a, b):
    return lax.dot_general(a, b, (((1,), (1,)), ((), ())), preferred_element_type=F32)


def _dot_exact_lhs(m_bf16, x):
    hi, mid, lo = _split3(x)
    return _dot(m_bf16, hi) + _dot(m_bf16, mid) + _dot(m_bf16, lo)


def _dot_hilo(a, b):
    ah = a.astype(BF16)
    al = (a - ah.astype(F32)).astype(BF16)
    bh = b.astype(BF16)
    bl = (b - bh.astype(F32)).astype(BF16)
    return _dot(ah, bh) + _dot(ah, bl) + _dot(al, bh) + _dot(al, bl)


def _layer_norm_rows(z, g, b):
    mu = jnp.mean(z, -1, keepdims=True)
    zc = z - mu
    var = jnp.mean(zc * zc, -1, keepdims=True)
    return zc * lax.rsqrt(var + LN_EPS) * g + b


def _params(*sem):
    return pltpu.CompilerParams(dimension_semantics=sem, vmem_limit_bytes=VMEM_LIMIT_BYTES)


def _ln_kernel(x_ref, g_ref, b_ref, o_ref):
    o_ref[...] = _layer_norm_rows(x_ref[...], g_ref[...], b_ref[...])


def _layer_norm_call(x2, g, b, tm=512):
    T, D = x2.shape
    return pl.pallas_call(
        _ln_kernel,
        grid=(T // tm,),
        in_specs=[pl.BlockSpec((tm, D), lambda i: (i, 0)),
                  pl.BlockSpec((1, D), lambda i: (0, 0)),
                  pl.BlockSpec((1, D), lambda i: (0, 0))],
        out_specs=pl.BlockSpec((tm, D), lambda i: (i, 0)),
        out_shape=jax.ShapeDtypeStruct((T, D), F32),
        compiler_params=_params("parallel"),
        name="embed_ln",
    )(x2, g.reshape(1, D), b.reshape(1, D))


def _inproj_kernel(x_ref, w_ref, b_ref, o_ref, xb_ref):
    @pl.when(pl.program_id(1) == 0)
    def _():
        xb_ref[...] = x_ref[...].astype(BF16)

    o_ref[...] = _dot(xb_ref[...], w_ref[...]) + b_ref[...]


def _inproj_call(x2, w_bf16, bias, tm=1024):
    T, D = x2.shape
    N = w_bf16.shape[1]
    tm = min(tm, T)
    tn = PROJ_TILE_N
    return pl.pallas_call(
        _inproj_kernel,
        grid=(T // tm, N // tn),
        in_specs=[pl.BlockSpec((tm, D), lambda i, j: (i, 0)),
                  pl.BlockSpec((D, tn), lambda i, j: (0, j)),
                  pl.BlockSpec((1, tn), lambda i, j: (0, j))],
        out_specs=pl.BlockSpec((tm, tn), lambda i, j: (i, j)),
        out_shape=jax.ShapeDtypeStruct((T, N), F32),
        scratch_shapes=[pltpu.VMEM((tm, D), BF16)],
        compiler_params=_params("parallel", "arbitrary"),
        name="in_proj",
    )(x2, w_bf16, bias.reshape(1, N))


HG_CHUNK = 128
HG_DIAG = SUBLANES


def _hgrn_constants(C):
    r = np.arange(C)
    tri = (r[None, :] <= r[:, None]).astype(np.float32)
    mats = [tri]
    masks = []
    w = HG_DIAG
    while 2 * w <= C:
        mid = (r // (2 * w)) * (2 * w) + w
        mats.append(tri - (r[None, :] <= mid[:, None]).astype(np.float32))
        same = (r[:, None] // (2 * w)) == (r[None, :] // (2 * w))
        masks.append((same & ((r[:, None] % (2 * w)) >= w) & ((r[None, :] % (2 * w)) < w)).astype(np.float32))
        w *= 2
    mats.append((r[None, :] > r[:, None]).astype(np.float32))
    return np.concatenate(mats, 0), np.stack(masks, 0)


def _hgrn_kernel(q_ref, f_ref, i_ref, g_ref, lb_ref, ng_ref, m_ref, mask_ref, o_ref, st_ref, *, C, nchunk):
    d = HG_HEAD_DIM
    nlev = mask_ref.shape[0]

    @pl.when(pl.program_id(2) == 0)
    def _():
        st_ref[...] = jnp.zeros_like(st_ref)

    lb = lb_ref[0]
    ng = ng_ref[0]
    nb = C // HG_DIAG
    row3 = lax.broadcasted_iota(jnp.int32, (nb, HG_DIAG, d), 1)

    fl_all = f_ref[0]
    lf_all = jnp.log(lb + (1.0 - lb) * jax.nn.sigmoid(fl_all))
    kk_all = (1.0 - lb) * jax.nn.sigmoid(-fl_all)
    allm_all = _dot_exact_lhs(m_ref[...], jnp.concatenate([lf_all[c * C:(c + 1) * C] for c in range(nchunk)], 1))

    for c in range(nchunk):
        sl = pl.ds(c * C, C)
        q = q_ref[0, sl, :]
        iv = i_ref[0, sl, :]
        g = g_ref[0, sl, :]
        kk = kk_all[c * C:(c + 1) * C]
        allm = allm_all[:, c * d:(c + 1) * d]
        bcs = allm[0:C]
        rem = allm[(nlev + 1) * C:(nlev + 2) * C]
        b_last = bcs[C - 1:C, :]
        iv_b = iv.astype(BF16)

        att = jnp.zeros((C, C), F32)
        for l in range(nlev):
            e = jnp.exp(-jnp.abs(allm[(1 + l) * C:(2 + l) * C]))
            att = att + mask_ref[l] * _dot_nt((q * e).astype(BF16), (kk * e).astype(BF16))
        o = _dot(att.astype(BF16), iv_b)

        q3 = q.reshape(nb, HG_DIAG, d)
        k3 = kk.reshape(nb, HG_DIAG, d)
        b3 = bcs.reshape(nb, HG_DIAG, d)
        i3 = iv.reshape(nb, HG_DIAG, d)
        acc = jnp.zeros((nb, HG_DIAG, d), F32)
        for s in range(HG_DIAG):
            dec = jnp.where(row3 >= s, jnp.exp(b3 - b3[:, s:s + 1, :]), 0.0)
            a = jnp.sum(q3 * dec * k3[:, s:s + 1, :], axis=-1, keepdims=True)
            acc = acc + a * i3[:, s:s + 1, :]
        o = o + acc.reshape(C, d)

        st = st_ref[...]
        o = o + _dot_nt((q * jnp.exp(bcs)).astype(BF16), st.astype(BF16))
        kdec = (kk * jnp.exp(rem)).astype(BF16)
        st_ref[...] = st * jnp.exp(b_last) + _dot(iv.T.astype(BF16), kdec)

        o = o * lax.rsqrt(jnp.mean(o * o, -1, keepdims=True) + 1e-6) * ng
        o_ref[0, sl, :] = (o * (g * jax.nn.sigmoid(g))).astype(o_ref.dtype)


def _hgrn_call(proj3, lb, norm_g, tt=512):
    B, S, _ = proj3.shape
    d = HG_HEAD_DIM
    C = HG_CHUNK
    tt = min(tt, S)
    mstack, masks = _hgrn_constants(C)
    nm = mstack.shape[0]
    col = lambda off: (lambda b, h, t: (b, t, off // d + h))
    blk = (1, tt, d)
    return pl.pallas_call(
        functools.partial(_hgrn_kernel, C=C, nchunk=tt // C),
        grid=(B, HG_HEADS, S // tt),
        in_specs=[pl.BlockSpec(blk, col(OFF_HQ)), pl.BlockSpec(blk, col(OFF_HF)),
                  pl.BlockSpec(blk, col(OFF_HI)), pl.BlockSpec(blk, col(OFF_HG)),
                  pl.BlockSpec((1, 1, d), lambda b, h, t: (h, 0, 0)),
                  pl.BlockSpec((1, 1, d), lambda b, h, t: (h, 0, 0)),
                  pl.BlockSpec((nm, C), lambda b, h, t: (0, 0)),
                  pl.BlockSpec(masks.shape, lambda b, h, t: (0, 0, 0))],
        out_specs=pl.BlockSpec(blk, lambda b, h, t: (b, t, h)),
        out_shape=jax.ShapeDtypeStruct((B, S, HG_WIDTH), BF16),
        scratch_shapes=[pltpu.VMEM((d, d), F32)],
        compiler_params=_params("parallel", "parallel", "arbitrary"),
        name="hgrn2",
    )(proj3, proj3, proj3, proj3, lb.reshape(HG_HEADS, 1, d), norm_g.reshape(HG_HEADS, 1, d),
      jnp.asarray(mstack, BF16), jnp.asarray(masks, F32))


LRU_PAD = SUBLANES
LRU_STEP = 16


def _lru_kernel(x_ref, y_ref, cw_ref, cb_ref, wa_ref, ba_ref, wx_ref, bx_ref, c_ref, o_ref,
                xpad_ref, h_ref, a_ref, u_ref, *, tt):
    W = LRU_WIDTH
    bw = W // LRU_BLOCKS

    @pl.when(pl.program_id(1) == 0)
    def _():
        xpad_ref[0:LRU_PAD, :] = jnp.zeros((LRU_PAD, W), F32)
        h_ref[...] = jnp.zeros_like(h_ref)

    x = x_ref[0]
    xpad_ref[LRU_PAD:LRU_PAD + tt, :] = x
    cw = cw_ref[...]
    xc = cb_ref[...] + cw[LRU_CONV - 1:LRU_CONV, :] * x
    for j in range(LRU_CONV - 1):
        off = LRU_PAD - (LRU_CONV - 1) + j
        xc = xc + cw[j:j + 1, :] * xpad_ref[off:off + tt, :]
    xpad_ref[0:LRU_PAD, :] = x[tt - LRU_PAD:tt, :]

    rs, is_ = [], []
    for gi in range(LRU_BLOCKS):
        xg = xc[:, gi * bw:(gi + 1) * bw].astype(BF16)
        rs.append(_dot(xg, wa_ref[gi]))
        is_.append(_dot(xg, wx_ref[gi]))
    r = jax.nn.sigmoid(jnp.concatenate(rs, -1) + ba_ref[...])
    ig = jax.nn.sigmoid(jnp.concatenate(is_, -1) + bx_ref[...])
    log_a = c_ref[...] * r
    a = jnp.exp(log_a)
    u = jnp.sqrt(-jnp.tanh(log_a) * (a * a + 1.0)) * (ig * xc)

    sub = lax.broadcasted_iota(jnp.int32, (tt, W), 0) % SUBLANES
    for dlt in (1, 2, 4):
        keep = sub >= dlt
        a_sh = jnp.where(keep, pltpu.roll(a, dlt, 0), 1.0)
        u_sh = jnp.where(keep, pltpu.roll(u, dlt, 0), 0.0)
        u = a * u_sh + u
        a = a * a_sh
    a_ref[...] = a
    u_ref[...] = u

    def step(i, h):
        r0 = pl.multiple_of(i * LRU_STEP, LRU_STEP)
        outs = []
        for k in range(LRU_STEP // SUBLANES):
            rows = pl.ds(r0 + k * SUBLANES, SUBLANES)
            hh = u_ref[rows, :] + a_ref[rows, :] * h
            outs.append(hh)
            h = hh[SUBLANES - 1:SUBLANES, :]
        rows = pl.ds(r0, LRU_STEP)
        o_ref[0, rows, :] = (jnp.concatenate(outs, 0) * jax.nn.gelu(y_ref[0, rows, :])).astype(o_ref.dtype)
        return h

    h_ref[...] = lax.fori_loop(0, tt // LRU_STEP, step, h_ref[...])


def _lru_call(proj3, conv_w, conv_b, wa, ba, wx, bx, lam, tt=512):
    B, S, _ = proj3.shape
    W = LRU_WIDTH
    bw = W // LRU_BLOCKS
    tt = min(tt, S)
    c = (-LRU_C * jax.nn.softplus(-lam.astype(F32))).reshape(1, W)
    vec = lambda: pl.BlockSpec((1, W), lambda b, t: (0, 0))
    return pl.pallas_call(
        functools.partial(_lru_kernel, tt=tt),
        grid=(B, S // tt),
        in_specs=[pl.BlockSpec((1, tt, W), lambda b, t: (b, t, OFF_LX // W)),
                  pl.BlockSpec((1, tt, W), lambda b, t: (b, t, OFF_LY // W)),
                  pl.BlockSpec((LRU_CONV, W), lambda b, t: (0, 0)), vec(),
                  pl.BlockSpec((LRU_BLOCKS, bw, bw), lambda b, t: (0, 0, 0)), vec(),
                  pl.BlockSpec((LRU_BLOCKS, bw, bw), lambda b, t: (0, 0, 0)), vec(), vec()],
        out_specs=pl.BlockSpec((1, tt, W), lambda b, t: (b, t, 0)),
        out_shape=jax.ShapeDtypeStruct((B, S, W), BF16),
        scratch_shapes=[pltpu.VMEM((tt + LRU_PAD, W), F32), pltpu.VMEM((1, W), F32),
                        pltpu.VMEM((tt, W), F32), pltpu.VMEM((tt, W), F32)],
        compiler_params=_params("parallel", "arbitrary"),
        name="rglru",
    )(proj3, proj3, conv_w.astype(F32), conv_b.reshape(1, W).astype(F32), wa.astype(BF16),
      ba.reshape(1, W), wx.astype(BF16), bx.reshape(1, W), c)


NSA_AUG = 4
NSA_KW = LANES
SEL_TILE = 256
SEL_TILE_BLOCKS = SEL_TILE // SEL_LEN
WIN_TILE = 128
POS_SPLIT = 128
SEL_ONEHOT = LANES
PLAN_HALF = 16
SELECT_PARTS = 4
SELECT_QB = 2
ATTEND_QB = 2
NSA_VROWS = NSA_HEAD_DIM + 16
KV_PREP_ROWS = 512


def _key_aug(pos, width, dead=False):
    n = pos.shape[0]
    col = lax.broadcasted_iota(jnp.int32, (n, width), 1)
    hi = ((pos // POS_SPLIT) * POS_SPLIT).astype(F32)
    lo = (pos % POS_SPLIT).astype(F32)
    live = jnp.where(col == 0, hi, jnp.where(col == 1, lo, jnp.where(col < NSA_AUG, 1.0, 0.0)))
    return jnp.where(dead, jnp.where(col == NSA_AUG, 1.0, 0.0), live)


def _query_t(q_ref, slope_ref, c0, row0=0):
    hd, P, T = NSA_HEAD_DIM, NSA_PER_GROUP, Q_BLOCK
    PT = P * T
    qT = (q_ref[0, row0:row0 + T, :] * (hd ** -0.5)).T
    qpart = jnp.concatenate([qT[p * hd:(p + 1) * hd, :] for p in range(P)], 1)
    tlane = c0 + lax.broadcasted_iota(jnp.int32, (1, PT), 1) % T
    slope = slope_ref[0]
    t_hi = ((tlane // POS_SPLIT) * POS_SPLIT).astype(F32)
    t_lo = (tlane % POS_SPLIT).astype(F32)
    rowi = lax.broadcasted_iota(jnp.int32, (NSA_KW - hd, PT), 0)
    aug = jnp.where(rowi < 2, slope,
                    jnp.where(rowi == 2, -slope * t_hi,
                              jnp.where(rowi == 3, -slope * t_lo, jnp.where(rowi == NSA_AUG, NEG_BIG, 0.0))))
    return jnp.concatenate([qpart, aug], 0).astype(BF16), tlane


def _compress_kernel(xk_ref, xv_ref, w1k_ref, pek_ref, w1fk_ref, w2k_ref, w1v_ref, pev_ref, w1fv_ref, w2v_ref,
                     kc_ref, vct_ref, *, nc):
    hd = NSA_HEAD_DIM
    pos = lax.broadcasted_iota(jnp.int32, (nc, 1), 0) * CMP_STRIDE + (CMP_LEN - 1)
    aug = _key_aug(pos, NSA_KW - hd)

    def one(x_ref, w1_ref, pe_ref, w1f_ref, w2_ref, g):
        uv = jnp.zeros((nc, 2 * CMP_HIDDEN), F32)
        for r in range(CMP_STRIDE):
            xr = x_ref[0, pl.ds(r, nc, stride=CMP_STRIDE), :][:, g * hd:(g + 1) * hd]
            uv = uv + _dot_hilo(xr, w1_ref[r])
        cvec = _dot_hilo(pe_ref[...], w1f_ref[...])[0:1, :]
        hid = uv[:, 0:CMP_HIDDEN] + pltpu.roll(uv[:, CMP_HIDDEN:], nc - 1, 0) + cvec
        return _dot_hilo(jax.nn.gelu(hid), w2_ref[...])

    for g in range(NSA_GROUPS):
        kc = one(xk_ref, w1k_ref, pek_ref, w1fk_ref, w2k_ref, g)
        kc_ref[0, g] = jnp.concatenate([kc, aug], 1).astype(BF16)
        vct_ref[0, g] = one(xv_ref, w1v_ref, pev_ref, w1fv_ref, w2v_ref, g).T.astype(BF16)


def _compress_call(proj3, pe_k, w1_k, w2_k, pe_v, w1_v, w2_v):
    B, S, _ = proj3.shape
    G, hd = NSA_GROUPS, NSA_HEAD_DIM
    nc = S // CMP_STRIDE
    half = CMP_STRIDE * hd

    def prep(pe, w1):
        w1 = w1.astype(F32)
        wr = jnp.concatenate([w1[:half].reshape(CMP_STRIDE, hd, CMP_HIDDEN),
                              w1[half:].reshape(CMP_STRIDE, hd, CMP_HIDDEN)], -1)
        pe8 = jnp.broadcast_to(pe.reshape(1, CMP_LEN * hd).astype(F32), (SUBLANES, CMP_LEN * hd))
        return wr, pe8, w1

    full = lambda a: pl.BlockSpec(a.shape, lambda b: tuple(0 for _ in a.shape))
    col = lambda off: pl.BlockSpec((1, S, LANES), lambda b: (b, 0, off // LANES))
    args = prep(pe_k, w1_k) + (w2_k.astype(F32),) + prep(pe_v, w1_v) + (w2_v.astype(F32),)
    return pl.pallas_call(
        functools.partial(_compress_kernel, nc=nc),
        grid=(B,),
        in_specs=[col(OFF_KC), col(OFF_VC)] + [full(a) for a in args],
        out_specs=[pl.BlockSpec((1, G, nc, NSA_KW), lambda b: (b, 0, 0, 0)),
                   pl.BlockSpec((1, G, hd, nc), lambda b: (b, 0, 0, 0))],
        out_shape=[jax.ShapeDtypeStruct((B, G, nc, NSA_KW), BF16),
                   jax.ShapeDtypeStruct((B, G, hd, nc), BF16)],
        compiler_params=_params("parallel"),
        name="nsa_compress",
    )(proj3, proj3, *args)


def _kvprep_kernel(ks_ref, vs_ref, kw_ref, vw_ref, ksa_ref, vst_ref, kwa_ref, vwt_ref, *, tt, n_live):
    hd = NSA_HEAD_DIM
    t = pl.program_id(1)
    dead = t >= n_live
    pos = t * tt + lax.broadcasted_iota(jnp.int32, (tt, 1), 0)
    aug = _key_aug(pos, NSA_KW - hd, dead)
    blk_col = lax.broadcasted_iota(jnp.int32, (tt, SEL_ONEHOT), 1)
    onehot = jnp.where(jnp.logical_and(blk_col == pos // SEL_LEN, jnp.logical_not(dead)), 1.0, 0.0)
    vrow = lax.broadcasted_iota(jnp.int32, (NSA_VROWS - hd, tt), 0)
    ones_rows = jnp.where(jnp.logical_and(vrow == 0, jnp.logical_not(dead)), 1.0, 0.0)
    for g in range(NSA_GROUPS):
        lanes = slice(g * hd, (g + 1) * hd)
        live = lambda ref: jnp.where(dead, 0.0, ref[0][:, lanes])
        ksa_ref[0, g] = jnp.concatenate([live(ks_ref), aug, onehot], 1).astype(BF16)
        kwa_ref[0, g] = jnp.concatenate([live(kw_ref), aug], 1).astype(BF16)
        vs_t = jnp.concatenate([live(vs_ref).T, ones_rows], 0).astype(BF16)
        vw_t = jnp.concatenate([live(vw_ref).T, ones_rows], 0).astype(BF16)
        for c in range(tt // SEL_TILE):
            vst_ref[0, g, c] = vs_t[:, c * SEL_TILE:(c + 1) * SEL_TILE]
        for c in range(tt // WIN_TILE):
            vwt_ref[0, g, c] = vw_t[:, c * WIN_TILE:(c + 1) * WIN_TILE]


def _kvprep_call(proj3):
    B, S, _ = proj3.shape
    G, hd = NSA_GROUPS, NSA_HEAD_DIM
    tt = min(KV_PREP_ROWS, S)
    n_live = S // tt
    assert tt == WINDOW
    sp = S + tt
    col = lambda off: pl.BlockSpec((1, tt, LANES), lambda b, t: (b, jnp.minimum(t, n_live - 1), off // LANES))
    front = lambda t: (t + 1) % (n_live + 1)
    return pl.pallas_call(
        functools.partial(_kvprep_kernel, tt=tt, n_live=n_live),
        grid=(B, n_live + 1),
        in_specs=[col(OFF_KS), col(OFF_VS), col(OFF_KW), col(OFF_VW)],
        out_specs=[pl.BlockSpec((1, G, tt, NSA_KW + SEL_ONEHOT), lambda b, t: (b, 0, t, 0)),
                   pl.BlockSpec((1, G, tt // SEL_TILE, NSA_VROWS, SEL_TILE), lambda b, t: (b, 0, t, 0, 0)),
                   pl.BlockSpec((1, G, tt, NSA_KW), lambda b, t: (b, 0, front(t), 0)),
                   pl.BlockSpec((1, G, tt // WIN_TILE, NSA_VROWS, WIN_TILE), lambda b, t: (b, 0, front(t), 0, 0))],
        out_shape=[jax.ShapeDtypeStruct((B, G, sp, NSA_KW + SEL_ONEHOT), BF16),
                   jax.ShapeDtypeStruct((B, G, sp // SEL_TILE, NSA_VROWS, SEL_TILE), BF16),
                   jax.ShapeDtypeStruct((B, G, sp, NSA_KW), BF16),
                   jax.ShapeDtypeStruct((B, G, sp // WIN_TILE, NSA_VROWS, WIN_TILE), BF16)],
        compiler_params=_params("parallel", "parallel"),
        name="nsa_kvprep",
    )(proj3, proj3, proj3, proj3)


def _importance_matrix(ns, nc):
    ratio = SEL_LEN // CMP_STRIDE
    a = np.zeros((ns, nc), np.float32)
    for j in range(ns):
        for n, wgt in ((ratio * j - 1, 0.5), (ratio * j, 1.0), (ratio * j + 1, 1.0),
                       (ratio * j + 2, 1.0), (ratio * j + 3, 0.5)):
            if 0 <= n < nc - 1:
                a[j, n] = wgt
    return a


def _plan_constants(ns):
    nt = ns // SEL_TILE_BLOCKS
    grp = (np.arange(ns)[None, :] // SEL_TILE_BLOCKS == np.arange(nt)[:, None]).astype(np.float32)
    k = np.arange(nt)
    w_lo = np.where(k < PLAN_HALF, 2.0 ** np.minimum(k, PLAN_HALF - 1), 0.0)
    w_hi = np.where(k >= PLAN_HALF, 2.0 ** np.maximum(k - PLAN_HALF, 0), 0.0)
    wts = np.stack([np.repeat(w_lo[:, None], LANES, 1), np.repeat(w_hi[:, None], LANES, 1)]).astype(np.float32)
    return grp, wts


def _select_kernel(q_ref, slope_ref, kc_ref, vct_ref, imp_ref, grp_ref, wts_ref, ocmp_ref, bias_ref, bits_ref,
                   *, ns, nc, qb):
    P, T = NSA_PER_GROUP, Q_BLOCK
    step = pl.program_id(2)
    nsteps = pl.num_programs(2)
    subs = range(qb)
    c0s = [(step * qb + u) * T for u in subs]
    qts, tlanes = zip(*[_query_t(q_ref, slope_ref, c0s[u], u * T) for u in subs])

    def body(rows, nblk):
        cend = lax.broadcasted_iota(jnp.int32, (rows, 1), 0) * CMP_STRIDE + (CMP_LEN - 1)
        cmask = [cend <= tlanes[u] for u in subs]
        s = [jnp.where(cmask[u], _dot(kc_ref[0, 0, 0:rows, :], qts[u]), NEG_BIG) for u in subs]
        m = [jnp.max(s[u], 0, keepdims=True) for u in subs]
        p_c = [jnp.where(cmask[u], jnp.exp(s[u] - m[u]), 0.0) for u in subs]
        p_c = [p_c[u] * (1.0 / jnp.maximum(jnp.sum(p_c[u], 0, keepdims=True), 1e-30)) for u in subs]
        for u in subs:
            ocmp_ref[0, 0, u] = _dot(vct_ref[0, 0][:, 0:rows], p_c[u].astype(BF16))

        psum = [p_c[u][:, 0:T] for u in subs]
        for p in range(1, P):
            psum = [psum[u] + p_c[u][:, p * T:(p + 1) * T] for u in subs]
        imp = [_dot_exact_lhs(imp_ref[0:nblk, 0:rows], psum[u]) for u in subs]
        blk = lax.broadcasted_iota(jnp.int32, (nblk, T), 0)
        val, chosen = [], []
        for u in subs:
            cur = (c0s[u] + lax.broadcasted_iota(jnp.int32, (nblk, T), 1)) // SEL_LEN
            forced = (blk == 0) | (blk == cur) | (blk == cur - 1)
            val.append(jnp.where(forced, 3e38, jnp.where(blk > cur, -1.0, imp[u])))
            chosen.append(jnp.zeros((nblk, T), F32))
        for _ in range(min(SEL_TOPK, nblk)):
            for u in subs:
                mx = jnp.max(val[u], 0, keepdims=True)
                first = jnp.min(jnp.where(val[u] == mx, blk, ns), 0, keepdims=True)
                pick = blk == first
                chosen[u] = jnp.where(pick, 1.0, chosen[u])
                val[u] = jnp.where(pick, -2.0, val[u])
        for u in subs:
            bias_ref[0, 0, u, 0:nblk, :] = jnp.where(chosen[u] > 0.5, 0.0, NEG_BIG)
            if nblk < ns:
                bias_ref[0, 0, u, nblk:ns, :] = jnp.full((ns - nblk, T), NEG_BIG, F32)
            ch = chosen[u]
            if nblk < ns:
                ch = jnp.concatenate([ch, jnp.zeros((ns - nblk, T), F32)], 0)
            cnt = _dot(grp_ref[...], ch.astype(BF16))
            act = jnp.where(jnp.max(cnt, 1, keepdims=True) > 0.5, 1.0, 0.0)
            lo = jnp.sum(act * wts_ref[0], 0, keepdims=True).astype(jnp.int32)
            hi = jnp.sum(act * wts_ref[1], 0, keepdims=True).astype(jnp.int32)
            bits_ref[u] = jnp.broadcast_to(lo | (hi << PLAN_HALF), (SUBLANES, LANES))

    parts = max(p for p in (SELECT_PARTS, 2, 1) if nc % (p * LANES) == 0 and ns % (p * 2 * SUBLANES) == 0)
    part = step * parts // nsteps
    for v in range(parts):
        pl.when(part == v)(functools.partial(body, nc * (v + 1) // parts, ns * (v + 1) // parts))


def _attend_kernel(bits_sref, q_ref, gt_ref, slope_ref, ocmp_ref, bias_ref, ks_ref, vst_ref, kw_ref, vwt_ref,
                   o_ref, list_ref, s_ref, m_ref, *, n_tiles_total, ns, qb):
    hd, P, T = NSA_HEAD_DIM, NSA_PER_GROUP, Q_BLOCK
    PT = P * T
    b, g, step = pl.program_id(0), pl.program_id(1), pl.program_id(2)
    subs = range(qb)
    iqs = [step * qb + u for u in subs]
    c0s = [iq * T for iq in iqs]
    qts, tlanes = zip(*[_query_t(q_ref, slope_ref, c0s[u], u * T) for u in subs])
    tl_row = lax.broadcasted_iota(jnp.int32, (1, PT), 1) % T

    gts = [jax.nn.sigmoid(gt_ref[0, u * T:(u + 1) * T, :]).T for u in subs]

    def gate_row(u, branch):
        return jnp.concatenate([gts[u][p * 3 + branch:p * 3 + branch + 1, :] for p in range(P)], 1)

    list_len = n_tiles_total + 1
    last_tiles = [(c0s[u] + T - 1) // SEL_TILE for u in subs]
    prev_bits = []
    for u in subs:
        bits = bits_sref[(b * pl.num_programs(1) + g) * (pl.num_programs(2) * qb) + iqs[u]]
        prev_bits.append(bits & (lax.shift_left(jnp.int32(1), last_tiles[u]) - 1))
    n_prevs = [jnp.int32(0) for _ in subs]
    for k in range(n_tiles_total):
        for u in subs:
            list_ref[u * list_len + n_prevs[u]] = k
            n_prevs[u] = n_prevs[u] + (lax.shift_right_logical(prev_bits[u], k) & 1)

    qt_sels = []
    for u in subs:
        sel_b = bias_ref[0, 0, u]
        if ns < SEL_ONEHOT:
            sel_b = jnp.concatenate([sel_b, jnp.zeros((SEL_ONEHOT - ns, T), F32)], 0)
        qt_sels.append(jnp.concatenate([qts[u], jnp.concatenate([sel_b.astype(BF16)] * P, 1)], 0))

    def tile_scores(u, kt):
        k0 = pl.multiple_of(kt * SEL_TILE, SEL_TILE)
        return _dot(ks_ref[0, 0, pl.ds(k0, SEL_TILE), :], qt_sels[u])

    def col_max(scores):
        m = jnp.max(scores[0], 0, keepdims=True)
        for sc in scores[1:]:
            m = jnp.maximum(m, jnp.max(sc, 0, keepdims=True))
        return m

    def softmax_update(carry, scores, vts, m_tile):
        m_run, acc = carry
        m_new = jnp.maximum(m_run, m_tile)
        acc = jnp.exp(m_run - m_new) * acc
        for sc, vt in zip(scores, vts):
            acc = acc + _dot(vt, jnp.exp(sc - m_new).astype(BF16))
        return m_new, acc

    init = (jnp.full((1, PT), NEG_BIG, F32), jnp.zeros((NSA_VROWS, PT), F32))
    dead_tile = n_tiles_total
    n_pairs = [jnp.maximum((n_prevs[u] + 1) // 2, 1) for u in subs]

    def pair_tiles(u, i):
        has_a, has_b = 2 * i < n_prevs[u], 2 * i + 1 < n_prevs[u]
        ka = jnp.where(has_a, list_ref[u * list_len + jnp.where(has_a, 2 * i, 0)], dead_tile)
        kb = jnp.where(has_b, list_ref[u * list_len + jnp.where(has_b, 2 * i + 1, 0)], dead_tile)
        return ka, kb

    def pair_scores(u, i):
        ka, kb = pair_tiles(u, i)
        keys = jnp.concatenate([ks_ref[0, 0, pl.ds(pl.multiple_of(k * SEL_TILE, SEL_TILE), SEL_TILE), :]
                                for k in (ka, kb)], 0)
        return _dot(keys, qt_sels[u])

    def put_scores(u, i, sc):
        slot = i % 2
        s_ref[u, slot] = sc
        m_ref[u, slot] = col_max([sc])

    def pair_probs(u, i, m_run):
        slot = i % 2
        m_new = jnp.maximum(m_run, m_ref[u, slot])
        return m_new, jnp.exp(s_ref[u, slot] - m_new).astype(BF16)

    def add_values(u, i, carry, m_new, pr):
        ka, kb = pair_tiles(u, i)
        m_run, acc = carry
        vt = jnp.concatenate([vst_ref[0, 0, ka], vst_ref[0, 0, kb]], 1)
        return m_new, jnp.exp(m_run - m_new) * acc + _dot(vt, pr)

    first = [pair_scores(u, 0) for u in subs]
    for u in subs:
        put_scores(u, 0, first[u])

    n_win = (WINDOW + T) // WIN_TILE
    roff = lax.broadcasted_iota(jnp.int32, (WIN_TILE, 1), 0)
    w_sc = [_dot(kw_ref[0, 0, pl.ds(pl.multiple_of(c0s[u], WIN_TILE), WINDOW + T), :], qts[u]) for u in subs]
    w_parts = [[jnp.where(roff > tl_row, w_sc[u][0:WIN_TILE], NEG_BIG),
                w_sc[u][WIN_TILE:WINDOW],
                jnp.where(roff <= tl_row, w_sc[u][WINDOW:WINDOW + T], NEG_BIG)]
               for u in subs]
    w_max = [col_max(w_parts[u]) for u in subs]
    out_cw = []
    for u in subs:
        vt = jnp.concatenate([vwt_ref[0, 0, iqs[u] + j] for j in range(n_win)], 1)
        _, acc_win = softmax_update(init, [jnp.concatenate(w_parts[u], 0)], [vt], w_max[u])
        out_cw.append(gate_row(u, 0) * ocmp_ref[0, 0, u] + gate_row(u, 2) * (acc_win[0:hd] / acc_win[hd:hd + 1]))

    def pipe_step(u, i, carry):
        m_new, pr = pair_probs(u, i, carry[0])
        sc_next = pair_scores(u, i + 1)
        carry = add_values(u, i, carry, m_new, pr)
        put_scores(u, i + 1, sc_next)
        return carry

    carries = [lax.fori_loop(0, n_pairs[u] - 1, functools.partial(pipe_step, u), init) for u in subs]
    probs = [pair_probs(u, n_pairs[u] - 1, carries[u][0]) for u in subs]
    s_last = []
    for u in subs:
        kpos = last_tiles[u] * SEL_TILE + lax.broadcasted_iota(jnp.int32, (SEL_TILE, 1), 0)
        s_last.append(jnp.where(kpos <= tlanes[u], tile_scores(u, last_tiles[u]), NEG_BIG))
    carries = [add_values(u, n_pairs[u] - 1, carries[u], *probs[u]) for u in subs]
    for u in subs:
        _, acc_sel = softmax_update(carries[u], [s_last[u]], [vst_ref[0, 0, last_tiles[u]]], col_max([s_last[u]]))
        out = out_cw[u] + gate_row(u, 1) * (acc_sel[0:hd] / acc_sel[hd:hd + 1])
        o_ref[0, u * T:(u + 1) * T, :] = jnp.concatenate(
            [out[:, p * T:(p + 1) * T] for p in range(P)], 0).T.astype(o_ref.dtype)


def _nsa_call(proj3, pe_k, w1_k, w2_k, pe_v, w1_v, w2_v):
    B, S, _ = proj3.shape
    G, hd, P, T = NSA_GROUPS, NSA_HEAD_DIM, NSA_PER_GROUP, Q_BLOCK
    PT = P * T
    nc = S // CMP_STRIDE
    ns = S // SEL_LEN
    nq = S // T
    nt = ns // SEL_TILE_BLOCKS
    assert nt <= 2 * PLAN_HALF and S % KV_PREP_ROWS == 0

    kc_a, vc_t = _compress_call(proj3, pe_k, w1_k, w2_k, pe_v, w1_v, w2_v)
    ks_a, vs_t, kw_a, vw_t = _kvprep_call(proj3)
    imp_m = jnp.asarray(_importance_matrix(ns, nc), BF16)
    grp, wts = _plan_constants(ns)
    head_slopes = 2.0 ** (-(8.0 / NSA_HEADS) * np.arange(1, NSA_HEADS + 1, dtype=np.float32))
    slopes = jnp.asarray(np.repeat(head_slopes.reshape(G, 1, P), T, axis=-1), F32)

    q_spec = lambda im: pl.BlockSpec((1, T, P * hd), im)
    bg = lambda shp: pl.BlockSpec((1, 1) + shp, lambda b, g, i: (b, g) + tuple(0 for _ in shp))
    const = lambda a: pl.BlockSpec(a.shape, lambda b, g, i: tuple(0 for _ in a.shape))
    qb = SELECT_QB if nq % (SELECT_QB * SELECT_PARTS) == 0 else 1
    nsteps = nq // qb
    ocmp, sel_bias, bits = pl.pallas_call(
        functools.partial(_select_kernel, ns=ns, nc=nc, qb=qb),
        grid=(B, G, nsteps),
        in_specs=[pl.BlockSpec((1, qb * T, P * hd), lambda b, g, i: (b, i, OFF_NQ // (P * hd) + g)),
                  pl.BlockSpec((1, 1, PT), lambda b, g, i: (g, 0, 0)),
                  bg((nc, NSA_KW)), bg((hd, nc)), const(imp_m),
                  pl.BlockSpec(grp.shape, lambda b, g, i: (0, 0)),
                  pl.BlockSpec(wts.shape, lambda b, g, i: (0, 0, 0))],
        out_specs=[pl.BlockSpec((1, 1, qb, hd, PT), lambda b, g, i: (b, g, i, 0, 0)),
                   pl.BlockSpec((1, 1, qb, ns, T), lambda b, g, i: (b, g, i, 0, 0)),
                   pl.BlockSpec((qb, SUBLANES, LANES), lambda b, g, i: ((b * G + g) * nsteps + i, 0, 0))],
        out_shape=[jax.ShapeDtypeStruct((B, G, nq, hd, PT), F32),
                   jax.ShapeDtypeStruct((B, G, nq, ns, T), F32),
                   jax.ShapeDtypeStruct((B * G * nq, SUBLANES, LANES), jnp.int32)],
        compiler_params=_params("parallel", "parallel", "parallel"),
        name="nsa_select",
    )(proj3, slopes, kc_a, vc_t, imp_m, jnp.asarray(grp, BF16), jnp.asarray(wts, F32))

    aq = ATTEND_QB if nq % ATTEND_QB == 0 else 1
    bg2 = lambda shp: pl.BlockSpec((1, 1) + shp, lambda b, g, i, s: (b, g) + tuple(0 for _ in shp))
    return pl.pallas_call(
        functools.partial(_attend_kernel, n_tiles_total=nt, ns=ns, qb=aq),
        grid_spec=pltpu.PrefetchScalarGridSpec(
            num_scalar_prefetch=1,
            grid=(B, G, nq // aq),
            in_specs=[pl.BlockSpec((1, aq * T, P * hd), lambda b, g, i, s: (b, i, OFF_NQ // (P * hd) + g)),
                      pl.BlockSpec((1, aq * T, LANES), lambda b, g, i, s: (b, i, OFF_NG // LANES + g)),
                      pl.BlockSpec((1, 1, PT), lambda b, g, i, s: (g, 0, 0)),
                      pl.BlockSpec((1, 1, aq, hd, PT), lambda b, g, i, s: (b, g, i, 0, 0)),
                      pl.BlockSpec((1, 1, aq, ns, T), lambda b, g, i, s: (b, g, i, 0, 0)),
                      bg2(ks_a.shape[2:]), bg2(vs_t.shape[2:]), bg2(kw_a.shape[2:]), bg2(vw_t.shape[2:])],
            out_specs=pl.BlockSpec((1, aq * T, P * hd), lambda b, g, i, s: (b, i, g)),
            scratch_shapes=[pltpu.SMEM((aq * (nt + 1),), jnp.int32),
                            pltpu.VMEM((aq, 2, 2 * SEL_TILE, PT), F32), pltpu.VMEM((aq, 2, 1, PT), F32)]),
        out_shape=jax.ShapeDtypeStruct((B, S, NSA_WIDTH), BF16),
        compiler_params=_params("parallel", "parallel", "arbitrary"),
        name="nsa_attend",
    )(bits[:, 0, 0], proj3, proj3, slopes, ocmp, sel_bias, ks_a, vs_t, kw_a, vw_t)


def _merge_kernel(x_ref, ya_ref, yb_ref, yc_ref, g0_ref, g1_ref, g2_ref, wa_ref, wb_ref, wc_ref, wo_ref,
                  lg_ref, lb_ref, o_ref):
    m = jax.nn.sigmoid(g0_ref[...]) * _dot(ya_ref[...], wa_ref[...])
    m = m + jax.nn.sigmoid(g1_ref[...]) * _dot(yb_ref[...], wb_ref[...])
    m = m + jax.nn.sigmoid(g2_ref[...]) * _dot(yc_ref[...], wc_ref[...])
    z = DEEPNORM_ALPHA * x_ref[...] + _dot(m.astype(BF16), wo_ref[...])
    o_ref[...] = _layer_norm_rows(z, lg_ref[...], lb_ref[...])


def _merge_call(x2, ya, yb, yc, proj2, w_hg, w_nsa, w_lru, w_out, ln_g, ln_b, tm=512):
    T, D = x2.shape
    tm = min(tm, T)
    rows = lambda w: pl.BlockSpec((tm, w), lambda i: (i, 0))
    gate = lambda n: pl.BlockSpec((tm, D), lambda i: (i, OFF_MG // D + n))
    full = lambda a: pl.BlockSpec(a.shape, lambda i: (0, 0))
    ws = [w.astype(BF16) for w in (w_hg, w_nsa, w_lru, w_out)]
    return pl.pallas_call(
        _merge_kernel,
        grid=(T // tm,),
        in_specs=[rows(D), rows(HG_WIDTH), rows(NSA_WIDTH), rows(LRU_WIDTH), gate(0), gate(1), gate(2)]
                 + [full(w) for w in ws] + [pl.BlockSpec((1, D), lambda i: (0, 0))] * 2,
        out_specs=rows(D),
        out_shape=jax.ShapeDtypeStruct((T, D), F32),
        compiler_params=_params("parallel"),
        name="merge_out",
    )(x2, ya, yb, yc, proj2, proj2, proj2, *ws, ln_g.reshape(1, D), ln_b.reshape(1, D))


FFN_SPLIT = 2
FFN_PAD = SUBLANES
FFN_SUB = 256


def _ffn_kernel(x_ref, wu_ref, wv_ref, cw_ref, cb_ref, wd_ref, lg_ref, lb_ref, o_ref,
                xb_ref, acc_ref, upad_ref, carry_ref, *, tm, tiles_per_seq):
    i = pl.program_id(0)
    j = pl.program_id(1)

    @pl.when(j == 0)
    def _():
        xb_ref[...] = x_ref[...].astype(BF16)

    xb = xb_ref[...]
    fc = wu_ref.shape[1]
    first = (i % tiles_per_seq) == 0

    @pl.when(first)
    def _():
        upad_ref[0:FFN_PAD, :] = jnp.zeros((FFN_PAD, fc), F32)

    @pl.when(jnp.logical_not(first))
    def _():
        upad_ref[0:FFN_PAD, :] = carry_ref[j]

    @pl.when(j == 0)
    def _():
        acc_ref[...] = jnp.zeros_like(acc_ref)

    for c0 in range(0, fc, FFN_SUB):
        cs = slice(c0, min(c0 + FFN_SUB, fc))
        u = _dot(xb, wu_ref[:, cs])
        v = _dot(xb, wv_ref[:, cs])
        upad_ref[FFN_PAD:FFN_PAD + tm, cs] = u
        cw = cw_ref[:, cs]
        cv = cb_ref[:, cs] + cw[FFN_CONV - 1:FFN_CONV, :] * u
        for k in range(FFN_CONV - 1):
            off = FFN_PAD - (FFN_CONV - 1) + k
            cv = cv + cw[k:k + 1, :] * upad_ref[off:off + tm, cs]
        carry_ref[j, :, cs] = u[tm - FFN_PAD:tm, :]
        h = (jax.nn.gelu(cv) * v).astype(BF16)
        acc_ref[...] += _dot(h, wd_ref[cs, :])

    @pl.when(j == FFN_SPLIT - 1)
    def _():
        z = DEEPNORM_ALPHA * x_ref[...] + acc_ref[...]
        o_ref[...] = _layer_norm_rows(z, lg_ref[...], lb_ref[...])


def _ffn_call(x2, seq_len, w_up, conv_w, conv_b, w_down, ln_g, ln_b, tm=512):
    T, D = x2.shape
    tm = min(tm, seq_len)
    fc = FFN_DIM // FFN_SPLIT
    wu = w_up.astype(BF16)
    return pl.pallas_call(
        functools.partial(_ffn_kernel, tm=tm, tiles_per_seq=seq_len // tm),
        grid=(T // tm, FFN_SPLIT),
        in_specs=[pl.BlockSpec((tm, D), lambda i, j: (i, 0)),
                  pl.BlockSpec((D, fc), lambda i, j: (0, j)),
                  pl.BlockSpec((D, fc), lambda i, j: (0, FFN_SPLIT + j)),
                  pl.BlockSpec((FFN_CONV, fc), lambda i, j: (0, j)),
                  pl.BlockSpec((1, fc), lambda i, j: (0, j)),
                  pl.BlockSpec((fc, D), lambda i, j: (j, 0)),
                  pl.BlockSpec((1, D), lambda i, j: (0, 0)),
                  pl.BlockSpec((1, D), lambda i, j: (0, 0))],
        out_specs=pl.BlockSpec((tm, D), lambda i, j: (i, 0)),
        out_shape=jax.ShapeDtypeStruct((T, D), F32),
        scratch_shapes=[pltpu.VMEM((tm, D), BF16), pltpu.VMEM((tm, D), F32),
                        pltpu.VMEM((tm + FFN_PAD, fc), F32), pltpu.VMEM((FFN_SPLIT, FFN_PAD, fc), F32)],
        compiler_params=_params("arbitrary", "arbitrary"),
        name="conv_ffn",
    )(x2, wu, wu, conv_w.astype(F32), conv_b.reshape(1, FFN_DIM).astype(F32), w_down.astype(BF16),
      ln_g.reshape(1, D), ln_b.reshape(1, D))


def _permute_in_proj(w, b):
    src = np.cumsum([0, HG_WIDTH, HG_WIDTH, HG_WIDTH, HG_WIDTH, NSA_WIDTH, NSA_KV, NSA_KV, NSA_KV, NSA_KV,
                     NSA_KV, NSA_KV, NSA_HEADS * 3, LRU_WIDTH, LRU_WIDTH, N_BRANCH * 1024])
    names = ["hq", "hf", "hi", "hg", "nq", "kc", "vc", "ks", "vs", "kw", "vw", "ng", "lx", "ly", "mg"]
    seg = {n: (int(src[k]), int(src[k + 1])) for k, n in enumerate(names)}
    order = ["mg", "hq", "hf", "hi", "hg", "nq", "lx", "ly", "kc", "vc", "ks", "vs", "kw", "vw"]
    per_group = NSA_PER_GROUP * 3
    ng0 = seg["ng"][0]
    w_parts = [w[:, seg[n][0]:seg[n][1]] for n in order]
    b_parts = [b[seg[n][0]:seg[n][1]] for n in order]
    for gi in range(NSA_GROUPS):
        lo = ng0 + gi * per_group
        w_parts += [w[:, lo:lo + per_group], jnp.zeros((w.shape[0], LANES - per_group), w.dtype)]
        b_parts += [b[lo:lo + per_group], jnp.zeros((LANES - per_group,), b.dtype)]
    wp = jnp.concatenate(w_parts, 1)
    bp = jnp.concatenate(b_parts)
    assert wp.shape[1] == PROJ_COLS
    return wp.astype(BF16), bp.astype(F32)


def kernel(x, ln_emb_g, ln_emb_b, w_in, b_in, hg_lb_logits, hg_norm_g, cmp_pe_k, cmp_w1_k, cmp_w2_k, cmp_pe_v, cmp_w1_v, cmp_w2_v, lru_conv_w, lru_conv_b, lru_wa, lru_ba, lru_wx, lru_bx, lru_lambda, w_branch_hg, w_branch_nsa, w_branch_lru, w_out, ln1_g, ln1_b, ffn_w_up, ffn_conv_w, ffn_conv_b, ffn_w_down, ln2_g, ln2_b):
    B, S, D = x.shape
    T = B * S
    gam = jax.nn.softmax(hg_lb_logits.astype(F32), axis=0)
    lb_all = jnp.cumsum(gam, axis=0) - gam[0]
    h = _layer_norm_call(x.reshape(T, D), ln_emb_g, ln_emb_b)
    for l in range(DEPTH):
        wp, bp = _permute_in_proj(w_in[l], b_in[l])
        proj2 = _inproj_call(h, wp, bp)
        proj3 = proj2.reshape(B, S, PROJ_COLS)
        y_a = _hgrn_call(proj3, lb_all[l], hg_norm_g[l])
        y_b = _nsa_call(proj3, cmp_pe_k[l], cmp_w1_k[l], cmp_w2_k[l], cmp_pe_v[l], cmp_w1_v[l], cmp_w2_v[l])
        y_c = _lru_call(proj3, lru_conv_w[l], lru_conv_b[l], lru_wa[l], lru_ba[l], lru_wx[l], lru_bx[l],
                        lru_lambda[l])
        h = _merge_call(h, y_a.reshape(T, HG_WIDTH), y_b.reshape(T, NSA_WIDTH), y_c.reshape(T, LRU_WIDTH),
                        proj2, w_branch_hg[l], w_branch_nsa[l], w_branch_lru[l], w_out[l], ln1_g[l], ln1_b[l])
        h = _ffn_call(h, S, ffn_w_up[l], ffn_conv_w[l], ffn_conv_b[l], ffn_w_down[l], ln2_g[l], ln2_b[l])
    return h.reshape(B, S, D)
```

```python
import functools

import numpy as np
import jax
import jax.numpy as jnp
from jax import lax
from jax.experimental import pallas as pl
from jax.experimental.pallas import tpu as pltpu

F32 = jnp.float32
BF16 = jnp.bfloat16

DEPTH = 2
HG_HEADS = 4
HG_HEAD_DIM = 128
HG_WIDTH = HG_HEADS * HG_HEAD_DIM
NSA_HEADS = 8
NSA_GROUPS = 2
NSA_PER_GROUP = NSA_HEADS // NSA_GROUPS
NSA_HEAD_DIM = 64
NSA_WIDTH = NSA_HEADS * NSA_HEAD_DIM
NSA_KV = NSA_GROUPS * NSA_HEAD_DIM
CMP_LEN = 32
CMP_STRIDE = 16
CMP_HIDDEN = 64
SEL_LEN = 64
SEL_TOPK = 16
WINDOW = 512
Q_BLOCK = 128
LRU_WIDTH = 512
LRU_BLOCKS = 4
LRU_CONV = 4
LRU_C = 8.0
FFN_DIM = 2816
FFN_CONV = 3
DEEPNORM_ALPHA = (2 * DEPTH) ** 0.25
LN_EPS = 1e-5
N_BRANCH = 3

LANES = 128
SUBLANES = 8
VMEM_LIMIT_BYTES = 52 * 1024 * 1024

OFF_MG = 0
OFF_HQ = N_BRANCH * 1024
OFF_HF = OFF_HQ + HG_WIDTH
OFF_HI = OFF_HF + HG_WIDTH
OFF_HG = OFF_HI + HG_WIDTH
OFF_NQ = OFF_HG + HG_WIDTH
OFF_LX = OFF_NQ + NSA_WIDTH
OFF_LY = OFF_LX + LRU_WIDTH
OFF_KC = OFF_LY + LRU_WIDTH
OFF_VC = OFF_KC + NSA_KV
OFF_KS = OFF_VC + NSA_KV
OFF_VS = OFF_KS + NSA_KV
OFF_KW = OFF_VS + NSA_KV
OFF_VW = OFF_KW + NSA_KV
OFF_NG = OFF_VW + NSA_KV
PROJ_TILE_N = 1536
PROJ_COLS = 5 * PROJ_TILE_N

NEG_BIG = -1e30


def _split3(x):
    hi = x.astype(BF16)
    r1 = x - hi.astype(F32)
    mid = r1.astype(BF16)
    lo = (r1 - mid.astype(F32)).astype(BF16)
    return hi, mid, lo


def _dot(a, b):
    return jnp.dot(a, b, preferred_element_type=F32)


def _dot_nt(a, b):
    return lax.dot_general(a, b, (((1,), (1,)), ((), ())), preferred_element_type=F32)


def _dot_exact_lhs(m_bf16, x):
    hi, mid, lo = _split3(x)
    return _dot(m_bf16, hi) + _dot(m_bf16, mid) + _dot(m_bf16, lo)


def _dot_hilo(a, b):
    ah = a.astype(BF16)
    al = (a - ah.astype(F32)).astype(BF16)
    bh = b.astype(BF16)
    bl = (b - bh.astype(F32)).astype(BF16)
    return _dot(ah, bh) + _dot(ah, bl) + _dot(al, bh) + _dot(al, bl)


def _layer_norm_rows(z, g, b):
    mu = jnp.mean(z, -1, keepdims=True)
    zc = z - mu
    var = jnp.mean(zc * zc, -1, keepdims=True)
    return zc * lax.rsqrt(var + LN_EPS) * g + b


def _params(*sem):
    return pltpu.CompilerParams(dimension_semantics=sem, vmem_limit_bytes=VMEM_LIMIT_BYTES)


def _ln_kernel(x_ref, g_ref, b_ref, o_ref):
    o_ref[...] = _layer_norm_rows(x_ref[...], g_ref[...], b_ref[...])


def _layer_norm_call(x2, g, b, tm=512):
    T, D = x2.shape
    return pl.pallas_call(
        _ln_kernel,
        grid=(T // tm,),
        in_specs=[pl.BlockSpec((tm, D), lambda i: (i, 0)),
                  pl.BlockSpec((1, D), lambda i: (0, 0)),
                  pl.BlockSpec((1, D), lambda i: (0, 0))],
        out_specs=pl.BlockSpec((tm, D), lambda i: (i, 0)),
        out_shape=jax.ShapeDtypeStruct((T, D), F32),
        compiler_params=_params("parallel"),
        name="embed_ln",
    )(x2, g.reshape(1, D), b.reshape(1, D))


def _inproj_kernel(x_ref, w_ref, b_ref, o_ref, xb_ref):
    @pl.when(pl.program_id(1) == 0)
    def _():
        xb_ref[...] = x_ref[...].astype(BF16)

    o_ref[...] = _dot(xb_ref[...], w_ref[...]) + b_ref[...]


def _inproj_ln_kernel(x_ref, g_ref, be_ref, w_ref, b_ref, o_ref, xn_ref, xb_ref):
    @pl.when(pl.program_id(1) == 0)
    def _():
        xn = _layer_norm_rows(x_ref[...], g_ref[...], be_ref[...])
        xn_ref[...] = xn
        xb_ref[...] = xn.astype(BF16)

    o_ref[...] = _dot(xb_ref[...], w_ref[...]) + b_ref[...]


def _inproj_ln_call(x2, ln_g, ln_b, w_bf16, bias, tm=1024):
    T, D = x2.shape
    N = w_bf16.shape[1]
    tm = min(tm, T)
    tn = PROJ_TILE_N
    vec = pl.BlockSpec((1, D), lambda i, j: (0, 0))
    return pl.pallas_call(
        _inproj_ln_kernel,
        grid=(T // tm, N // tn),
        in_specs=[pl.BlockSpec((tm, D), lambda i, j: (i, 0)), vec, vec,
                  pl.BlockSpec((D, tn), lambda i, j: (0, j)),
                  pl.BlockSpec((1, tn), lambda i, j: (0, j))],
        out_specs=[pl.BlockSpec((tm, tn), lambda i, j: (i, j)),
                   pl.BlockSpec((tm, D), lambda i, j: (i, 0))],
        out_shape=[jax.ShapeDtypeStruct((T, N), F32), jax.ShapeDtypeStruct((T, D), F32)],
        scratch_shapes=[pltpu.VMEM((tm, D), BF16)],
        compiler_params=_params("parallel", "arbitrary"),
        name="in_proj_ln",
    )(x2, ln_g.reshape(1, D), ln_b.reshape(1, D), w_bf16, bias.reshape(1, N))


def _inproj_call(x2, w_bf16, bias, tm=1024):
    T, D = x2.shape
    N = w_bf16.shape[1]
    tm = min(tm, T)
    tn = PROJ_TILE_N
    return pl.pallas_call(
        _inproj_kernel,
        grid=(T // tm, N // tn),
        in_specs=[pl.BlockSpec((tm, D), lambda i, j: (i, 0)),
                  pl.BlockSpec((D, tn), lambda i, j: (0, j)),
                  pl.BlockSpec((1, tn), lambda i, j: (0, j))],
        out_specs=pl.BlockSpec((tm, tn), lambda i, j: (i, j)),
        out_shape=jax.ShapeDtypeStruct((T, N), F32),
        scratch_shapes=[pltpu.VMEM((tm, D), BF16)],
        compiler_params=_params("parallel", "arbitrary"),
        name="in_proj",
    )(x2, w_bf16, bias.reshape(1, N))


HG_CHUNK = 128
HG_DIAG = SUBLANES
HG_HEADS_PER_STEP = 2


def _hgrn_constants(C):
    r = np.arange(C)
    tri = (r[None, :] <= r[:, None]).astype(np.float32)
    mats = [tri]
    masks = []
    w = HG_DIAG
    while 2 * w <= C:
        mid = (r // (2 * w)) * (2 * w) + w
        mats.append(tri - (r[None, :] <= mid[:, None]).astype(np.float32))
        same = (r[:, None] // (2 * w)) == (r[None, :] // (2 * w))
        masks.append((same & ((r[:, None] % (2 * w)) >= w) & ((r[None, :] % (2 * w)) < w)).astype(np.float32))
        w *= 2
    mats.append((r[None, :] > r[:, None]).astype(np.float32))
    return np.concatenate(mats, 0), np.stack(masks, 0)


def _hgrn_kernel(q_ref, f_ref, i_ref, g_ref, lb_ref, ng_ref, m_ref, mask_ref, o_ref, st_ref, *, C, nchunk, nh):
    d = HG_HEAD_DIM
    nlev = mask_ref.shape[0]

    @pl.when(pl.program_id(2) == 0)
    def _():
        st_ref[...] = jnp.zeros_like(st_ref)

    hs = range(nh)
    lane = [slice(h * d, (h + 1) * d) for h in hs]
    lb = [lb_ref[h] for h in hs]
    ng = [ng_ref[h] for h in hs]
    nb = C // HG_DIAG
    row3 = lax.broadcasted_iota(jnp.int32, (nb, HG_DIAG, d), 1)

    fl_all = f_ref[0]
    lf_all = [jnp.log(lb[h] + (1.0 - lb[h]) * jax.nn.sigmoid(fl_all[:, lane[h]])) for h in hs]
    kk_all = [(1.0 - lb[h]) * jax.nn.sigmoid(-fl_all[:, lane[h]]) for h in hs]
    allm_all = _dot_exact_lhs(m_ref[...], jnp.concatenate(
        [lf_all[h][c * C:(c + 1) * C] for h in hs for c in range(nchunk)], 1))

    for c in range(nchunk):
        sl = pl.ds(c * C, C)
        q = [q_ref[0, sl, lane[h]] for h in hs]
        iv = [i_ref[0, sl, lane[h]] for h in hs]
        kk = [kk_all[h][c * C:(c + 1) * C] for h in hs]
        allm = [allm_all[:, (h * nchunk + c) * d:(h * nchunk + c + 1) * d] for h in hs]
        bcs = [allm[h][0:C] for h in hs]
        rem = [allm[h][(nlev + 1) * C:(nlev + 2) * C] for h in hs]
        iv_b = [iv[h].astype(BF16) for h in hs]

        att = [jnp.zeros((C, C), F32) for _ in hs]
        for l in range(nlev):
            e = [jnp.exp(-jnp.abs(allm[h][(1 + l) * C:(2 + l) * C])) for h in hs]
            att = [att[h] + mask_ref[l] * _dot_nt((q[h] * e[h]).astype(BF16), (kk[h] * e[h]).astype(BF16))
                   for h in hs]
        o = [_dot(att[h].astype(BF16), iv_b[h]) for h in hs]

        q3 = [q[h].reshape(nb, HG_DIAG, d) for h in hs]
        k3 = [kk[h].reshape(nb, HG_DIAG, d) for h in hs]
        b3 = [bcs[h].reshape(nb, HG_DIAG, d) for h in hs]
        i3 = [iv[h].reshape(nb, HG_DIAG, d) for h in hs]
        acc = [jnp.zeros((nb, HG_DIAG, d), F32) for _ in hs]
        for s in range(HG_DIAG):
            for h in hs:
                dec = jnp.where(row3 >= s, jnp.exp(b3[h] - b3[h][:, s:s + 1, :]), 0.0)
                a = jnp.sum(q3[h] * dec * k3[h][:, s:s + 1, :], axis=-1, keepdims=True)
                acc[h] = acc[h] + a * i3[h][:, s:s + 1, :]

        for h in hs:
            st = st_ref[h]
            oh = o[h] + acc[h].reshape(C, d) + _dot_nt((q[h] * jnp.exp(bcs[h])).astype(BF16), st.astype(BF16))
            kdec = (kk[h] * jnp.exp(rem[h])).astype(BF16)
            st_ref[h] = st * jnp.exp(bcs[h][C - 1:C, :]) + _dot(iv[h].T.astype(BF16), kdec)
            oh = oh * lax.rsqrt(jnp.mean(oh * oh, -1, keepdims=True) + 1e-6) * ng[h]
            g = g_ref[0, sl, lane[h]]
            o_ref[0, sl, lane[h]] = (oh * (g * jax.nn.sigmoid(g))).astype(o_ref.dtype)


def _hgrn_call(proj3, lb, norm_g, tt=512):
    B, S, _ = proj3.shape
    d = HG_HEAD_DIM
    C = HG_CHUNK
    tt = min(tt, S)
    mstack, masks = _hgrn_constants(C)
    nm = mstack.shape[0]
    nh = HG_HEADS_PER_STEP
    wid = nh * d
    col = lambda off: (lambda b, h, t: (b, t, off // wid + h))
    blk = (1, tt, wid)
    return pl.pallas_call(
        functools.partial(_hgrn_kernel, C=C, nchunk=tt // C, nh=nh),
        grid=(B, HG_HEADS // nh, S // tt),
        in_specs=[pl.BlockSpec(blk, col(OFF_HQ)), pl.BlockSpec(blk, col(OFF_HF)),
                  pl.BlockSpec(blk, col(OFF_HI)), pl.BlockSpec(blk, col(OFF_HG)),
                  pl.BlockSpec((nh, 1, d), lambda b, h, t: (h, 0, 0)),
                  pl.BlockSpec((nh, 1, d), lambda b, h, t: (h, 0, 0)),
                  pl.BlockSpec((nm, C), lambda b, h, t: (0, 0)),
                  pl.BlockSpec(masks.shape, lambda b, h, t: (0, 0, 0))],
        out_specs=pl.BlockSpec(blk, lambda b, h, t: (b, t, h)),
        out_shape=jax.ShapeDtypeStruct((B, S, HG_WIDTH), BF16),
        scratch_shapes=[pltpu.VMEM((nh, d, d), F32)],
        compiler_params=_params("parallel", "parallel", "arbitrary"),
        name="hgrn2",
    )(proj3, proj3, proj3, proj3, lb.reshape(HG_HEADS, 1, d), norm_g.reshape(HG_HEADS, 1, d),
      jnp.asarray(mstack, BF16), jnp.asarray(masks, F32))


LRU_PAD = SUBLANES
LRU_STEP = 16


def _lru_kernel(x_ref, y_ref, cw_ref, cb_ref, wa_ref, ba_ref, wx_ref, bx_ref, c_ref, o_ref,
                xpad_ref, h_ref, a_ref, u_ref, *, tt):
    W = LRU_WIDTH
    bw = W // LRU_BLOCKS

    @pl.when(pl.program_id(1) == 0)
    def _():
        xpad_ref[0:LRU_PAD, :] = jnp.zeros((LRU_PAD, W), F32)
        h_ref[...] = jnp.zeros_like(h_ref)

    x = x_ref[0]
    xpad_ref[LRU_PAD:LRU_PAD + tt, :] = x
    cw = cw_ref[...]
    xc = cb_ref[...] + cw[LRU_CONV - 1:LRU_CONV, :] * x
    for j in range(LRU_CONV - 1):
        off = LRU_PAD - (LRU_CONV - 1) + j
        xc = xc + cw[j:j + 1, :] * xpad_ref[off:off + tt, :]
    xpad_ref[0:LRU_PAD, :] = x[tt - LRU_PAD:tt, :]

    rs, is_ = [], []
    for gi in range(LRU_BLOCKS):
        xg = xc[:, gi * bw:(gi + 1) * bw].astype(BF16)
        rs.append(_dot(xg, wa_ref[gi]))
        is_.append(_dot(xg, wx_ref[gi]))
    r = jax.nn.sigmoid(jnp.concatenate(rs, -1) + ba_ref[...])
    ig = jax.nn.sigmoid(jnp.concatenate(is_, -1) + bx_ref[...])
    log_a = c_ref[...] * r
    a = jnp.exp(log_a)
    u = jnp.sqrt(-jnp.tanh(log_a) * (a * a + 1.0)) * (ig * xc)

    sub = lax.broadcasted_iota(jnp.int32, (tt, W), 0) % SUBLANES
    for dlt in (1, 2, 4):
        keep = sub >= dlt
        a_sh = jnp.where(keep, pltpu.roll(a, dlt, 0), 1.0)
        u_sh = jnp.where(keep, pltpu.roll(u, dlt, 0), 0.0)
        u = a * u_sh + u
        a = a * a_sh
    a_ref[...] = a
    u_ref[...] = u

    def step(i, h):
        r0 = pl.multiple_of(i * LRU_STEP, LRU_STEP)
        outs = []
        for k in range(LRU_STEP // SUBLANES):
            rows = pl.ds(r0 + k * SUBLANES, SUBLANES)
            hh = u_ref[rows, :] + a_ref[rows, :] * h
            outs.append(hh)
            h = hh[SUBLANES - 1:SUBLANES, :]
        rows = pl.ds(r0, LRU_STEP)
        o_ref[0, rows, :] = (jnp.concatenate(outs, 0) * jax.nn.gelu(y_ref[0, rows, :])).astype(o_ref.dtype)
        return h

    h_ref[...] = lax.fori_loop(0, tt // LRU_STEP, step, h_ref[...])


def _lru_call(proj3, conv_w, conv_b, wa, ba, wx, bx, lam, tt=512):
    B, S, _ = proj3.shape
    W = LRU_WIDTH
    bw = W // LRU_BLOCKS
    tt = min(tt, S)
    c = (-LRU_C * jax.nn.softplus(-lam.astype(F32))).reshape(1, W)
    vec = lambda: pl.BlockSpec((1, W), lambda b, t: (0, 0))
    return pl.pallas_call(
        functools.partial(_lru_kernel, tt=tt),
        grid=(B, S // tt),
        in_specs=[pl.BlockSpec((1, tt, W), lambda b, t: (b, t, OFF_LX // W)),
                  pl.BlockSpec((1, tt, W), lambda b, t: (b, t, OFF_LY // W)),
                  pl.BlockSpec((LRU_CONV, W), lambda b, t: (0, 0)), vec(),
                  pl.BlockSpec((LRU_BLOCKS, bw, bw), lambda b, t: (0, 0, 0)), vec(),
                  pl.BlockSpec((LRU_BLOCKS, bw, bw), lambda b, t: (0, 0, 0)), vec(), vec()],
        out_specs=pl.BlockSpec((1, tt, W), lambda b, t: (b, t, 0)),
        out_shape=jax.ShapeDtypeStruct((B, S, W), BF16),
        scratch_shapes=[pltpu.VMEM((tt + LRU_PAD, W), F32), pltpu.VMEM((1, W), F32),
                        pltpu.VMEM((tt, W), F32), pltpu.VMEM((tt, W), F32)],
        compiler_params=_params("parallel", "arbitrary"),
        name="rglru",
    )(proj3, proj3, conv_w.astype(F32), conv_b.reshape(1, W).astype(F32), wa.astype(BF16),
      ba.reshape(1, W), wx.astype(BF16), bx.reshape(1, W), c)


NSA_AUG = 4
NSA_KW = LANES
SEL_TILE = 256
SEL_TILE_BLOCKS = SEL_TILE // SEL_LEN
WIN_TILE = 128
POS_SPLIT = 128
SEL_ONEHOT = LANES
PLAN_HALF = 16
SELECT_PARTS = 4
SELECT_QB = 2
ATTEND_QB = 2
NSA_VROWS = NSA_HEAD_DIM + 16
KV_PREP_ROWS = 512


def _key_aug(pos, width, dead=False):
    n = pos.shape[0]
    col = lax.broadcasted_iota(jnp.int32, (n, width), 1)
    hi = ((pos // POS_SPLIT) * POS_SPLIT).astype(F32)
    lo = (pos % POS_SPLIT).astype(F32)
    live = jnp.where(col == 0, hi, jnp.where(col == 1, lo, jnp.where(col < NSA_AUG, 1.0, 0.0)))
    return jnp.where(dead, jnp.where(col == NSA_AUG, 1.0, 0.0), live)


def _query_t(q_ref, slope_ref, c0, row0=0):
    hd, P, T = NSA_HEAD_DIM, NSA_PER_GROUP, Q_BLOCK
    PT = P * T
    qT = (q_ref[0, row0:row0 + T, :] * (hd ** -0.5)).T
    qpart = jnp.concatenate([qT[p * hd:(p + 1) * hd, :] for p in range(P)], 1)
    tlane = c0 + lax.broadcasted_iota(jnp.int32, (1, PT), 1) % T
    slope = slope_ref[0]
    t_hi = ((tlane // POS_SPLIT) * POS_SPLIT).astype(F32)
    t_lo = (tlane % POS_SPLIT).astype(F32)
    rowi = lax.broadcasted_iota(jnp.int32, (NSA_KW - hd, PT), 0)
    aug = jnp.where(rowi < 2, slope,
                    jnp.where(rowi == 2, -slope * t_hi,
                              jnp.where(rowi == 3, -slope * t_lo, jnp.where(rowi == NSA_AUG, NEG_BIG, 0.0))))
    return jnp.concatenate([qpart, aug], 0).astype(BF16), tlane


def _compress_kernel(xk_ref, xv_ref, w1k_ref, pek_ref, w1fk_ref, w2k_ref, w1v_ref, pev_ref, w1fv_ref, w2v_ref,
                     kc_ref, vct_ref, *, nc):
    hd = NSA_HEAD_DIM
    pos = lax.broadcasted_iota(jnp.int32, (nc, 1), 0) * CMP_STRIDE + (CMP_LEN - 1)
    aug = _key_aug(pos, NSA_KW - hd)

    def one(x_ref, w1_ref, pe_ref, w1f_ref, w2_ref, g):
        uv = jnp.zeros((nc, 2 * CMP_HIDDEN), F32)
        for r in range(CMP_STRIDE):
            xr = x_ref[0, pl.ds(r, nc, stride=CMP_STRIDE), :][:, g * hd:(g + 1) * hd]
            uv = uv + _dot_hilo(xr, w1_ref[r])
        cvec = _dot_hilo(pe_ref[...], w1f_ref[...])[0:1, :]
        hid = uv[:, 0:CMP_HIDDEN] + pltpu.roll(uv[:, CMP_HIDDEN:], nc - 1, 0) + cvec
        return _dot_hilo(jax.nn.gelu(hid), w2_ref[...])

    for g in range(NSA_GROUPS):
        kc = one(xk_ref, w1k_ref, pek_ref, w1fk_ref, w2k_ref, g)
        kc_ref[0, g] = jnp.concatenate([kc, aug], 1).astype(BF16)
        vct_ref[0, g] = one(xv_ref, w1v_ref, pev_ref, w1fv_ref, w2v_ref, g).T.astype(BF16)


def _compress_call(proj3, pe_k, w1_k, w2_k, pe_v, w1_v, w2_v):
    B, S, _ = proj3.shape
    G, hd = NSA_GROUPS, NSA_HEAD_DIM
    nc = S // CMP_STRIDE
    half = CMP_STRIDE * hd

    def prep(pe, w1):
        w1 = w1.astype(F32)
        wr = jnp.concatenate([w1[:half].reshape(CMP_STRIDE, hd, CMP_HIDDEN),
                              w1[half:].reshape(CMP_STRIDE, hd, CMP_HIDDEN)], -1)
        pe8 = jnp.broadcast_to(pe.reshape(1, CMP_LEN * hd).astype(F32), (SUBLANES, CMP_LEN * hd))
        return wr, pe8, w1

    full = lambda a: pl.BlockSpec(a.shape, lambda b: tuple(0 for _ in a.shape))
    col = lambda off: pl.BlockSpec((1, S, LANES), lambda b: (b, 0, off // LANES))
    args = prep(pe_k, w1_k) + (w2_k.astype(F32),) + prep(pe_v, w1_v) + (w2_v.astype(F32),)
    return pl.pallas_call(
        functools.partial(_compress_kernel, nc=nc),
        grid=(B,),
        in_specs=[col(OFF_KC), col(OFF_VC)] + [full(a) for a in args],
        out_specs=[pl.BlockSpec((1, G, nc, NSA_KW), lambda b: (b, 0, 0, 0)),
                   pl.BlockSpec((1, G, hd, nc), lambda b: (b, 0, 0, 0))],
        out_shape=[jax.ShapeDtypeStruct((B, G, nc, NSA_KW), BF16),
                   jax.ShapeDtypeStruct((B, G, hd, nc), BF16)],
        compiler_params=_params("parallel"),
        name="nsa_compress",
    )(proj3, proj3, *args)


def _kvprep_kernel(ks_ref, vs_ref, kw_ref, vw_ref, ksa_ref, vst_ref, kwa_ref, vwt_ref, *, tt, n_live):
    hd = NSA_HEAD_DIM
    t = pl.program_id(1)
    dead = t >= n_live
    pos = t * tt + lax.broadcasted_iota(jnp.int32, (tt, 1), 0)
    aug = _key_aug(pos, NSA_KW - hd, dead)
    blk_col = lax.broadcasted_iota(jnp.int32, (tt, SEL_ONEHOT), 1)
    onehot = jnp.where(jnp.logical_and(blk_col == pos // SEL_LEN, jnp.logical_not(dead)), 1.0, 0.0)
    vrow = lax.broadcasted_iota(jnp.int32, (NSA_VROWS - hd, tt), 0)
    ones_rows = jnp.where(jnp.logical_and(vrow == 0, jnp.logical_not(dead)), 1.0, 0.0)
    for g in range(NSA_GROUPS):
        lanes = slice(g * hd, (g + 1) * hd)
        live = lambda ref: jnp.where(dead, 0.0, ref[0][:, lanes])
        ksa_ref[0, g] = jnp.concatenate([live(ks_ref), aug, onehot], 1).astype(BF16)
        kwa_ref[0, g] = jnp.concatenate([live(kw_ref), aug], 1).astype(BF16)
        vs_t = jnp.concatenate([live(vs_ref).T, ones_rows], 0).astype(BF16)
        vw_t = jnp.concatenate([live(vw_ref).T, ones_rows], 0).astype(BF16)
        for c in range(tt // SEL_TILE):
            vst_ref[0, g, c] = vs_t[:, c * SEL_TILE:(c + 1) * SEL_TILE]
        for c in range(tt // WIN_TILE):
            vwt_ref[0, g, c] = vw_t[:, c * WIN_TILE:(c + 1) * WIN_TILE]


def _kvprep_call(proj3):
    B, S, _ = proj3.shape
    G, hd = NSA_GROUPS, NSA_HEAD_DIM
    tt = min(KV_PREP_ROWS, S)
    n_live = S // tt
    assert tt == WINDOW
    sp = S + tt
    col = lambda off: pl.BlockSpec((1, tt, LANES), lambda b, t: (b, jnp.minimum(t, n_live - 1), off // LANES))
    front = lambda t: (t + 1) % (n_live + 1)
    return pl.pallas_call(
        functools.partial(_kvprep_kernel, tt=tt, n_live=n_live),
        grid=(B, n_live + 1),
        in_specs=[col(OFF_KS), col(OFF_VS), col(OFF_KW), col(OFF_VW)],
        out_specs=[pl.BlockSpec((1, G, tt, NSA_KW + SEL_ONEHOT), lambda b, t: (b, 0, t, 0)),
                   pl.BlockSpec((1, G, tt // SEL_TILE, NSA_VROWS, SEL_TILE), lambda b, t: (b, 0, t, 0, 0)),
                   pl.BlockSpec((1, G, tt, NSA_KW), lambda b, t: (b, 0, front(t), 0)),
                   pl.BlockSpec((1, G, tt // WIN_TILE, NSA_VROWS, WIN_TILE), lambda b, t: (b, 0, front(t), 0, 0))],
        out_shape=[jax.ShapeDtypeStruct((B, G, sp, NSA_KW + SEL_ONEHOT), BF16),
                   jax.ShapeDtypeStruct((B, G, sp // SEL_TILE, NSA_VROWS, SEL_TILE), BF16),
                   jax.ShapeDtypeStruct((B, G, sp, NSA_KW), BF16),
                   jax.ShapeDtypeStruct((B, G, sp // WIN_TILE, NSA_VROWS, WIN_TILE), BF16)],
        compiler_params=_params("parallel", "parallel"),
        name="nsa_kvprep",
    )(proj3, proj3, proj3, proj3)


def _importance_matrix(ns, nc):
    ratio = SEL_LEN // CMP_STRIDE
    a = np.zeros((ns, nc), np.float32)
    for j in range(ns):
        for n, wgt in ((ratio * j - 1, 0.5), (ratio * j, 1.0), (ratio * j + 1, 1.0),
                       (ratio * j + 2, 1.0), (ratio * j + 3, 0.5)):
            if 0 <= n < nc - 1:
                a[j, n] = wgt
    return a


def _plan_constants(ns):
    nt = ns // SEL_TILE_BLOCKS
    grp = (np.arange(ns)[None, :] // SEL_TILE_BLOCKS == np.arange(nt)[:, None]).astype(np.float32)
    k = np.arange(nt)
    w_lo = np.where(k < PLAN_HALF, 2.0 ** np.minimum(k, PLAN_HALF - 1), 0.0)
    w_hi = np.where(k >= PLAN_HALF, 2.0 ** np.maximum(k - PLAN_HALF, 0), 0.0)
    wts = np.stack([np.repeat(w_lo[:, None], LANES, 1), np.repeat(w_hi[:, None], LANES, 1)]).astype(np.float32)
    return grp, wts


def _select_kernel(q_ref, slope_ref, kc_ref, vct_ref, imp_ref, grp_ref, wts_ref, ocmp_ref, bias_ref, bits_ref,
                   *, ns, nc, qb):
    P, T = NSA_PER_GROUP, Q_BLOCK
    step = pl.program_id(2)
    nsteps = pl.num_programs(2)
    subs = range(qb)
    c0s = [(step * qb + u) * T for u in subs]
    qts, tlanes = zip(*[_query_t(q_ref, slope_ref, c0s[u], u * T) for u in subs])

    def body(rows, nblk):
        cend = lax.broadcasted_iota(jnp.int32, (rows, 1), 0) * CMP_STRIDE + (CMP_LEN - 1)
        cmask = [cend <= tlanes[u] for u in subs]
        s = [jnp.where(cmask[u], _dot(kc_ref[0, 0, 0:rows, :], qts[u]), NEG_BIG) for u in subs]
        m = [jnp.max(s[u], 0, keepdims=True) for u in subs]
        p_c = [jnp.where(cmask[u], jnp.exp(s[u] - m[u]), 0.0) for u in subs]
        p_c = [p_c[u] * (1.0 / jnp.maximum(jnp.sum(p_c[u], 0, keepdims=True), 1e-30)) for u in subs]
        for u in subs:
            ocmp_ref[0, 0, u] = _dot(vct_ref[0, 0][:, 0:rows], p_c[u].astype(BF16))

        psum = [p_c[u][:, 0:T] for u in subs]
        for p in range(1, P):
            psum = [psum[u] + p_c[u][:, p * T:(p + 1) * T] for u in subs]
        imp = [_dot_exact_lhs(imp_ref[0:nblk, 0:rows], psum[u]) for u in subs]
        blk = lax.broadcasted_iota(jnp.int32, (nblk, T), 0)
        val, chosen = [], []
        for u in subs:
            cur = (c0s[u] + lax.broadcasted_iota(jnp.int32, (nblk, T), 1)) // SEL_LEN
            forced = (blk == 0) | (blk == cur) | (blk == cur - 1)
            val.append(jnp.where(forced, 3e38, jnp.where(blk > cur, -1.0, imp[u])))
            chosen.append(jnp.zeros((nblk, T), F32))
        for _ in range(min(SEL_TOPK, nblk)):
            for u in subs:
                mx = jnp.max(val[u], 0, keepdims=True)
                first = jnp.min(jnp.where(val[u] == mx, blk, ns), 0, keepdims=True)
                pick = blk == first
                chosen[u] = jnp.where(pick, 1.0, chosen[u])
                val[u] = jnp.where(pick, -2.0, val[u])
        for u in subs:
            bias_ref[0, 0, u, 0:nblk, :] = jnp.where(chosen[u] > 0.5, 0.0, NEG_BIG)
            if nblk < ns:
                bias_ref[0, 0, u, nblk:ns, :] = jnp.full((ns - nblk, T), NEG_BIG, F32)
            ch = chosen[u]
            if nblk < ns:
                ch = jnp.concatenate([ch, jnp.zeros((ns - nblk, T), F32)], 0)
            cnt = _dot(grp_ref[...], ch.astype(BF16))
            act = jnp.where(jnp.max(cnt, 1, keepdims=True) > 0.5, 1.0, 0.0)
            lo = jnp.sum(act * wts_ref[0], 0, keepdims=True).astype(jnp.int32)
            hi = jnp.sum(act * wts_ref[1], 0, keepdims=True).astype(jnp.int32)
            bits_ref[u] = jnp.broadcast_to(lo | (hi << PLAN_HALF), (SUBLANES, LANES))

    parts = max(p for p in (SELECT_PARTS, 2, 1) if nc % (p * LANES) == 0 and ns % (p * 2 * SUBLANES) == 0)
    part = step * parts // nsteps
    for v in range(parts):
        pl.when(part == v)(functools.partial(body, nc * (v + 1) // parts, ns * (v + 1) // parts))


def _attend_kernel(bits_sref, q_ref, gt_ref, slope_ref, ocmp_ref, bias_ref, ks_ref, vst_ref, kw_ref, vwt_ref,
                   o_ref, list_ref, s_ref, m_ref, *, n_tiles_total, ns, qb):
    hd, P, T = NSA_HEAD_DIM, NSA_PER_GROUP, Q_BLOCK
    PT = P * T
    b, g, step = pl.program_id(0), pl.program_id(1), pl.program_id(2)
    subs = range(qb)
    iqs = [step * qb + u for u in subs]
    c0s = [iq * T for iq in iqs]
    qts, tlanes = zip(*[_query_t(q_ref, slope_ref, c0s[u], u * T) for u in subs])
    tl_row = lax.broadcasted_iota(jnp.int32, (1, PT), 1) % T

    gts = [jax.nn.sigmoid(gt_ref[0, u * T:(u + 1) * T, :]).T for u in subs]

    def gate_row(u, branch):
        return jnp.concatenate([gts[u][p * 3 + branch:p * 3 + branch + 1, :] for p in range(P)], 1)

    list_len = n_tiles_total + 1
    last_tiles = [(c0s[u] + T - 1) // SEL_TILE for u in subs]
    prev_bits = []
    for u in subs:
        bits = bits_sref[(b * pl.num_programs(1) + g) * (pl.num_programs(2) * qb) + iqs[u]]
        prev_bits.append(bits & (lax.shift_left(jnp.int32(1), last_tiles[u]) - 1))
    n_prevs = [jnp.int32(0) for _ in subs]
    for k in range(n_tiles_total):
        for u in subs:
            list_ref[u * list_len + n_prevs[u]] = k
            n_prevs[u] = n_prevs[u] + (lax.shift_right_logical(prev_bits[u], k) & 1)

    qt_sels = []
    for u in subs:
        sel_b = bias_ref[0, 0, u]
        if ns < SEL_ONEHOT:
            sel_b = jnp.concatenate([sel_b, jnp.zeros((SEL_ONEHOT - ns, T), F32)], 0)
        qt_sels.append(jnp.concatenate([qts[u], jnp.concatenate([sel_b.astype(BF16)] * P, 1)], 0))

    def tile_scores(u, kt):
        k0 = pl.multiple_of(kt * SEL_TILE, SEL_TILE)
        return _dot(ks_ref[0, 0, pl.ds(k0, SEL_TILE), :], qt_sels[u])

    def col_max(scores):
        m = jnp.max(scores[0], 0, keepdims=True)
        for sc in scores[1:]:
            m = jnp.maximum(m, jnp.max(sc, 0, keepdims=True))
        return m

    def softmax_update(carry, scores, vts, m_tile):
        m_run, acc = carry
        m_new = jnp.maximum(m_run, m_tile)
        acc = jnp.exp(m_run - m_new) * acc
        for sc, vt in zip(scores, vts):
            acc = acc + _dot(vt, jnp.exp(sc - m_new).astype(BF16))
        return m_new, acc

    init = (jnp.full((1, PT), NEG_BIG, F32), jnp.zeros((NSA_VROWS, PT), F32))
    dead_tile = n_tiles_total
    n_pairs = [jnp.maximum((n_prevs[u] + 1) // 2, 1) for u in subs]

    def pair_tiles(u, i):
        has_a, has_b = 2 * i < n_prevs[u], 2 * i + 1 < n_prevs[u]
        ka = jnp.where(has_a, list_ref[u * list_len + jnp.where(has_a, 2 * i, 0)], dead_tile)
        kb = jnp.where(has_b, list_ref[u * list_len + jnp.where(has_b, 2 * i + 1, 0)], dead_tile)
        return ka, kb

    def pair_scores(u, i):
        ka, kb = pair_tiles(u, i)
        keys = jnp.concatenate([ks_ref[0, 0, pl.ds(pl.multiple_of(k * SEL_TILE, SEL_TILE), SEL_TILE), :]
                                for k in (ka, kb)], 0)
        return _dot(keys, qt_sels[u])

    def put_scores(u, i, sc):
        slot = i % 2
        s_ref[u, slot] = sc
        m_ref[u, slot] = col_max([sc])

    def pair_probs(u, i, m_run):
        slot = i % 2
        m_new = jnp.maximum(m_run, m_ref[u, slot])
        return m_new, jnp.exp(s_ref[u, slot] - m_new).astype(BF16)

    def add_values(u, i, carry, m_new, pr):
        ka, kb = pair_tiles(u, i)
        m_run, acc = carry
        vt = jnp.concatenate([vst_ref[0, 0, ka], vst_ref[0, 0, kb]], 1)
        return m_new, jnp.exp(m_run - m_new) * acc + _dot(vt, pr)

    first = [pair_scores(u, 0) for u in subs]
    for u in subs:
        put_scores(u, 0, first[u])

    n_win = (WINDOW + T) // WIN_TILE
    roff = lax.broadcasted_iota(jnp.int32, (WIN_TILE, 1), 0)
    w_sc = [_dot(kw_ref[0, 0, pl.ds(pl.multiple_of(c0s[u], WIN_TILE), WINDOW + T), :], qts[u]) for u in subs]
    w_parts = [[jnp.where(roff > tl_row, w_sc[u][0:WIN_TILE], NEG_BIG),
                w_sc[u][WIN_TILE:WINDOW],
                jnp.where(roff <= tl_row, w_sc[u][WINDOW:WINDOW + T], NEG_BIG)]
               for u in subs]
    w_max = [col_max(w_parts[u]) for u in subs]
    out_cw = []
    for u in subs:
        vt = jnp.concatenate([vwt_ref[0, 0, iqs[u] + j] for j in range(n_win)], 1)
        _, acc_win = softmax_update(init, [jnp.concatenate(w_parts[u], 0)], [vt], w_max[u])
        out_cw.append(gate_row(u, 0) * ocmp_ref[0, 0, u] + gate_row(u, 2) * (acc_win[0:hd] / acc_win[hd:hd + 1]))

    def pipe_step(u, i, carry):
        m_new, pr = pair_probs(u, i, carry[0])
        sc_next = pair_scores(u, i + 1)
        carry = add_values(u, i, carry, m_new, pr)
        put_scores(u, i + 1, sc_next)
        return carry

    carries = [lax.fori_loop(0, n_pairs[u] - 1, functools.partial(pipe_step, u), init) for u in subs]
    probs = [pair_probs(u, n_pairs[u] - 1, carries[u][0]) for u in subs]
    s_last = []
    for u in subs:
        kpos = last_tiles[u] * SEL_TILE + lax.broadcasted_iota(jnp.int32, (SEL_TILE, 1), 0)
        s_last.append(jnp.where(kpos <= tlanes[u], tile_scores(u, last_tiles[u]), NEG_BIG))
    carries = [add_values(u, n_pairs[u] - 1, carries[u], *probs[u]) for u in subs]
    for u in subs:
        _, acc_sel = softmax_update(carries[u], [s_last[u]], [vst_ref[0, 0, last_tiles[u]]], col_max([s_last[u]]))
        out = out_cw[u] + gate_row(u, 1) * (acc_sel[0:hd] / acc_sel[hd:hd + 1])
        o_ref[0, u * T:(u + 1) * T, :] = jnp.concatenate(
            [out[:, p * T:(p + 1) * T] for p in range(P)], 0).T.astype(o_ref.dtype)


def _nsa_call(proj3, pe_k, w1_k, w2_k, pe_v, w1_v, w2_v):
    B, S, _ = proj3.shape
    G, hd, P, T = NSA_GROUPS, NSA_HEAD_DIM, NSA_PER_GROUP, Q_BLOCK
    PT = P * T
    nc = S // CMP_STRIDE
    ns = S // SEL_LEN
    nq = S // T
    nt = ns // SEL_TILE_BLOCKS
    assert nt <= 2 * PLAN_HALF and S % KV_PREP_ROWS == 0

    kc_a, vc_t = _compress_call(proj3, pe_k, w1_k, w2_k, pe_v, w1_v, w2_v)
    ks_a, vs_t, kw_a, vw_t = _kvprep_call(proj3)
    imp_m = jnp.asarray(_importance_matrix(ns, nc), BF16)
    grp, wts = _plan_constants(ns)
    head_slopes = 2.0 ** (-(8.0 / NSA_HEADS) * np.arange(1, NSA_HEADS + 1, dtype=np.float32))
    slopes = jnp.asarray(np.repeat(head_slopes.reshape(G, 1, P), T, axis=-1), F32)

    q_spec = lambda im: pl.BlockSpec((1, T, P * hd), im)
    bg = lambda shp: pl.BlockSpec((1, 1) + shp, lambda b, g, i: (b, g) + tuple(0 for _ in shp))
    const = lambda a: pl.BlockSpec(a.shape, lambda b, g, i: tuple(0 for _ in a.shape))
    qb = SELECT_QB if nq % (SELECT_QB * SELECT_PARTS) == 0 else 1
    nsteps = nq // qb
    ocmp, sel_bias, bits = pl.pallas_call(
        functools.partial(_select_kernel, ns=ns, nc=nc, qb=qb),
        grid=(B, G, nsteps),
        in_specs=[pl.BlockSpec((1, qb * T, P * hd), lambda b, g, i: (b, i, OFF_NQ // (P * hd) + g)),
                  pl.BlockSpec((1, 1, PT), lambda b, g, i: (g, 0, 0)),
                  bg((nc, NSA_KW)), bg((hd, nc)), const(imp_m),
                  pl.BlockSpec(grp.shape, lambda b, g, i: (0, 0)),
                  pl.BlockSpec(wts.shape, lambda b, g, i: (0, 0, 0))],
        out_specs=[pl.BlockSpec((1, 1, qb, hd, PT), lambda b, g, i: (b, g, i, 0, 0)),
                   pl.BlockSpec((1, 1, qb, ns, T), lambda b, g, i: (b, g, i, 0, 0)),
                   pl.BlockSpec((qb, SUBLANES, LANES), lambda b, g, i: ((b * G + g) * nsteps + i, 0, 0))],
        out_shape=[jax.ShapeDtypeStruct((B, G, nq, hd, PT), F32),
                   jax.ShapeDtypeStruct((B, G, nq, ns, T), F32),
                   jax.ShapeDtypeStruct((B * G * nq, SUBLANES, LANES), jnp.int32)],
        compiler_params=_params("parallel", "parallel", "parallel"),
        name="nsa_select",
    )(proj3, slopes, kc_a, vc_t, imp_m, jnp.asarray(grp, BF16), jnp.asarray(wts, F32))

    aq = ATTEND_QB if nq % ATTEND_QB == 0 else 1
    bg2 = lambda shp: pl.BlockSpec((1, 1) + shp, lambda b, g, i, s: (b, g) + tuple(0 for _ in shp))
    return pl.pallas_call(
        functools.partial(_attend_kernel, n_tiles_total=nt, ns=ns, qb=aq),
        grid_spec=pltpu.PrefetchScalarGridSpec(
            num_scalar_prefetch=1,
            grid=(B, G, nq // aq),
            in_specs=[pl.BlockSpec((1, aq * T, P * hd), lambda b, g, i, s: (b, i, OFF_NQ // (P * hd) + g)),
                      pl.BlockSpec((1, aq * T, LANES), lambda b, g, i, s: (b, i, OFF_NG // LANES + g)),
                      pl.BlockSpec((1, 1, PT), lambda b, g, i, s: (g, 0, 0)),
                      pl.BlockSpec((1, 1, aq, hd, PT), lambda b, g, i, s: (b, g, i, 0, 0)),
                      pl.BlockSpec((1, 1, aq, ns, T), lambda b, g, i, s: (b, g, i, 0, 0)),
                      bg2(ks_a.shape[2:]), bg2(vs_t.shape[2:]), bg2(kw_a.shape[2:]), bg2(vw_t.shape[2:])],
            out_specs=pl.BlockSpec((1, aq * T, P * hd), lambda b, g, i, s: (b, i, g)),
            scratch_shapes=[pltpu.SMEM((aq * (nt + 1),), jnp.int32),
                            pltpu.VMEM((aq, 2, 2 * SEL_TILE, PT), F32), pltpu.VMEM((aq, 2, 1, PT), F32)]),
        out_shape=jax.ShapeDtypeStruct((B, S, NSA_WIDTH), BF16),
        compiler_params=_params("parallel", "parallel", "arbitrary"),
        name="nsa_attend",
    )(bits[:, 0, 0], proj3, proj3, slopes, ocmp, sel_bias, ks_a, vs_t, kw_a, vw_t)


def _merge_kernel(x_ref, ya_ref, yb_ref, yc_ref, g0_ref, g1_ref, g2_ref, wa_ref, wb_ref, wc_ref, wo_ref,
                  lg_ref, lb_ref, o_ref):
    m = jax.nn.sigmoid(g0_ref[...]) * _dot(ya_ref[...], wa_ref[...])
    m = m + jax.nn.sigmoid(g1_ref[...]) * _dot(yb_ref[...], wb_ref[...])
    m = m + jax.nn.sigmoid(g2_ref[...]) * _dot(yc_ref[...], wc_ref[...])
    z = DEEPNORM_ALPHA * x_ref[...] + _dot(m.astype(BF16), wo_ref[...])
    o_ref[...] = _layer_norm_rows(z, lg_ref[...], lb_ref[...])


def _merge_call(x2, ya, yb, yc, proj2, w_hg, w_nsa, w_lru, w_out, ln_g, ln_b, tm=512):
    T, D = x2.shape
    tm = min(tm, T)
    rows = lambda w: pl.BlockSpec((tm, w), lambda i: (i, 0))
    gate = lambda n: pl.BlockSpec((tm, D), lambda i: (i, OFF_MG // D + n))
    full = lambda a: pl.BlockSpec(a.shape, lambda i: (0, 0))
    ws = [w.astype(BF16) for w in (w_hg, w_nsa, w_lru, w_out)]
    return pl.pallas_call(
        _merge_kernel,
        grid=(T // tm,),
        in_specs=[rows(D), rows(HG_WIDTH), rows(NSA_WIDTH), rows(LRU_WIDTH), gate(0), gate(1), gate(2)]
                 + [full(w) for w in ws] + [pl.BlockSpec((1, D), lambda i: (0, 0))] * 2,
        out_specs=rows(D),
        out_shape=jax.ShapeDtypeStruct((T, D), F32),
        compiler_params=_params("parallel"),
        name="merge_out",
    )(x2, ya, yb, yc, proj2, proj2, proj2, *ws, ln_g.reshape(1, D), ln_b.reshape(1, D))


FFN_SPLIT = 2
FFN_PAD = SUBLANES
FFN_SUB = 256


def _ffn_kernel(x_ref, wu_ref, wv_ref, cw_ref, cb_ref, wd_ref, lg_ref, lb_ref, o_ref,
                xb_ref, acc_ref, upad_ref, carry_ref, *, tm, tiles_per_seq):
    i = pl.program_id(0)
    j = pl.program_id(1)

    @pl.when(j == 0)
    def _():
        xb_ref[...] = x_ref[...].astype(BF16)

    xb = xb_ref[...]
    fc = wu_ref.shape[1]
    first = (i % tiles_per_seq) == 0

    @pl.when(first)
    def _():
        upad_ref[0:FFN_PAD, :] = jnp.zeros((FFN_PAD, fc), F32)

    @pl.when(jnp.logical_not(first))
    def _():
        upad_ref[0:FFN_PAD, :] = carry_ref[j]

    @pl.when(j == 0)
    def _():
        acc_ref[...] = jnp.zeros_like(acc_ref)

    for c0 in range(0, fc, FFN_SUB):
        cs = slice(c0, min(c0 + FFN_SUB, fc))
        u = _dot(xb, wu_ref[:, cs])
        v = _dot(xb, wv_ref[:, cs])
        upad_ref[FFN_PAD:FFN_PAD + tm, cs] = u
        cw = cw_ref[:, cs]
        cv = cb_ref[:, cs] + cw[FFN_CONV - 1:FFN_CONV, :] * u
        for k in range(FFN_CONV - 1):
            off = FFN_PAD - (FFN_CONV - 1) + k
            cv = cv + cw[k:k + 1, :] * upad_ref[off:off + tm, cs]
        carry_ref[j, :, cs] = u[tm - FFN_PAD:tm, :]
        h = (jax.nn.gelu(cv) * v).astype(BF16)
        acc_ref[...] += _dot(h, wd_ref[cs, :])

    @pl.when(j == FFN_SPLIT - 1)
    def _():
        z = DEEPNORM_ALPHA * x_ref[...] + acc_ref[...]
        o_ref[...] = _layer_norm_rows(z, lg_ref[...], lb_ref[...])


def _ffn_call(x2, seq_len, w_up, conv_w, conv_b, w_down, ln_g, ln_b, tm=512):
    T, D = x2.shape
    tm = min(tm, seq_len)
    fc = FFN_DIM // FFN_SPLIT
    wu = w_up.astype(BF16)
    return pl.pallas_call(
        functools.partial(_ffn_kernel, tm=tm, tiles_per_seq=seq_len // tm),
        grid=(T // tm, FFN_SPLIT),
        in_specs=[pl.BlockSpec((tm, D), lambda i, j: (i, 0)),
                  pl.BlockSpec((D, fc), lambda i, j: (0, j)),
                  pl.BlockSpec((D, fc), lambda i, j: (0, FFN_SPLIT + j)),
                  pl.BlockSpec((FFN_CONV, fc), lambda i, j: (0, j)),
                  pl.BlockSpec((1, fc), lambda i, j: (0, j)),
                  pl.BlockSpec((fc, D), lambda i, j: (j, 0)),
                  pl.BlockSpec((1, D), lambda i, j: (0, 0)),
                  pl.BlockSpec((1, D), lambda i, j: (0, 0))],
        out_specs=pl.BlockSpec((tm, D), lambda i, j: (i, 0)),
        out_shape=jax.ShapeDtypeStruct((T, D), F32),
        scratch_shapes=[pltpu.VMEM((tm, D), BF16), pltpu.VMEM((tm, D), F32),
                        pltpu.VMEM((tm + FFN_PAD, fc), F32), pltpu.VMEM((FFN_SPLIT, FFN_PAD, fc), F32)],
        compiler_params=_params("arbitrary", "arbitrary"),
        name="conv_ffn",
    )(x2, wu, wu, conv_w.astype(F32), conv_b.reshape(1, FFN_DIM).astype(F32), w_down.astype(BF16),
      ln_g.reshape(1, D), ln_b.reshape(1, D))


def _permute_in_proj(w, b):
    src = np.cumsum([0, HG_WIDTH, HG_WIDTH, HG_WIDTH, HG_WIDTH, NSA_WIDTH, NSA_KV, NSA_KV, NSA_KV, NSA_KV,
                     NSA_KV, NSA_KV, NSA_HEADS * 3, LRU_WIDTH, LRU_WIDTH, N_BRANCH * 1024])
    names = ["hq", "hf", "hi", "hg", "nq", "kc", "vc", "ks", "vs", "kw", "vw", "ng", "lx", "ly", "mg"]
    seg = {n: (int(src[k]), int(src[k + 1])) for k, n in enumerate(names)}
    order = ["mg", "hq", "hf", "hi", "hg", "nq", "lx", "ly", "kc", "vc", "ks", "vs", "kw", "vw"]
    per_group = NSA_PER_GROUP * 3
    ng0 = seg["ng"][0]
    w = w.astype(BF16)
    w_parts = [w[:, seg[n][0]:seg[n][1]] for n in order]
    b_parts = [b[seg[n][0]:seg[n][1]] for n in order]
    for gi in range(NSA_GROUPS):
        lo = ng0 + gi * per_group
        w_parts += [w[:, lo:lo + per_group], jnp.zeros((w.shape[0], LANES - per_group), w.dtype)]
        b_parts += [b[lo:lo + per_group], jnp.zeros((LANES - per_group,), b.dtype)]
    wp = jnp.concatenate(w_parts, 1)
    bp = jnp.concatenate(b_parts)
    assert wp.shape[1] == PROJ_COLS
    return wp.astype(BF16), bp.astype(F32)


def kernel(x, ln_emb_g, ln_emb_b, w_in, b_in, hg_lb_logits, hg_norm_g, cmp_pe_k, cmp_w1_k, cmp_w2_k, cmp_pe_v, cmp_w1_v, cmp_w2_v, lru_conv_w, lru_conv_b, lru_wa, lru_ba, lru_wx, lru_bx, lru_lambda, w_branch_hg, w_branch_nsa, w_branch_lru, w_out, ln1_g, ln1_b, ffn_w_up, ffn_conv_w, ffn_conv_b, ffn_w_down, ln2_g, ln2_b):
    B, S, D = x.shape
    T = B * S
    gam = jax.nn.softmax(hg_lb_logits.astype(F32), axis=0)
    lb_all = jnp.cumsum(gam, axis=0) - gam[0]
    h = x.reshape(T, D)
    for l in range(DEPTH):
        wp, bp = _permute_in_proj(w_in[l], b_in[l])
        if l == 0:
            proj2, h = _inproj_ln_call(h, ln_emb_g, ln_emb_b, wp, bp)
        else:
            proj2 = _inproj_call(h, wp, bp)
        proj3 = proj2.reshape(B, S, PROJ_COLS)
        y_a = _hgrn_call(proj3, lb_all[l], hg_norm_g[l])
        y_b = _nsa_call(proj3, cmp_pe_k[l], cmp_w1_k[l], cmp_w2_k[l], cmp_pe_v[l], cmp_w1_v[l], cmp_w2_v[l])
        y_c = _lru_call(proj3, lru_conv_w[l], lru_conv_b[l], lru_wa[l], lru_ba[l], lru_wx[l], lru_bx[l],
                        lru_lambda[l])
        h = _merge_call(h, y_a.reshape(T, HG_WIDTH), y_b.reshape(T, NSA_WIDTH), y_c.reshape(T, LRU_WIDTH),
                        proj2, w_branch_hg[l], w_branch_nsa[l], w_branch_lru[l], w_out[l], ln1_g[l], ln1_b[l])
        h = _ffn_call(h, S, ffn_w_up[l], ffn_conv_w[l], ffn_conv_b[l], ffn_w_down[l], ln2_g[l], ln2_b[l])
    return h.reshape(B, S, D)
```

```python
import functools

import numpy as np
import jax
import jax.numpy as jnp
from jax import lax
from jax.experimental import pallas as pl
from jax.experimental.pallas import tpu as pltpu

F32 = jnp.float32
BF16 = jnp.bfloat16

DEPTH = 2
HG_HEADS = 4
HG_HEAD_DIM = 128
HG_WIDTH = HG_HEADS * HG_HEAD_DIM
NSA_HEADS = 8
NSA_GROUPS = 2
NSA_PER_GROUP = NSA_HEADS // NSA_GROUPS
NSA_HEAD_DIM = 64
NSA_WIDTH = NSA_HEADS * NSA_HEAD_DIM
NSA_KV = NSA_GROUPS * NSA_HEAD_DIM
CMP_LEN = 32
CMP_STRIDE = 16
CMP_HIDDEN = 64
SEL_LEN = 64
SEL_TOPK = 16
WINDOW = 512
Q_BLOCK = 128
LRU_WIDTH = 512
LRU_BLOCKS = 4
LRU_CONV = 4
LRU_C = 8.0
FFN_DIM = 2816
FFN_CONV = 3
DEEPNORM_ALPHA = (2 * DEPTH) ** 0.25
LN_EPS = 1e-5
N_BRANCH = 3

LANES = 128
SUBLANES = 8
VMEM_LIMIT_BYTES = 52 * 1024 * 1024

GATE_COLS = N_BRANCH * 1024
OFF_HQ = 0
OFF_HF = OFF_HQ + HG_WIDTH
OFF_HI = OFF_HF + HG_WIDTH
OFF_HG = OFF_HI + HG_WIDTH
OFF_NQ = OFF_HG + HG_WIDTH
OFF_LX = OFF_NQ + NSA_WIDTH
OFF_LY = OFF_LX + LRU_WIDTH
OFF_KC = OFF_LY + LRU_WIDTH
OFF_VC = OFF_KC + NSA_KV
OFF_KS = OFF_VC + NSA_KV
OFF_VS = OFF_KS + NSA_KV
OFF_KW = OFF_VS + NSA_KV
OFF_VW = OFF_KW + NSA_KV
OFF_NG = OFF_VW + NSA_KV
PROJ_TILE_N = 1536
GATE_TILES = GATE_COLS // PROJ_TILE_N
PROJ_COLS = OFF_NG + NSA_GROUPS * LANES

NEG_BIG = -1e30


def _split3(x):
    hi = x.astype(BF16)
    r1 = x - hi.astype(F32)
    mid = r1.astype(BF16)
    lo = (r1 - mid.astype(F32)).astype(BF16)
    return hi, mid, lo


def _dot(a, b):
    return jnp.dot(a, b, preferred_element_type=F32)


def _dot_nt(a, b):
    return lax.dot_general(a, b, (((1,), (1,)), ((), ())), preferred_element_type=F32)


def _dot_exact_lhs(m_bf16, x):
    hi, mid, lo = _split3(x)
    return _dot(m_bf16, hi) + _dot(m_bf16, mid) + _dot(m_bf16, lo)


def _dot_hilo(a, b):
    ah = a.astype(BF16)
    al = (a - ah.astype(F32)).astype(BF16)
    bh = b.astype(BF16)
    bl = (b - bh.astype(F32)).astype(BF16)
    return _dot(ah, bh) + _dot(ah, bl) + _dot(al, bh) + _dot(al, bl)


def _layer_norm_rows(z, g, b):
    mu = jnp.mean(z, -1, keepdims=True)
    zc = z - mu
    var = jnp.mean(zc * zc, -1, keepdims=True)
    return zc * lax.rsqrt(var + LN_EPS) * g + b


def _params(*sem):
    return pltpu.CompilerParams(dimension_semantics=sem, vmem_limit_bytes=VMEM_LIMIT_BYTES)


def _inproj_kernel(*refs, with_ln):
    if with_ln:
        x_ref, g_ref, be_ref, w_ref, b_ref, gate_ref, o_ref, xn_ref, xb_ref = refs
    else:
        x_ref, w_ref, b_ref, gate_ref, o_ref, xb_ref = refs
    j = pl.program_id(1)

    @pl.when(j == 0)
    def _():
        x = x_ref[...]
        if with_ln:
            x = _layer_norm_rows(x, g_ref[...], be_ref[...])
            xn_ref[...] = x
        xb_ref[...] = x.astype(BF16)

    acc = _dot(xb_ref[...], w_ref[...]) + b_ref[...]

    @pl.when(j < GATE_TILES)
    def _():
        gate_ref[...] = acc.astype(BF16)

    @pl.when(j >= GATE_TILES)
    def _():
        o_ref[...] = acc


def _inproj_call(x2, w_bf16, bias, ln=None, tm=1024):
    T, D = x2.shape
    N = w_bf16.shape[1]
    tm = min(tm, T)
    tn = PROJ_TILE_N
    vec = pl.BlockSpec((1, D), lambda i, j: (0, 0))
    in_specs = [pl.BlockSpec((tm, D), lambda i, j: (i, 0))] + ([vec, vec] if ln else []) + [
        pl.BlockSpec((D, tn), lambda i, j: (0, j)), pl.BlockSpec((1, tn), lambda i, j: (0, j))]
    out_specs = [pl.BlockSpec((tm, tn), lambda i, j: (i, jnp.minimum(j, GATE_TILES - 1))),
                 pl.BlockSpec((tm, tn), lambda i, j: (i, jnp.maximum(j - GATE_TILES, 0)))]
    out_shape = [jax.ShapeDtypeStruct((T, GATE_COLS), BF16), jax.ShapeDtypeStruct((T, N - GATE_COLS), F32)]
    if ln:
        out_specs.append(pl.BlockSpec((tm, D), lambda i, j: (i, 0)))
        out_shape.append(jax.ShapeDtypeStruct((T, D), F32))
    ln_args = [ln[0].reshape(1, D), ln[1].reshape(1, D)] if ln else []
    return pl.pallas_call(
        functools.partial(_inproj_kernel, with_ln=bool(ln)),
        grid=(T // tm, N // tn),
        in_specs=in_specs, out_specs=out_specs, out_shape=out_shape,
        scratch_shapes=[pltpu.VMEM((tm, D), BF16)],
        compiler_params=_params("parallel", "arbitrary"),
        name="in_proj_ln" if ln else "in_proj",
    )(x2, *ln_args, w_bf16, bias.reshape(1, N))


HG_CHUNK = 128
HG_DIAG = SUBLANES
HG_HEADS_PER_STEP = 2


def _hgrn_constants(C):
    r = np.arange(C)
    tri = (r[None, :] <= r[:, None]).astype(np.float32)
    mats = [tri]
    masks = []
    w = HG_DIAG
    while 2 * w <= C:
        mid = (r // (2 * w)) * (2 * w) + w
        mats.append(tri - (r[None, :] <= mid[:, None]).astype(np.float32))
        same = (r[:, None] // (2 * w)) == (r[None, :] // (2 * w))
        masks.append((same & ((r[:, None] % (2 * w)) >= w) & ((r[None, :] % (2 * w)) < w)).astype(np.float32))
        w *= 2
    mats.append((r[None, :] > r[:, None]).astype(np.float32))
    return np.concatenate(mats, 0), np.stack(masks, 0)


def _hgrn_kernel(q_ref, f_ref, i_ref, g_ref, lb_ref, ng_ref, m_ref, mask_ref, o_ref, st_ref, *, C, nchunk, nh):
    d = HG_HEAD_DIM
    nlev = mask_ref.shape[0]

    @pl.when(pl.program_id(2) == 0)
    def _():
        st_ref[...] = jnp.zeros_like(st_ref)

    hs = range(nh)
    lane = [slice(h * d, (h + 1) * d) for h in hs]
    lb = [lb_ref[h] for h in hs]
    ng = [ng_ref[h] for h in hs]
    nb = C // HG_DIAG
    row3 = lax.broadcasted_iota(jnp.int32, (nb, HG_DIAG, d), 1)

    fl_all = f_ref[0]
    lf_all = [jnp.log(lb[h] + (1.0 - lb[h]) * jax.nn.sigmoid(fl_all[:, lane[h]])) for h in hs]
    kk_all = [(1.0 - lb[h]) * jax.nn.sigmoid(-fl_all[:, lane[h]]) for h in hs]
    allm_all = _dot_exact_lhs(m_ref[...], jnp.concatenate(
        [lf_all[h][c * C:(c + 1) * C] for h in hs for c in range(nchunk)], 1))

    for c in range(nchunk):
        sl = pl.ds(c * C, C)
        q = [q_ref[0, sl, lane[h]] for h in hs]
        iv = [i_ref[0, sl, lane[h]] for h in hs]
        kk = [kk_all[h][c * C:(c + 1) * C] for h in hs]
        allm = [allm_all[:, (h * nchunk + c) * d:(h * nchunk + c + 1) * d] for h in hs]
        bcs = [allm[h][0:C] for h in hs]
        rem = [allm[h][(nlev + 1) * C:(nlev + 2) * C] for h in hs]
        iv_b = [iv[h].astype(BF16) for h in hs]

        att = [jnp.zeros((C, C), F32) for _ in hs]
        for l in range(nlev):
            e = [jnp.exp(-jnp.abs(allm[h][(1 + l) * C:(2 + l) * C])) for h in hs]
            att = [att[h] + mask_ref[l] * _dot_nt((q[h] * e[h]).astype(BF16), (kk[h] * e[h]).astype(BF16))
                   for h in hs]
        o = [_dot(att[h].astype(BF16), iv_b[h]) for h in hs]

        q3 = [q[h].reshape(nb, HG_DIAG, d) for h in hs]
        k3 = [kk[h].reshape(nb, HG_DIAG, d) for h in hs]
        b3 = [bcs[h].reshape(nb, HG_DIAG, d) for h in hs]
        i3 = [iv[h].reshape(nb, HG_DIAG, d) for h in hs]
        acc = [jnp.zeros((nb, HG_DIAG, d), F32) for _ in hs]
        for s in range(HG_DIAG):
            for h in hs:
                dec = jnp.where(row3 >= s, jnp.exp(b3[h] - b3[h][:, s:s + 1, :]), 0.0)
                a = jnp.sum(q3[h] * dec * k3[h][:, s:s + 1, :], axis=-1, keepdims=True)
                acc[h] = acc[h] + a * i3[h][:, s:s + 1, :]

        for h in hs:
            st = st_ref[h]
            oh = o[h] + acc[h].reshape(C, d) + _dot_nt((q[h] * jnp.exp(bcs[h])).astype(BF16), st.astype(BF16))
            kdec = (kk[h] * jnp.exp(rem[h])).astype(BF16)
            st_ref[h] = st * jnp.exp(bcs[h][C - 1:C, :]) + _dot(iv[h].T.astype(BF16), kdec)
            oh = oh * lax.rsqrt(jnp.mean(oh * oh, -1, keepdims=True) + 1e-6) * ng[h]
            g = g_ref[0, sl, lane[h]]
            o_ref[0, sl, lane[h]] = (oh * (g * jax.nn.sigmoid(g))).astype(o_ref.dtype)


def _hgrn_call(proj3, lb, norm_g, tt=512):
    B, S, _ = proj3.shape
    d = HG_HEAD_DIM
    C = HG_CHUNK
    tt = min(tt, S)
    mstack, masks = _hgrn_constants(C)
    nm = mstack.shape[0]
    nh = HG_HEADS_PER_STEP
    wid = nh * d
    col = lambda off: (lambda b, h, t: (b, t, off // wid + h))
    blk = (1, tt, wid)
    return pl.pallas_call(
        functools.partial(_hgrn_kernel, C=C, nchunk=tt // C, nh=nh),
        grid=(B, HG_HEADS // nh, S // tt),
        in_specs=[pl.BlockSpec(blk, col(OFF_HQ)), pl.BlockSpec(blk, col(OFF_HF)),
                  pl.BlockSpec(blk, col(OFF_HI)), pl.BlockSpec(blk, col(OFF_HG)),
                  pl.BlockSpec((nh, 1, d), lambda b, h, t: (h, 0, 0)),
                  pl.BlockSpec((nh, 1, d), lambda b, h, t: (h, 0, 0)),
                  pl.BlockSpec((nm, C), lambda b, h, t: (0, 0)),
                  pl.BlockSpec(masks.shape, lambda b, h, t: (0, 0, 0))],
        out_specs=pl.BlockSpec(blk, lambda b, h, t: (b, t, h)),
        out_shape=jax.ShapeDtypeStruct((B, S, HG_WIDTH), BF16),
        scratch_shapes=[pltpu.VMEM((nh, d, d), F32)],
        compiler_params=_params("parallel", "parallel", "arbitrary"),
        name="hgrn2",
    )(proj3, proj3, proj3, proj3, lb.reshape(HG_HEADS, 1, d), norm_g.reshape(HG_HEADS, 1, d),
      jnp.asarray(mstack, BF16), jnp.asarray(masks, F32))


LRU_PAD = SUBLANES
LRU_STEP = 16


def _lru_kernel(x_ref, y_ref, cw_ref, cb_ref, wa_ref, ba_ref, wx_ref, bx_ref, c_ref, o_ref,
                xpad_ref, h_ref, a_ref, u_ref, *, tt):
    W = LRU_WIDTH
    bw = W // LRU_BLOCKS

    @pl.when(pl.program_id(1) == 0)
    def _():
        xpad_ref[0:LRU_PAD, :] = jnp.zeros((LRU_PAD, W), F32)
        h_ref[...] = jnp.zeros_like(h_ref)

    x = x_ref[0]
    xpad_ref[LRU_PAD:LRU_PAD + tt, :] = x
    cw = cw_ref[...]
    xc = cb_ref[...] + cw[LRU_CONV - 1:LRU_CONV, :] * x
    for j in range(LRU_CONV - 1):
        off = LRU_PAD - (LRU_CONV - 1) + j
        xc = xc + cw[j:j + 1, :] * xpad_ref[off:off + tt, :]
    xpad_ref[0:LRU_PAD, :] = x[tt - LRU_PAD:tt, :]

    rs, is_ = [], []
    for gi in range(LRU_BLOCKS):
        xg = xc[:, gi * bw:(gi + 1) * bw].astype(BF16)
        rs.append(_dot(xg, wa_ref[gi]))
        is_.append(_dot(xg, wx_ref[gi]))
    r = jax.nn.sigmoid(jnp.concatenate(rs, -1) + ba_ref[...])
    ig = jax.nn.sigmoid(jnp.concatenate(is_, -1) + bx_ref[...])
    log_a = c_ref[...] * r
    a = jnp.exp(log_a)
    u = jnp.sqrt(-jnp.tanh(log_a) * (a * a + 1.0)) * (ig * xc)

    sub = lax.broadcasted_iota(jnp.int32, (tt, W), 0) % SUBLANES
    for dlt in (1, 2, 4):
        keep = sub >= dlt
        a_sh = jnp.where(keep, pltpu.roll(a, dlt, 0), 1.0)
        u_sh = jnp.where(keep, pltpu.roll(u, dlt, 0), 0.0)
        u = a * u_sh + u
        a = a * a_sh
    a_ref[...] = a
    u_ref[...] = u

    def step(i, h):
        r0 = pl.multiple_of(i * LRU_STEP, LRU_STEP)
        outs = []
        for k in range(LRU_STEP // SUBLANES):
            rows = pl.ds(r0 + k * SUBLANES, SUBLANES)
            hh = u_ref[rows, :] + a_ref[rows, :] * h
            outs.append(hh)
            h = hh[SUBLANES - 1:SUBLANES, :]
        rows = pl.ds(r0, LRU_STEP)
        o_ref[0, rows, :] = (jnp.concatenate(outs, 0) * jax.nn.gelu(y_ref[0, rows, :])).astype(o_ref.dtype)
        return h

    h_ref[...] = lax.fori_loop(0, tt // LRU_STEP, step, h_ref[...])


def _lru_call(proj3, conv_w, conv_b, wa, ba, wx, bx, lam, tt=512):
    B, S, _ = proj3.shape
    W = LRU_WIDTH
    bw = W // LRU_BLOCKS
    tt = min(tt, S)
    c = (-LRU_C * jax.nn.softplus(-lam.astype(F32))).reshape(1, W)
    vec = lambda: pl.BlockSpec((1, W), lambda b, t: (0, 0))
    return pl.pallas_call(
        functools.partial(_lru_kernel, tt=tt),
        grid=(B, S // tt),
        in_specs=[pl.BlockSpec((1, tt, W), lambda b, t: (b, t, OFF_LX // W)),
                  pl.BlockSpec((1, tt, W), lambda b, t: (b, t, OFF_LY // W)),
                  pl.BlockSpec((LRU_CONV, W), lambda b, t: (0, 0)), vec(),
                  pl.BlockSpec((LRU_BLOCKS, bw, bw), lambda b, t: (0, 0, 0)), vec(),
                  pl.BlockSpec((LRU_BLOCKS, bw, bw), lambda b, t: (0, 0, 0)), vec(), vec()],
        out_specs=pl.BlockSpec((1, tt, W), lambda b, t: (b, t, 0)),
        out_shape=jax.ShapeDtypeStruct((B, S, W), BF16),
        scratch_shapes=[pltpu.VMEM((tt + LRU_PAD, W), F32), pltpu.VMEM((1, W), F32),
                        pltpu.VMEM((tt, W), F32), pltpu.VMEM((tt, W), F32)],
        compiler_params=_params("parallel", "arbitrary"),
        name="rglru",
    )(proj3, proj3, conv_w.astype(F32), conv_b.reshape(1, W).astype(F32), wa.astype(BF16),
      ba.reshape(1, W), wx.astype(BF16), bx.reshape(1, W), c)


NSA_AUG = 4
NSA_KW = LANES
SEL_TILE = 256
SEL_TILE_BLOCKS = SEL_TILE // SEL_LEN
WIN_TILE = 128
POS_SPLIT = 128
SEL_ONEHOT = LANES
PLAN_HALF = 16
SELECT_PARTS = 4
SELECT_QB = 2
ATTEND_QB = 2
NSA_VROWS = NSA_HEAD_DIM + 16
KV_PREP_ROWS = 512


def _key_aug(pos, width, dead=False):
    n = pos.shape[0]
    col = lax.broadcasted_iota(jnp.int32, (n, width), 1)
    hi = ((pos // POS_SPLIT) * POS_SPLIT).astype(F32)
    lo = (pos % POS_SPLIT).astype(F32)
    live = jnp.where(col == 0, hi, jnp.where(col == 1, lo, jnp.where(col < NSA_AUG, 1.0, 0.0)))
    return jnp.where(dead, jnp.where(col == NSA_AUG, 1.0, 0.0), live)


def _query_t(q_ref, slope_ref, c0, row0=0):
    hd, P, T = NSA_HEAD_DIM, NSA_PER_GROUP, Q_BLOCK
    PT = P * T
    qT = (q_ref[0, row0:row0 + T, :] * (hd ** -0.5)).T
    qpart = jnp.concatenate([qT[p * hd:(p + 1) * hd, :] for p in range(P)], 1)
    tlane = c0 + lax.broadcasted_iota(jnp.int32, (1, PT), 1) % T
    slope = slope_ref[0]
    t_hi = ((tlane // POS_SPLIT) * POS_SPLIT).astype(F32)
    t_lo = (tlane % POS_SPLIT).astype(F32)
    rowi = lax.broadcasted_iota(jnp.int32, (NSA_KW - hd, PT), 0)
    aug = jnp.where(rowi < 2, slope,
                    jnp.where(rowi == 2, -slope * t_hi,
                              jnp.where(rowi == 3, -slope * t_lo, jnp.where(rowi == NSA_AUG, NEG_BIG, 0.0))))
    return jnp.concatenate([qpart, aug], 0).astype(BF16), tlane


def _compress_kernel(xk_ref, xv_ref, w1k_ref, pek_ref, w1fk_ref, w2k_ref, w1v_ref, pev_ref, w1fv_ref, w2v_ref,
                     kc_ref, vct_ref, *, nc):
    hd = NSA_HEAD_DIM
    pos = lax.broadcasted_iota(jnp.int32, (nc, 1), 0) * CMP_STRIDE + (CMP_LEN - 1)
    aug = _key_aug(pos, NSA_KW - hd)

    def one(x_ref, w1_ref, pe_ref, w1f_ref, w2_ref, g):
        uv = jnp.zeros((nc, 2 * CMP_HIDDEN), F32)
        for r in range(CMP_STRIDE):
            xr = x_ref[0, pl.ds(r, nc, stride=CMP_STRIDE), :][:, g * hd:(g + 1) * hd]
            uv = uv + _dot(xr.astype(BF16), w1_ref[r])
        cvec = _dot_hilo(pe_ref[...], w1f_ref[...])[0:1, :]
        hid = uv[:, 0:CMP_HIDDEN] + pltpu.roll(uv[:, CMP_HIDDEN:], nc - 1, 0) + cvec
        return _dot_hilo(jax.nn.gelu(hid), w2_ref[...])

    for g in range(NSA_GROUPS):
        kc = one(xk_ref, w1k_ref, pek_ref, w1fk_ref, w2k_ref, g)
        kc_ref[0, g] = jnp.concatenate([kc, aug], 1).astype(BF16)
        vct_ref[0, g] = one(xv_ref, w1v_ref, pev_ref, w1fv_ref, w2v_ref, g).T.astype(BF16)


def _compress_call(proj3, pe_k, w1_k, w2_k, pe_v, w1_v, w2_v):
    B, S, _ = proj3.shape
    G, hd = NSA_GROUPS, NSA_HEAD_DIM
    nc = S // CMP_STRIDE
    half = CMP_STRIDE * hd

    def prep(pe, w1):
        w1 = w1.astype(F32)
        wr = jnp.concatenate([w1[:half].reshape(CMP_STRIDE, hd, CMP_HIDDEN),
                              w1[half:].reshape(CMP_STRIDE, hd, CMP_HIDDEN)], -1)
        pe8 = jnp.broadcast_to(pe.reshape(1, CMP_LEN * hd).astype(F32), (SUBLANES, CMP_LEN * hd))
        return wr.astype(BF16), pe8, w1

    full = lambda a: pl.BlockSpec(a.shape, lambda b: tuple(0 for _ in a.shape))
    col = lambda off: pl.BlockSpec((1, S, LANES), lambda b: (b, 0, off // LANES))
    args = prep(pe_k, w1_k) + (w2_k.astype(F32),) + prep(pe_v, w1_v) + (w2_v.astype(F32),)
    return pl.pallas_call(
        functools.partial(_compress_kernel, nc=nc),
        grid=(B,),
        in_specs=[col(OFF_KC), col(OFF_VC)] + [full(a) for a in args],
        out_specs=[pl.BlockSpec((1, G, nc, NSA_KW), lambda b: (b, 0, 0, 0)),
                   pl.BlockSpec((1, G, hd, nc), lambda b: (b, 0, 0, 0))],
        out_shape=[jax.ShapeDtypeStruct((B, G, nc, NSA_KW), BF16),
                   jax.ShapeDtypeStruct((B, G, hd, nc), BF16)],
        compiler_params=_params("parallel"),
        name="nsa_compress",
    )(proj3, proj3, *args)


def _kvprep_kernel(ks_ref, vs_ref, kw_ref, vw_ref, ksa_ref, vst_ref, kwa_ref, vwt_ref, *, tt, n_live):
    hd = NSA_HEAD_DIM
    t = pl.program_id(1)
    dead = t >= n_live
    pos = t * tt + lax.broadcasted_iota(jnp.int32, (tt, 1), 0)
    aug = _key_aug(pos, NSA_KW - hd, dead)
    blk_col = lax.broadcasted_iota(jnp.int32, (tt, SEL_ONEHOT), 1)
    onehot = jnp.where(jnp.logical_and(blk_col == pos // SEL_LEN, jnp.logical_not(dead)), 1.0, 0.0)
    vrow = lax.broadcasted_iota(jnp.int32, (NSA_VROWS - hd, tt), 0)
    ones_rows = jnp.where(jnp.logical_and(vrow == 0, jnp.logical_not(dead)), 1.0, 0.0)
    for g in range(NSA_GROUPS):
        lanes = slice(g * hd, (g + 1) * hd)
        live = lambda ref: jnp.where(dead, 0.0, ref[0][:, lanes])
        ksa_ref[0, g] = jnp.concatenate([live(ks_ref), aug, onehot], 1).astype(BF16)
        kwa_ref[0, g] = jnp.concatenate([live(kw_ref), aug], 1).astype(BF16)
        vs_t = jnp.concatenate([live(vs_ref).T, ones_rows], 0).astype(BF16)
        vw_t = jnp.concatenate([live(vw_ref).T, ones_rows], 0).astype(BF16)
        for c in range(tt // SEL_TILE):
            vst_ref[0, g, c] = vs_t[:, c * SEL_TILE:(c + 1) * SEL_TILE]
        for c in range(tt // WIN_TILE):
            vwt_ref[0, g, c] = vw_t[:, c * WIN_TILE:(c + 1) * WIN_TILE]


def _kvprep_call(proj3):
    B, S, _ = proj3.shape
    G, hd = NSA_GROUPS, NSA_HEAD_DIM
    tt = min(KV_PREP_ROWS, S)
    n_live = S // tt
    assert tt == WINDOW
    sp = S + tt
    col = lambda off: pl.BlockSpec((1, tt, LANES), lambda b, t: (b, jnp.minimum(t, n_live - 1), off // LANES))
    front = lambda t: (t + 1) % (n_live + 1)
    return pl.pallas_call(
        functools.partial(_kvprep_kernel, tt=tt, n_live=n_live),
        grid=(B, n_live + 1),
        in_specs=[col(OFF_KS), col(OFF_VS), col(OFF_KW), col(OFF_VW)],
        out_specs=[pl.BlockSpec((1, G, tt, NSA_KW + SEL_ONEHOT), lambda b, t: (b, 0, t, 0)),
                   pl.BlockSpec((1, G, tt // SEL_TILE, NSA_VROWS, SEL_TILE), lambda b, t: (b, 0, t, 0, 0)),
                   pl.BlockSpec((1, G, tt, NSA_KW), lambda b, t: (b, 0, front(t), 0)),
                   pl.BlockSpec((1, G, tt // WIN_TILE, NSA_VROWS, WIN_TILE), lambda b, t: (b, 0, front(t), 0, 0))],
        out_shape=[jax.ShapeDtypeStruct((B, G, sp, NSA_KW + SEL_ONEHOT), BF16),
                   jax.ShapeDtypeStruct((B, G, sp // SEL_TILE, NSA_VROWS, SEL_TILE), BF16),
                   jax.ShapeDtypeStruct((B, G, sp, NSA_KW), BF16),
                   jax.ShapeDtypeStruct((B, G, sp // WIN_TILE, NSA_VROWS, WIN_TILE), BF16)],
        compiler_params=_params("parallel", "parallel"),
        name="nsa_kvprep",
    )(proj3, proj3, proj3, proj3)


def _importance_matrix(ns, nc):
    ratio = SEL_LEN // CMP_STRIDE
    a = np.zeros((ns, nc), np.float32)
    for j in range(ns):
        for n, wgt in ((ratio * j - 1, 0.5), (ratio * j, 1.0), (ratio * j + 1, 1.0),
                       (ratio * j + 2, 1.0), (ratio * j + 3, 0.5)):
            if 0 <= n < nc - 1:
                a[j, n] = wgt
    return a


def _plan_constants(ns):
    nt = ns // SEL_TILE_BLOCKS
    grp = (np.arange(ns)[None, :] // SEL_TILE_BLOCKS == np.arange(nt)[:, None]).astype(np.float32)
    k = np.arange(nt)
    w_lo = np.where(k < PLAN_HALF, 2.0 ** np.minimum(k, PLAN_HALF - 1), 0.0)
    w_hi = np.where(k >= PLAN_HALF, 2.0 ** np.maximum(k - PLAN_HALF, 0), 0.0)
    wts = np.stack([np.repeat(w_lo[:, None], LANES, 1), np.repeat(w_hi[:, None], LANES, 1)]).astype(np.float32)
    return grp, wts


def _select_kernel(q_ref, slope_ref, kc_ref, vct_ref, imp_ref, grp_ref, wts_ref, ocmp_ref, bias_ref, bits_ref,
                   *, ns, nc, qb):
    P, T = NSA_PER_GROUP, Q_BLOCK
    step = pl.program_id(2)
    nsteps = pl.num_programs(2)
    subs = range(qb)
    c0s = [(step * qb + u) * T for u in subs]
    qts, tlanes = zip(*[_query_t(q_ref, slope_ref, c0s[u], u * T) for u in subs])

    def body(rows, nblk):
        cend = lax.broadcasted_iota(jnp.int32, (rows, 1), 0) * CMP_STRIDE + (CMP_LEN - 1)
        cmask = [cend <= tlanes[u] for u in subs]
        s = [jnp.where(cmask[u], _dot(kc_ref[0, 0, 0:rows, :], qts[u]), NEG_BIG) for u in subs]
        m = [jnp.max(s[u], 0, keepdims=True) for u in subs]
        p_c = [jnp.where(cmask[u], jnp.exp(s[u] - m[u]), 0.0) for u in subs]
        p_c = [p_c[u] * (1.0 / jnp.maximum(jnp.sum(p_c[u], 0, keepdims=True), 1e-30)) for u in subs]
        for u in subs:
            ocmp_ref[0, 0, u] = _dot(vct_ref[0, 0][:, 0:rows], p_c[u].astype(BF16))

        psum = [p_c[u][:, 0:T] for u in subs]
        for p in range(1, P):
            psum = [psum[u] + p_c[u][:, p * T:(p + 1) * T] for u in subs]
        imp = [_dot_exact_lhs(imp_ref[0:nblk, 0:rows], psum[u]) for u in subs]
        blk = lax.broadcasted_iota(jnp.int32, (nblk, T), 0)
        val, chosen = [], []
        for u in subs:
            cur = (c0s[u] + lax.broadcasted_iota(jnp.int32, (nblk, T), 1)) // SEL_LEN
            forced = (blk == 0) | (blk == cur) | (blk == cur - 1)
            val.append(jnp.where(forced, 3e38, jnp.where(blk > cur, -1.0, imp[u])))
            chosen.append(jnp.zeros((nblk, T), F32))
        for _ in range(min(SEL_TOPK, nblk)):
            for u in subs:
                mx = jnp.max(val[u], 0, keepdims=True)
                first = jnp.min(jnp.where(val[u] == mx, blk, ns), 0, keepdims=True)
                pick = blk == first
                chosen[u] = jnp.where(pick, 1.0, chosen[u])
                val[u] = jnp.where(pick, -2.0, val[u])
        for u in subs:
            bias_ref[0, 0, u, 0:nblk, :] = jnp.where(chosen[u] > 0.5, 0.0, NEG_BIG)
            if nblk < ns:
                bias_ref[0, 0, u, nblk:ns, :] = jnp.full((ns - nblk, T), NEG_BIG, F32)
            ch = chosen[u]
            if nblk < ns:
                ch = jnp.concatenate([ch, jnp.zeros((ns - nblk, T), F32)], 0)
            cnt = _dot(grp_ref[...], ch.astype(BF16))
            act = jnp.where(jnp.max(cnt, 1, keepdims=True) > 0.5, 1.0, 0.0)
            lo = jnp.sum(act * wts_ref[0], 0, keepdims=True).astype(jnp.int32)
            hi = jnp.sum(act * wts_ref[1], 0, keepdims=True).astype(jnp.int32)
            bits_ref[u] = jnp.broadcast_to(lo | (hi << PLAN_HALF), (SUBLANES, LANES))

    parts = max(p for p in (SELECT_PARTS, 2, 1) if nc % (p * LANES) == 0 and ns % (p * 2 * SUBLANES) == 0)
    part = step * parts // nsteps
    for v in range(parts):
        pl.when(part == v)(functools.partial(body, nc * (v + 1) // parts, ns * (v + 1) // parts))


def _attend_kernel(bits_sref, q_ref, gt_ref, slope_ref, ocmp_ref, bias_ref, ks_ref, vst_ref, kw_ref, vwt_ref,
                   o_ref, list_ref, s_ref, m_ref, *, n_tiles_total, ns, qb):
    hd, P, T = NSA_HEAD_DIM, NSA_PER_GROUP, Q_BLOCK
    PT = P * T
    b, g, step = pl.program_id(0), pl.program_id(1), pl.program_id(2)
    subs = range(qb)
    iqs = [step * qb + u for u in subs]
    c0s = [iq * T for iq in iqs]
    qts, tlanes = zip(*[_query_t(q_ref, slope_ref, c0s[u], u * T) for u in subs])
    tl_row = lax.broadcasted_iota(jnp.int32, (1, PT), 1) % T

    gts = [jax.nn.sigmoid(gt_ref[0, u * T:(u + 1) * T, :]).T for u in subs]

    def gate_row(u, branch):
        return jnp.concatenate([gts[u][p * 3 + branch:p * 3 + branch + 1, :] for p in range(P)], 1)

    list_len = n_tiles_total + 1
    last_tiles = [(c0s[u] + T - 1) // SEL_TILE for u in subs]
    prev_bits = []
    for u in subs:
        bits = bits_sref[(b * pl.num_programs(1) + g) * (pl.num_programs(2) * qb) + iqs[u]]
        prev_bits.append(bits & (lax.shift_left(jnp.int32(1), last_tiles[u]) - 1))
    n_prevs = [jnp.int32(0) for _ in subs]
    for k in range(n_tiles_total):
        for u in subs:
            list_ref[u * list_len + n_prevs[u]] = k
            n_prevs[u] = n_prevs[u] + (lax.shift_right_logical(prev_bits[u], k) & 1)

    qt_sels = []
    for u in subs:
        sel_b = bias_ref[0, 0, u]
        if ns < SEL_ONEHOT:
            sel_b = jnp.concatenate([sel_b, jnp.zeros((SEL_ONEHOT - ns, T), F32)], 0)
        qt_sels.append(jnp.concatenate([qts[u], jnp.concatenate([sel_b.astype(BF16)] * P, 1)], 0))

    def tile_scores(u, kt):
        k0 = pl.multiple_of(kt * SEL_TILE, SEL_TILE)
        return _dot(ks_ref[0, 0, pl.ds(k0, SEL_TILE), :], qt_sels[u])

    def col_max(scores):
        m = jnp.max(scores[0], 0, keepdims=True)
        for sc in scores[1:]:
            m = jnp.maximum(m, jnp.max(sc, 0, keepdims=True))
        return m

    def softmax_update(carry, scores, vts, m_tile):
        m_run, acc = carry
        m_new = jnp.maximum(m_run, m_tile)
        acc = jnp.exp(m_run - m_new) * acc
        for sc, vt in zip(scores, vts):
            acc = acc + _dot(vt, jnp.exp(sc - m_new).astype(BF16))
        return m_new, acc

    init = (jnp.full((1, PT), NEG_BIG, F32), jnp.zeros((NSA_VROWS, PT), F32))
    dead_tile = n_tiles_total
    n_pairs = [jnp.maximum((n_prevs[u] + 1) // 2, 1) for u in subs]

    def pair_tiles(u, i):
        has_a, has_b = 2 * i < n_prevs[u], 2 * i + 1 < n_prevs[u]
        ka = jnp.where(has_a, list_ref[u * list_len + jnp.where(has_a, 2 * i, 0)], dead_tile)
        kb = jnp.where(has_b, list_ref[u * list_len + jnp.where(has_b, 2 * i + 1, 0)], dead_tile)
        return ka, kb

    def pair_scores(u, i):
        ka, kb = pair_tiles(u, i)
        keys = jnp.concatenate([ks_ref[0, 0, pl.ds(pl.multiple_of(k * SEL_TILE, SEL_TILE), SEL_TILE), :]
                                for k in (ka, kb)], 0)
        return _dot(keys, qt_sels[u])

    def put_scores(u, i, sc):
        slot = i % 2
        s_ref[u, slot] = sc
        m_ref[u, slot] = col_max([sc])

    def pair_probs(u, i, m_run):
        slot = i % 2
        m_new = jnp.maximum(m_run, m_ref[u, slot])
        return m_new, jnp.exp(s_ref[u, slot] - m_new).astype(BF16)

    def add_values(u, i, carry, m_new, pr):
        ka, kb = pair_tiles(u, i)
        m_run, acc = carry
        vt = jnp.concatenate([vst_ref[0, 0, ka], vst_ref[0, 0, kb]], 1)
        return m_new, jnp.exp(m_run - m_new) * acc + _dot(vt, pr)

    first = [pair_scores(u, 0) for u in subs]
    for u in subs:
        put_scores(u, 0, first[u])

    n_win = (WINDOW + T) // WIN_TILE
    roff = lax.broadcasted_iota(jnp.int32, (WIN_TILE, 1), 0)
    w_sc = [_dot(kw_ref[0, 0, pl.ds(pl.multiple_of(c0s[u], WIN_TILE), WINDOW + T), :], qts[u]) for u in subs]
    w_parts = [[jnp.where(roff > tl_row, w_sc[u][0:WIN_TILE], NEG_BIG),
                w_sc[u][WIN_TILE:WINDOW],
                jnp.where(roff <= tl_row, w_sc[u][WINDOW:WINDOW + T], NEG_BIG)]
               for u in subs]
    w_max = [col_max(w_parts[u]) for u in subs]
    out_cw = []
    for u in subs:
        vt = jnp.concatenate([vwt_ref[0, 0, iqs[u] + j] for j in range(n_win)], 1)
        _, acc_win = softmax_update(init, [jnp.concatenate(w_parts[u], 0)], [vt], w_max[u])
        out_cw.append(gate_row(u, 0) * ocmp_ref[0, 0, u] + gate_row(u, 2) * (acc_win[0:hd] / acc_win[hd:hd + 1]))

    def pipe_step(u, i, carry):
        m_new, pr = pair_probs(u, i, carry[0])
        sc_next = pair_scores(u, i + 1)
        carry = add_values(u, i, carry, m_new, pr)
        put_scores(u, i + 1, sc_next)
        return carry

    carries = [lax.fori_loop(0, n_pairs[u] - 1, functools.partial(pipe_step, u), init) for u in subs]
    probs = [pair_probs(u, n_pairs[u] - 1, carries[u][0]) for u in subs]
    s_last = []
    for u in subs:
        kpos = last_tiles[u] * SEL_TILE + lax.broadcasted_iota(jnp.int32, (SEL_TILE, 1), 0)
        s_last.append(jnp.where(kpos <= tlanes[u], tile_scores(u, last_tiles[u]), NEG_BIG))
    carries = [add_values(u, n_pairs[u] - 1, carries[u], *probs[u]) for u in subs]
    for u in subs:
        _, acc_sel = softmax_update(carries[u], [s_last[u]], [vst_ref[0, 0, last_tiles[u]]], col_max([s_last[u]]))
        out = out_cw[u] + gate_row(u, 1) * (acc_sel[0:hd] / acc_sel[hd:hd + 1])
        o_ref[0, u * T:(u + 1) * T, :] = jnp.concatenate(
            [out[:, p * T:(p + 1) * T] for p in range(P)], 0).T.astype(o_ref.dtype)


def _nsa_call(proj3, pe_k, w1_k, w2_k, pe_v, w1_v, w2_v):
    B, S, _ = proj3.shape
    G, hd, P, T = NSA_GROUPS, NSA_HEAD_DIM, NSA_PER_GROUP, Q_BLOCK
    PT = P * T
    nc = S // CMP_STRIDE
    ns = S // SEL_LEN
    nq = S // T
    nt = ns // SEL_TILE_BLOCKS
    assert nt <= 2 * PLAN_HALF and S % KV_PREP_ROWS == 0

    kc_a, vc_t = _compress_call(proj3, pe_k, w1_k, w2_k, pe_v, w1_v, w2_v)
    ks_a, vs_t, kw_a, vw_t = _kvprep_call(proj3)
    imp_m = jnp.asarray(_importance_matrix(ns, nc), BF16)
    grp, wts = _plan_constants(ns)
    head_slopes = 2.0 ** (-(8.0 / NSA_HEADS) * np.arange(1, NSA_HEADS + 1, dtype=np.float32))
    slopes = jnp.asarray(np.repeat(head_slopes.reshape(G, 1, P), T, axis=-1), F32)

    q_spec = lambda im: pl.BlockSpec((1, T, P * hd), im)
    bg = lambda shp: pl.BlockSpec((1, 1) + shp, lambda b, g, i: (b, g) + tuple(0 for _ in shp))
    const = lambda a: pl.BlockSpec(a.shape, lambda b, g, i: tuple(0 for _ in a.shape))
    qb = SELECT_QB if nq % (SELECT_QB * SELECT_PARTS) == 0 else 1
    nsteps = nq // qb
    ocmp, sel_bias, bits = pl.pallas_call(
        functools.partial(_select_kernel, ns=ns, nc=nc, qb=qb),
        grid=(B, G, nsteps),
        in_specs=[pl.BlockSpec((1, qb * T, P * hd), lambda b, g, i: (b, i, OFF_NQ // (P * hd) + g)),
                  pl.BlockSpec((1, 1, PT), lambda b, g, i: (g, 0, 0)),
                  bg((nc, NSA_KW)), bg((hd, nc)), const(imp_m),
                  pl.BlockSpec(grp.shape, lambda b, g, i: (0, 0)),
                  pl.BlockSpec(wts.shape, lambda b, g, i: (0, 0, 0))],
        out_specs=[pl.BlockSpec((1, 1, qb, hd, PT), lambda b, g, i: (b, g, i, 0, 0)),
                   pl.BlockSpec((1, 1, qb, ns, T), lambda b, g, i: (b, g, i, 0, 0)),
                   pl.BlockSpec((qb, SUBLANES, LANES), lambda b, g, i: ((b * G + g) * nsteps + i, 0, 0))],
        out_shape=[jax.ShapeDtypeStruct((B, G, nq, hd, PT), F32),
                   jax.ShapeDtypeStruct((B, G, nq, ns, T), F32),
                   jax.ShapeDtypeStruct((B * G * nq, SUBLANES, LANES), jnp.int32)],
        compiler_params=_params("parallel", "parallel", "parallel"),
        name="nsa_select",
    )(proj3, slopes, kc_a, vc_t, imp_m, jnp.asarray(grp, BF16), jnp.asarray(wts, F32))

    aq = ATTEND_QB if nq % ATTEND_QB == 0 else 1
    bg2 = lambda shp: pl.BlockSpec((1, 1) + shp, lambda b, g, i, s: (b, g) + tuple(0 for _ in shp))
    return pl.pallas_call(
        functools.partial(_attend_kernel, n_tiles_total=nt, ns=ns, qb=aq),
        grid_spec=pltpu.PrefetchScalarGridSpec(
            num_scalar_prefetch=1,
            grid=(B, G, nq // aq),
            in_specs=[pl.BlockSpec((1, aq * T, P * hd), lambda b, g, i, s: (b, i, OFF_NQ // (P * hd) + g)),
                      pl.BlockSpec((1, aq * T, LANES), lambda b, g, i, s: (b, i, OFF_NG // LANES + g)),
                      pl.BlockSpec((1, 1, PT), lambda b, g, i, s: (g, 0, 0)),
                      pl.BlockSpec((1, 1, aq, hd, PT), lambda b, g, i, s: (b, g, i, 0, 0)),
                      pl.BlockSpec((1, 1, aq, ns, T), lambda b, g, i, s: (b, g, i, 0, 0)),
                      bg2(ks_a.shape[2:]), bg2(vs_t.shape[2:]), bg2(kw_a.shape[2:]), bg2(vw_t.shape[2:])],
            out_specs=pl.BlockSpec((1, aq * T, P * hd), lambda b, g, i, s: (b, i, g)),
            scratch_shapes=[pltpu.SMEM((aq * (nt + 1),), jnp.int32),
                            pltpu.VMEM((aq, 2, 2 * SEL_TILE, PT), F32), pltpu.VMEM((aq, 2, 1, PT), F32)]),
        out_shape=jax.ShapeDtypeStruct((B, S, NSA_WIDTH), BF16),
        compiler_params=_params("parallel", "parallel", "arbitrary"),
        name="nsa_attend",
    )(bits[:, 0, 0], proj3, proj3, slopes, ocmp, sel_bias, ks_a, vs_t, kw_a, vw_t)


def _merge_kernel(x_ref, ya_ref, yb_ref, yc_ref, g0_ref, g1_ref, g2_ref, wa_ref, wb_ref, wc_ref, wo_ref,
                  lg_ref, lb_ref, o_ref):
    gate = lambda r: jax.nn.sigmoid(r[...].astype(F32))
    m = gate(g0_ref) * _dot(ya_ref[...], wa_ref[...])
    m = m + gate(g1_ref) * _dot(yb_ref[...], wb_ref[...])
    m = m + gate(g2_ref) * _dot(yc_ref[...], wc_ref[...])
    z = DEEPNORM_ALPHA * x_ref[...] + _dot(m.astype(BF16), wo_ref[...])
    o_ref[...] = _layer_norm_rows(z, lg_ref[...], lb_ref[...])


def _merge_call(x2, ya, yb, yc, gates, w_hg, w_nsa, w_lru, w_out, ln_g, ln_b, tm=512):
    T, D = x2.shape
    tm = min(tm, T)
    rows = lambda w: pl.BlockSpec((tm, w), lambda i: (i, 0))
    gate = lambda n: pl.BlockSpec((tm, D), lambda i: (i, n))
    full = lambda a: pl.BlockSpec(a.shape, lambda i: (0, 0))
    ws = [w.astype(BF16) for w in (w_hg, w_nsa, w_lru, w_out)]
    return pl.pallas_call(
        _merge_kernel,
        grid=(T // tm,),
        in_specs=[rows(D), rows(HG_WIDTH), rows(NSA_WIDTH), rows(LRU_WIDTH), gate(0), gate(1), gate(2)]
                 + [full(w) for w in ws] + [pl.BlockSpec((1, D), lambda i: (0, 0))] * 2,
        out_specs=rows(D),
        out_shape=jax.ShapeDtypeStruct((T, D), F32),
        compiler_params=_params("parallel"),
        name="merge_out",
    )(x2, ya, yb, yc, gates, gates, gates, *ws, ln_g.reshape(1, D), ln_b.reshape(1, D))


FFN_SPLIT = 2
FFN_PAD = SUBLANES
FFN_SUB = 256


def _ffn_kernel(x_ref, wu_ref, wv_ref, cw_ref, cb_ref, wd_ref, lg_ref, lb_ref, o_ref,
                xb_ref, acc_ref, upad_ref, carry_ref, *, tm, tiles_per_seq):
    i = pl.program_id(0)
    j = pl.program_id(1)

    @pl.when(j == 0)
    def _():
        xb_ref[...] = x_ref[...].astype(BF16)

    xb = xb_ref[...]
    fc = wu_ref.shape[1]
    first = (i % tiles_per_seq) == 0

    @pl.when(first)
    def _():
        upad_ref[0:FFN_PAD, :] = jnp.zeros((FFN_PAD, fc), F32)

    @pl.when(jnp.logical_not(first))
    def _():
        upad_ref[0:FFN_PAD, :] = carry_ref[j]

    @pl.when(j == 0)
    def _():
        acc_ref[...] = jnp.zeros_like(acc_ref)

    for c0 in range(0, fc, FFN_SUB):
        cs = slice(c0, min(c0 + FFN_SUB, fc))
        u = _dot(xb, wu_ref[:, cs])
        v = _dot(xb, wv_ref[:, cs])
        upad_ref[FFN_PAD:FFN_PAD + tm, cs] = u
        cw = cw_ref[:, cs]
        cv = cb_ref[:, cs] + cw[FFN_CONV - 1:FFN_CONV, :] * u
        for k in range(FFN_CONV - 1):
            off = FFN_PAD - (FFN_CONV - 1) + k
            cv = cv + cw[k:k + 1, :] * upad_ref[off:off + tm, cs]
        carry_ref[j, :, cs] = u[tm - FFN_PAD:tm, :]
        h = (jax.nn.gelu(cv) * v).astype(BF16)
        acc_ref[...] += _dot(h, wd_ref[cs, :])

    @pl.when(j == FFN_SPLIT - 1)
    def _():
        z = DEEPNORM_ALPHA * x_ref[...] + acc_ref[...]
        o_ref[...] = _layer_norm_rows(z, lg_ref[...], lb_ref[...])


def _ffn_call(x2, seq_len, w_up, conv_w, conv_b, w_down, ln_g, ln_b, tm=512):
    T, D = x2.shape
    tm = min(tm, seq_len)
    fc = FFN_DIM // FFN_SPLIT
    wu = w_up.astype(BF16)
    return pl.pallas_call(
        functools.partial(_ffn_kernel, tm=tm, tiles_per_seq=seq_len // tm),
        grid=(T // tm, FFN_SPLIT),
        in_specs=[pl.BlockSpec((tm, D), lambda i, j: (i, 0)),
                  pl.BlockSpec((D, fc), lambda i, j: (0, j)),
                  pl.BlockSpec((D, fc), lambda i, j: (0, FFN_SPLIT + j)),
                  pl.BlockSpec((FFN_CONV, fc), lambda i, j: (0, j)),
                  pl.BlockSpec((1, fc), lambda i, j: (0, j)),
                  pl.BlockSpec((fc, D), lambda i, j: (j, 0)),
                  pl.BlockSpec((1, D), lambda i, j: (0, 0)),
                  pl.BlockSpec((1, D), lambda i, j: (0, 0))],
        out_specs=pl.BlockSpec((tm, D), lambda i, j: (i, 0)),
        out_shape=jax.ShapeDtypeStruct((T, D), F32),
        scratch_shapes=[pltpu.VMEM((tm, D), BF16), pltpu.VMEM((tm, D), F32),
                        pltpu.VMEM((tm + FFN_PAD, fc), F32), pltpu.VMEM((FFN_SPLIT, FFN_PAD, fc), F32)],
        compiler_params=_params("arbitrary", "arbitrary"),
        name="conv_ffn",
    )(x2, wu, wu, conv_w.astype(F32), conv_b.reshape(1, FFN_DIM).astype(F32), w_down.astype(BF16),
      ln_g.reshape(1, D), ln_b.reshape(1, D))


def _permute_in_proj(w, b):
    src = np.cumsum([0, HG_WIDTH, HG_WIDTH, HG_WIDTH, HG_WIDTH, NSA_WIDTH, NSA_KV, NSA_KV, NSA_KV, NSA_KV,
                     NSA_KV, NSA_KV, NSA_HEADS * 3, LRU_WIDTH, LRU_WIDTH, N_BRANCH * 1024])
    names = ["hq", "hf", "hi", "hg", "nq", "kc", "vc", "ks", "vs", "kw", "vw", "ng", "lx", "ly", "mg"]
    seg = {n: (int(src[k]), int(src[k + 1])) for k, n in enumerate(names)}
    order = ["mg", "hq", "hf", "hi", "hg", "nq", "lx", "ly", "kc", "vc", "ks", "vs", "kw", "vw"]
    per_group = NSA_PER_GROUP * 3
    ng0 = seg["ng"][0]
    w = w.astype(BF16)
    w_parts = [w[:, seg[n][0]:seg[n][1]] for n in order]
    b_parts = [b[seg[n][0]:seg[n][1]] for n in order]
    for gi in range(NSA_GROUPS):
        lo = ng0 + gi * per_group
        w_parts += [w[:, lo:lo + per_group], jnp.zeros((w.shape[0], LANES - per_group), w.dtype)]
        b_parts += [b[lo:lo + per_group], jnp.zeros((LANES - per_group,), b.dtype)]
    wp = jnp.concatenate(w_parts, 1)
    bp = jnp.concatenate(b_parts)
    assert wp.shape[1] == GATE_COLS + PROJ_COLS
    return wp.astype(BF16), bp.astype(F32)


def kernel(x, ln_emb_g, ln_emb_b, w_in, b_in, hg_lb_logits, hg_norm_g, cmp_pe_k, cmp_w1_k, cmp_w2_k, cmp_pe_v, cmp_w1_v, cmp_w2_v, lru_conv_w, lru_conv_b, lru_wa, lru_ba, lru_wx, lru_bx, lru_lambda, w_branch_hg, w_branch_nsa, w_branch_lru, w_out, ln1_g, ln1_b, ffn_w_up, ffn_conv_w, ffn_conv_b, ffn_w_down, ln2_g, ln2_b):
    B, S, D = x.shape
    T = B * S
    gam = jax.nn.softmax(hg_lb_logits.astype(F32), axis=0)
    lb_all = jnp.cumsum(gam, axis=0) - gam[0]
    h = x.reshape(T, D)
    for l in range(DEPTH):
        wp, bp = _permute_in_proj(w_in[l], b_in[l])
        if l == 0:
            gates, proj2, h = _inproj_call(h, wp, bp, ln=(ln_emb_g, ln_emb_b))
        else:
            gates, proj2 = _inproj_call(h, wp, bp)
        proj3 = proj2.reshape(B, S, PROJ_COLS)
        y_a = _hgrn_call(proj3, lb_all[l], hg_norm_g[l])
        y_b = _nsa_call(proj3, cmp_pe_k[l], cmp_w1_k[l], cmp_w2_k[l], cmp_pe_v[l], cmp_w1_v[l], cmp_w2_v[l])
        y_c = _lru_call(proj3, lru_conv_w[l], lru_conv_b[l], lru_wa[l], lru_ba[l], lru_wx[l], lru_bx[l],
                        lru_lambda[l])
        h = _merge_call(h, y_a.reshape(T, HG_WIDTH), y_b.reshape(T, NSA_WIDTH), y_c.reshape(T, LRU_WIDTH),
                        gates, w_branch_hg[l], w_branch_nsa[l], w_branch_lru[l], w_out[l], ln1_g[l], ln1_b[l])
        h = _ffn_call(h, S, ffn_w_up[l], ffn_conv_w[l], ffn_conv_b[l], ffn_w_down[l], ln2_g[l], ln2_b[l])
    return h.reshape(B, S, D)
```

```python
import functools

import numpy as np
import jax
import jax.numpy as jnp
from jax import lax
from jax.experimental import pallas as pl
from jax.experimental.pallas import tpu as pltpu

F32 = jnp.float32
BF16 = jnp.bfloat16

DEPTH = 2
HG_HEADS = 4
HG_HEAD_DIM = 128
HG_WIDTH = HG_HEADS * HG_HEAD_DIM
NSA_HEADS = 8
NSA_GROUPS = 2
NSA_PER_GROUP = NSA_HEADS // NSA_GROUPS
NSA_HEAD_DIM = 64
NSA_WIDTH = NSA_HEADS * NSA_HEAD_DIM
NSA_KV = NSA_GROUPS * NSA_HEAD_DIM
CMP_LEN = 32
CMP_STRIDE = 16
CMP_HIDDEN = 64
SEL_LEN = 64
SEL_TOPK = 16
WINDOW = 512
Q_BLOCK = 128
LRU_WIDTH = 512
LRU_BLOCKS = 4
LRU_CONV = 4
LRU_C = 8.0
FFN_DIM = 2816
FFN_CONV = 3
DEEPNORM_ALPHA = (2 * DEPTH) ** 0.25
LN_EPS = 1e-5
N_BRANCH = 3

LANES = 128
SUBLANES = 8
VMEM_LIMIT_BYTES = 52 * 1024 * 1024

GATE_COLS = N_BRANCH * 1024
OFF_HQ = 0
OFF_HF = OFF_HQ + HG_WIDTH
OFF_HI = OFF_HF + HG_WIDTH
OFF_HG = OFF_HI + HG_WIDTH
OFF_NQ = OFF_HG + HG_WIDTH
OFF_LX = OFF_NQ + NSA_WIDTH
OFF_LY = OFF_LX + LRU_WIDTH
OFF_KC = OFF_LY + LRU_WIDTH
OFF_VC = OFF_KC + NSA_KV
OFF_KS = OFF_VC + NSA_KV
OFF_VS = OFF_KS + NSA_KV
OFF_KW = OFF_VS + NSA_KV
OFF_VW = OFF_KW + NSA_KV
OFF_NG = OFF_VW + NSA_KV
PROJ_TILE_N = 1536
GATE_TILES = GATE_COLS // PROJ_TILE_N
PROJ_COLS = OFF_NG + NSA_GROUPS * LANES

NEG_BIG = -1e30


def _split3(x):
    hi = x.astype(BF16)
    r1 = x - hi.astype(F32)
    mid = r1.astype(BF16)
    lo = (r1 - mid.astype(F32)).astype(BF16)
    return hi, mid, lo


def _dot(a, b):
    return jnp.dot(a, b, preferred_element_type=F32)


def _dot_nt(a, b):
    return lax.dot_general(a, b, (((1,), (1,)), ((), ())), preferred_element_type=F32)


def _dot_exact_lhs(m_bf16, x):
    hi, mid, lo = _split3(x)
    return _dot(m_bf16, hi) + _dot(m_bf16, mid) + _dot(m_bf16, lo)


def _dot_hilo(a, b):
    ah = a.astype(BF16)
    al = (a - ah.astype(F32)).astype(BF16)
    bh = b.astype(BF16)
    bl = (b - bh.astype(F32)).astype(BF16)
    return _dot(ah, bh) + _dot(ah, bl) + _dot(al, bh) + _dot(al, bl)


def _layer_norm_rows(z, g, b):
    mu = jnp.mean(z, -1, keepdims=True)
    zc = z - mu
    var = jnp.mean(zc * zc, -1, keepdims=True)
    return zc * lax.rsqrt(var + LN_EPS) * g + b


def _params(*sem):
    return pltpu.CompilerParams(dimension_semantics=sem, vmem_limit_bytes=VMEM_LIMIT_BYTES)


def _inproj_kernel(*refs, with_ln):
    if with_ln:
        x_ref, g_ref, be_ref, w_ref, b_ref, gate_ref, o_ref, xn_ref, xb_ref = refs
    else:
        x_ref, w_ref, b_ref, gate_ref, o_ref, xb_ref = refs
    j = pl.program_id(1)

    @pl.when(j == 0)
    def _():
        x = x_ref[...]
        if with_ln:
            x = _layer_norm_rows(x, g_ref[...], be_ref[...])
            xn_ref[...] = x
        xb_ref[...] = x.astype(BF16)

    acc = _dot(xb_ref[...], w_ref[...]) + b_ref[...]

    @pl.when(j < GATE_TILES)
    def _():
        gate_ref[...] = acc.astype(BF16)

    @pl.when(j >= GATE_TILES)
    def _():
        o_ref[...] = acc


def _inproj_call(x2, w_bf16, bias, ln=None, tm=1024):
    T, D = x2.shape
    N = w_bf16.shape[1]
    tm = min(tm, T)
    tn = PROJ_TILE_N
    vec = pl.BlockSpec((1, D), lambda i, j: (0, 0))
    in_specs = [pl.BlockSpec((tm, D), lambda i, j: (i, 0))] + ([vec, vec] if ln else []) + [
        pl.BlockSpec((D, tn), lambda i, j: (0, j)), pl.BlockSpec((1, tn), lambda i, j: (0, j))]
    out_specs = [pl.BlockSpec((tm, tn), lambda i, j: (i, jnp.minimum(j, GATE_TILES - 1))),
                 pl.BlockSpec((tm, tn), lambda i, j: (i, jnp.maximum(j - GATE_TILES, 0)))]
    out_shape = [jax.ShapeDtypeStruct((T, GATE_COLS), BF16), jax.ShapeDtypeStruct((T, N - GATE_COLS), F32)]
    if ln:
        out_specs.append(pl.BlockSpec((tm, D), lambda i, j: (i, 0)))
        out_shape.append(jax.ShapeDtypeStruct((T, D), F32))
    ln_args = [ln[0].reshape(1, D), ln[1].reshape(1, D)] if ln else []
    return pl.pallas_call(
        functools.partial(_inproj_kernel, with_ln=bool(ln)),
        grid=(T // tm, N // tn),
        in_specs=in_specs, out_specs=out_specs, out_shape=out_shape,
        scratch_shapes=[pltpu.VMEM((tm, D), BF16)],
        compiler_params=_params("parallel", "arbitrary"),
        name="in_proj_ln" if ln else "in_proj",
    )(x2, *ln_args, w_bf16, bias.reshape(1, N))


HG_CHUNK = 128
HG_DIAG = SUBLANES
HG_HEADS_PER_STEP = 4


def _hgrn_constants(C):
    r = np.arange(C)
    tri = (r[None, :] <= r[:, None]).astype(np.float32)
    mats = [tri]
    masks = []
    w = HG_DIAG
    while 2 * w <= C:
        mid = (r // (2 * w)) * (2 * w) + w
        mats.append(tri - (r[None, :] <= mid[:, None]).astype(np.float32))
        same = (r[:, None] // (2 * w)) == (r[None, :] // (2 * w))
        masks.append((same & ((r[:, None] % (2 * w)) >= w) & ((r[None, :] % (2 * w)) < w)).astype(np.float32))
        w *= 2
    mats.append((r[None, :] > r[:, None]).astype(np.float32))
    return np.concatenate(mats, 0), np.stack(masks, 0)


def _hgrn_kernel(q_ref, f_ref, i_ref, g_ref, lb_ref, ng_ref, m_ref, mask_ref, o_ref, st_ref, *, C, nchunk, nh):
    d = HG_HEAD_DIM
    nlev = mask_ref.shape[0]

    @pl.when(pl.program_id(2) == 0)
    def _():
        st_ref[...] = jnp.zeros_like(st_ref)

    hs = range(nh)
    lane = [slice(h * d, (h + 1) * d) for h in hs]
    lb = [lb_ref[h] for h in hs]
    ng = [ng_ref[h] for h in hs]
    nb = C // HG_DIAG
    row3 = lax.broadcasted_iota(jnp.int32, (nb, HG_DIAG, d), 1)

    fl_all = f_ref[0]
    lf_all = [jnp.log(lb[h] + (1.0 - lb[h]) * jax.nn.sigmoid(fl_all[:, lane[h]])) for h in hs]
    kk_all = [(1.0 - lb[h]) * jax.nn.sigmoid(-fl_all[:, lane[h]]) for h in hs]
    allm_all = _dot_exact_lhs(m_ref[...], jnp.concatenate(
        [lf_all[h][c * C:(c + 1) * C] for h in hs for c in range(nchunk)], 1))

    for c in range(nchunk):
        sl = pl.ds(c * C, C)
        q = [q_ref[0, sl, lane[h]] for h in hs]
        iv = [i_ref[0, sl, lane[h]] for h in hs]
        kk = [kk_all[h][c * C:(c + 1) * C] for h in hs]
        allm = [allm_all[:, (h * nchunk + c) * d:(h * nchunk + c + 1) * d] for h in hs]
        bcs = [allm[h][0:C] for h in hs]
        rem = [allm[h][(nlev + 1) * C:(nlev + 2) * C] for h in hs]
        iv_b = [iv[h].astype(BF16) for h in hs]

        att = [jnp.zeros((C, C), F32) for _ in hs]
        for l in range(nlev):
            e = [jnp.exp(-jnp.abs(allm[h][(1 + l) * C:(2 + l) * C])) for h in hs]
            att = [att[h] + mask_ref[l] * _dot_nt((q[h] * e[h]).astype(BF16), (kk[h] * e[h]).astype(BF16))
                   for h in hs]
        o = [_dot(att[h].astype(BF16), iv_b[h]) for h in hs]

        q3 = [q[h].reshape(nb, HG_DIAG, d) for h in hs]
        k3 = [kk[h].reshape(nb, HG_DIAG, d) for h in hs]
        b3 = [bcs[h].reshape(nb, HG_DIAG, d) for h in hs]
        i3 = [iv[h].reshape(nb, HG_DIAG, d) for h in hs]
        acc = [jnp.zeros((nb, HG_DIAG, d), F32) for _ in hs]
        for s in range(HG_DIAG):
            for h in hs:
                dec = jnp.where(row3 >= s, jnp.exp(b3[h] - b3[h][:, s:s + 1, :]), 0.0)
                a = jnp.sum(q3[h] * dec * k3[h][:, s:s + 1, :], axis=-1, keepdims=True)
                acc[h] = acc[h] + a * i3[h][:, s:s + 1, :]

        for h in hs:
            st = st_ref[h]
            oh = o[h] + acc[h].reshape(C, d) + _dot_nt((q[h] * jnp.exp(bcs[h])).astype(BF16), st.astype(BF16))
            kdec = (kk[h] * jnp.exp(rem[h])).astype(BF16)
            st_ref[h] = st * jnp.exp(bcs[h][C - 1:C, :]) + _dot(iv[h].T.astype(BF16), kdec)
            oh = oh * lax.rsqrt(jnp.mean(oh * oh, -1, keepdims=True) + 1e-6) * ng[h]
            g = g_ref[0, sl, lane[h]]
            o_ref[0, sl, lane[h]] = (oh * (g * jax.nn.sigmoid(g))).astype(o_ref.dtype)


def _hgrn_call(proj3, lb, norm_g, tt=512):
    B, S, _ = proj3.shape
    d = HG_HEAD_DIM
    C = HG_CHUNK
    tt = min(tt, S)
    mstack, masks = _hgrn_constants(C)
    nm = mstack.shape[0]
    nh = HG_HEADS_PER_STEP
    wid = nh * d
    col = lambda off: (lambda b, h, t: (b, t, off // wid + h))
    blk = (1, tt, wid)
    return pl.pallas_call(
        functools.partial(_hgrn_kernel, C=C, nchunk=tt // C, nh=nh),
        grid=(B, HG_HEADS // nh, S // tt),
        in_specs=[pl.BlockSpec(blk, col(OFF_HQ)), pl.BlockSpec(blk, col(OFF_HF)),
                  pl.BlockSpec(blk, col(OFF_HI)), pl.BlockSpec(blk, col(OFF_HG)),
                  pl.BlockSpec((nh, 1, d), lambda b, h, t: (h, 0, 0)),
                  pl.BlockSpec((nh, 1, d), lambda b, h, t: (h, 0, 0)),
                  pl.BlockSpec((nm, C), lambda b, h, t: (0, 0)),
                  pl.BlockSpec(masks.shape, lambda b, h, t: (0, 0, 0))],
        out_specs=pl.BlockSpec(blk, lambda b, h, t: (b, t, h)),
        out_shape=jax.ShapeDtypeStruct((B, S, HG_WIDTH), BF16),
        scratch_shapes=[pltpu.VMEM((nh, d, d), F32)],
        compiler_params=_params("parallel", "parallel", "arbitrary"),
        name="hgrn2",
    )(proj3, proj3, proj3, proj3, lb.reshape(HG_HEADS, 1, d), norm_g.reshape(HG_HEADS, 1, d),
      jnp.asarray(mstack, BF16), jnp.asarray(masks, F32))


LRU_PAD = SUBLANES
LRU_STEP = 16


def _lru_kernel(x_ref, y_ref, cw_ref, cb_ref, wa_ref, ba_ref, wx_ref, bx_ref, c_ref, o_ref,
                xpad_ref, h_ref, a_ref, u_ref, *, tt):
    W = LRU_WIDTH
    bw = W // LRU_BLOCKS

    @pl.when(pl.program_id(1) == 0)
    def _():
        xpad_ref[0:LRU_PAD, :] = jnp.zeros((LRU_PAD, W), F32)
        h_ref[...] = jnp.zeros_like(h_ref)

    x = x_ref[0]
    xpad_ref[LRU_PAD:LRU_PAD + tt, :] = x
    cw = cw_ref[...]
    xc = cb_ref[...] + cw[LRU_CONV - 1:LRU_CONV, :] * x
    for j in range(LRU_CONV - 1):
        off = LRU_PAD - (LRU_CONV - 1) + j
        xc = xc + cw[j:j + 1, :] * xpad_ref[off:off + tt, :]
    xpad_ref[0:LRU_PAD, :] = x[tt - LRU_PAD:tt, :]

    rs, is_ = [], []
    for gi in range(LRU_BLOCKS):
        xg = xc[:, gi * bw:(gi + 1) * bw].astype(BF16)
        rs.append(_dot(xg, wa_ref[gi]))
        is_.append(_dot(xg, wx_ref[gi]))
    r = jax.nn.sigmoid(jnp.concatenate(rs, -1) + ba_ref[...])
    ig = jax.nn.sigmoid(jnp.concatenate(is_, -1) + bx_ref[...])
    log_a = c_ref[...] * r
    a = jnp.exp(log_a)
    u = jnp.sqrt(-jnp.tanh(log_a) * (a * a + 1.0)) * (ig * xc)

    sub = lax.broadcasted_iota(jnp.int32, (tt, W), 0) % SUBLANES
    for dlt in (1, 2, 4):
        keep = sub >= dlt
        a_sh = jnp.where(keep, pltpu.roll(a, dlt, 0), 1.0)
        u_sh = jnp.where(keep, pltpu.roll(u, dlt, 0), 0.0)
        u = a * u_sh + u
        a = a * a_sh
    a_ref[...] = a
    u_ref[...] = u

    def step(i, h):
        r0 = pl.multiple_of(i * LRU_STEP, LRU_STEP)
        outs = []
        for k in range(LRU_STEP // SUBLANES):
            rows = pl.ds(r0 + k * SUBLANES, SUBLANES)
            hh = u_ref[rows, :] + a_ref[rows, :] * h
            outs.append(hh)
            h = hh[SUBLANES - 1:SUBLANES, :]
        rows = pl.ds(r0, LRU_STEP)
        o_ref[0, rows, :] = (jnp.concatenate(outs, 0) * jax.nn.gelu(y_ref[0, rows, :])).astype(o_ref.dtype)
        return h

    h_ref[...] = lax.fori_loop(0, tt // LRU_STEP, step, h_ref[...])


def _lru_call(proj3, conv_w, conv_b, wa, ba, wx, bx, lam, tt=512):
    B, S, _ = proj3.shape
    W = LRU_WIDTH
    bw = W // LRU_BLOCKS
    tt = min(tt, S)
    c = (-LRU_C * jax.nn.softplus(-lam.astype(F32))).reshape(1, W)
    vec = lambda: pl.BlockSpec((1, W), lambda b, t: (0, 0))
    return pl.pallas_call(
        functools.partial(_lru_kernel, tt=tt),
        grid=(B, S // tt),
        in_specs=[pl.BlockSpec((1, tt, W), lambda b, t: (b, t, OFF_LX // W)),
                  pl.BlockSpec((1, tt, W), lambda b, t: (b, t, OFF_LY // W)),
                  pl.BlockSpec((LRU_CONV, W), lambda b, t: (0, 0)), vec(),
                  pl.BlockSpec((LRU_BLOCKS, bw, bw), lambda b, t: (0, 0, 0)), vec(),
                  pl.BlockSpec((LRU_BLOCKS, bw, bw), lambda b, t: (0, 0, 0)), vec(), vec()],
        out_specs=pl.BlockSpec((1, tt, W), lambda b, t: (b, t, 0)),
        out_shape=jax.ShapeDtypeStruct((B, S, W), BF16),
        scratch_shapes=[pltpu.VMEM((tt + LRU_PAD, W), F32), pltpu.VMEM((1, W), F32),
                        pltpu.VMEM((tt, W), F32), pltpu.VMEM((tt, W), F32)],
        compiler_params=_params("parallel", "arbitrary"),
        name="rglru",
    )(proj3, proj3, conv_w.astype(F32), conv_b.reshape(1, W).astype(F32), wa.astype(BF16),
      ba.reshape(1, W), wx.astype(BF16), bx.reshape(1, W), c)


NSA_AUG = 4
NSA_KW = LANES
SEL_TILE = 256
SEL_TILE_BLOCKS = SEL_TILE // SEL_LEN
WIN_TILE = 128
POS_SPLIT = 128
SEL_ONEHOT = LANES
PLAN_HALF = 16
SELECT_PARTS = 4
SELECT_QB = 8
ATTEND_QB = 4
NSA_VROWS = NSA_HEAD_DIM + 16
KV_PREP_ROWS = 512


def _key_aug(pos, width, dead=False):
    n = pos.shape[0]
    col = lax.broadcasted_iota(jnp.int32, (n, width), 1)
    hi = ((pos // POS_SPLIT) * POS_SPLIT).astype(F32)
    lo = (pos % POS_SPLIT).astype(F32)
    live = jnp.where(col == 0, hi, jnp.where(col == 1, lo, jnp.where(col < NSA_AUG, 1.0, 0.0)))
    return jnp.where(dead, jnp.where(col == NSA_AUG, 1.0, 0.0), live)


def _query_t(q_ref, slope_ref, c0, row0=0):
    hd, P, T = NSA_HEAD_DIM, NSA_PER_GROUP, Q_BLOCK
    PT = P * T
    qT = (q_ref[0, row0:row0 + T, :] * (hd ** -0.5)).T
    qpart = jnp.concatenate([qT[p * hd:(p + 1) * hd, :] for p in range(P)], 1)
    tlane = c0 + lax.broadcasted_iota(jnp.int32, (1, PT), 1) % T
    slope = slope_ref[0]
    t_hi = ((tlane // POS_SPLIT) * POS_SPLIT).astype(F32)
    t_lo = (tlane % POS_SPLIT).astype(F32)
    rowi = lax.broadcasted_iota(jnp.int32, (NSA_KW - hd, PT), 0)
    aug = jnp.where(rowi < 2, slope,
                    jnp.where(rowi == 2, -slope * t_hi,
                              jnp.where(rowi == 3, -slope * t_lo, jnp.where(rowi == NSA_AUG, NEG_BIG, 0.0))))
    return jnp.concatenate([qpart, aug], 0).astype(BF16), tlane


def _compress_kernel(xk_ref, xv_ref, w1k_ref, pek_ref, w1fk_ref, w2k_ref, w1v_ref, pev_ref, w1fv_ref, w2v_ref,
                     kc_ref, vct_ref, *, nc):
    hd = NSA_HEAD_DIM
    pos = lax.broadcasted_iota(jnp.int32, (nc, 1), 0) * CMP_STRIDE + (CMP_LEN - 1)
    aug = _key_aug(pos, NSA_KW - hd)

    def one(x_ref, w1_ref, pe_ref, w1f_ref, w2_ref, g):
        uv = jnp.zeros((nc, 2 * CMP_HIDDEN), F32)
        for r in range(CMP_STRIDE):
            xr = x_ref[0, pl.ds(r, nc, stride=CMP_STRIDE), :][:, g * hd:(g + 1) * hd]
            uv = uv + _dot(xr.astype(BF16), w1_ref[r])
        cvec = _dot_hilo(pe_ref[...], w1f_ref[...])[0:1, :]
        hid = uv[:, 0:CMP_HIDDEN] + pltpu.roll(uv[:, CMP_HIDDEN:], nc - 1, 0) + cvec
        return _dot_hilo(jax.nn.gelu(hid), w2_ref[...])

    for g in range(NSA_GROUPS):
        kc = one(xk_ref, w1k_ref, pek_ref, w1fk_ref, w2k_ref, g)
        kc_ref[0, g] = jnp.concatenate([kc, aug], 1).astype(BF16)
        vct_ref[0, g] = one(xv_ref, w1v_ref, pev_ref, w1fv_ref, w2v_ref, g).T.astype(BF16)


def _compress_call(proj3, pe_k, w1_k, w2_k, pe_v, w1_v, w2_v):
    B, S, _ = proj3.shape
    G, hd = NSA_GROUPS, NSA_HEAD_DIM
    nc = S // CMP_STRIDE
    half = CMP_STRIDE * hd

    def prep(pe, w1):
        w1 = w1.astype(F32)
        wr = jnp.concatenate([w1[:half].reshape(CMP_STRIDE, hd, CMP_HIDDEN),
                              w1[half:].reshape(CMP_STRIDE, hd, CMP_HIDDEN)], -1)
        pe8 = jnp.broadcast_to(pe.reshape(1, CMP_LEN * hd).astype(F32), (SUBLANES, CMP_LEN * hd))
        return wr.astype(BF16), pe8, w1

    full = lambda a: pl.BlockSpec(a.shape, lambda b: tuple(0 for _ in a.shape))
    col = lambda off: pl.BlockSpec((1, S, LANES), lambda b: (b, 0, off // LANES))
    args = prep(pe_k, w1_k) + (w2_k.astype(F32),) + prep(pe_v, w1_v) + (w2_v.astype(F32),)
    return pl.pallas_call(
        functools.partial(_compress_kernel, nc=nc),
        grid=(B,),
        in_specs=[col(OFF_KC), col(OFF_VC)] + [full(a) for a in args],
        out_specs=[pl.BlockSpec((1, G, nc, NSA_KW), lambda b: (b, 0, 0, 0)),
                   pl.BlockSpec((1, G, hd, nc), lambda b: (b, 0, 0, 0))],
        out_shape=[jax.ShapeDtypeStruct((B, G, nc, NSA_KW), BF16),
                   jax.ShapeDtypeStruct((B, G, hd, nc), BF16)],
        compiler_params=_params("parallel"),
        name="nsa_compress",
    )(proj3, proj3, *args)


def _kvprep_kernel(ks_ref, vs_ref, kw_ref, vw_ref, ksa_ref, vst_ref, kwa_ref, vwt_ref, *, tt, n_live):
    hd = NSA_HEAD_DIM
    t = pl.program_id(1)
    dead = t >= n_live
    pos = t * tt + lax.broadcasted_iota(jnp.int32, (tt, 1), 0)
    aug = _key_aug(pos, NSA_KW - hd, dead)
    blk_col = lax.broadcasted_iota(jnp.int32, (tt, SEL_ONEHOT), 1)
    onehot = jnp.where(jnp.logical_and(blk_col == pos // SEL_LEN, jnp.logical_not(dead)), 1.0, 0.0)
    vrow = lax.broadcasted_iota(jnp.int32, (NSA_VROWS - hd, tt), 0)
    ones_rows = jnp.where(jnp.logical_and(vrow == 0, jnp.logical_not(dead)), 1.0, 0.0)
    for g in range(NSA_GROUPS):
        lanes = slice(g * hd, (g + 1) * hd)
        live = lambda ref: jnp.where(dead, 0.0, ref[0][:, lanes])
        ksa_ref[0, g] = jnp.concatenate([live(ks_ref), aug, onehot], 1).astype(BF16)
        kwa_ref[0, g] = jnp.concatenate([live(kw_ref), aug], 1).astype(BF16)
        vs_t = jnp.concatenate([live(vs_ref).T, ones_rows], 0).astype(BF16)
        vw_t = jnp.concatenate([live(vw_ref).T, ones_rows], 0).astype(BF16)
        for c in range(tt // SEL_TILE):
            vst_ref[0, g, c] = vs_t[:, c * SEL_TILE:(c + 1) * SEL_TILE]
        for c in range(tt // WIN_TILE):
            vwt_ref[0, g, c] = vw_t[:, c * WIN_TILE:(c + 1) * WIN_TILE]


def _kvprep_call(proj3):
    B, S, _ = proj3.shape
    G, hd = NSA_GROUPS, NSA_HEAD_DIM
    tt = min(KV_PREP_ROWS, S)
    n_live = S // tt
    assert tt == WINDOW
    sp = S + tt
    col = lambda off: pl.BlockSpec((1, tt, LANES), lambda b, t: (b, jnp.minimum(t, n_live - 1), off // LANES))
    front = lambda t: (t + 1) % (n_live + 1)
    return pl.pallas_call(
        functools.partial(_kvprep_kernel, tt=tt, n_live=n_live),
        grid=(B, n_live + 1),
        in_specs=[col(OFF_KS), col(OFF_VS), col(OFF_KW), col(OFF_VW)],
        out_specs=[pl.BlockSpec((1, G, tt, NSA_KW + SEL_ONEHOT), lambda b, t: (b, 0, t, 0)),
                   pl.BlockSpec((1, G, tt // SEL_TILE, NSA_VROWS, SEL_TILE), lambda b, t: (b, 0, t, 0, 0)),
                   pl.BlockSpec((1, G, tt, NSA_KW), lambda b, t: (b, 0, front(t), 0)),
                   pl.BlockSpec((1, G, tt // WIN_TILE, NSA_VROWS, WIN_TILE), lambda b, t: (b, 0, front(t), 0, 0))],
        out_shape=[jax.ShapeDtypeStruct((B, G, sp, NSA_KW + SEL_ONEHOT), BF16),
                   jax.ShapeDtypeStruct((B, G, sp // SEL_TILE, NSA_VROWS, SEL_TILE), BF16),
                   jax.ShapeDtypeStruct((B, G, sp, NSA_KW), BF16),
                   jax.ShapeDtypeStruct((B, G, sp // WIN_TILE, NSA_VROWS, WIN_TILE), BF16)],
        compiler_params=_params("parallel", "parallel"),
        name="nsa_kvprep",
    )(proj3, proj3, proj3, proj3)


def _importance_matrix(ns, nc):
    ratio = SEL_LEN // CMP_STRIDE
    a = np.zeros((ns, nc), np.float32)
    for j in range(ns):
        for n, wgt in ((ratio * j - 1, 0.5), (ratio * j, 1.0), (ratio * j + 1, 1.0),
                       (ratio * j + 2, 1.0), (ratio * j + 3, 0.5)):
            if 0 <= n < nc - 1:
                a[j, n] = wgt
    return a


def _plan_constants(ns):
    nt = ns // SEL_TILE_BLOCKS
    grp = (np.arange(ns)[None, :] // SEL_TILE_BLOCKS == np.arange(nt)[:, None]).astype(np.float32)
    k = np.arange(nt)
    w_lo = np.where(k < PLAN_HALF, 2.0 ** np.minimum(k, PLAN_HALF - 1), 0.0)
    w_hi = np.where(k >= PLAN_HALF, 2.0 ** np.maximum(k - PLAN_HALF, 0), 0.0)
    wts = np.stack([np.repeat(w_lo[:, None], LANES, 1), np.repeat(w_hi[:, None], LANES, 1)]).astype(np.float32)
    return grp, wts


def _select_kernel(q_ref, slope_ref, kc_ref, vct_ref, imp_ref, grp_ref, wts_ref, ocmp_ref, bias_ref, bits_ref,
                   *, ns, nc, qb):
    P, T = NSA_PER_GROUP, Q_BLOCK
    step = pl.program_id(2)
    nsteps = pl.num_programs(2)
    subs = range(qb)
    c0s = [(step * qb + u) * T for u in subs]
    qts, tlanes = zip(*[_query_t(q_ref, slope_ref, c0s[u], u * T) for u in subs])

    def body(rows, nblk):
        cend = lax.broadcasted_iota(jnp.int32, (rows, 1), 0) * CMP_STRIDE + (CMP_LEN - 1)
        cmask = [cend <= tlanes[u] for u in subs]
        s = [jnp.where(cmask[u], _dot(kc_ref[0, 0, 0:rows, :], qts[u]), NEG_BIG) for u in subs]
        m = [jnp.max(s[u], 0, keepdims=True) for u in subs]
        p_c = [jnp.where(cmask[u], jnp.exp(s[u] - m[u]), 0.0) for u in subs]
        p_c = [p_c[u] * (1.0 / jnp.maximum(jnp.sum(p_c[u], 0, keepdims=True), 1e-30)) for u in subs]
        for u in subs:
            ocmp_ref[0, 0, u] = _dot(vct_ref[0, 0][:, 0:rows], p_c[u].astype(BF16))

        psum = [p_c[u][:, 0:T] for u in subs]
        for p in range(1, P):
            psum = [psum[u] + p_c[u][:, p * T:(p + 1) * T] for u in subs]
        imp = [_dot_exact_lhs(imp_ref[0:nblk, 0:rows], psum[u]) for u in subs]
        blk = lax.broadcasted_iota(jnp.int32, (nblk, T), 0)
        val, chosen = [], []
        for u in subs:
            cur = (c0s[u] + lax.broadcasted_iota(jnp.int32, (nblk, T), 1)) // SEL_LEN
            forced = (blk == 0) | (blk == cur) | (blk == cur - 1)
            val.append(jnp.where(forced, 3e38, jnp.where(blk > cur, -1.0, imp[u])))
            chosen.append(jnp.zeros((nblk, T), F32))
        for _ in range(min(SEL_TOPK, nblk)):
            for u in subs:
                mx = jnp.max(val[u], 0, keepdims=True)
                first = jnp.min(jnp.where(val[u] == mx, blk, ns), 0, keepdims=True)
                pick = blk == first
                chosen[u] = jnp.where(pick, 1.0, chosen[u])
                val[u] = jnp.where(pick, -2.0, val[u])
        for u in subs:
            bias_ref[0, 0, u, 0:nblk, :] = jnp.where(chosen[u] > 0.5, 0.0, NEG_BIG)
            if nblk < ns:
                bias_ref[0, 0, u, nblk:ns, :] = jnp.full((ns - nblk, T), NEG_BIG, F32)
            ch = chosen[u]
            if nblk < ns:
                ch = jnp.concatenate([ch, jnp.zeros((ns - nblk, T), F32)], 0)
            cnt = _dot(grp_ref[...], ch.astype(BF16))
            act = jnp.where(jnp.max(cnt, 1, keepdims=True) > 0.5, 1.0, 0.0)
            lo = jnp.sum(act * wts_ref[0], 0, keepdims=True).astype(jnp.int32)
            hi = jnp.sum(act * wts_ref[1], 0, keepdims=True).astype(jnp.int32)
            bits_ref[u] = jnp.broadcast_to(lo | (hi << PLAN_HALF), (SUBLANES, LANES))

    parts = max(p for p in (SELECT_PARTS, 2, 1) if nc % (p * LANES) == 0 and ns % (p * 2 * SUBLANES) == 0)
    part = step * parts // nsteps
    for v in range(parts):
        pl.when(part == v)(functools.partial(body, nc * (v + 1) // parts, ns * (v + 1) // parts))


def _attend_kernel(bits_sref, q_ref, gt_ref, slope_ref, ocmp_ref, bias_ref, ks_ref, vst_ref, kw_ref, vwt_ref,
                   o_ref, list_ref, s_ref, m_ref, *, n_tiles_total, ns, qb):
    hd, P, T = NSA_HEAD_DIM, NSA_PER_GROUP, Q_BLOCK
    PT = P * T
    b, g, step = pl.program_id(0), pl.program_id(1), pl.program_id(2)
    subs = range(qb)
    iqs = [step * qb + u for u in subs]
    c0s = [iq * T for iq in iqs]
    qts, tlanes = zip(*[_query_t(q_ref, slope_ref, c0s[u], u * T) for u in subs])
    tl_row = lax.broadcasted_iota(jnp.int32, (1, PT), 1) % T

    gts = [jax.nn.sigmoid(gt_ref[0, u * T:(u + 1) * T, :]).T for u in subs]

    def gate_row(u, branch):
        return jnp.concatenate([gts[u][p * 3 + branch:p * 3 + branch + 1, :] for p in range(P)], 1)

    list_len = n_tiles_total + 1
    last_tiles = [(c0s[u] + T - 1) // SEL_TILE for u in subs]
    prev_bits = []
    for u in subs:
        bits = bits_sref[(b * pl.num_programs(1) + g) * (pl.num_programs(2) * qb) + iqs[u]]
        prev_bits.append(bits & (lax.shift_left(jnp.int32(1), last_tiles[u]) - 1))
    n_prevs = [jnp.int32(0) for _ in subs]
    for k in range(n_tiles_total):
        for u in subs:
            list_ref[u * list_len + n_prevs[u]] = k
            n_prevs[u] = n_prevs[u] + (lax.shift_right_logical(prev_bits[u], k) & 1)

    qt_sels = []
    for u in subs:
        sel_b = bias_ref[0, 0, u]
        if ns < SEL_ONEHOT:
            sel_b = jnp.concatenate([sel_b, jnp.zeros((SEL_ONEHOT - ns, T), F32)], 0)
        qt_sels.append(jnp.concatenate([qts[u], jnp.concatenate([sel_b.astype(BF16)] * P, 1)], 0))

    def tile_scores(u, kt):
        k0 = pl.multiple_of(kt * SEL_TILE, SEL_TILE)
        return _dot(ks_ref[0, 0, pl.ds(k0, SEL_TILE), :], qt_sels[u])

    def col_max(scores):
        m = jnp.max(scores[0], 0, keepdims=True)
        for sc in scores[1:]:
            m = jnp.maximum(m, jnp.max(sc, 0, keepdims=True))
        return m

    def softmax_update(carry, scores, vts, m_tile):
        m_run, acc = carry
        m_new = jnp.maximum(m_run, m_tile)
        acc = jnp.exp(m_run - m_new) * acc
        for sc, vt in zip(scores, vts):
            acc = acc + _dot(vt, jnp.exp(sc - m_new).astype(BF16))
        return m_new, acc

    init = (jnp.full((1, PT), NEG_BIG, F32), jnp.zeros((NSA_VROWS, PT), F32))
    dead_tile = n_tiles_total
    n_pairs = [jnp.maximum((n_prevs[u] + 1) // 2, 1) for u in subs]

    def pair_tiles(u, i):
        has_a, has_b = 2 * i < n_prevs[u], 2 * i + 1 < n_prevs[u]
        ka = jnp.where(has_a, list_ref[u * list_len + jnp.where(has_a, 2 * i, 0)], dead_tile)
        kb = jnp.where(has_b, list_ref[u * list_len + jnp.where(has_b, 2 * i + 1, 0)], dead_tile)
        return ka, kb

    def pair_scores(u, i):
        ka, kb = pair_tiles(u, i)
        keys = jnp.concatenate([ks_ref[0, 0, pl.ds(pl.multiple_of(k * SEL_TILE, SEL_TILE), SEL_TILE), :]
                                for k in (ka, kb)], 0)
        return _dot(keys, qt_sels[u])

    def put_scores(u, i, sc):
        slot = i % 2
        s_ref[u, slot] = sc
        m_ref[u, slot] = col_max([sc])

    def pair_probs(u, i, m_run):
        slot = i % 2
        m_new = jnp.maximum(m_run, m_ref[u, slot])
        return m_new, jnp.exp(s_ref[u, slot] - m_new).astype(BF16)

    def add_values(u, i, carry, m_new, pr):
        ka, kb = pair_tiles(u, i)
        m_run, acc = carry
        vt = jnp.concatenate([vst_ref[0, 0, ka], vst_ref[0, 0, kb]], 1)
        return m_new, jnp.exp(m_run - m_new) * acc + _dot(vt, pr)

    first = [pair_scores(u, 0) for u in subs]
    for u in subs:
        put_scores(u, 0, first[u])

    n_win = (WINDOW + T) // WIN_TILE
    roff = lax.broadcasted_iota(jnp.int32, (WIN_TILE, 1), 0)
    w_sc = [_dot(kw_ref[0, 0, pl.ds(pl.multiple_of(c0s[u], WIN_TILE), WINDOW + T), :], qts[u]) for u in subs]
    w_parts = [[jnp.where(roff > tl_row, w_sc[u][0:WIN_TILE], NEG_BIG),
                w_sc[u][WIN_TILE:WINDOW],
                jnp.where(roff <= tl_row, w_sc[u][WINDOW:WINDOW + T], NEG_BIG)]
               for u in subs]
    w_max = [col_max(w_parts[u]) for u in subs]
    out_cw = []
    for u in subs:
        vt = jnp.concatenate([vwt_ref[0, 0, iqs[u] + j] for j in range(n_win)], 1)
        _, acc_win = softmax_update(init, [jnp.concatenate(w_parts[u], 0)], [vt], w_max[u])
        out_cw.append(gate_row(u, 0) * ocmp_ref[0, 0, u] + gate_row(u, 2) * (acc_win[0:hd] / acc_win[hd:hd + 1]))

    def pipe_step(u, i, carry):
        m_new, pr = pair_probs(u, i, carry[0])
        sc_next = pair_scores(u, i + 1)
        carry = add_values(u, i, carry, m_new, pr)
        put_scores(u, i + 1, sc_next)
        return carry

    carries = [lax.fori_loop(0, n_pairs[u] - 1, functools.partial(pipe_step, u), init) for u in subs]
    probs = [pair_probs(u, n_pairs[u] - 1, carries[u][0]) for u in subs]
    s_last = []
    for u in subs:
        kpos = last_tiles[u] * SEL_TILE + lax.broadcasted_iota(jnp.int32, (SEL_TILE, 1), 0)
        s_last.append(jnp.where(kpos <= tlanes[u], tile_scores(u, last_tiles[u]), NEG_BIG))
    carries = [add_values(u, n_pairs[u] - 1, carries[u], *probs[u]) for u in subs]
    for u in subs:
        _, acc_sel = softmax_update(carries[u], [s_last[u]], [vst_ref[0, 0, last_tiles[u]]], col_max([s_last[u]]))
        out = out_cw[u] + gate_row(u, 1) * (acc_sel[0:hd] / acc_sel[hd:hd + 1])
        o_ref[0, u * T:(u + 1) * T, :] = jnp.concatenate(
            [out[:, p * T:(p + 1) * T] for p in range(P)], 0).T.astype(o_ref.dtype)


def _nsa_call(proj3, pe_k, w1_k, w2_k, pe_v, w1_v, w2_v):
    B, S, _ = proj3.shape
    G, hd, P, T = NSA_GROUPS, NSA_HEAD_DIM, NSA_PER_GROUP, Q_BLOCK
    PT = P * T
    nc = S // CMP_STRIDE
    ns = S // SEL_LEN
    nq = S // T
    nt = ns // SEL_TILE_BLOCKS
    assert nt <= 2 * PLAN_HALF and S % KV_PREP_ROWS == 0

    kc_a, vc_t = _compress_call(proj3, pe_k, w1_k, w2_k, pe_v, w1_v, w2_v)
    ks_a, vs_t, kw_a, vw_t = _kvprep_call(proj3)
    imp_m = jnp.asarray(_importance_matrix(ns, nc), BF16)
    grp, wts = _plan_constants(ns)
    head_slopes = 2.0 ** (-(8.0 / NSA_HEADS) * np.arange(1, NSA_HEADS + 1, dtype=np.float32))
    slopes = jnp.asarray(np.repeat(head_slopes.reshape(G, 1, P), T, axis=-1), F32)

    q_spec = lambda im: pl.BlockSpec((1, T, P * hd), im)
    bg = lambda shp: pl.BlockSpec((1, 1) + shp, lambda b, g, i: (b, g) + tuple(0 for _ in shp))
    const = lambda a: pl.BlockSpec(a.shape, lambda b, g, i: tuple(0 for _ in a.shape))
    qb = SELECT_QB if nq % (SELECT_QB * SELECT_PARTS) == 0 else 1
    nsteps = nq // qb
    ocmp, sel_bias, bits = pl.pallas_call(
        functools.partial(_select_kernel, ns=ns, nc=nc, qb=qb),
        grid=(B, G, nsteps),
        in_specs=[pl.BlockSpec((1, qb * T, P * hd), lambda b, g, i: (b, i, OFF_NQ // (P * hd) + g)),
                  pl.BlockSpec((1, 1, PT), lambda b, g, i: (g, 0, 0)),
                  bg((nc, NSA_KW)), bg((hd, nc)), const(imp_m),
                  pl.BlockSpec(grp.shape, lambda b, g, i: (0, 0)),
                  pl.BlockSpec(wts.shape, lambda b, g, i: (0, 0, 0))],
        out_specs=[pl.BlockSpec((1, 1, qb, hd, PT), lambda b, g, i: (b, g, i, 0, 0)),
                   pl.BlockSpec((1, 1, qb, ns, T), lambda b, g, i: (b, g, i, 0, 0)),
                   pl.BlockSpec((qb, SUBLANES, LANES), lambda b, g, i: ((b * G + g) * nsteps + i, 0, 0))],
        out_shape=[jax.ShapeDtypeStruct((B, G, nq, hd, PT), F32),
                   jax.ShapeDtypeStruct((B, G, nq, ns, T), F32),
                   jax.ShapeDtypeStruct((B * G * nq, SUBLANES, LANES), jnp.int32)],
        compiler_params=_params("parallel", "parallel", "parallel"),
        name="nsa_select",
    )(proj3, slopes, kc_a, vc_t, imp_m, jnp.asarray(grp, BF16), jnp.asarray(wts, F32))

    aq = ATTEND_QB if nq % ATTEND_QB == 0 else 1
    bg2 = lambda shp: pl.BlockSpec((1, 1) + shp, lambda b, g, i, s: (b, g) + tuple(0 for _ in shp))
    return pl.pallas_call(
        functools.partial(_attend_kernel, n_tiles_total=nt, ns=ns, qb=aq),
        grid_spec=pltpu.PrefetchScalarGridSpec(
            num_scalar_prefetch=1,
            grid=(B, G, nq // aq),
            in_specs=[pl.BlockSpec((1, aq * T, P * hd), lambda b, g, i, s: (b, i, OFF_NQ // (P * hd) + g)),
                      pl.BlockSpec((1, aq * T, LANES), lambda b, g, i, s: (b, i, OFF_NG // LANES + g)),
                      pl.BlockSpec((1, 1, PT), lambda b, g, i, s: (g, 0, 0)),
                      pl.BlockSpec((1, 1, aq, hd, PT), lambda b, g, i, s: (b, g, i, 0, 0)),
                      pl.BlockSpec((1, 1, aq, ns, T), lambda b, g, i, s: (b, g, i, 0, 0)),
                      bg2(ks_a.shape[2:]), bg2(vs_t.shape[2:]), bg2(kw_a.shape[2:]), bg2(vw_t.shape[2:])],
            out_specs=pl.BlockSpec((1, aq * T, P * hd), lambda b, g, i, s: (b, i, g)),
            scratch_shapes=[pltpu.SMEM((aq * (nt + 1),), jnp.int32),
                            pltpu.VMEM((aq, 2, 2 * SEL_TILE, PT), F32), pltpu.VMEM((aq, 2, 1, PT), F32)]),
        out_shape=jax.ShapeDtypeStruct((B, S, NSA_WIDTH), BF16),
        compiler_params=_params("parallel", "parallel", "arbitrary"),
        name="nsa_attend",
    )(bits[:, 0, 0], proj3, proj3, slopes, ocmp, sel_bias, ks_a, vs_t, kw_a, vw_t)


def _merge_kernel(x_ref, ya_ref, yb_ref, yc_ref, g0_ref, g1_ref, g2_ref, wa_ref, wb_ref, wc_ref, wo_ref,
                  lg_ref, lb_ref, o_ref):
    gate = lambda r: jax.nn.sigmoid(r[...].astype(F32))
    m = gate(g0_ref) * _dot(ya_ref[...], wa_ref[...])
    m = m + gate(g1_ref) * _dot(yb_ref[...], wb_ref[...])
    m = m + gate(g2_ref) * _dot(yc_ref[...], wc_ref[...])
    z = DEEPNORM_ALPHA * x_ref[...] + _dot(m.astype(BF16), wo_ref[...])
    o_ref[...] = _layer_norm_rows(z, lg_ref[...], lb_ref[...])


def _merge_call(x2, ya, yb, yc, gates, w_hg, w_nsa, w_lru, w_out, ln_g, ln_b, tm=512):
    T, D = x2.shape
    tm = min(tm, T)
    rows = lambda w: pl.BlockSpec((tm, w), lambda i: (i, 0))
    gate = lambda n: pl.BlockSpec((tm, D), lambda i: (i, n))
    full = lambda a: pl.BlockSpec(a.shape, lambda i: (0, 0))
    ws = [w.astype(BF16) for w in (w_hg, w_nsa, w_lru, w_out)]
    return pl.pallas_call(
        _merge_kernel,
        grid=(T // tm,),
        in_specs=[rows(D), rows(HG_WIDTH), rows(NSA_WIDTH), rows(LRU_WIDTH), gate(0), gate(1), gate(2)]
                 + [full(w) for w in ws] + [pl.BlockSpec((1, D), lambda i: (0, 0))] * 2,
        out_specs=rows(D),
        out_shape=jax.ShapeDtypeStruct((T, D), F32),
        compiler_params=_params("parallel"),
        name="merge_out",
    )(x2, ya, yb, yc, gates, gates, gates, *ws, ln_g.reshape(1, D), ln_b.reshape(1, D))


FFN_SPLIT = 2
FFN_PAD = SUBLANES
FFN_SUB = 256


def _ffn_kernel(x_ref, wu_ref, wv_ref, cw_ref, cb_ref, wd_ref, lg_ref, lb_ref, o_ref,
                xb_ref, acc_ref, upad_ref, carry_ref, *, tm, tiles_per_seq):
    i = pl.program_id(0)
    j = pl.program_id(1)

    @pl.when(j == 0)
    def _():
        xb_ref[...] = x_ref[...].astype(BF16)

    xb = xb_ref[...]
    fc = wu_ref.shape[1]
    first = (i % tiles_per_seq) == 0

    @pl.when(first)
    def _():
        upad_ref[0:FFN_PAD, :] = jnp.zeros((FFN_PAD, fc), F32)

    @pl.when(jnp.logical_not(first))
    def _():
        upad_ref[0:FFN_PAD, :] = carry_ref[j]

    @pl.when(j == 0)
    def _():
        acc_ref[...] = jnp.zeros_like(acc_ref)

    for c0 in range(0, fc, FFN_SUB):
        cs = slice(c0, min(c0 + FFN_SUB, fc))
        u = _dot(xb, wu_ref[:, cs])
        v = _dot(xb, wv_ref[:, cs])
        upad_ref[FFN_PAD:FFN_PAD + tm, cs] = u
        cw = cw_ref[:, cs]
        cv = cb_ref[:, cs] + cw[FFN_CONV - 1:FFN_CONV, :] * u
        for k in range(FFN_CONV - 1):
            off = FFN_PAD - (FFN_CONV - 1) + k
            cv = cv + cw[k:k + 1, :] * upad_ref[off:off + tm, cs]
        carry_ref[j, :, cs] = u[tm - FFN_PAD:tm, :]
        h = (jax.nn.gelu(cv) * v).astype(BF16)
        acc_ref[...] += _dot(h, wd_ref[cs, :])

    @pl.when(j == FFN_SPLIT - 1)
    def _():
        z = DEEPNORM_ALPHA * x_ref[...] + acc_ref[...]
        o_ref[...] = _layer_norm_rows(z, lg_ref[...], lb_ref[...])


def _ffn_call(x2, seq_len, w_up, conv_w, conv_b, w_down, ln_g, ln_b, tm=512):
    T, D = x2.shape
    tm = min(tm, seq_len)
    fc = FFN_DIM // FFN_SPLIT
    wu = w_up.astype(BF16)
    return pl.pallas_call(
        functools.partial(_ffn_kernel, tm=tm, tiles_per_seq=seq_len // tm),
        grid=(T // tm, FFN_SPLIT),
        in_specs=[pl.BlockSpec((tm, D), lambda i, j: (i, 0)),
                  pl.BlockSpec((D, fc), lambda i, j: (0, j)),
                  pl.BlockSpec((D, fc), lambda i, j: (0, FFN_SPLIT + j)),
                  pl.BlockSpec((FFN_CONV, fc), lambda i, j: (0, j)),
                  pl.BlockSpec((1, fc), lambda i, j: (0, j)),
                  pl.BlockSpec((fc, D), lambda i, j: (j, 0)),
                  pl.BlockSpec((1, D), lambda i, j: (0, 0)),
                  pl.BlockSpec((1, D), lambda i, j: (0, 0))],
        out_specs=pl.BlockSpec((tm, D), lambda i, j: (i, 0)),
        out_shape=jax.ShapeDtypeStruct((T, D), F32),
        scratch_shapes=[pltpu.VMEM((tm, D), BF16), pltpu.VMEM((tm, D), F32),
                        pltpu.VMEM((tm + FFN_PAD, fc), F32), pltpu.VMEM((FFN_SPLIT, FFN_PAD, fc), F32)],
        compiler_params=_params("arbitrary", "arbitrary"),
        name="conv_ffn",
    )(x2, wu, wu, conv_w.astype(F32), conv_b.reshape(1, FFN_DIM).astype(F32), w_down.astype(BF16),
      ln_g.reshape(1, D), ln_b.reshape(1, D))


def _permute_in_proj(w, b):
    src = np.cumsum([0, HG_WIDTH, HG_WIDTH, HG_WIDTH, HG_WIDTH, NSA_WIDTH, NSA_KV, NSA_KV, NSA_KV, NSA_KV,
                     NSA_KV, NSA_KV, NSA_HEADS * 3, LRU_WIDTH, LRU_WIDTH, N_BRANCH * 1024])
    names = ["hq", "hf", "hi", "hg", "nq", "kc", "vc", "ks", "vs", "kw", "vw", "ng", "lx", "ly", "mg"]
    seg = {n: (int(src[k]), int(src[k + 1])) for k, n in enumerate(names)}
    order = ["mg", "hq", "hf", "hi", "hg", "nq", "lx", "ly", "kc", "vc", "ks", "vs", "kw", "vw"]
    per_group = NSA_PER_GROUP * 3
    ng0 = seg["ng"][0]
    w = w.astype(BF16)
    w_parts = [w[:, seg[n][0]:seg[n][1]] for n in order]
    b_parts = [b[seg[n][0]:seg[n][1]] for n in order]
    for gi in range(NSA_GROUPS):
        lo = ng0 + gi * per_group
        w_parts += [w[:, lo:lo + per_group], jnp.zeros((w.shape[0], LANES - per_group), w.dtype)]
        b_parts += [b[lo:lo + per_group], jnp.zeros((LANES - per_group,), b.dtype)]
    wp = jnp.concatenate(w_parts, 1)
    bp = jnp.concatenate(b_parts)
    assert wp.shape[1] == GATE_COLS + PROJ_COLS
    return wp.astype(BF16), bp.astype(F32)


def kernel(x, ln_emb_g, ln_emb_b, w_in, b_in, hg_lb_logits, hg_norm_g, cmp_pe_k, cmp_w1_k, cmp_w2_k, cmp_pe_v, cmp_w1_v, cmp_w2_v, lru_conv_w, lru_conv_b, lru_wa, lru_ba, lru_wx, lru_bx, lru_lambda, w_branch_hg, w_branch_nsa, w_branch_lru, w_out, ln1_g, ln1_b, ffn_w_up, ffn_conv_w, ffn_conv_b, ffn_w_down, ln2_g, ln2_b):
    B, S, D = x.shape
    T = B * S
    gam = jax.nn.softmax(hg_lb_logits.astype(F32), axis=0)
    lb_all = jnp.cumsum(gam, axis=0) - gam[0]
    h = x.reshape(T, D)
    for l in range(DEPTH):
        wp, bp = _permute_in_proj(w_in[l], b_in[l])
        if l == 0:
            gates, proj2, h = _inproj_call(h, wp, bp, ln=(ln_emb_g, ln_emb_b))
        else:
            gates, proj2 = _inproj_call(h, wp, bp)
        proj3 = proj2.reshape(B, S, PROJ_COLS)
        y_a = _hgrn_call(proj3, lb_all[l], hg_norm_g[l])
        y_b = _nsa_call(proj3, cmp_pe_k[l], cmp_w1_k[l], cmp_w2_k[l], cmp_pe_v[l], cmp_w1_v[l], cmp_w2_v[l])
        y_c = _lru_call(proj3, lru_conv_w[l], lru_conv_b[l], lru_wa[l], lru_ba[l], lru_wx[l], lru_bx[l],
                        lru_lambda[l])
        h = _merge_call(h, y_a.reshape(T, HG_WIDTH), y_b.reshape(T, NSA_WIDTH), y_c.reshape(T, LRU_WIDTH),
                        gates, w_branch_hg[l], w_branch_nsa[l], w_branch_lru[l], w_out[l], ln1_g[l], ln1_b[l])
        h = _ffn_call(h, S, ffn_w_up[l], ffn_conv_w[l], ffn_conv_b[l], ffn_w_down[l], ln2_g[l], ln2_b[l])
    return h.reshape(B, S, D)
```

```python
import functools

import numpy as np
import jax
import jax.numpy as jnp
from jax import lax
from jax.experimental import pallas as pl
from jax.experimental.pallas import tpu as pltpu

F32 = jnp.float32
BF16 = jnp.bfloat16

DEPTH = 2
HG_HEADS = 4
HG_HEAD_DIM = 128
HG_WIDTH = HG_HEADS * HG_HEAD_DIM
NSA_HEADS = 8
NSA_GROUPS = 2
NSA_PER_GROUP = NSA_HEADS // NSA_GROUPS
NSA_HEAD_DIM = 64
NSA_WIDTH = NSA_HEADS * NSA_HEAD_DIM
NSA_KV = NSA_GROUPS * NSA_HEAD_DIM
CMP_LEN = 32
CMP_STRIDE = 16
CMP_HIDDEN = 64
SEL_LEN = 64
SEL_TOPK = 16
WINDOW = 512
Q_BLOCK = 128
LRU_WIDTH = 512
LRU_BLOCKS = 4
LRU_CONV = 4
LRU_C = 8.0
FFN_DIM = 2816
FFN_CONV = 3
DEEPNORM_ALPHA = (2 * DEPTH) ** 0.25
LN_EPS = 1e-5
N_BRANCH = 3

LANES = 128
SUBLANES = 8
VMEM_LIMIT_BYTES = 52 * 1024 * 1024

GATE_COLS = N_BRANCH * 1024
OFF_HQ = 0
OFF_HF = OFF_HQ + HG_WIDTH
OFF_HI = OFF_HF + HG_WIDTH
OFF_HG = OFF_HI + HG_WIDTH
OFF_NQ = OFF_HG + HG_WIDTH
OFF_LX = OFF_NQ + NSA_WIDTH
OFF_LY = OFF_LX + LRU_WIDTH
OFF_KC = OFF_LY + LRU_WIDTH
OFF_VC = OFF_KC + NSA_KV
OFF_KS = OFF_VC + NSA_KV
OFF_VS = OFF_KS + NSA_KV
OFF_KW = OFF_VS + NSA_KV
OFF_VW = OFF_KW + NSA_KV
OFF_NG = OFF_VW + NSA_KV
PROJ_TILE_N = 1536
GATE_TILES = GATE_COLS // PROJ_TILE_N
PROJ_COLS = OFF_NG + NSA_GROUPS * LANES

NEG_BIG = -1e30


def _split3(x):
    hi = x.astype(BF16)
    r1 = x - hi.astype(F32)
    mid = r1.astype(BF16)
    lo = (r1 - mid.astype(F32)).astype(BF16)
    return hi, mid, lo


def _dot(a, b):
    return jnp.dot(a, b, preferred_element_type=F32)


def _dot_nt(a, b):
    return lax.dot_general(a, b, (((1,), (1,)), ((), ())), preferred_element_type=F32)


def _dot_exact_lhs(m_bf16, x):
    hi, mid, lo = _split3(x)
    return _dot(m_bf16, hi) + _dot(m_bf16, mid) + _dot(m_bf16, lo)


def _dot_hilo(a, b):
    ah = a.astype(BF16)
    al = (a - ah.astype(F32)).astype(BF16)
    bh = b.astype(BF16)
    bl = (b - bh.astype(F32)).astype(BF16)
    return _dot(ah, bh) + _dot(ah, bl) + _dot(al, bh) + _dot(al, bl)


def _layer_norm_rows(z, g, b):
    mu = jnp.mean(z, -1, keepdims=True)
    zc = z - mu
    var = jnp.mean(zc * zc, -1, keepdims=True)
    return zc * lax.rsqrt(var + LN_EPS) * g + b


def _params(*sem):
    return pltpu.CompilerParams(dimension_semantics=sem, vmem_limit_bytes=VMEM_LIMIT_BYTES)


def _inproj_kernel(*refs, with_ln):
    if with_ln:
        x_ref, g_ref, be_ref, w_ref, b_ref, gate_ref, o_ref, xn_ref, xb_ref = refs
    else:
        x_ref, w_ref, b_ref, gate_ref, o_ref, xb_ref = refs
    j = pl.program_id(1)

    @pl.when(j == 0)
    def _():
        x = x_ref[...]
        if with_ln:
            x = _layer_norm_rows(x, g_ref[...], be_ref[...])
            xn_ref[...] = x
        xb_ref[...] = x.astype(BF16)

    acc = _dot(xb_ref[...], w_ref[...]) + b_ref[...]

    @pl.when(j < GATE_TILES)
    def _():
        gate_ref[...] = acc.astype(BF16)

    @pl.when(j >= GATE_TILES)
    def _():
        o_ref[...] = acc


def _inproj_call(x2, w_bf16, bias, ln=None, tm=1024):
    T, D = x2.shape
    N = w_bf16.shape[1]
    tm = min(tm, T)
    tn = PROJ_TILE_N
    vec = pl.BlockSpec((1, D), lambda i, j: (0, 0))
    in_specs = [pl.BlockSpec((tm, D), lambda i, j: (i, 0))] + ([vec, vec] if ln else []) + [
        pl.BlockSpec((D, tn), lambda i, j: (0, j)), pl.BlockSpec((1, tn), lambda i, j: (0, j))]
    out_specs = [pl.BlockSpec((tm, tn), lambda i, j: (i, jnp.minimum(j, GATE_TILES - 1))),
                 pl.BlockSpec((tm, tn), lambda i, j: (i, jnp.maximum(j - GATE_TILES, 0)))]
    out_shape = [jax.ShapeDtypeStruct((T, GATE_COLS), BF16), jax.ShapeDtypeStruct((T, N - GATE_COLS), F32)]
    if ln:
        out_specs.append(pl.BlockSpec((tm, D), lambda i, j: (i, 0)))
        out_shape.append(jax.ShapeDtypeStruct((T, D), F32))
    ln_args = [ln[0].reshape(1, D), ln[1].reshape(1, D)] if ln else []
    return pl.pallas_call(
        functools.partial(_inproj_kernel, with_ln=bool(ln)),
        grid=(T // tm, N // tn),
        in_specs=in_specs, out_specs=out_specs, out_shape=out_shape,
        scratch_shapes=[pltpu.VMEM((tm, D), BF16)],
        compiler_params=_params("parallel", "arbitrary"),
        name="in_proj_ln" if ln else "in_proj",
    )(x2, *ln_args, w_bf16, bias.reshape(1, N))


HG_CHUNK = 128
HG_DIAG = SUBLANES
HG_HEADS_PER_STEP = 4


def _hgrn_constants(C):
    r = np.arange(C)
    tri = (r[None, :] <= r[:, None]).astype(np.float32)
    mats = [tri]
    masks = []
    w = HG_DIAG
    while 2 * w <= C:
        mid = (r // (2 * w)) * (2 * w) + w
        mats.append(tri - (r[None, :] <= mid[:, None]).astype(np.float32))
        same = (r[:, None] // (2 * w)) == (r[None, :] // (2 * w))
        masks.append((same & ((r[:, None] % (2 * w)) >= w) & ((r[None, :] % (2 * w)) < w)).astype(np.float32))
        w *= 2
    mats.append((r[None, :] > r[:, None]).astype(np.float32))
    return np.concatenate(mats, 0), np.stack(masks, 0)


def _hgrn_kernel(q_ref, f_ref, i_ref, g_ref, lb_ref, ng_ref, m_ref, mask_ref, o_ref, st_ref, *, C, nchunk, nh):
    d = HG_HEAD_DIM
    nlev = mask_ref.shape[0]

    @pl.when(pl.program_id(2) == 0)
    def _():
        st_ref[...] = jnp.zeros_like(st_ref)

    hs = range(nh)
    lane = [slice(h * d, (h + 1) * d) for h in hs]
    lb = [lb_ref[h] for h in hs]
    ng = [ng_ref[h] for h in hs]
    nb = C // HG_DIAG
    row3 = lax.broadcasted_iota(jnp.int32, (nb, HG_DIAG, d), 1)

    fl_all = f_ref[0]
    lf_all = [jnp.log(lb[h] + (1.0 - lb[h]) * jax.nn.sigmoid(fl_all[:, lane[h]])) for h in hs]
    kk_all = [(1.0 - lb[h]) * jax.nn.sigmoid(-fl_all[:, lane[h]]) for h in hs]
    allm_all = _dot_exact_lhs(m_ref[...], jnp.concatenate(
        [lf_all[h][c * C:(c + 1) * C] for h in hs for c in range(nchunk)], 1))

    for c in range(nchunk):
        sl = pl.ds(c * C, C)
        q = [q_ref[0, sl, lane[h]] for h in hs]
        iv = [i_ref[0, sl, lane[h]] for h in hs]
        kk = [kk_all[h][c * C:(c + 1) * C] for h in hs]
        allm = [allm_all[:, (h * nchunk + c) * d:(h * nchunk + c + 1) * d] for h in hs]
        bcs = [allm[h][0:C] for h in hs]
        rem = [allm[h][(nlev + 1) * C:(nlev + 2) * C] for h in hs]
        iv_b = [iv[h].astype(BF16) for h in hs]

        att = [jnp.zeros((C, C), F32) for _ in hs]
        for l in range(nlev):
            e = [jnp.exp(-jnp.abs(allm[h][(1 + l) * C:(2 + l) * C])) for h in hs]
            att = [att[h] + mask_ref[l] * _dot_nt((q[h] * e[h]).astype(BF16), (kk[h] * e[h]).astype(BF16))
                   for h in hs]
        o = [_dot(att[h].astype(BF16), iv_b[h]) for h in hs]

        q3 = [q[h].reshape(nb, HG_DIAG, d) for h in hs]
        k3 = [kk[h].reshape(nb, HG_DIAG, d) for h in hs]
        b3 = [bcs[h].reshape(nb, HG_DIAG, d) for h in hs]
        i3 = [iv[h].reshape(nb, HG_DIAG, d) for h in hs]
        acc = [jnp.zeros((nb, HG_DIAG, d), F32) for _ in hs]
        for s in range(HG_DIAG):
            for h in hs:
                dec = jnp.where(row3 >= s, jnp.exp(b3[h] - b3[h][:, s:s + 1, :]), 0.0)
                a = jnp.sum(q3[h] * dec * k3[h][:, s:s + 1, :], axis=-1, keepdims=True)
                acc[h] = acc[h] + a * i3[h][:, s:s + 1, :]

        for h in hs:
            st = st_ref[h]
            oh = o[h] + acc[h].reshape(C, d) + _dot_nt((q[h] * jnp.exp(bcs[h])).astype(BF16), st.astype(BF16))
            kdec = (kk[h] * jnp.exp(rem[h])).astype(BF16)
            st_ref[h] = st * jnp.exp(bcs[h][C - 1:C, :]) + _dot(iv[h].T.astype(BF16), kdec)
            oh = oh * lax.rsqrt(jnp.mean(oh * oh, -1, keepdims=True) + 1e-6) * ng[h]
            g = g_ref[0, sl, lane[h]]
            o_ref[0, sl, lane[h]] = (oh * (g * jax.nn.sigmoid(g))).astype(o_ref.dtype)


def _hgrn_call(proj3, lb, norm_g, tt=512):
    B, S, _ = proj3.shape
    d = HG_HEAD_DIM
    C = HG_CHUNK
    tt = min(tt, S)
    mstack, masks = _hgrn_constants(C)
    nm = mstack.shape[0]
    nh = HG_HEADS_PER_STEP
    wid = nh * d
    col = lambda off: (lambda b, h, t: (b, t, off // wid + h))
    blk = (1, tt, wid)
    return pl.pallas_call(
        functools.partial(_hgrn_kernel, C=C, nchunk=tt // C, nh=nh),
        grid=(B, HG_HEADS // nh, S // tt),
        in_specs=[pl.BlockSpec(blk, col(OFF_HQ)), pl.BlockSpec(blk, col(OFF_HF)),
                  pl.BlockSpec(blk, col(OFF_HI)), pl.BlockSpec(blk, col(OFF_HG)),
                  pl.BlockSpec((nh, 1, d), lambda b, h, t: (h, 0, 0)),
                  pl.BlockSpec((nh, 1, d), lambda b, h, t: (h, 0, 0)),
                  pl.BlockSpec((nm, C), lambda b, h, t: (0, 0)),
                  pl.BlockSpec(masks.shape, lambda b, h, t: (0, 0, 0))],
        out_specs=pl.BlockSpec(blk, lambda b, h, t: (b, t, h)),
        out_shape=jax.ShapeDtypeStruct((B, S, HG_WIDTH), BF16),
        scratch_shapes=[pltpu.VMEM((nh, d, d), F32)],
        compiler_params=_params("parallel", "parallel", "arbitrary"),
        name="hgrn2",
    )(proj3, proj3, proj3, proj3, lb.reshape(HG_HEADS, 1, d), norm_g.reshape(HG_HEADS, 1, d),
      jnp.asarray(mstack, BF16), jnp.asarray(masks, F32))


LRU_PAD = SUBLANES
LRU_STEP = 16


def _lru_kernel(x_ref, y_ref, cw_ref, cb_ref, wa_ref, ba_ref, wx_ref, bx_ref, c_ref, o_ref,
                xpad_ref, h_ref, a_ref, u_ref, *, tt):
    W = LRU_WIDTH
    bw = W // LRU_BLOCKS

    @pl.when(pl.program_id(1) == 0)
    def _():
        xpad_ref[0:LRU_PAD, :] = jnp.zeros((LRU_PAD, W), F32)
        h_ref[...] = jnp.zeros_like(h_ref)

    x = x_ref[0]
    xpad_ref[LRU_PAD:LRU_PAD + tt, :] = x
    cw = cw_ref[...]
    xc = cb_ref[...] + cw[LRU_CONV - 1:LRU_CONV, :] * x
    for j in range(LRU_CONV - 1):
        off = LRU_PAD - (LRU_CONV - 1) + j
        xc = xc + cw[j:j + 1, :] * xpad_ref[off:off + tt, :]
    xpad_ref[0:LRU_PAD, :] = x[tt - LRU_PAD:tt, :]

    rs, is_ = [], []
    for gi in range(LRU_BLOCKS):
        xg = xc[:, gi * bw:(gi + 1) * bw].astype(BF16)
        rs.append(_dot(xg, wa_ref[gi]))
        is_.append(_dot(xg, wx_ref[gi]))
    r = jax.nn.sigmoid(jnp.concatenate(rs, -1) + ba_ref[...])
    ig = jax.nn.sigmoid(jnp.concatenate(is_, -1) + bx_ref[...])
    log_a = c_ref[...] * r
    a = jnp.exp(log_a)
    u = jnp.sqrt(-jnp.tanh(log_a) * (a * a + 1.0)) * (ig * xc)

    sub = lax.broadcasted_iota(jnp.int32, (tt, W), 0) % SUBLANES
    for dlt in (1, 2, 4):
        keep = sub >= dlt
        a_sh = jnp.where(keep, pltpu.roll(a, dlt, 0), 1.0)
        u_sh = jnp.where(keep, pltpu.roll(u, dlt, 0), 0.0)
        u = a * u_sh + u
        a = a * a_sh
    a_ref[...] = a
    u_ref[...] = u

    def step(i, h):
        r0 = pl.multiple_of(i * LRU_STEP, LRU_STEP)
        outs = []
        for k in range(LRU_STEP // SUBLANES):
            rows = pl.ds(r0 + k * SUBLANES, SUBLANES)
            hh = u_ref[rows, :] + a_ref[rows, :] * h
            outs.append(hh)
            h = hh[SUBLANES - 1:SUBLANES, :]
        rows = pl.ds(r0, LRU_STEP)
        o_ref[0, rows, :] = (jnp.concatenate(outs, 0) * jax.nn.gelu(y_ref[0, rows, :])).astype(o_ref.dtype)
        return h

    h_ref[...] = lax.fori_loop(0, tt // LRU_STEP, step, h_ref[...])


def _lru_call(proj3, conv_w, conv_b, wa, ba, wx, bx, lam, tt=512):
    B, S, _ = proj3.shape
    W = LRU_WIDTH
    bw = W // LRU_BLOCKS
    tt = min(tt, S)
    c = (-LRU_C * jax.nn.softplus(-lam.astype(F32))).reshape(1, W)
    vec = lambda: pl.BlockSpec((1, W), lambda b, t: (0, 0))
    return pl.pallas_call(
        functools.partial(_lru_kernel, tt=tt),
        grid=(B, S // tt),
        in_specs=[pl.BlockSpec((1, tt, W), lambda b, t: (b, t, OFF_LX // W)),
                  pl.BlockSpec((1, tt, W), lambda b, t: (b, t, OFF_LY // W)),
                  pl.BlockSpec((LRU_CONV, W), lambda b, t: (0, 0)), vec(),
                  pl.BlockSpec((LRU_BLOCKS, bw, bw), lambda b, t: (0, 0, 0)), vec(),
                  pl.BlockSpec((LRU_BLOCKS, bw, bw), lambda b, t: (0, 0, 0)), vec(), vec()],
        out_specs=pl.BlockSpec((1, tt, W), lambda b, t: (b, t, 0)),
        out_shape=jax.ShapeDtypeStruct((B, S, W), BF16),
        scratch_shapes=[pltpu.VMEM((tt + LRU_PAD, W), F32), pltpu.VMEM((1, W), F32),
                        pltpu.VMEM((tt, W), F32), pltpu.VMEM((tt, W), F32)],
        compiler_params=_params("parallel", "arbitrary"),
        name="rglru",
    )(proj3, proj3, conv_w.astype(F32), conv_b.reshape(1, W).astype(F32), wa.astype(BF16),
      ba.reshape(1, W), wx.astype(BF16), bx.reshape(1, W), c)


NSA_AUG = 4
NSA_KW = LANES
SEL_TILE = 256
SEL_TILE_BLOCKS = SEL_TILE // SEL_LEN
WIN_TILE = 128
POS_SPLIT = 128
SEL_ONEHOT = LANES
PLAN_HALF = 16
SELECT_PARTS = 4
SELECT_QB = 8
ATTEND_QB = 4
NSA_VROWS = NSA_HEAD_DIM + 16
KV_PREP_ROWS = 512


def _key_aug(pos, width, dead=False):
    n = pos.shape[0]
    col = lax.broadcasted_iota(jnp.int32, (n, width), 1)
    hi = ((pos // POS_SPLIT) * POS_SPLIT).astype(F32)
    lo = (pos % POS_SPLIT).astype(F32)
    live = jnp.where(col == 0, hi, jnp.where(col == 1, lo, jnp.where(col < NSA_AUG, 1.0, 0.0)))
    return jnp.where(dead, jnp.where(col == NSA_AUG, 1.0, 0.0), live)


def _query_t(q_ref, slope_ref, c0, row0=0):
    hd, P, T = NSA_HEAD_DIM, NSA_PER_GROUP, Q_BLOCK
    PT = P * T
    qT = (q_ref[0, row0:row0 + T, :] * (hd ** -0.5)).T
    qpart = jnp.concatenate([qT[p * hd:(p + 1) * hd, :] for p in range(P)], 1)
    tlane = c0 + lax.broadcasted_iota(jnp.int32, (1, PT), 1) % T
    slope = slope_ref[0]
    t_hi = ((tlane // POS_SPLIT) * POS_SPLIT).astype(F32)
    t_lo = (tlane % POS_SPLIT).astype(F32)
    rowi = lax.broadcasted_iota(jnp.int32, (NSA_KW - hd, PT), 0)
    aug = jnp.where(rowi < 2, slope,
                    jnp.where(rowi == 2, -slope * t_hi,
                              jnp.where(rowi == 3, -slope * t_lo, jnp.where(rowi == NSA_AUG, NEG_BIG, 0.0))))
    return jnp.concatenate([qpart, aug], 0).astype(BF16), tlane


def _compress_kernel(xk_ref, xv_ref, w1k_ref, pek_ref, w1fk_ref, w2k_ref, w1v_ref, pev_ref, w1fv_ref, w2v_ref,
                     kc_ref, vct_ref, *, nc):
    hd = NSA_HEAD_DIM
    pos = lax.broadcasted_iota(jnp.int32, (nc, 1), 0) * CMP_STRIDE + (CMP_LEN - 1)
    aug = _key_aug(pos, NSA_KW - hd)

    def one(x_ref, w1_ref, pe_ref, w1f_ref, w2_ref, g):
        uv = jnp.zeros((nc, 2 * CMP_HIDDEN), F32)
        for r in range(CMP_STRIDE):
            xr = x_ref[0, pl.ds(r, nc, stride=CMP_STRIDE), :][:, g * hd:(g + 1) * hd]
            uv = uv + _dot(xr.astype(BF16), w1_ref[r])
        cvec = _dot_hilo(pe_ref[...], w1f_ref[...])[0:1, :]
        hid = uv[:, 0:CMP_HIDDEN] + pltpu.roll(uv[:, CMP_HIDDEN:], nc - 1, 0) + cvec
        return _dot_hilo(jax.nn.gelu(hid), w2_ref[...])

    for g in range(NSA_GROUPS):
        kc = one(xk_ref, w1k_ref, pek_ref, w1fk_ref, w2k_ref, g)
        kc_ref[0, g] = jnp.concatenate([kc, aug], 1).astype(BF16)
        vct_ref[0, g] = one(xv_ref, w1v_ref, pev_ref, w1fv_ref, w2v_ref, g).T.astype(BF16)


def _compress_call(proj3, pe_k, w1_k, w2_k, pe_v, w1_v, w2_v):
    B, S, _ = proj3.shape
    G, hd = NSA_GROUPS, NSA_HEAD_DIM
    nc = S // CMP_STRIDE
    half = CMP_STRIDE * hd

    def prep(pe, w1):
        w1 = w1.astype(F32)
        wr = jnp.concatenate([w1[:half].reshape(CMP_STRIDE, hd, CMP_HIDDEN),
                              w1[half:].reshape(CMP_STRIDE, hd, CMP_HIDDEN)], -1)
        pe8 = jnp.broadcast_to(pe.reshape(1, CMP_LEN * hd).astype(F32), (SUBLANES, CMP_LEN * hd))
        return wr.astype(BF16), pe8, w1

    full = lambda a: pl.BlockSpec(a.shape, lambda b: tuple(0 for _ in a.shape))
    col = lambda off: pl.BlockSpec((1, S, LANES), lambda b: (b, 0, off // LANES))
    args = prep(pe_k, w1_k) + (w2_k.astype(F32),) + prep(pe_v, w1_v) + (w2_v.astype(F32),)
    return pl.pallas_call(
        functools.partial(_compress_kernel, nc=nc),
        grid=(B,),
        in_specs=[col(OFF_KC), col(OFF_VC)] + [full(a) for a in args],
        out_specs=[pl.BlockSpec((1, G, nc, NSA_KW), lambda b: (b, 0, 0, 0)),
                   pl.BlockSpec((1, G, hd, nc), lambda b: (b, 0, 0, 0))],
        out_shape=[jax.ShapeDtypeStruct((B, G, nc, NSA_KW), BF16),
                   jax.ShapeDtypeStruct((B, G, hd, nc), BF16)],
        compiler_params=_params("parallel"),
        name="nsa_compress",
    )(proj3, proj3, *args)


def _kvprep_kernel(ks_ref, vs_ref, kw_ref, vw_ref, ksa_ref, vst_ref, kwa_ref, vwt_ref, *, tt, n_live):
    hd = NSA_HEAD_DIM
    t = pl.program_id(1)
    dead = t >= n_live
    pos = t * tt + lax.broadcasted_iota(jnp.int32, (tt, 1), 0)
    aug = _key_aug(pos, NSA_KW - hd, dead)
    blk_col = lax.broadcasted_iota(jnp.int32, (tt, SEL_ONEHOT), 1)
    onehot = jnp.where(jnp.logical_and(blk_col == pos // SEL_LEN, jnp.logical_not(dead)), 1.0, 0.0)
    vrow = lax.broadcasted_iota(jnp.int32, (NSA_VROWS - hd, tt), 0)
    ones_rows = jnp.where(jnp.logical_and(vrow == 0, jnp.logical_not(dead)), 1.0, 0.0)
    for g in range(NSA_GROUPS):
        lanes = slice(g * hd, (g + 1) * hd)
        live = lambda ref: jnp.where(dead, 0.0, ref[0][:, lanes])
        ksa_ref[0, g] = jnp.concatenate([live(ks_ref), aug, onehot], 1).astype(BF16)
        kwa_ref[0, g] = jnp.concatenate([live(kw_ref), aug], 1).astype(BF16)
        vs_t = jnp.concatenate([live(vs_ref).T, ones_rows], 0).astype(BF16)
        vw_t = jnp.concatenate([live(vw_ref).T, ones_rows], 0).astype(BF16)
        for c in range(tt // SEL_TILE):
            vst_ref[0, g, c] = vs_t[:, c * SEL_TILE:(c + 1) * SEL_TILE]
        for c in range(tt // WIN_TILE):
            vwt_ref[0, g, c] = vw_t[:, c * WIN_TILE:(c + 1) * WIN_TILE]


def _kvprep_call(proj3):
    B, S, _ = proj3.shape
    G, hd = NSA_GROUPS, NSA_HEAD_DIM
    tt = min(KV_PREP_ROWS, S)
    n_live = S // tt
    assert tt == WINDOW
    sp = S + tt
    col = lambda off: pl.BlockSpec((1, tt, LANES), lambda b, t: (b, jnp.minimum(t, n_live - 1), off // LANES))
    front = lambda t: (t + 1) % (n_live + 1)
    return pl.pallas_call(
        functools.partial(_kvprep_kernel, tt=tt, n_live=n_live),
        grid=(B, n_live + 1),
        in_specs=[col(OFF_KS), col(OFF_VS), col(OFF_KW), col(OFF_VW)],
        out_specs=[pl.BlockSpec((1, G, tt, NSA_KW + SEL_ONEHOT), lambda b, t: (b, 0, t, 0)),
                   pl.BlockSpec((1, G, tt // SEL_TILE, NSA_VROWS, SEL_TILE), lambda b, t: (b, 0, t, 0, 0)),
                   pl.BlockSpec((1, G, tt, NSA_KW), lambda b, t: (b, 0, front(t), 0)),
                   pl.BlockSpec((1, G, tt // WIN_TILE, NSA_VROWS, WIN_TILE), lambda b, t: (b, 0, front(t), 0, 0))],
        out_shape=[jax.ShapeDtypeStruct((B, G, sp, NSA_KW + SEL_ONEHOT), BF16),
                   jax.ShapeDtypeStruct((B, G, sp // SEL_TILE, NSA_VROWS, SEL_TILE), BF16),
                   jax.ShapeDtypeStruct((B, G, sp, NSA_KW), BF16),
                   jax.ShapeDtypeStruct((B, G, sp // WIN_TILE, NSA_VROWS, WIN_TILE), BF16)],
        compiler_params=_params("parallel", "parallel"),
        name="nsa_kvprep",
    )(proj3, proj3, proj3, proj3)


def _importance_matrix(ns, nc):
    ratio = SEL_LEN // CMP_STRIDE
    a = np.zeros((ns, nc), np.float32)
    for j in range(ns):
        for n, wgt in ((ratio * j - 1, 0.5), (ratio * j, 1.0), (ratio * j + 1, 1.0),
                       (ratio * j + 2, 1.0), (ratio * j + 3, 0.5)):
            if 0 <= n < nc - 1:
                a[j, n] = wgt
    return a


def _plan_constants(ns):
    nt = ns // SEL_TILE_BLOCKS
    grp = (np.arange(ns)[None, :] // SEL_TILE_BLOCKS == np.arange(nt)[:, None]).astype(np.float32)
    k = np.arange(nt)
    w_lo = np.where(k < PLAN_HALF, 2.0 ** np.minimum(k, PLAN_HALF - 1), 0.0)
    w_hi = np.where(k >= PLAN_HALF, 2.0 ** np.maximum(k - PLAN_HALF, 0), 0.0)
    wts = np.stack([np.repeat(w_lo[:, None], LANES, 1), np.repeat(w_hi[:, None], LANES, 1)]).astype(np.float32)
    return grp, wts


def _select_kernel(q_ref, slope_ref, kc_ref, vct_ref, imp_ref, grp_ref, wts_ref, ocmp_ref, bias_ref, bits_ref,
                   *, ns, nc, qb):
    P, T = NSA_PER_GROUP, Q_BLOCK
    step = pl.program_id(2)
    nsteps = pl.num_programs(2)
    subs = range(qb)
    c0s = [(step * qb + u) * T for u in subs]
    qts, tlanes = zip(*[_query_t(q_ref, slope_ref, c0s[u], u * T) for u in subs])

    def body(rows, nblk):
        cend = lax.broadcasted_iota(jnp.int32, (rows, 1), 0) * CMP_STRIDE + (CMP_LEN - 1)
        cmask = [cend <= tlanes[u] for u in subs]
        s = [jnp.where(cmask[u], _dot(kc_ref[0, 0, 0:rows, :], qts[u]), NEG_BIG) for u in subs]
        m = [jnp.max(s[u], 0, keepdims=True) for u in subs]
        p_c = [jnp.where(cmask[u], jnp.exp(s[u] - m[u]), 0.0) for u in subs]
        p_c = [p_c[u] * (1.0 / jnp.maximum(jnp.sum(p_c[u], 0, keepdims=True), 1e-30)) for u in subs]
        for u in subs:
            ocmp_ref[0, 0, u] = _dot(vct_ref[0, 0][:, 0:rows], p_c[u].astype(BF16))

        psum = [p_c[u][:, 0:T] for u in subs]
        for p in range(1, P):
            psum = [psum[u] + p_c[u][:, p * T:(p + 1) * T] for u in subs]
        imp = [_dot_exact_lhs(imp_ref[0:nblk, 0:rows], psum[u]) for u in subs]
        blk = lax.broadcasted_iota(jnp.int32, (nblk, T), 0)
        val, chosen = [], []
        for u in subs:
            cur = (c0s[u] + lax.broadcasted_iota(jnp.int32, (nblk, T), 1)) // SEL_LEN
            forced = (blk == 0) | (blk == cur) | (blk == cur - 1)
            val.append(jnp.where(forced, 3e38, jnp.where(blk > cur, -1.0, imp[u])))
            chosen.append(jnp.zeros((nblk, T), F32))
        for _ in range(min(SEL_TOPK, nblk)):
            for u in subs:
                mx = jnp.max(val[u], 0, keepdims=True)
                first = jnp.min(jnp.where(val[u] == mx, blk, ns), 0, keepdims=True)
                pick = blk == first
                chosen[u] = jnp.where(pick, 1.0, chosen[u])
                val[u] = jnp.where(pick, -2.0, val[u])
        for u in subs:
            bias_ref[0, 0, u, 0:nblk, :] = jnp.where(chosen[u] > 0.5, 0.0, NEG_BIG)
            if nblk < ns:
                bias_ref[0, 0, u, nblk:ns, :] = jnp.full((ns - nblk, T), NEG_BIG, F32)
            ch = chosen[u]
            if nblk < ns:
                ch = jnp.concatenate([ch, jnp.zeros((ns - nblk, T), F32)], 0)
            cnt = _dot(grp_ref[...], ch.astype(BF16))
            act = jnp.where(jnp.max(cnt, 1, keepdims=True) > 0.5, 1.0, 0.0)
            lo = jnp.sum(act * wts_ref[0], 0, keepdims=True).astype(jnp.int32)
            hi = jnp.sum(act * wts_ref[1], 0, keepdims=True).astype(jnp.int32)
            bits_ref[u] = jnp.broadcast_to(lo | (hi << PLAN_HALF), (SUBLANES, LANES))

    parts = max(p for p in (SELECT_PARTS, 2, 1) if nc % (p * LANES) == 0 and ns % (p * 2 * SUBLANES) == 0)
    part = step * parts // nsteps
    for v in range(parts):
        pl.when(part == v)(functools.partial(body, nc * (v + 1) // parts, ns * (v + 1) // parts))


def _attend_kernel(bits_sref, q_ref, gt_ref, slope_ref, ocmp_ref, bias_ref, ks_ref, vst_ref, kw_ref, vwt_ref,
                   o_ref, list_ref, s_ref, m_ref, *, n_tiles_total, ns, qb):
    hd, P, T = NSA_HEAD_DIM, NSA_PER_GROUP, Q_BLOCK
    PT = P * T
    b, g, step = pl.program_id(0), pl.program_id(1), pl.program_id(2)
    subs = range(qb)
    iqs = [step * qb + u for u in subs]
    c0s = [iq * T for iq in iqs]
    qts, tlanes = zip(*[_query_t(q_ref, slope_ref, c0s[u], u * T) for u in subs])
    tl_row = lax.broadcasted_iota(jnp.int32, (1, PT), 1) % T

    gts = [jax.nn.sigmoid(gt_ref[0, u * T:(u + 1) * T, :]).T for u in subs]

    def gate_row(u, branch):
        return jnp.concatenate([gts[u][p * 3 + branch:p * 3 + branch + 1, :] for p in range(P)], 1)

    list_len = n_tiles_total + 1
    last_tiles = [(c0s[u] + T - 1) // SEL_TILE for u in subs]
    prev_bits = []
    for u in subs:
        bits = bits_sref[(b * pl.num_programs(1) + g) * (pl.num_programs(2) * qb) + iqs[u]]
        prev_bits.append(bits & (lax.shift_left(jnp.int32(1), last_tiles[u]) - 1))
    n_prevs = [jnp.int32(0) for _ in subs]
    for k in range(n_tiles_total):
        for u in subs:
            list_ref[u * list_len + n_prevs[u]] = k
            n_prevs[u] = n_prevs[u] + (lax.shift_right_logical(prev_bits[u], k) & 1)

    qt_sels = []
    for u in subs:
        sel_b = bias_ref[0, 0, u]
        if ns < SEL_ONEHOT:
            sel_b = jnp.concatenate([sel_b, jnp.zeros((SEL_ONEHOT - ns, T), F32)], 0)
        qt_sels.append(jnp.concatenate([qts[u], jnp.concatenate([sel_b.astype(BF16)] * P, 1)], 0))

    def tile_scores(u, kt):
        k0 = pl.multiple_of(kt * SEL_TILE, SEL_TILE)
        return _dot(ks_ref[0, 0, pl.ds(k0, SEL_TILE), :], qt_sels[u])

    def col_max(scores):
        m = jnp.max(scores[0], 0, keepdims=True)
        for sc in scores[1:]:
            m = jnp.maximum(m, jnp.max(sc, 0, keepdims=True))
        return m

    def softmax_update(carry, scores, vts, m_tile):
        m_run, acc = carry
        m_new = jnp.maximum(m_run, m_tile)
        acc = jnp.exp(m_run - m_new) * acc
        for sc, vt in zip(scores, vts):
            acc = acc + _dot(vt, jnp.exp(sc - m_new).astype(BF16))
        return m_new, acc

    init = (jnp.full((1, PT), NEG_BIG, F32), jnp.zeros((NSA_VROWS, PT), F32))
    dead_tile = n_tiles_total
    n_pairs = [jnp.maximum((n_prevs[u] + 1) // 2, 1) for u in subs]

    def pair_tiles(u, i):
        has_a, has_b = 2 * i < n_prevs[u], 2 * i + 1 < n_prevs[u]
        ka = jnp.where(has_a, list_ref[u * list_len + jnp.where(has_a, 2 * i, 0)], dead_tile)
        kb = jnp.where(has_b, list_ref[u * list_len + jnp.where(has_b, 2 * i + 1, 0)], dead_tile)
        return ka, kb

    def pair_scores(u, i):
        ka, kb = pair_tiles(u, i)
        keys = jnp.concatenate([ks_ref[0, 0, pl.ds(pl.multiple_of(k * SEL_TILE, SEL_TILE), SEL_TILE), :]
                                for k in (ka, kb)], 0)
        return _dot(keys, qt_sels[u])

    def put_scores(u, i, sc):
        slot = i % 2
        s_ref[u, slot] = sc
        m_ref[u, slot] = col_max([sc])

    def pair_probs(u, i, m_run):
        slot = i % 2
        m_new = jnp.maximum(m_run, m_ref[u, slot])
        return m_new, jnp.exp(s_ref[u, slot] - m_new).astype(BF16)

    def add_values(u, i, carry, m_new, pr):
        ka, kb = pair_tiles(u, i)
        m_run, acc = carry
        vt = jnp.concatenate([vst_ref[0, 0, ka], vst_ref[0, 0, kb]], 1)
        return m_new, jnp.exp(m_run - m_new) * acc + _dot(vt, pr)

    first = [pair_scores(u, 0) for u in subs]
    for u in subs:
        put_scores(u, 0, first[u])

    n_win = (WINDOW + T) // WIN_TILE
    roff = lax.broadcasted_iota(jnp.int32, (WIN_TILE, 1), 0)
    w_sc = [_dot(kw_ref[0, 0, pl.ds(pl.multiple_of(c0s[u], WIN_TILE), WINDOW + T), :], qts[u]) for u in subs]
    w_parts = [[jnp.where(roff > tl_row, w_sc[u][0:WIN_TILE], NEG_BIG),
                w_sc[u][WIN_TILE:WINDOW],
                jnp.where(roff <= tl_row, w_sc[u][WINDOW:WINDOW + T], NEG_BIG)]
               for u in subs]
    w_max = [col_max(w_parts[u]) for u in subs]
    out_cw = []
    for u in subs:
        vt = jnp.concatenate([vwt_ref[0, 0, iqs[u] + j] for j in range(n_win)], 1)
        _, acc_win = softmax_update(init, [jnp.concatenate(w_parts[u], 0)], [vt], w_max[u])
        out_cw.append(gate_row(u, 0) * ocmp_ref[0, 0, u] + gate_row(u, 2) * (acc_win[0:hd] / acc_win[hd:hd + 1]))

    def pipe_step(u, i, carry):
        m_new, pr = pair_probs(u, i, carry[0])
        sc_next = pair_scores(u, i + 1)
        carry = add_values(u, i, carry, m_new, pr)
        put_scores(u, i + 1, sc_next)
        return carry

    carries = [lax.fori_loop(0, n_pairs[u] - 1, functools.partial(pipe_step, u), init) for u in subs]
    probs = [pair_probs(u, n_pairs[u] - 1, carries[u][0]) for u in subs]
    s_last = []
    for u in subs:
        kpos = last_tiles[u] * SEL_TILE + lax.broadcasted_iota(jnp.int32, (SEL_TILE, 1), 0)
        s_last.append(jnp.where(kpos <= tlanes[u], tile_scores(u, last_tiles[u]), NEG_BIG))
    carries = [add_values(u, n_pairs[u] - 1, carries[u], *probs[u]) for u in subs]
    for u in subs:
        _, acc_sel = softmax_update(carries[u], [s_last[u]], [vst_ref[0, 0, last_tiles[u]]], col_max([s_last[u]]))
        out = out_cw[u] + gate_row(u, 1) * (acc_sel[0:hd] / acc_sel[hd:hd + 1])
        o_ref[0, u * T:(u + 1) * T, :] = jnp.concatenate(
            [out[:, p * T:(p + 1) * T] for p in range(P)], 0).T.astype(o_ref.dtype)


def _nsa_call(proj3, pe_k, w1_k, w2_k, pe_v, w1_v, w2_v):
    B, S, _ = proj3.shape
    G, hd, P, T = NSA_GROUPS, NSA_HEAD_DIM, NSA_PER_GROUP, Q_BLOCK
    PT = P * T
    nc = S // CMP_STRIDE
    ns = S // SEL_LEN
    nq = S // T
    nt = ns // SEL_TILE_BLOCKS
    assert nt <= 2 * PLAN_HALF and S % KV_PREP_ROWS == 0

    kc_a, vc_t = _compress_call(proj3, pe_k, w1_k, w2_k, pe_v, w1_v, w2_v)
    ks_a, vs_t, kw_a, vw_t = _kvprep_call(proj3)
    imp_m = jnp.asarray(_importance_matrix(ns, nc), BF16)
    grp, wts = _plan_constants(ns)
    head_slopes = 2.0 ** (-(8.0 / NSA_HEADS) * np.arange(1, NSA_HEADS + 1, dtype=np.float32))
    slopes = jnp.asarray(np.repeat(head_slopes.reshape(G, 1, P), T, axis=-1), F32)

    q_spec = lambda im: pl.BlockSpec((1, T, P * hd), im)
    bg = lambda shp: pl.BlockSpec((1, 1) + shp, lambda b, g, i: (b, g) + tuple(0 for _ in shp))
    const = lambda a: pl.BlockSpec(a.shape, lambda b, g, i: tuple(0 for _ in a.shape))
    qb = SELECT_QB if nq % (SELECT_QB * SELECT_PARTS) == 0 else 1
    nsteps = nq // qb
    ocmp, sel_bias, bits = pl.pallas_call(
        functools.partial(_select_kernel, ns=ns, nc=nc, qb=qb),
        grid=(B, G, nsteps),
        in_specs=[pl.BlockSpec((1, qb * T, P * hd), lambda b, g, i: (b, i, OFF_NQ // (P * hd) + g)),
                  pl.BlockSpec((1, 1, PT), lambda b, g, i: (g, 0, 0)),
                  bg((nc, NSA_KW)), bg((hd, nc)), const(imp_m),
                  pl.BlockSpec(grp.shape, lambda b, g, i: (0, 0)),
                  pl.BlockSpec(wts.shape, lambda b, g, i: (0, 0, 0))],
        out_specs=[pl.BlockSpec((1, 1, qb, hd, PT), lambda b, g, i: (b, g, i, 0, 0)),
                   pl.BlockSpec((1, 1, qb, ns, T), lambda b, g, i: (b, g, i, 0, 0)),
                   pl.BlockSpec((qb, SUBLANES, LANES), lambda b, g, i: ((b * G + g) * nsteps + i, 0, 0))],
        out_shape=[jax.ShapeDtypeStruct((B, G, nq, hd, PT), F32),
                   jax.ShapeDtypeStruct((B, G, nq, ns, T), F32),
                   jax.ShapeDtypeStruct((B * G * nq, SUBLANES, LANES), jnp.int32)],
        compiler_params=_params("parallel", "parallel", "parallel"),
        name="nsa_select",
    )(proj3, slopes, kc_a, vc_t, imp_m, jnp.asarray(grp, BF16), jnp.asarray(wts, F32))

    aq = ATTEND_QB if nq % ATTEND_QB == 0 else 1
    bg2 = lambda shp: pl.BlockSpec((1, 1) + shp, lambda b, g, i, s: (b, g) + tuple(0 for _ in shp))
    return pl.pallas_call(
        functools.partial(_attend_kernel, n_tiles_total=nt, ns=ns, qb=aq),
        grid_spec=pltpu.PrefetchScalarGridSpec(
            num_scalar_prefetch=1,
            grid=(B, G, nq // aq),
            in_specs=[pl.BlockSpec((1, aq * T, P * hd), lambda b, g, i, s: (b, i, OFF_NQ // (P * hd) + g)),
                      pl.BlockSpec((1, aq * T, LANES), lambda b, g, i, s: (b, i, OFF_NG // LANES + g)),
                      pl.BlockSpec((1, 1, PT), lambda b, g, i, s: (g, 0, 0)),
                      pl.BlockSpec((1, 1, aq, hd, PT), lambda b, g, i, s: (b, g, i, 0, 0)),
                      pl.BlockSpec((1, 1, aq, ns, T), lambda b, g, i, s: (b, g, i, 0, 0)),
                      bg2(ks_a.shape[2:]), bg2(vs_t.shape[2:]), bg2(kw_a.shape[2:]), bg2(vw_t.shape[2:])],
            out_specs=pl.BlockSpec((1, aq * T, P * hd), lambda b, g, i, s: (b, i, g)),
            scratch_shapes=[pltpu.SMEM((aq * (nt + 1),), jnp.int32),
                            pltpu.VMEM((aq, 2, 2 * SEL_TILE, PT), F32), pltpu.VMEM((aq, 2, 1, PT), F32)]),
        out_shape=jax.ShapeDtypeStruct((B, S, NSA_WIDTH), BF16),
        compiler_params=_params("parallel", "parallel", "arbitrary"),
        name="nsa_attend",
    )(bits[:, 0, 0], proj3, proj3, slopes, ocmp, sel_bias, ks_a, vs_t, kw_a, vw_t)


def _merge_kernel(x_ref, ya_ref, yb_ref, yc_ref, g0_ref, g1_ref, g2_ref, wa_ref, wb_ref, wc_ref, wo_ref,
                  lg_ref, lb_ref, o_ref):
    gate = lambda r: jax.nn.sigmoid(r[...].astype(F32))
    m = gate(g0_ref) * _dot(ya_ref[...], wa_ref[...])
    m = m + gate(g1_ref) * _dot(yb_ref[...], wb_ref[...])
    m = m + gate(g2_ref) * _dot(yc_ref[...], wc_ref[...])
    z = DEEPNORM_ALPHA * x_ref[...] + _dot(m.astype(BF16), wo_ref[...])
    o_ref[...] = _layer_norm_rows(z, lg_ref[...], lb_ref[...])


def _merge_call(x2, ya, yb, yc, gates, w_hg, w_nsa, w_lru, w_out, ln_g, ln_b, tm=512):
    T, D = x2.shape
    tm = min(tm, T)
    rows = lambda w: pl.BlockSpec((tm, w), lambda i: (i, 0))
    gate = lambda n: pl.BlockSpec((tm, D), lambda i: (i, n))
    full = lambda a: pl.BlockSpec(a.shape, lambda i: (0, 0))
    ws = [w.astype(BF16) for w in (w_hg, w_nsa, w_lru, w_out)]
    return pl.pallas_call(
        _merge_kernel,
        grid=(T // tm,),
        in_specs=[rows(D), rows(HG_WIDTH), rows(NSA_WIDTH), rows(LRU_WIDTH), gate(0), gate(1), gate(2)]
                 + [full(w) for w in ws] + [pl.BlockSpec((1, D), lambda i: (0, 0))] * 2,
        out_specs=rows(D),
        out_shape=jax.ShapeDtypeStruct((T, D), F32),
        compiler_params=_params("parallel"),
        name="merge_out",
    )(x2, ya, yb, yc, gates, gates, gates, *ws, ln_g.reshape(1, D), ln_b.reshape(1, D))


FFN_SPLIT = 1
FFN_PAD = SUBLANES
FFN_SUB = 256


def _ffn_kernel(x_ref, wu_ref, wv_ref, cw_ref, cb_ref, wd_ref, lg_ref, lb_ref, o_ref,
                xb_ref, acc_ref, upad_ref, carry_ref, *, tm, tiles_per_seq):
    i = pl.program_id(0)
    j = pl.program_id(1)

    @pl.when(j == 0)
    def _():
        xb_ref[...] = x_ref[...].astype(BF16)

    xb = xb_ref[...]
    fc = wu_ref.shape[1]
    first = (i % tiles_per_seq) == 0

    @pl.when(first)
    def _():
        upad_ref[0:FFN_PAD, :] = jnp.zeros((FFN_PAD, fc), F32)

    @pl.when(jnp.logical_not(first))
    def _():
        upad_ref[0:FFN_PAD, :] = carry_ref[j]

    @pl.when(j == 0)
    def _():
        acc_ref[...] = jnp.zeros_like(acc_ref)

    for c0 in range(0, fc, FFN_SUB):
        cs = slice(c0, min(c0 + FFN_SUB, fc))
        u = _dot(xb, wu_ref[:, cs])
        v = _dot(xb, wv_ref[:, cs])
        upad_ref[FFN_PAD:FFN_PAD + tm, cs] = u
        cw = cw_ref[:, cs]
        cv = cb_ref[:, cs] + cw[FFN_CONV - 1:FFN_CONV, :] * u
        for k in range(FFN_CONV - 1):
            off = FFN_PAD - (FFN_CONV - 1) + k
            cv = cv + cw[k:k + 1, :] * upad_ref[off:off + tm, cs]
        carry_ref[j, :, cs] = u[tm - FFN_PAD:tm, :]
        h = (jax.nn.gelu(cv) * v).astype(BF16)
        acc_ref[...] += _dot(h, wd_ref[cs, :])

    @pl.when(j == FFN_SPLIT - 1)
    def _():
        z = DEEPNORM_ALPHA * x_ref[...] + acc_ref[...]
        o_ref[...] = _layer_norm_rows(z, lg_ref[...], lb_ref[...])


def _ffn_call(x2, seq_len, w_up, conv_w, conv_b, w_down, ln_g, ln_b, tm=512):
    T, D = x2.shape
    tm = min(tm, seq_len)
    fc = FFN_DIM // FFN_SPLIT
    wu = w_up.astype(BF16)
    wmode = pl.Buffered(1) if FFN_SPLIT == 1 else None
    return pl.pallas_call(
        functools.partial(_ffn_kernel, tm=tm, tiles_per_seq=seq_len // tm),
        grid=(T // tm, FFN_SPLIT),
        in_specs=[pl.BlockSpec((tm, D), lambda i, j: (i, 0)),
                  pl.BlockSpec((D, fc), lambda i, j: (0, j), pipeline_mode=wmode),
                  pl.BlockSpec((D, fc), lambda i, j: (0, FFN_SPLIT + j), pipeline_mode=wmode),
                  pl.BlockSpec((FFN_CONV, fc), lambda i, j: (0, j)),
                  pl.BlockSpec((1, fc), lambda i, j: (0, j)),
                  pl.BlockSpec((fc, D), lambda i, j: (j, 0), pipeline_mode=wmode),
                  pl.BlockSpec((1, D), lambda i, j: (0, 0)),
                  pl.BlockSpec((1, D), lambda i, j: (0, 0))],
        out_specs=pl.BlockSpec((tm, D), lambda i, j: (i, 0)),
        out_shape=jax.ShapeDtypeStruct((T, D), F32),
        scratch_shapes=[pltpu.VMEM((tm, D), BF16), pltpu.VMEM((tm, D), F32),
                        pltpu.VMEM((tm + FFN_PAD, fc), F32), pltpu.VMEM((FFN_SPLIT, FFN_PAD, fc), F32)],
        compiler_params=_params("arbitrary", "arbitrary"),
        name="conv_ffn",
    )(x2, wu, wu, conv_w.astype(F32), conv_b.reshape(1, FFN_DIM).astype(F32), w_down.astype(BF16),
      ln_g.reshape(1, D), ln_b.reshape(1, D))


def _permute_in_proj(w, b):
    src = np.cumsum([0, HG_WIDTH, HG_WIDTH, HG_WIDTH, HG_WIDTH, NSA_WIDTH, NSA_KV, NSA_KV, NSA_KV, NSA_KV,
                     NSA_KV, NSA_KV, NSA_HEADS * 3, LRU_WIDTH, LRU_WIDTH, N_BRANCH * 1024])
    names = ["hq", "hf", "hi", "hg", "nq", "kc", "vc", "ks", "vs", "kw", "vw", "ng", "lx", "ly", "mg"]
    seg = {n: (int(src[k]), int(src[k + 1])) for k, n in enumerate(names)}
    order = ["mg", "hq", "hf", "hi", "hg", "nq", "lx", "ly", "kc", "vc", "ks", "vs", "kw", "vw"]
    per_group = NSA_PER_GROUP * 3
    ng0 = seg["ng"][0]
    w = w.astype(BF16)
    w_parts = [w[:, seg[n][0]:seg[n][1]] for n in order]
    b_parts = [b[seg[n][0]:seg[n][1]] for n in order]
    for gi in range(NSA_GROUPS):
        lo = ng0 + gi * per_group
        w_parts += [w[:, lo:lo + per_group], jnp.zeros((w.shape[0], LANES - per_group), w.dtype)]
        b_parts += [b[lo:lo + per_group], jnp.zeros((LANES - per_group,), b.dtype)]
    wp = jnp.concatenate(w_parts, 1)
    bp = jnp.concatenate(b_parts)
    assert wp.shape[1] == GATE_COLS + PROJ_COLS
    return wp.astype(BF16), bp.astype(F32)


def kernel(x, ln_emb_g, ln_emb_b, w_in, b_in, hg_lb_logits, hg_norm_g, cmp_pe_k, cmp_w1_k, cmp_w2_k, cmp_pe_v, cmp_w1_v, cmp_w2_v, lru_conv_w, lru_conv_b, lru_wa, lru_ba, lru_wx, lru_bx, lru_lambda, w_branch_hg, w_branch_nsa, w_branch_lru, w_out, ln1_g, ln1_b, ffn_w_up, ffn_conv_w, ffn_conv_b, ffn_w_down, ln2_g, ln2_b):
    B, S, D = x.shape
    T = B * S
    gam = jax.nn.softmax(hg_lb_logits.astype(F32), axis=0)
    lb_all = jnp.cumsum(gam, axis=0) - gam[0]
    h = x.reshape(T, D)
    for l in range(DEPTH):
        wp, bp = _permute_in_proj(w_in[l], b_in[l])
        if l == 0:
            gates, proj2, h = _inproj_call(h, wp, bp, ln=(ln_emb_g, ln_emb_b))
        else:
            gates, proj2 = _inproj_call(h, wp, bp)
        proj3 = proj2.reshape(B, S, PROJ_COLS)
        y_a = _hgrn_call(proj3, lb_all[l], hg_norm_g[l])
        y_b = _nsa_call(proj3, cmp_pe_k[l], cmp_w1_k[l], cmp_w2_k[l], cmp_pe_v[l], cmp_w1_v[l], cmp_w2_v[l])
        y_c = _lru_call(proj3, lru_conv_w[l], lru_conv_b[l], lru_wa[l], lru_ba[l], lru_wx[l], lru_bx[l],
                        lru_lambda[l])
        h = _merge_call(h, y_a.reshape(T, HG_WIDTH), y_b.reshape(T, NSA_WIDTH), y_c.reshape(T, LRU_WIDTH),
                        gates, w_branch_hg[l], w_branch_nsa[l], w_branch_lru[l], w_out[l], ln1_g[l], ln1_b[l])
        h = _ffn_call(h, S, ffn_w_up[l], ffn_conv_w[l], ffn_conv_b[l], ffn_w_down[l], ln2_g[l], ln2_b[l])
    return h.reshape(B, S, D)
```

```python
import functools

import numpy as np
import jax
import jax.numpy as jnp
from jax import lax
from jax.experimental import pallas as pl
from jax.experimental.pallas import tpu as pltpu

F32 = jnp.float32
BF16 = jnp.bfloat16

DEPTH = 2
HG_HEADS = 4
HG_HEAD_DIM = 128
HG_WIDTH = HG_HEADS * HG_HEAD_DIM
NSA_HEADS = 8
NSA_GROUPS = 2
NSA_PER_GROUP = NSA_HEADS // NSA_GROUPS
NSA_HEAD_DIM = 64
NSA_WIDTH = NSA_HEADS * NSA_HEAD_DIM
NSA_KV = NSA_GROUPS * NSA_HEAD_DIM
CMP_LEN = 32
CMP_STRIDE = 16
CMP_HIDDEN = 64
SEL_LEN = 64
SEL_TOPK = 16
WINDOW = 512
Q_BLOCK = 128
LRU_WIDTH = 512
LRU_BLOCKS = 4
LRU_CONV = 4
LRU_C = 8.0
FFN_DIM = 2816
FFN_CONV = 3
DEEPNORM_ALPHA = (2 * DEPTH) ** 0.25
LN_EPS = 1e-5
N_BRANCH = 3

LANES = 128
SUBLANES = 8
VMEM_LIMIT_BYTES = 52 * 1024 * 1024

GATE_COLS = N_BRANCH * 1024
OFF_HQ = 0
OFF_HF = OFF_HQ + HG_WIDTH
OFF_HI = OFF_HF + HG_WIDTH
OFF_HG = OFF_HI + HG_WIDTH
OFF_NQ = OFF_HG + HG_WIDTH
OFF_LX = OFF_NQ + NSA_WIDTH
OFF_LY = OFF_LX + LRU_WIDTH
OFF_KC = OFF_LY + LRU_WIDTH
OFF_VC = OFF_KC + NSA_KV
OFF_KS = OFF_VC + NSA_KV
OFF_VS = OFF_KS + NSA_KV
OFF_KW = OFF_VS + NSA_KV
OFF_VW = OFF_KW + NSA_KV
OFF_NG = OFF_VW + NSA_KV
PROJ_TILE_N = 1536
PROJ_COLS = OFF_NG + NSA_GROUPS * LANES

NEG_BIG = -1e30


def _split3(x):
    hi = x.astype(BF16)
    r1 = x - hi.astype(F32)
    mid = r1.astype(BF16)
    lo = (r1 - mid.astype(F32)).astype(BF16)
    return hi, mid, lo


def _dot(a, b):
    return jnp.dot(a, b, preferred_element_type=F32)


def _dot_nt(a, b):
    return lax.dot_general(a, b, (((1,), (1,)), ((), ())), preferred_element_type=F32)


def _dot_exact_lhs(m_bf16, x):
    hi, mid, lo = _split3(x)
    return _dot(m_bf16, hi) + _dot(m_bf16, mid) + _dot(m_bf16, lo)


def _dot_hilo(a, b):
    ah = a.astype(BF16)
    al = (a - ah.astype(F32)).astype(BF16)
    bh = b.astype(BF16)
    bl = (b - bh.astype(F32)).astype(BF16)
    return _dot(ah, bh) + _dot(ah, bl) + _dot(al, bh) + _dot(al, bl)


def _layer_norm_rows(z, g, b):
    mu = jnp.mean(z, -1, keepdims=True)
    zc = z - mu
    var = jnp.mean(zc * zc, -1, keepdims=True)
    return zc * lax.rsqrt(var + LN_EPS) * g + b


def _params(*sem):
    return pltpu.CompilerParams(dimension_semantics=sem, vmem_limit_bytes=VMEM_LIMIT_BYTES)


def _inproj_kernel(*refs, with_ln):
    if with_ln:
        x_ref, g_ref, be_ref, w_ref, b_ref, gate_ref, o_ref, xn_ref = refs
    else:
        x_ref, w_ref, b_ref, gate_ref, o_ref = refs
    x = x_ref[...]
    if with_ln:
        x = _layer_norm_rows(x, g_ref[...], be_ref[...])
        xn_ref[...] = x
    xb = x.astype(BF16)
    n = w_ref.shape[1]
    for c0 in range(0, n, PROJ_TILE_N):
        cs = slice(c0, c0 + PROJ_TILE_N)
        acc = _dot(xb, w_ref[:, cs]) + b_ref[:, cs]
        if c0 < GATE_COLS:
            gate_ref[:, cs] = acc.astype(BF16)
        else:
            o_ref[:, c0 - GATE_COLS:c0 - GATE_COLS + PROJ_TILE_N] = acc


def _inproj_call(x2, w_bf16, bias, ln=None, tm=512):
    T, D = x2.shape
    N = w_bf16.shape[1]
    tm = min(tm, T)
    assert GATE_COLS % PROJ_TILE_N == 0 and N % PROJ_TILE_N == 0
    vec = pl.BlockSpec((1, D), lambda i: (0, 0))
    in_specs = [pl.BlockSpec((tm, D), lambda i: (i, 0))] + ([vec, vec] if ln else []) + [
        pl.BlockSpec((D, N), lambda i: (0, 0), pipeline_mode=pl.Buffered(1)),
        pl.BlockSpec((1, N), lambda i: (0, 0))]
    out_specs = [pl.BlockSpec((tm, GATE_COLS), lambda i: (i, 0)),
                 pl.BlockSpec((tm, N - GATE_COLS), lambda i: (i, 0))]
    out_shape = [jax.ShapeDtypeStruct((T, GATE_COLS), BF16), jax.ShapeDtypeStruct((T, N - GATE_COLS), F32)]
    if ln:
        out_specs.append(pl.BlockSpec((tm, D), lambda i: (i, 0)))
        out_shape.append(jax.ShapeDtypeStruct((T, D), F32))
    ln_args = [ln[0].reshape(1, D), ln[1].reshape(1, D)] if ln else []
    return pl.pallas_call(
        functools.partial(_inproj_kernel, with_ln=bool(ln)),
        grid=(T // tm,),
        in_specs=in_specs, out_specs=out_specs, out_shape=out_shape,
        compiler_params=_params("parallel"),
        name="in_proj_ln" if ln else "in_proj",
    )(x2, *ln_args, w_bf16, bias.reshape(1, N))


HG_CHUNK = 128
HG_DIAG = SUBLANES
HG_HEADS_PER_STEP = 4


def _hgrn_constants(C):
    r = np.arange(C)
    tri = (r[None, :] <= r[:, None]).astype(np.float32)
    mats = [tri]
    masks = []
    w = HG_DIAG
    while 2 * w <= C:
        mid = (r // (2 * w)) * (2 * w) + w
        mats.append(tri - (r[None, :] <= mid[:, None]).astype(np.float32))
        same = (r[:, None] // (2 * w)) == (r[None, :] // (2 * w))
        masks.append((same & ((r[:, None] % (2 * w)) >= w) & ((r[None, :] % (2 * w)) < w)).astype(np.float32))
        w *= 2
    mats.append((r[None, :] > r[:, None]).astype(np.float32))
    return np.concatenate(mats, 0), np.stack(masks, 0)


def _hgrn_kernel(q_ref, f_ref, i_ref, g_ref, lb_ref, ng_ref, m_ref, mask_ref, o_ref, st_ref, *, C, nchunk, nh):
    d = HG_HEAD_DIM
    nlev = mask_ref.shape[0]

    @pl.when(pl.program_id(2) == 0)
    def _():
        st_ref[...] = jnp.zeros_like(st_ref)

    hs = range(nh)
    lane = [slice(h * d, (h + 1) * d) for h in hs]
    lb = [lb_ref[h] for h in hs]
    ng = [ng_ref[h] for h in hs]
    nb = C // HG_DIAG
    row3 = lax.broadcasted_iota(jnp.int32, (nb, HG_DIAG, d), 1)

    fl_all = f_ref[0]
    lf_all = [jnp.log(lb[h] + (1.0 - lb[h]) * jax.nn.sigmoid(fl_all[:, lane[h]])) for h in hs]
    kk_all = [(1.0 - lb[h]) * jax.nn.sigmoid(-fl_all[:, lane[h]]) for h in hs]
    allm_all = _dot_exact_lhs(m_ref[...], jnp.concatenate(
        [lf_all[h][c * C:(c + 1) * C] for h in hs for c in range(nchunk)], 1))

    for c in range(nchunk):
        sl = pl.ds(c * C, C)
        q = [q_ref[0, sl, lane[h]] for h in hs]
        iv = [i_ref[0, sl, lane[h]] for h in hs]
        kk = [kk_all[h][c * C:(c + 1) * C] for h in hs]
        allm = [allm_all[:, (h * nchunk + c) * d:(h * nchunk + c + 1) * d] for h in hs]
        bcs = [allm[h][0:C] for h in hs]
        rem = [allm[h][(nlev + 1) * C:(nlev + 2) * C] for h in hs]
        iv_b = [iv[h].astype(BF16) for h in hs]

        att = [jnp.zeros((C, C), F32) for _ in hs]
        for l in range(nlev):
            e = [jnp.exp(-jnp.abs(allm[h][(1 + l) * C:(2 + l) * C])) for h in hs]
            att = [att[h] + mask_ref[l] * _dot_nt((q[h] * e[h]).astype(BF16), (kk[h] * e[h]).astype(BF16))
                   for h in hs]
        o = [_dot(att[h].astype(BF16), iv_b[h]) for h in hs]

        q3 = [q[h].reshape(nb, HG_DIAG, d) for h in hs]
        k3 = [kk[h].reshape(nb, HG_DIAG, d) for h in hs]
        b3 = [bcs[h].reshape(nb, HG_DIAG, d) for h in hs]
        i3 = [iv[h].reshape(nb, HG_DIAG, d) for h in hs]
        acc = [jnp.zeros((nb, HG_DIAG, d), F32) for _ in hs]
        for s in range(HG_DIAG):
            for h in hs:
                dec = jnp.where(row3 >= s, jnp.exp(b3[h] - b3[h][:, s:s + 1, :]), 0.0)
                a = jnp.sum(q3[h] * dec * k3[h][:, s:s + 1, :], axis=-1, keepdims=True)
                acc[h] = acc[h] + a * i3[h][:, s:s + 1, :]

        for h in hs:
            st = st_ref[h]
            oh = o[h] + acc[h].reshape(C, d) + _dot_nt((q[h] * jnp.exp(bcs[h])).astype(BF16), st.astype(BF16))
            kdec = (kk[h] * jnp.exp(rem[h])).astype(BF16)
            st_ref[h] = st * jnp.exp(bcs[h][C - 1:C, :]) + _dot(iv[h].T.astype(BF16), kdec)
            oh = oh * lax.rsqrt(jnp.mean(oh * oh, -1, keepdims=True) + 1e-6) * ng[h]
            g = g_ref[0, sl, lane[h]]
            o_ref[0, sl, lane[h]] = (oh * (g * jax.nn.sigmoid(g))).astype(o_ref.dtype)


def _hgrn_call(proj3, lb, norm_g, tt=512):
    B, S, _ = proj3.shape
    d = HG_HEAD_DIM
    C = HG_CHUNK
    tt = min(tt, S)
    mstack, masks = _hgrn_constants(C)
    nm = mstack.shape[0]
    nh = HG_HEADS_PER_STEP
    wid = nh * d
    col = lambda off: (lambda b, h, t: (b, t, off // wid + h))
    blk = (1, tt, wid)
    return pl.pallas_call(
        functools.partial(_hgrn_kernel, C=C, nchunk=tt // C, nh=nh),
        grid=(B, HG_HEADS // nh, S // tt),
        in_specs=[pl.BlockSpec(blk, col(OFF_HQ)), pl.BlockSpec(blk, col(OFF_HF)),
                  pl.BlockSpec(blk, col(OFF_HI)), pl.BlockSpec(blk, col(OFF_HG)),
                  pl.BlockSpec((nh, 1, d), lambda b, h, t: (h, 0, 0)),
                  pl.BlockSpec((nh, 1, d), lambda b, h, t: (h, 0, 0)),
                  pl.BlockSpec((nm, C), lambda b, h, t: (0, 0)),
                  pl.BlockSpec(masks.shape, lambda b, h, t: (0, 0, 0))],
        out_specs=pl.BlockSpec(blk, lambda b, h, t: (b, t, h)),
        out_shape=jax.ShapeDtypeStruct((B, S, HG_WIDTH), BF16),
        scratch_shapes=[pltpu.VMEM((nh, d, d), F32)],
        compiler_params=_params("parallel", "parallel", "arbitrary"),
        name="hgrn2",
    )(proj3, proj3, proj3, proj3, lb.reshape(HG_HEADS, 1, d), norm_g.reshape(HG_HEADS, 1, d),
      jnp.asarray(mstack, BF16), jnp.asarray(masks, F32))


LRU_PAD = SUBLANES
LRU_STEP = 16


def _lru_kernel(x_ref, y_ref, cw_ref, cb_ref, wa_ref, ba_ref, wx_ref, bx_ref, c_ref, o_ref,
                xpad_ref, h_ref, a_ref, u_ref, *, tt):
    W = LRU_WIDTH
    bw = W // LRU_BLOCKS

    @pl.when(pl.program_id(1) == 0)
    def _():
        xpad_ref[0:LRU_PAD, :] = jnp.zeros((LRU_PAD, W), F32)
        h_ref[...] = jnp.zeros_like(h_ref)

    x = x_ref[0]
    xpad_ref[LRU_PAD:LRU_PAD + tt, :] = x
    cw = cw_ref[...]
    xc = cb_ref[...] + cw[LRU_CONV - 1:LRU_CONV, :] * x
    for j in range(LRU_CONV - 1):
        off = LRU_PAD - (LRU_CONV - 1) + j
        xc = xc + cw[j:j + 1, :] * xpad_ref[off:off + tt, :]
    xpad_ref[0:LRU_PAD, :] = x[tt - LRU_PAD:tt, :]

    rs, is_ = [], []
    for gi in range(LRU_BLOCKS):
        xg = xc[:, gi * bw:(gi + 1) * bw].astype(BF16)
        rs.append(_dot(xg, wa_ref[gi]))
        is_.append(_dot(xg, wx_ref[gi]))
    r = jax.nn.sigmoid(jnp.concatenate(rs, -1) + ba_ref[...])
    ig = jax.nn.sigmoid(jnp.concatenate(is_, -1) + bx_ref[...])
    log_a = c_ref[...] * r
    a = jnp.exp(log_a)
    u = jnp.sqrt(-jnp.tanh(log_a) * (a * a + 1.0)) * (ig * xc)

    sub = lax.broadcasted_iota(jnp.int32, (tt, W), 0) % SUBLANES
    for dlt in (1, 2, 4):
        keep = sub >= dlt
        a_sh = jnp.where(keep, pltpu.roll(a, dlt, 0), 1.0)
        u_sh = jnp.where(keep, pltpu.roll(u, dlt, 0), 0.0)
        u = a * u_sh + u
        a = a * a_sh
    a_ref[...] = a
    u_ref[...] = u

    def step(i, h):
        r0 = pl.multiple_of(i * LRU_STEP, LRU_STEP)
        outs = []
        for k in range(LRU_STEP // SUBLANES):
            rows = pl.ds(r0 + k * SUBLANES, SUBLANES)
            hh = u_ref[rows, :] + a_ref[rows, :] * h
            outs.append(hh)
            h = hh[SUBLANES - 1:SUBLANES, :]
        rows = pl.ds(r0, LRU_STEP)
        o_ref[0, rows, :] = (jnp.concatenate(outs, 0) * jax.nn.gelu(y_ref[0, rows, :])).astype(o_ref.dtype)
        return h

    h_ref[...] = lax.fori_loop(0, tt // LRU_STEP, step, h_ref[...])


def _lru_call(proj3, conv_w, conv_b, wa, ba, wx, bx, lam, tt=512):
    B, S, _ = proj3.shape
    W = LRU_WIDTH
    bw = W // LRU_BLOCKS
    tt = min(tt, S)
    c = (-LRU_C * jax.nn.softplus(-lam.astype(F32))).reshape(1, W)
    vec = lambda: pl.BlockSpec((1, W), lambda b, t: (0, 0))
    return pl.pallas_call(
        functools.partial(_lru_kernel, tt=tt),
        grid=(B, S // tt),
        in_specs=[pl.BlockSpec((1, tt, W), lambda b, t: (b, t, OFF_LX // W)),
                  pl.BlockSpec((1, tt, W), lambda b, t: (b, t, OFF_LY // W)),
                  pl.BlockSpec((LRU_CONV, W), lambda b, t: (0, 0)), vec(),
                  pl.BlockSpec((LRU_BLOCKS, bw, bw), lambda b, t: (0, 0, 0)), vec(),
                  pl.BlockSpec((LRU_BLOCKS, bw, bw), lambda b, t: (0, 0, 0)), vec(), vec()],
        out_specs=pl.BlockSpec((1, tt, W), lambda b, t: (b, t, 0)),
        out_shape=jax.ShapeDtypeStruct((B, S, W), BF16),
        scratch_shapes=[pltpu.VMEM((tt + LRU_PAD, W), F32), pltpu.VMEM((1, W), F32),
                        pltpu.VMEM((tt, W), F32), pltpu.VMEM((tt, W), F32)],
        compiler_params=_params("parallel", "arbitrary"),
        name="rglru",
    )(proj3, proj3, conv_w.astype(F32), conv_b.reshape(1, W).astype(F32), wa.astype(BF16),
      ba.reshape(1, W), wx.astype(BF16), bx.reshape(1, W), c)


NSA_AUG = 4
NSA_KW = LANES
SEL_TILE = 256
SEL_TILE_BLOCKS = SEL_TILE // SEL_LEN
WIN_TILE = 128
POS_SPLIT = 128
SEL_ONEHOT = LANES
PLAN_HALF = 16
SELECT_PARTS = 4
SELECT_QB = 8
ATTEND_QB = 4
NSA_VROWS = NSA_HEAD_DIM + 16
KV_PREP_ROWS = 512


def _key_aug(pos, width, dead=False):
    n = pos.shape[0]
    col = lax.broadcasted_iota(jnp.int32, (n, width), 1)
    hi = ((pos // POS_SPLIT) * POS_SPLIT).astype(F32)
    lo = (pos % POS_SPLIT).astype(F32)
    live = jnp.where(col == 0, hi, jnp.where(col == 1, lo, jnp.where(col < NSA_AUG, 1.0, 0.0)))
    return jnp.where(dead, jnp.where(col == NSA_AUG, 1.0, 0.0), live)


def _query_t(q_ref, slope_ref, c0, row0=0):
    hd, P, T = NSA_HEAD_DIM, NSA_PER_GROUP, Q_BLOCK
    PT = P * T
    qT = (q_ref[0, row0:row0 + T, :] * (hd ** -0.5)).T
    qpart = jnp.concatenate([qT[p * hd:(p + 1) * hd, :] for p in range(P)], 1)
    tlane = c0 + lax.broadcasted_iota(jnp.int32, (1, PT), 1) % T
    slope = slope_ref[0]
    t_hi = ((tlane // POS_SPLIT) * POS_SPLIT).astype(F32)
    t_lo = (tlane % POS_SPLIT).astype(F32)
    rowi = lax.broadcasted_iota(jnp.int32, (NSA_KW - hd, PT), 0)
    aug = jnp.where(rowi < 2, slope,
                    jnp.where(rowi == 2, -slope * t_hi,
                              jnp.where(rowi == 3, -slope * t_lo, jnp.where(rowi == NSA_AUG, NEG_BIG, 0.0))))
    return jnp.concatenate([qpart, aug], 0).astype(BF16), tlane


def _compress_kernel(xk_ref, xv_ref, w1k_ref, pek_ref, w1fk_ref, w2k_ref, w1v_ref, pev_ref, w1fv_ref, w2v_ref,
                     kc_ref, vct_ref, *, nc):
    hd = NSA_HEAD_DIM
    pos = lax.broadcasted_iota(jnp.int32, (nc, 1), 0) * CMP_STRIDE + (CMP_LEN - 1)
    aug = _key_aug(pos, NSA_KW - hd)

    def one(x_ref, w1_ref, pe_ref, w1f_ref, w2_ref, g):
        uv = jnp.zeros((nc, 2 * CMP_HIDDEN), F32)
        for r in range(CMP_STRIDE):
            xr = x_ref[0, pl.ds(r, nc, stride=CMP_STRIDE), :][:, g * hd:(g + 1) * hd]
            uv = uv + _dot(xr.astype(BF16), w1_ref[r])
        cvec = _dot_hilo(pe_ref[...], w1f_ref[...])[0:1, :]
        hid = uv[:, 0:CMP_HIDDEN] + pltpu.roll(uv[:, CMP_HIDDEN:], nc - 1, 0) + cvec
        return _dot_hilo(jax.nn.gelu(hid), w2_ref[...])

    for g in range(NSA_GROUPS):
        kc = one(xk_ref, w1k_ref, pek_ref, w1fk_ref, w2k_ref, g)
        kc_ref[0, g] = jnp.concatenate([kc, aug], 1).astype(BF16)
        vct_ref[0, g] = one(xv_ref, w1v_ref, pev_ref, w1fv_ref, w2v_ref, g).T.astype(BF16)


def _compress_call(proj3, pe_k, w1_k, w2_k, pe_v, w1_v, w2_v):
    B, S, _ = proj3.shape
    G, hd = NSA_GROUPS, NSA_HEAD_DIM
    nc = S // CMP_STRIDE
    half = CMP_STRIDE * hd

    def prep(pe, w1):
        w1 = w1.astype(F32)
        wr = jnp.concatenate([w1[:half].reshape(CMP_STRIDE, hd, CMP_HIDDEN),
                              w1[half:].reshape(CMP_STRIDE, hd, CMP_HIDDEN)], -1)
        pe8 = jnp.broadcast_to(pe.reshape(1, CMP_LEN * hd).astype(F32), (SUBLANES, CMP_LEN * hd))
        return wr.astype(BF16), pe8, w1

    full = lambda a: pl.BlockSpec(a.shape, lambda b: tuple(0 for _ in a.shape))
    col = lambda off: pl.BlockSpec((1, S, LANES), lambda b: (b, 0, off // LANES))
    args = prep(pe_k, w1_k) + (w2_k.astype(F32),) + prep(pe_v, w1_v) + (w2_v.astype(F32),)
    return pl.pallas_call(
        functools.partial(_compress_kernel, nc=nc),
        grid=(B,),
        in_specs=[col(OFF_KC), col(OFF_VC)] + [full(a) for a in args],
        out_specs=[pl.BlockSpec((1, G, nc, NSA_KW), lambda b: (b, 0, 0, 0)),
                   pl.BlockSpec((1, G, hd, nc), lambda b: (b, 0, 0, 0))],
        out_shape=[jax.ShapeDtypeStruct((B, G, nc, NSA_KW), BF16),
                   jax.ShapeDtypeStruct((B, G, hd, nc), BF16)],
        compiler_params=_params("parallel"),
        name="nsa_compress",
    )(proj3, proj3, *args)


def _kvprep_kernel(ks_ref, vs_ref, kw_ref, vw_ref, ksa_ref, vst_ref, kwa_ref, vwt_ref, *, tt, n_live):
    hd = NSA_HEAD_DIM
    t = pl.program_id(1)
    dead = t >= n_live
    pos = t * tt + lax.broadcasted_iota(jnp.int32, (tt, 1), 0)
    aug = _key_aug(pos, NSA_KW - hd, dead)
    blk_col = lax.broadcasted_iota(jnp.int32, (tt, SEL_ONEHOT), 1)
    onehot = jnp.where(jnp.logical_and(blk_col == pos // SEL_LEN, jnp.logical_not(dead)), 1.0, 0.0)
    vrow = lax.broadcasted_iota(jnp.int32, (NSA_VROWS - hd, tt), 0)
    ones_rows = jnp.where(jnp.logical_and(vrow == 0, jnp.logical_not(dead)), 1.0, 0.0)
    for g in range(NSA_GROUPS):
        lanes = slice(g * hd, (g + 1) * hd)
        live = lambda ref: jnp.where(dead, 0.0, ref[0][:, lanes])
        ksa_ref[0, g] = jnp.concatenate([live(ks_ref), aug, onehot], 1).astype(BF16)
        kwa_ref[0, g] = jnp.concatenate([live(kw_ref), aug], 1).astype(BF16)
        vs_t = jnp.concatenate([live(vs_ref).T, ones_rows], 0).astype(BF16)
        vw_t = jnp.concatenate([live(vw_ref).T, ones_rows], 0).astype(BF16)
        for c in range(tt // SEL_TILE):
            vst_ref[0, g, c] = vs_t[:, c * SEL_TILE:(c + 1) * SEL_TILE]
        for c in range(tt // WIN_TILE):
            vwt_ref[0, g, c] = vw_t[:, c * WIN_TILE:(c + 1) * WIN_TILE]


def _kvprep_call(proj3):
    B, S, _ = proj3.shape
    G, hd = NSA_GROUPS, NSA_HEAD_DIM
    tt = min(KV_PREP_ROWS, S)
    n_live = S // tt
    assert tt == WINDOW
    sp = S + tt
    col = lambda off: pl.BlockSpec((1, tt, LANES), lambda b, t: (b, jnp.minimum(t, n_live - 1), off // LANES))
    front = lambda t: (t + 1) % (n_live + 1)
    return pl.pallas_call(
        functools.partial(_kvprep_kernel, tt=tt, n_live=n_live),
        grid=(B, n_live + 1),
        in_specs=[col(OFF_KS), col(OFF_VS), col(OFF_KW), col(OFF_VW)],
        out_specs=[pl.BlockSpec((1, G, tt, NSA_KW + SEL_ONEHOT), lambda b, t: (b, 0, t, 0)),
                   pl.BlockSpec((1, G, tt // SEL_TILE, NSA_VROWS, SEL_TILE), lambda b, t: (b, 0, t, 0, 0)),
                   pl.BlockSpec((1, G, tt, NSA_KW), lambda b, t: (b, 0, front(t), 0)),
                   pl.BlockSpec((1, G, tt // WIN_TILE, NSA_VROWS, WIN_TILE), lambda b, t: (b, 0, front(t), 0, 0))],
        out_shape=[jax.ShapeDtypeStruct((B, G, sp, NSA_KW + SEL_ONEHOT), BF16),
                   jax.ShapeDtypeStruct((B, G, sp // SEL_TILE, NSA_VROWS, SEL_TILE), BF16),
                   jax.ShapeDtypeStruct((B, G, sp, NSA_KW), BF16),
                   jax.ShapeDtypeStruct((B, G, sp // WIN_TILE, NSA_VROWS, WIN_TILE), BF16)],
        compiler_params=_params("parallel", "parallel"),
        name="nsa_kvprep",
    )(proj3, proj3, proj3, proj3)


def _importance_matrix(ns, nc):
    ratio = SEL_LEN // CMP_STRIDE
    a = np.zeros((ns, nc), np.float32)
    for j in range(ns):
        for n, wgt in ((ratio * j - 1, 0.5), (ratio * j, 1.0), (ratio * j + 1, 1.0),
                       (ratio * j + 2, 1.0), (ratio * j + 3, 0.5)):
            if 0 <= n < nc - 1:
                a[j, n] = wgt
    return a


def _plan_constants(ns):
    nt = ns // SEL_TILE_BLOCKS
    grp = (np.arange(ns)[None, :] // SEL_TILE_BLOCKS == np.arange(nt)[:, None]).astype(np.float32)
    k = np.arange(nt)
    w_lo = np.where(k < PLAN_HALF, 2.0 ** np.minimum(k, PLAN_HALF - 1), 0.0)
    w_hi = np.where(k >= PLAN_HALF, 2.0 ** np.maximum(k - PLAN_HALF, 0), 0.0)
    wts = np.stack([np.repeat(w_lo[:, None], LANES, 1), np.repeat(w_hi[:, None], LANES, 1)]).astype(np.float32)
    return grp, wts


def _select_kernel(q_ref, slope_ref, kc_ref, vct_ref, imp_ref, grp_ref, wts_ref, ocmp_ref, bias_ref, bits_ref,
                   *, ns, nc, qb):
    P, T = NSA_PER_GROUP, Q_BLOCK
    step = pl.program_id(2)
    nsteps = pl.num_programs(2)
    subs = range(qb)
    c0s = [(step * qb + u) * T for u in subs]
    qts, tlanes = zip(*[_query_t(q_ref, slope_ref, c0s[u], u * T) for u in subs])

    def body(rows, nblk):
        cend = lax.broadcasted_iota(jnp.int32, (rows, 1), 0) * CMP_STRIDE + (CMP_LEN - 1)
        cmask = [cend <= tlanes[u] for u in subs]
        s = [jnp.where(cmask[u], _dot(kc_ref[0, 0, 0:rows, :], qts[u]), NEG_BIG) for u in subs]
        m = [jnp.max(s[u], 0, keepdims=True) for u in subs]
        p_c = [jnp.where(cmask[u], jnp.exp(s[u] - m[u]), 0.0) for u in subs]
        p_c = [p_c[u] * (1.0 / jnp.maximum(jnp.sum(p_c[u], 0, keepdims=True), 1e-30)) for u in subs]
        for u in subs:
            ocmp_ref[0, 0, u] = _dot(vct_ref[0, 0][:, 0:rows], p_c[u].astype(BF16))

        psum = [p_c[u][:, 0:T] for u in subs]
        for p in range(1, P):
            psum = [psum[u] + p_c[u][:, p * T:(p + 1) * T] for u in subs]
        imp = [_dot_exact_lhs(imp_ref[0:nblk, 0:rows], psum[u]) for u in subs]
        blk = lax.broadcasted_iota(jnp.int32, (nblk, T), 0)
        val, chosen = [], []
        for u in subs:
            cur = (c0s[u] + lax.broadcasted_iota(jnp.int32, (nblk, T), 1)) // SEL_LEN
            forced = (blk == 0) | (blk == cur) | (blk == cur - 1)
            val.append(jnp.where(forced, 3e38, jnp.where(blk > cur, -1.0, imp[u])))
            chosen.append(jnp.zeros((nblk, T), F32))
        for _ in range(min(SEL_TOPK, nblk)):
            for u in subs:
                mx = jnp.max(val[u], 0, keepdims=True)
                first = jnp.min(jnp.where(val[u] == mx, blk, ns), 0, keepdims=True)
                pick = blk == first
                chosen[u] = jnp.where(pick, 1.0, chosen[u])
                val[u] = jnp.where(pick, -2.0, val[u])
        for u in subs:
            bias_ref[0, 0, u, 0:nblk, :] = jnp.where(chosen[u] > 0.5, 0.0, NEG_BIG)
            if nblk < ns:
                bias_ref[0, 0, u, nblk:ns, :] = jnp.full((ns - nblk, T), NEG_BIG, F32)
            ch = chosen[u]
            if nblk < ns:
                ch = jnp.concatenate([ch, jnp.zeros((ns - nblk, T), F32)], 0)
            cnt = _dot(grp_ref[...], ch.astype(BF16))
            act = jnp.where(jnp.max(cnt, 1, keepdims=True) > 0.5, 1.0, 0.0)
            lo = jnp.sum(act * wts_ref[0], 0, keepdims=True).astype(jnp.int32)
            hi = jnp.sum(act * wts_ref[1], 0, keepdims=True).astype(jnp.int32)
            bits_ref[u] = jnp.broadcast_to(lo | (hi << PLAN_HALF), (SUBLANES, LANES))

    parts = max(p for p in (SELECT_PARTS, 2, 1) if nc % (p * LANES) == 0 and ns % (p * 2 * SUBLANES) == 0)
    part = step * parts // nsteps
    for v in range(parts):
        pl.when(part == v)(functools.partial(body, nc * (v + 1) // parts, ns * (v + 1) // parts))


def _attend_kernel(bits_sref, q_ref, gt_ref, slope_ref, ocmp_ref, bias_ref, ks_ref, vst_ref, kw_ref, vwt_ref,
                   o_ref, list_ref, s_ref, m_ref, *, n_tiles_total, ns, qb):
    hd, P, T = NSA_HEAD_DIM, NSA_PER_GROUP, Q_BLOCK
    PT = P * T
    b, g, step = pl.program_id(0), pl.program_id(1), pl.program_id(2)
    subs = range(qb)
    iqs = [step * qb + u for u in subs]
    c0s = [iq * T for iq in iqs]
    qts, tlanes = zip(*[_query_t(q_ref, slope_ref, c0s[u], u * T) for u in subs])
    tl_row = lax.broadcasted_iota(jnp.int32, (1, PT), 1) % T

    gts = [jax.nn.sigmoid(gt_ref[0, u * T:(u + 1) * T, :]).T for u in subs]

    def gate_row(u, branch):
        return jnp.concatenate([gts[u][p * 3 + branch:p * 3 + branch + 1, :] for p in range(P)], 1)

    list_len = n_tiles_total + 1
    last_tiles = [(c0s[u] + T - 1) // SEL_TILE for u in subs]
    prev_bits = []
    for u in subs:
        bits = bits_sref[(b * pl.num_programs(1) + g) * (pl.num_programs(2) * qb) + iqs[u]]
        prev_bits.append(bits & (lax.shift_left(jnp.int32(1), last_tiles[u]) - 1))
    n_prevs = [jnp.int32(0) for _ in subs]
    for k in range(n_tiles_total):
        for u in subs:
            list_ref[u * list_len + n_prevs[u]] = k
            n_prevs[u] = n_prevs[u] + (lax.shift_right_logical(prev_bits[u], k) & 1)

    qt_sels = []
    for u in subs:
        sel_b = bias_ref[0, 0, u]
        if ns < SEL_ONEHOT:
            sel_b = jnp.concatenate([sel_b, jnp.zeros((SEL_ONEHOT - ns, T), F32)], 0)
        qt_sels.append(jnp.concatenate([qts[u], jnp.concatenate([sel_b.astype(BF16)] * P, 1)], 0))

    def tile_scores(u, kt):
        k0 = pl.multiple_of(kt * SEL_TILE, SEL_TILE)
        return _dot(ks_ref[0, 0, pl.ds(k0, SEL_TILE), :], qt_sels[u])

    def col_max(scores):
        m = jnp.max(scores[0], 0, keepdims=True)
        for sc in scores[1:]:
            m = jnp.maximum(m, jnp.max(sc, 0, keepdims=True))
        return m

    def softmax_update(carry, scores, vts, m_tile):
        m_run, acc = carry
        m_new = jnp.maximum(m_run, m_tile)
        acc = jnp.exp(m_run - m_new) * acc
        for sc, vt in zip(scores, vts):
            acc = acc + _dot(vt, jnp.exp(sc - m_new).astype(BF16))
        return m_new, acc

    init = (jnp.full((1, PT), NEG_BIG, F32), jnp.zeros((NSA_VROWS, PT), F32))
    dead_tile = n_tiles_total
    n_pairs = [jnp.maximum((n_prevs[u] + 1) // 2, 1) for u in subs]

    def pair_tiles(u, i):
        has_a, has_b = 2 * i < n_prevs[u], 2 * i + 1 < n_prevs[u]
        ka = jnp.where(has_a, list_ref[u * list_len + jnp.where(has_a, 2 * i, 0)], dead_tile)
        kb = jnp.where(has_b, list_ref[u * list_len + jnp.where(has_b, 2 * i + 1, 0)], dead_tile)
        return ka, kb

    def pair_scores(u, i):
        ka, kb = pair_tiles(u, i)
        keys = jnp.concatenate([ks_ref[0, 0, pl.ds(pl.multiple_of(k * SEL_TILE, SEL_TILE), SEL_TILE), :]
                                for k in (ka, kb)], 0)
        return _dot(keys, qt_sels[u])

    def put_scores(u, i, sc):
        slot = i % 2
        s_ref[u, slot] = sc
        m_ref[u, slot] = col_max([sc])

    def pair_probs(u, i, m_run):
        slot = i % 2
        m_new = jnp.maximum(m_run, m_ref[u, slot])
        return m_new, jnp.exp(s_ref[u, slot] - m_new).astype(BF16)

    def add_values(u, i, carry, m_new, pr):
        ka, kb = pair_tiles(u, i)
        m_run, acc = carry
        vt = jnp.concatenate([vst_ref[0, 0, ka], vst_ref[0, 0, kb]], 1)
        return m_new, jnp.exp(m_run - m_new) * acc + _dot(vt, pr)

    first = [pair_scores(u, 0) for u in subs]
    for u in subs:
        put_scores(u, 0, first[u])

    n_win = (WINDOW + T) // WIN_TILE
    roff = lax.broadcasted_iota(jnp.int32, (WIN_TILE, 1), 0)
    w_sc = [_dot(kw_ref[0, 0, pl.ds(pl.multiple_of(c0s[u], WIN_TILE), WINDOW + T), :], qts[u]) for u in subs]
    w_parts = [[jnp.where(roff > tl_row, w_sc[u][0:WIN_TILE], NEG_BIG),
                w_sc[u][WIN_TILE:WINDOW],
                jnp.where(roff <= tl_row, w_sc[u][WINDOW:WINDOW + T], NEG_BIG)]
               for u in subs]
    w_max = [col_max(w_parts[u]) for u in subs]
    out_cw = []
    for u in subs:
        vt = jnp.concatenate([vwt_ref[0, 0, iqs[u] + j] for j in range(n_win)], 1)
        _, acc_win = softmax_update(init, [jnp.concatenate(w_parts[u], 0)], [vt], w_max[u])
        out_cw.append(gate_row(u, 0) * ocmp_ref[0, 0, u] + gate_row(u, 2) * (acc_win[0:hd] / acc_win[hd:hd + 1]))

    def pipe_step(u, i, carry):
        m_new, pr = pair_probs(u, i, carry[0])
        sc_next = pair_scores(u, i + 1)
        carry = add_values(u, i, carry, m_new, pr)
        put_scores(u, i + 1, sc_next)
        return carry

    carries = [lax.fori_loop(0, n_pairs[u] - 1, functools.partial(pipe_step, u), init) for u in subs]
    probs = [pair_probs(u, n_pairs[u] - 1, carries[u][0]) for u in subs]
    s_last = []
    for u in subs:
        kpos = last_tiles[u] * SEL_TILE + lax.broadcasted_iota(jnp.int32, (SEL_TILE, 1), 0)
        s_last.append(jnp.where(kpos <= tlanes[u], tile_scores(u, last_tiles[u]), NEG_BIG))
    carries = [add_values(u, n_pairs[u] - 1, carries[u], *probs[u]) for u in subs]
    for u in subs:
        _, acc_sel = softmax_update(carries[u], [s_last[u]], [vst_ref[0, 0, last_tiles[u]]], col_max([s_last[u]]))
        out = out_cw[u] + gate_row(u, 1) * (acc_sel[0:hd] / acc_sel[hd:hd + 1])
        o_ref[0, u * T:(u + 1) * T, :] = jnp.concatenate(
            [out[:, p * T:(p + 1) * T] for p in range(P)], 0).T.astype(o_ref.dtype)


def _nsa_call(proj3, pe_k, w1_k, w2_k, pe_v, w1_v, w2_v):
    B, S, _ = proj3.shape
    G, hd, P, T = NSA_GROUPS, NSA_HEAD_DIM, NSA_PER_GROUP, Q_BLOCK
    PT = P * T
    nc = S // CMP_STRIDE
    ns = S // SEL_LEN
    nq = S // T
    nt = ns // SEL_TILE_BLOCKS
    assert nt <= 2 * PLAN_HALF and S % KV_PREP_ROWS == 0

    kc_a, vc_t = _compress_call(proj3, pe_k, w1_k, w2_k, pe_v, w1_v, w2_v)
    ks_a, vs_t, kw_a, vw_t = _kvprep_call(proj3)
    imp_m = jnp.asarray(_importance_matrix(ns, nc), BF16)
    grp, wts = _plan_constants(ns)
    head_slopes = 2.0 ** (-(8.0 / NSA_HEADS) * np.arange(1, NSA_HEADS + 1, dtype=np.float32))
    slopes = jnp.asarray(np.repeat(head_slopes.reshape(G, 1, P), T, axis=-1), F32)

    q_spec = lambda im: pl.BlockSpec((1, T, P * hd), im)
    bg = lambda shp: pl.BlockSpec((1, 1) + shp, lambda b, g, i: (b, g) + tuple(0 for _ in shp))
    const = lambda a: pl.BlockSpec(a.shape, lambda b, g, i: tuple(0 for _ in a.shape))
    qb = SELECT_QB if nq % (SELECT_QB * SELECT_PARTS) == 0 else 1
    nsteps = nq // qb
    ocmp, sel_bias, bits = pl.pallas_call(
        functools.partial(_select_kernel, ns=ns, nc=nc, qb=qb),
        grid=(B, G, nsteps),
        in_specs=[pl.BlockSpec((1, qb * T, P * hd), lambda b, g, i: (b, i, OFF_NQ // (P * hd) + g)),
                  pl.BlockSpec((1, 1, PT), lambda b, g, i: (g, 0, 0)),
                  bg((nc, NSA_KW)), bg((hd, nc)), const(imp_m),
                  pl.BlockSpec(grp.shape, lambda b, g, i: (0, 0)),
                  pl.BlockSpec(wts.shape, lambda b, g, i: (0, 0, 0))],
        out_specs=[pl.BlockSpec((1, 1, qb, hd, PT), lambda b, g, i: (b, g, i, 0, 0)),
                   pl.BlockSpec((1, 1, qb, ns, T), lambda b, g, i: (b, g, i, 0, 0)),
                   pl.BlockSpec((qb, SUBLANES, LANES), lambda b, g, i: ((b * G + g) * nsteps + i, 0, 0))],
        out_shape=[jax.ShapeDtypeStruct((B, G, nq, hd, PT), F32),
                   jax.ShapeDtypeStruct((B, G, nq, ns, T), F32),
                   jax.ShapeDtypeStruct((B * G * nq, SUBLANES, LANES), jnp.int32)],
        compiler_params=_params("parallel", "parallel", "parallel"),
        name="nsa_select",
    )(proj3, slopes, kc_a, vc_t, imp_m, jnp.asarray(grp, BF16), jnp.asarray(wts, F32))

    aq = ATTEND_QB if nq % ATTEND_QB == 0 else 1
    bg2 = lambda shp: pl.BlockSpec((1, 1) + shp, lambda b, g, i, s: (b, g) + tuple(0 for _ in shp))
    return pl.pallas_call(
        functools.partial(_attend_kernel, n_tiles_total=nt, ns=ns, qb=aq),
        grid_spec=pltpu.PrefetchScalarGridSpec(
            num_scalar_prefetch=1,
            grid=(B, G, nq // aq),
            in_specs=[pl.BlockSpec((1, aq * T, P * hd), lambda b, g, i, s: (b, i, OFF_NQ // (P * hd) + g)),
                      pl.BlockSpec((1, aq * T, LANES), lambda b, g, i, s: (b, i, OFF_NG // LANES + g)),
                      pl.BlockSpec((1, 1, PT), lambda b, g, i, s: (g, 0, 0)),
                      pl.BlockSpec((1, 1, aq, hd, PT), lambda b, g, i, s: (b, g, i, 0, 0)),
                      pl.BlockSpec((1, 1, aq, ns, T), lambda b, g, i, s: (b, g, i, 0, 0)),
                      bg2(ks_a.shape[2:]), bg2(vs_t.shape[2:]), bg2(kw_a.shape[2:]), bg2(vw_t.shape[2:])],
            out_specs=pl.BlockSpec((1, aq * T, P * hd), lambda b, g, i, s: (b, i, g)),
            scratch_shapes=[pltpu.SMEM((aq * (nt + 1),), jnp.int32),
                            pltpu.VMEM((aq, 2, 2 * SEL_TILE, PT), F32), pltpu.VMEM((aq, 2, 1, PT), F32)]),
        out_shape=jax.ShapeDtypeStruct((B, S, NSA_WIDTH), BF16),
        compiler_params=_params("parallel", "parallel", "arbitrary"),
        name="nsa_attend",
    )(bits[:, 0, 0], proj3, proj3, slopes, ocmp, sel_bias, ks_a, vs_t, kw_a, vw_t)


def _merge_kernel(x_ref, ya_ref, yb_ref, yc_ref, g0_ref, g1_ref, g2_ref, wa_ref, wb_ref, wc_ref, wo_ref,
                  lg_ref, lb_ref, o_ref):
    gate = lambda r: jax.nn.sigmoid(r[...].astype(F32))
    m = gate(g0_ref) * _dot(ya_ref[...], wa_ref[...])
    m = m + gate(g1_ref) * _dot(yb_ref[...], wb_ref[...])
    m = m + gate(g2_ref) * _dot(yc_ref[...], wc_ref[...])
    z = DEEPNORM_ALPHA * x_ref[...] + _dot(m.astype(BF16), wo_ref[...])
    o_ref[...] = _layer_norm_rows(z, lg_ref[...], lb_ref[...])


def _merge_call(x2, ya, yb, yc, gates, w_hg, w_nsa, w_lru, w_out, ln_g, ln_b, tm=512):
    T, D = x2.shape
    tm = min(tm, T)
    rows = lambda w: pl.BlockSpec((tm, w), lambda i: (i, 0))
    gate = lambda n: pl.BlockSpec((tm, D), lambda i: (i, n))
    full = lambda a: pl.BlockSpec(a.shape, lambda i: (0, 0))
    ws = [w.astype(BF16) for w in (w_hg, w_nsa, w_lru, w_out)]
    return pl.pallas_call(
        _merge_kernel,
        grid=(T // tm,),
        in_specs=[rows(D), rows(HG_WIDTH), rows(NSA_WIDTH), rows(LRU_WIDTH), gate(0), gate(1), gate(2)]
                 + [full(w) for w in ws] + [pl.BlockSpec((1, D), lambda i: (0, 0))] * 2,
        out_specs=rows(D),
        out_shape=jax.ShapeDtypeStruct((T, D), F32),
        compiler_params=_params("parallel"),
        name="merge_out",
    )(x2, ya, yb, yc, gates, gates, gates, *ws, ln_g.reshape(1, D), ln_b.reshape(1, D))


FFN_SPLIT = 1
FFN_PAD = SUBLANES
FFN_SUB = 256


def _ffn_kernel(x_ref, wu_ref, wv_ref, cw_ref, cb_ref, wd_ref, lg_ref, lb_ref, o_ref,
                xb_ref, acc_ref, upad_ref, carry_ref, *, tm, tiles_per_seq):
    i = pl.program_id(0)
    j = pl.program_id(1)

    @pl.when(j == 0)
    def _():
        xb_ref[...] = x_ref[...].astype(BF16)

    xb = xb_ref[...]
    fc = wu_ref.shape[1]
    first = (i % tiles_per_seq) == 0

    @pl.when(first)
    def _():
        upad_ref[0:FFN_PAD, :] = jnp.zeros((FFN_PAD, fc), F32)

    @pl.when(jnp.logical_not(first))
    def _():
        upad_ref[0:FFN_PAD, :] = carry_ref[j]

    @pl.when(j == 0)
    def _():
        acc_ref[...] = jnp.zeros_like(acc_ref)

    for c0 in range(0, fc, FFN_SUB):
        cs = slice(c0, min(c0 + FFN_SUB, fc))
        u = _dot(xb, wu_ref[:, cs])
        v = _dot(xb, wv_ref[:, cs])
        upad_ref[FFN_PAD:FFN_PAD + tm, cs] = u
        cw = cw_ref[:, cs]
        cv = cb_ref[:, cs] + cw[FFN_CONV - 1:FFN_CONV, :] * u
        for k in range(FFN_CONV - 1):
            off = FFN_PAD - (FFN_CONV - 1) + k
            cv = cv + cw[k:k + 1, :] * upad_ref[off:off + tm, cs]
        carry_ref[j, :, cs] = u[tm - FFN_PAD:tm, :]
        h = (jax.nn.gelu(cv) * v).astype(BF16)
        acc_ref[...] += _dot(h, wd_ref[cs, :])

    @pl.when(j == FFN_SPLIT - 1)
    def _():
        z = DEEPNORM_ALPHA * x_ref[...] + acc_ref[...]
        o_ref[...] = _layer_norm_rows(z, lg_ref[...], lb_ref[...])


def _ffn_call(x2, seq_len, w_up, conv_w, conv_b, w_down, ln_g, ln_b, tm=512):
    T, D = x2.shape
    tm = min(tm, seq_len)
    fc = FFN_DIM // FFN_SPLIT
    wu = w_up.astype(BF16)
    wmode = pl.Buffered(1) if FFN_SPLIT == 1 else None
    return pl.pallas_call(
        functools.partial(_ffn_kernel, tm=tm, tiles_per_seq=seq_len // tm),
        grid=(T // tm, FFN_SPLIT),
        in_specs=[pl.BlockSpec((tm, D), lambda i, j: (i, 0)),
                  pl.BlockSpec((D, fc), lambda i, j: (0, j), pipeline_mode=wmode),
                  pl.BlockSpec((D, fc), lambda i, j: (0, FFN_SPLIT + j), pipeline_mode=wmode),
                  pl.BlockSpec((FFN_CONV, fc), lambda i, j: (0, j)),
                  pl.BlockSpec((1, fc), lambda i, j: (0, j)),
                  pl.BlockSpec((fc, D), lambda i, j: (j, 0), pipeline_mode=wmode),
                  pl.BlockSpec((1, D), lambda i, j: (0, 0)),
                  pl.BlockSpec((1, D), lambda i, j: (0, 0))],
        out_specs=pl.BlockSpec((tm, D), lambda i, j: (i, 0)),
        out_shape=jax.ShapeDtypeStruct((T, D), F32),
        scratch_shapes=[pltpu.VMEM((tm, D), BF16), pltpu.VMEM((tm, D), F32),
                        pltpu.VMEM((tm + FFN_PAD, fc), F32), pltpu.VMEM((FFN_SPLIT, FFN_PAD, fc), F32)],
        compiler_params=_params("arbitrary", "arbitrary"),
        name="conv_ffn",
    )(x2, wu, wu, conv_w.astype(F32), conv_b.reshape(1, FFN_DIM).astype(F32), w_down.astype(BF16),
      ln_g.reshape(1, D), ln_b.reshape(1, D))


def _permute_in_proj(w, b):
    src = np.cumsum([0, HG_WIDTH, HG_WIDTH, HG_WIDTH, HG_WIDTH, NSA_WIDTH, NSA_KV, NSA_KV, NSA_KV, NSA_KV,
                     NSA_KV, NSA_KV, NSA_HEADS * 3, LRU_WIDTH, LRU_WIDTH, N_BRANCH * 1024])
    names = ["hq", "hf", "hi", "hg", "nq", "kc", "vc", "ks", "vs", "kw", "vw", "ng", "lx", "ly", "mg"]
    seg = {n: (int(src[k]), int(src[k + 1])) for k, n in enumerate(names)}
    order = ["mg", "hq", "hf", "hi", "hg", "nq", "lx", "ly", "kc", "vc", "ks", "vs", "kw", "vw"]
    per_group = NSA_PER_GROUP * 3
    ng0 = seg["ng"][0]
    w = w.astype(BF16)
    w_parts = [w[:, seg[n][0]:seg[n][1]] for n in order]
    b_parts = [b[seg[n][0]:seg[n][1]] for n in order]
    for gi in range(NSA_GROUPS):
        lo = ng0 + gi * per_group
        w_parts += [w[:, lo:lo + per_group], jnp.zeros((w.shape[0], LANES - per_group), w.dtype)]
        b_parts += [b[lo:lo + per_group], jnp.zeros((LANES - per_group,), b.dtype)]
    wp = jnp.concatenate(w_parts, 1)
    bp = jnp.concatenate(b_parts)
    assert wp.shape[1] == GATE_COLS + PROJ_COLS
    return wp.astype(BF16), bp.astype(F32)


def kernel(x, ln_emb_g, ln_emb_b, w_in, b_in, hg_lb_logits, hg_norm_g, cmp_pe_k, cmp_w1_k, cmp_w2_k, cmp_pe_v, cmp_w1_v, cmp_w2_v, lru_conv_w, lru_conv_b, lru_wa, lru_ba, lru_wx, lru_bx, lru_lambda, w_branch_hg, w_branch_nsa, w_branch_lru, w_out, ln1_g, ln1_b, ffn_w_up, ffn_conv_w, ffn_conv_b, ffn_w_down, ln2_g, ln2_b):
    B, S, D = x.shape
    T = B * S
    gam = jax.nn.softmax(hg_lb_logits.astype(F32), axis=0)
    lb_all = jnp.cumsum(gam, axis=0) - gam[0]
    h = x.reshape(T, D)
    for l in range(DEPTH):
        wp, bp = _permute_in_proj(w_in[l], b_in[l])
        if l == 0:
            gates, proj2, h = _inproj_call(h, wp, bp, ln=(ln_emb_g, ln_emb_b))
        else:
            gates, proj2 = _inproj_call(h, wp, bp)
        proj3 = proj2.reshape(B, S, PROJ_COLS)
        y_a = _hgrn_call(proj3, lb_all[l], hg_norm_g[l])
        y_b = _nsa_call(proj3, cmp_pe_k[l], cmp_w1_k[l], cmp_w2_k[l], cmp_pe_v[l], cmp_w1_v[l], cmp_w2_v[l])
        y_c = _lru_call(proj3, lru_conv_w[l], lru_conv_b[l], lru_wa[l], lru_ba[l], lru_wx[l], lru_bx[l],
                        lru_lambda[l])
        h = _merge_call(h, y_a.reshape(T, HG_WIDTH), y_b.reshape(T, NSA_WIDTH), y_c.reshape(T, LRU_WIDTH),
                        gates, w_branch_hg[l], w_branch_nsa[l], w_branch_lru[l], w_out[l], ln1_g[l], ln1_b[l])
        h = _ffn_call(h, S, ffn_w_up[l], ffn_conv_w[l], ffn_conv_b[l], ffn_w_down[l], ln2_g[l], ln2_b[l])
    return h.reshape(B, S, D)
```

```python
import functools

import numpy as np
import jax
import jax.numpy as jnp
from jax import lax
from jax.experimental import pallas as pl
from jax.experimental.pallas import tpu as pltpu

F32 = jnp.float32
BF16 = jnp.bfloat16

DEPTH = 2
HG_HEADS = 4
HG_HEAD_DIM = 128
HG_WIDTH = HG_HEADS * HG_HEAD_DIM
NSA_HEADS = 8
NSA_GROUPS = 2
NSA_PER_GROUP = NSA_HEADS // NSA_GROUPS
NSA_HEAD_DIM = 64
NSA_WIDTH = NSA_HEADS * NSA_HEAD_DIM
NSA_KV = NSA_GROUPS * NSA_HEAD_DIM
CMP_LEN = 32
CMP_STRIDE = 16
CMP_HIDDEN = 64
SEL_LEN = 64
SEL_TOPK = 16
WINDOW = 512
Q_BLOCK = 128
LRU_WIDTH = 512
LRU_BLOCKS = 4
LRU_CONV = 4
LRU_C = 8.0
FFN_DIM = 2816
FFN_CONV = 3
DEEPNORM_ALPHA = (2 * DEPTH) ** 0.25
LN_EPS = 1e-5
N_BRANCH = 3

LANES = 128
SUBLANES = 8
VMEM_LIMIT_BYTES = 52 * 1024 * 1024

GATE_COLS = N_BRANCH * 1024
OFF_HQ = 0
OFF_HF = OFF_HQ + HG_WIDTH
OFF_HI = OFF_HF + HG_WIDTH
OFF_HG = OFF_HI + HG_WIDTH
OFF_NQ = OFF_HG + HG_WIDTH
OFF_LX = OFF_NQ + NSA_WIDTH
OFF_LY = OFF_LX + LRU_WIDTH
OFF_KC = OFF_LY + LRU_WIDTH
OFF_VC = OFF_KC + NSA_KV
OFF_KS = OFF_VC + NSA_KV
OFF_VS = OFF_KS + NSA_KV
OFF_KW = OFF_VS + NSA_KV
OFF_VW = OFF_KW + NSA_KV
OFF_NG = OFF_VW + NSA_KV
PROJ_TILE_N = 1536
PROJ_COLS = OFF_NG + NSA_GROUPS * LANES

NEG_BIG = -1e30


def _split3(x):
    hi = x.astype(BF16)
    r1 = x - hi.astype(F32)
    mid = r1.astype(BF16)
    lo = (r1 - mid.astype(F32)).astype(BF16)
    return hi, mid, lo


def _dot(a, b):
    return jnp.dot(a, b, preferred_element_type=F32)


def _dot_nt(a, b):
    return lax.dot_general(a, b, (((1,), (1,)), ((), ())), preferred_element_type=F32)


def _dot_exact_lhs(m_bf16, x):
    hi, mid, lo = _split3(x)
    return _dot(m_bf16, hi) + _dot(m_bf16, mid) + _dot(m_bf16, lo)


def _dot_hilo(a, b):
    ah = a.astype(BF16)
    al = (a - ah.astype(F32)).astype(BF16)
    bh = b.astype(BF16)
    bl = (b - bh.astype(F32)).astype(BF16)
    return _dot(ah, bh) + _dot(ah, bl) + _dot(al, bh) + _dot(al, bl)


def _layer_norm_rows(z, g, b):
    mu = jnp.mean(z, -1, keepdims=True)
    zc = z - mu
    var = jnp.mean(zc * zc, -1, keepdims=True)
    return zc * lax.rsqrt(var + LN_EPS) * g + b


def _params(*sem):
    return pltpu.CompilerParams(dimension_semantics=sem, vmem_limit_bytes=VMEM_LIMIT_BYTES)


def _inproj_kernel(*refs, with_ln):
    if with_ln:
        x_ref, g_ref, be_ref, w_ref, b_ref, gate_ref, o_ref, xn_ref = refs
    else:
        x_ref, w_ref, b_ref, gate_ref, o_ref = refs
    x = x_ref[...]
    if with_ln:
        x = _layer_norm_rows(x, g_ref[...], be_ref[...])
        xn_ref[...] = x
    xb = x.astype(BF16)
    n = w_ref.shape[1]
    for c0 in range(0, n, PROJ_TILE_N):
        cs = slice(c0, c0 + PROJ_TILE_N)
        acc = _dot(xb, w_ref[:, cs]) + b_ref[:, cs]
        if c0 < GATE_COLS:
            gate_ref[:, cs] = acc.astype(BF16)
        else:
            o_ref[:, c0 - GATE_COLS:c0 - GATE_COLS + PROJ_TILE_N] = acc


def _inproj_call(x2, w_bf16, bias, ln=None, tm=512):
    T, D = x2.shape
    N = w_bf16.shape[1]
    tm = min(tm, T)
    assert GATE_COLS % PROJ_TILE_N == 0 and N % PROJ_TILE_N == 0
    vec = pl.BlockSpec((1, D), lambda i: (0, 0))
    in_specs = [pl.BlockSpec((tm, D), lambda i: (i, 0))] + ([vec, vec] if ln else []) + [
        pl.BlockSpec((D, N), lambda i: (0, 0), pipeline_mode=pl.Buffered(1)),
        pl.BlockSpec((1, N), lambda i: (0, 0))]
    out_specs = [pl.BlockSpec((tm, GATE_COLS), lambda i: (i, 0)),
                 pl.BlockSpec((tm, N - GATE_COLS), lambda i: (i, 0))]
    out_shape = [jax.ShapeDtypeStruct((T, GATE_COLS), BF16), jax.ShapeDtypeStruct((T, N - GATE_COLS), F32)]
    if ln:
        out_specs.append(pl.BlockSpec((tm, D), lambda i: (i, 0)))
        out_shape.append(jax.ShapeDtypeStruct((T, D), F32))
    ln_args = [ln[0].reshape(1, D), ln[1].reshape(1, D)] if ln else []
    return pl.pallas_call(
        functools.partial(_inproj_kernel, with_ln=bool(ln)),
        grid=(T // tm,),
        in_specs=in_specs, out_specs=out_specs, out_shape=out_shape,
        compiler_params=_params("parallel"),
        name="in_proj_ln" if ln else "in_proj",
    )(x2, *ln_args, w_bf16, bias.reshape(1, N))


HG_CHUNK = 128
HG_DIAG = SUBLANES
HG_HEADS_PER_STEP = 4


def _hgrn_constants(C):
    r = np.arange(C)
    tri = (r[None, :] <= r[:, None]).astype(np.float32)
    mats = [tri]
    masks = []
    w = HG_DIAG
    while 2 * w <= C:
        mid = (r // (2 * w)) * (2 * w) + w
        mats.append(tri - (r[None, :] <= mid[:, None]).astype(np.float32))
        same = (r[:, None] // (2 * w)) == (r[None, :] // (2 * w))
        masks.append((same & ((r[:, None] % (2 * w)) >= w) & ((r[None, :] % (2 * w)) < w)).astype(np.float32))
        w *= 2
    mats.append((r[None, :] > r[:, None]).astype(np.float32))
    return np.concatenate(mats, 0), np.stack(masks, 0)


def _hgrn_kernel(q_ref, f_ref, i_ref, g_ref, lb_ref, ng_ref, m_ref, mask_ref, o_ref, st_ref, *, C, nchunk, nh):
    d = HG_HEAD_DIM
    nlev = mask_ref.shape[0]

    @pl.when(pl.program_id(2) == 0)
    def _():
        st_ref[...] = jnp.zeros_like(st_ref)

    hs = range(nh)
    lane = [slice(h * d, (h + 1) * d) for h in hs]
    lb = [lb_ref[h] for h in hs]
    ng = [ng_ref[h] for h in hs]
    nb = C // HG_DIAG
    row3 = lax.broadcasted_iota(jnp.int32, (nb, HG_DIAG, d), 1)

    fl_all = f_ref[0]
    lf_all = [jnp.log(lb[h] + (1.0 - lb[h]) * jax.nn.sigmoid(fl_all[:, lane[h]])) for h in hs]
    kk_all = [(1.0 - lb[h]) * jax.nn.sigmoid(-fl_all[:, lane[h]]) for h in hs]
    allm_all = _dot_exact_lhs(m_ref[...], jnp.concatenate(
        [lf_all[h][c * C:(c + 1) * C] for h in hs for c in range(nchunk)], 1))

    for c in range(nchunk):
        sl = pl.ds(c * C, C)
        q = [q_ref[0, sl, lane[h]] for h in hs]
        iv = [i_ref[0, sl, lane[h]] for h in hs]
        kk = [kk_all[h][c * C:(c + 1) * C] for h in hs]
        allm = [allm_all[:, (h * nchunk + c) * d:(h * nchunk + c + 1) * d] for h in hs]
        bcs = [allm[h][0:C] for h in hs]
        rem = [allm[h][(nlev + 1) * C:(nlev + 2) * C] for h in hs]
        iv_b = [iv[h].astype(BF16) for h in hs]

        att = [jnp.zeros((C, C), F32) for _ in hs]
        for l in range(nlev):
            e = [jnp.exp(-jnp.abs(allm[h][(1 + l) * C:(2 + l) * C])) for h in hs]
            att = [att[h] + mask_ref[l] * _dot_nt((q[h] * e[h]).astype(BF16), (kk[h] * e[h]).astype(BF16))
                   for h in hs]
        o = [_dot(att[h].astype(BF16), iv_b[h]) for h in hs]

        q3 = [q[h].reshape(nb, HG_DIAG, d) for h in hs]
        k3 = [kk[h].reshape(nb, HG_DIAG, d) for h in hs]
        b3 = [bcs[h].reshape(nb, HG_DIAG, d) for h in hs]
        i3 = [iv[h].reshape(nb, HG_DIAG, d) for h in hs]
        acc = [jnp.zeros((nb, HG_DIAG, d), F32) for _ in hs]
        for s in range(HG_DIAG):
            for h in hs:
                dec = jnp.where(row3 >= s, jnp.exp(b3[h] - b3[h][:, s:s + 1, :]), 0.0)
                a = jnp.sum(q3[h] * dec * k3[h][:, s:s + 1, :], axis=-1, keepdims=True)
                acc[h] = acc[h] + a * i3[h][:, s:s + 1, :]

        for h in hs:
            st = st_ref[h]
            oh = o[h] + acc[h].reshape(C, d) + _dot_nt((q[h] * jnp.exp(bcs[h])).astype(BF16), st.astype(BF16))
            kdec = (kk[h] * jnp.exp(rem[h])).astype(BF16)
            st_ref[h] = st * jnp.exp(bcs[h][C - 1:C, :]) + _dot(iv[h].T.astype(BF16), kdec)
            oh = oh * lax.rsqrt(jnp.mean(oh * oh, -1, keepdims=True) + 1e-6) * ng[h]
            g = g_ref[0, sl, lane[h]]
            o_ref[0, sl, lane[h]] = (oh * (g * jax.nn.sigmoid(g))).astype(o_ref.dtype)


def _hgrn_call(proj3, lb, norm_g, tt=512):
    B, S, _ = proj3.shape
    d = HG_HEAD_DIM
    C = HG_CHUNK
    tt = min(tt, S)
    mstack, masks = _hgrn_constants(C)
    nm = mstack.shape[0]
    nh = HG_HEADS_PER_STEP
    wid = nh * d
    col = lambda off: (lambda b, h, t: (b, t, off // wid + h))
    blk = (1, tt, wid)
    return pl.pallas_call(
        functools.partial(_hgrn_kernel, C=C, nchunk=tt // C, nh=nh),
        grid=(B, HG_HEADS // nh, S // tt),
        in_specs=[pl.BlockSpec(blk, col(OFF_HQ)), pl.BlockSpec(blk, col(OFF_HF)),
                  pl.BlockSpec(blk, col(OFF_HI)), pl.BlockSpec(blk, col(OFF_HG)),
                  pl.BlockSpec((nh, 1, d), lambda b, h, t: (h, 0, 0)),
                  pl.BlockSpec((nh, 1, d), lambda b, h, t: (h, 0, 0)),
                  pl.BlockSpec((nm, C), lambda b, h, t: (0, 0)),
                  pl.BlockSpec(masks.shape, lambda b, h, t: (0, 0, 0))],
        out_specs=pl.BlockSpec(blk, lambda b, h, t: (b, t, h)),
        out_shape=jax.ShapeDtypeStruct((B, S, HG_WIDTH), BF16),
        scratch_shapes=[pltpu.VMEM((nh, d, d), F32)],
        compiler_params=_params("parallel", "parallel", "arbitrary"),
        name="hgrn2",
    )(proj3, proj3, proj3, proj3, lb.reshape(HG_HEADS, 1, d), norm_g.reshape(HG_HEADS, 1, d),
      jnp.asarray(mstack, BF16), jnp.asarray(masks, F32))


LRU_PAD = SUBLANES


def _lru_kernel(x_ref, y_ref, cw_ref, cb_ref, wa_ref, ba_ref, wx_ref, bx_ref, c_ref, ex_ref, o_ref,
                xpad_ref, h_ref, a_ref, u_ref, *, tt):
    W = LRU_WIDTH
    bw = W // LRU_BLOCKS

    @pl.when(pl.program_id(1) == 0)
    def _():
        xpad_ref[0:LRU_PAD, :] = jnp.zeros((LRU_PAD, W), F32)
        h_ref[...] = jnp.zeros_like(h_ref)

    x = x_ref[0]
    xpad_ref[LRU_PAD:LRU_PAD + tt, :] = x
    cw = cw_ref[...]
    xc = cb_ref[...] + cw[LRU_CONV - 1:LRU_CONV, :] * x
    for j in range(LRU_CONV - 1):
        off = LRU_PAD - (LRU_CONV - 1) + j
        xc = xc + cw[j:j + 1, :] * xpad_ref[off:off + tt, :]
    xpad_ref[0:LRU_PAD, :] = x[tt - LRU_PAD:tt, :]

    rs, is_ = [], []
    for gi in range(LRU_BLOCKS):
        xg = xc[:, gi * bw:(gi + 1) * bw].astype(BF16)
        rs.append(_dot(xg, wa_ref[gi]))
        is_.append(_dot(xg, wx_ref[gi]))
    r = jax.nn.sigmoid(jnp.concatenate(rs, -1) + ba_ref[...])
    ig = jax.nn.sigmoid(jnp.concatenate(is_, -1) + bx_ref[...])
    log_a = c_ref[...] * r
    a = jnp.exp(log_a)
    u = jnp.sqrt(-jnp.tanh(log_a) * (a * a + 1.0)) * (ig * xc)

    def scan_rows(a, u, n, group):
        idx = lax.broadcasted_iota(jnp.int32, (n, W), 0) % group
        dlt = 1
        while dlt < group:
            keep = idx >= dlt
            a_sh = jnp.where(keep, pltpu.roll(a, dlt, 0), 1.0)
            u_sh = jnp.where(keep, pltpu.roll(u, dlt, 0), 0.0)
            u = a * u_sh + u
            a = a * a_sh
            dlt *= 2
        return a, u

    a, u = scan_rows(a, u, tt, SUBLANES)
    ng = tt // SUBLANES
    last = pl.ds(SUBLANES - 1, ng, stride=SUBLANES)
    ends = []
    for ref, val in ((a_ref, a), (u_ref, u)):
        for cb in range(W // LANES):
            ref[cb] = val[:, cb * LANES:(cb + 1) * LANES]
        ends.append(jnp.concatenate([ref[cb, last, :] for cb in range(W // LANES)], 1))
    ae, ue = scan_rows(ends[0], ends[1], ng, ng)
    h0 = h_ref[...]
    h_end = ue + ae * h0
    gidx = lax.broadcasted_iota(jnp.int32, (ng, W), 0)
    h_in = jnp.where(gidx >= 1, pltpu.roll(h_end, 1, 0), h0)
    h = u + a * _dot_exact_lhs(ex_ref[...], h_in)
    o_ref[0] = (h * jax.nn.gelu(y_ref[0])).astype(o_ref.dtype)
    h_ref[...] = h_end[ng - 1:ng, :]


def _lru_call(proj3, conv_w, conv_b, wa, ba, wx, bx, lam, tt=512):
    B, S, _ = proj3.shape
    W = LRU_WIDTH
    bw = W // LRU_BLOCKS
    tt = min(tt, S)
    c = (-LRU_C * jax.nn.softplus(-lam.astype(F32))).reshape(1, W)
    expand = (np.arange(tt)[:, None] // SUBLANES == np.arange(tt // SUBLANES)[None, :]).astype(np.float32)
    vec = lambda: pl.BlockSpec((1, W), lambda b, t: (0, 0))
    return pl.pallas_call(
        functools.partial(_lru_kernel, tt=tt),
        grid=(B, S // tt),
        in_specs=[pl.BlockSpec((1, tt, W), lambda b, t: (b, t, OFF_LX // W)),
                  pl.BlockSpec((1, tt, W), lambda b, t: (b, t, OFF_LY // W)),
                  pl.BlockSpec((LRU_CONV, W), lambda b, t: (0, 0)), vec(),
                  pl.BlockSpec((LRU_BLOCKS, bw, bw), lambda b, t: (0, 0, 0)), vec(),
                  pl.BlockSpec((LRU_BLOCKS, bw, bw), lambda b, t: (0, 0, 0)), vec(), vec(),
                  pl.BlockSpec(expand.shape, lambda b, t: (0, 0))],
        out_specs=pl.BlockSpec((1, tt, W), lambda b, t: (b, t, 0)),
        out_shape=jax.ShapeDtypeStruct((B, S, W), BF16),
        scratch_shapes=[pltpu.VMEM((tt + LRU_PAD, W), F32), pltpu.VMEM((1, W), F32),
                        pltpu.VMEM((W // LANES, tt, LANES), F32), pltpu.VMEM((W // LANES, tt, LANES), F32)],
        compiler_params=_params("parallel", "arbitrary"),
        name="rglru",
    )(proj3, proj3, conv_w.astype(F32), conv_b.reshape(1, W).astype(F32), wa.astype(BF16),
      ba.reshape(1, W), wx.astype(BF16), bx.reshape(1, W), c, jnp.asarray(expand, BF16))


NSA_AUG = 4
NSA_KW = LANES
SEL_TILE = 256
SEL_TILE_BLOCKS = SEL_TILE // SEL_LEN
WIN_TILE = 128
POS_SPLIT = 128
SEL_ONEHOT = LANES
PLAN_HALF = 16
SELECT_PARTS = 4
SELECT_QB = 8
ATTEND_QB = 4
NSA_VROWS = NSA_HEAD_DIM + 16
KV_PREP_ROWS = 512


def _key_aug(pos, width, dead=False):
    n = pos.shape[0]
    col = lax.broadcasted_iota(jnp.int32, (n, width), 1)
    hi = ((pos // POS_SPLIT) * POS_SPLIT).astype(F32)
    lo = (pos % POS_SPLIT).astype(F32)
    live = jnp.where(col == 0, hi, jnp.where(col == 1, lo, jnp.where(col < NSA_AUG, 1.0, 0.0)))
    return jnp.where(dead, jnp.where(col == NSA_AUG, 1.0, 0.0), live)


def _query_t(q_ref, slope_ref, c0, row0=0):
    hd, P, T = NSA_HEAD_DIM, NSA_PER_GROUP, Q_BLOCK
    PT = P * T
    qT = (q_ref[0, row0:row0 + T, :] * (hd ** -0.5)).T
    qpart = jnp.concatenate([qT[p * hd:(p + 1) * hd, :] for p in range(P)], 1)
    tlane = c0 + lax.broadcasted_iota(jnp.int32, (1, PT), 1) % T
    slope = slope_ref[0]
    t_hi = ((tlane // POS_SPLIT) * POS_SPLIT).astype(F32)
    t_lo = (tlane % POS_SPLIT).astype(F32)
    rowi = lax.broadcasted_iota(jnp.int32, (NSA_KW - hd, PT), 0)
    aug = jnp.where(rowi < 2, slope,
                    jnp.where(rowi == 2, -slope * t_hi,
                              jnp.where(rowi == 3, -slope * t_lo, jnp.where(rowi == NSA_AUG, NEG_BIG, 0.0))))
    return jnp.concatenate([qpart, aug], 0).astype(BF16), tlane


def _compress_kernel(xk_ref, xv_ref, w1k_ref, pek_ref, w1fk_ref, w2k_ref, w1v_ref, pev_ref, w1fv_ref, w2v_ref,
                     kc_ref, vct_ref, *, nc):
    hd = NSA_HEAD_DIM
    pos = lax.broadcasted_iota(jnp.int32, (nc, 1), 0) * CMP_STRIDE + (CMP_LEN - 1)
    aug = _key_aug(pos, NSA_KW - hd)

    def one(x_ref, w1_ref, pe_ref, w1f_ref, w2_ref, g):
        uv = jnp.zeros((nc, 2 * CMP_HIDDEN), F32)
        for r in range(CMP_STRIDE):
            xr = x_ref[0, pl.ds(r, nc, stride=CMP_STRIDE), :][:, g * hd:(g + 1) * hd]
            uv = uv + _dot(xr.astype(BF16), w1_ref[r])
        cvec = _dot_hilo(pe_ref[...], w1f_ref[...])[0:1, :]
        hid = uv[:, 0:CMP_HIDDEN] + pltpu.roll(uv[:, CMP_HIDDEN:], nc - 1, 0) + cvec
        return _dot_hilo(jax.nn.gelu(hid), w2_ref[...])

    for g in range(NSA_GROUPS):
        kc = one(xk_ref, w1k_ref, pek_ref, w1fk_ref, w2k_ref, g)
        kc_ref[0, g] = jnp.concatenate([kc, aug], 1).astype(BF16)
        vct_ref[0, g] = one(xv_ref, w1v_ref, pev_ref, w1fv_ref, w2v_ref, g).T.astype(BF16)


def _compress_call(proj3, pe_k, w1_k, w2_k, pe_v, w1_v, w2_v):
    B, S, _ = proj3.shape
    G, hd = NSA_GROUPS, NSA_HEAD_DIM
    nc = S // CMP_STRIDE
    half = CMP_STRIDE * hd

    def prep(pe, w1):
        w1 = w1.astype(F32)
        wr = jnp.concatenate([w1[:half].reshape(CMP_STRIDE, hd, CMP_HIDDEN),
                              w1[half:].reshape(CMP_STRIDE, hd, CMP_HIDDEN)], -1)
        pe8 = jnp.broadcast_to(pe.reshape(1, CMP_LEN * hd).astype(F32), (SUBLANES, CMP_LEN * hd))
        return wr.astype(BF16), pe8, w1

    full = lambda a: pl.BlockSpec(a.shape, lambda b: tuple(0 for _ in a.shape))
    col = lambda off: pl.BlockSpec((1, S, LANES), lambda b: (b, 0, off // LANES))
    args = prep(pe_k, w1_k) + (w2_k.astype(F32),) + prep(pe_v, w1_v) + (w2_v.astype(F32),)
    return pl.pallas_call(
        functools.partial(_compress_kernel, nc=nc),
        grid=(B,),
        in_specs=[col(OFF_KC), col(OFF_VC)] + [full(a) for a in args],
        out_specs=[pl.BlockSpec((1, G, nc, NSA_KW), lambda b: (b, 0, 0, 0)),
                   pl.BlockSpec((1, G, hd, nc), lambda b: (b, 0, 0, 0))],
        out_shape=[jax.ShapeDtypeStruct((B, G, nc, NSA_KW), BF16),
                   jax.ShapeDtypeStruct((B, G, hd, nc), BF16)],
        compiler_params=_params("parallel"),
        name="nsa_compress",
    )(proj3, proj3, *args)


def _kvprep_constants(S, tt):
    hd = NSA_HEAD_DIM
    pos = np.arange(S + tt)
    live = pos < S
    kcols = np.zeros((S + tt, NSA_KW - hd + SEL_ONEHOT), np.float32)
    kcols[:, 0] = np.where(live, (pos // POS_SPLIT) * POS_SPLIT, 0)
    kcols[:, 1] = np.where(live, pos % POS_SPLIT, 0)
    kcols[:, 2:NSA_AUG] = live[:, None]
    kcols[:, NSA_AUG] = ~live
    kcols[pos[live], NSA_KW - hd + pos[live] // SEL_LEN] = 1.0
    vrows = np.zeros((2, NSA_VROWS - hd, tt), np.float32)
    vrows[0, 0] = 1.0
    return kcols, vrows


def _kvprep_kernel(ks_ref, vs_ref, kw_ref, vw_ref, kc_ref, vr_ref, ksa_ref, vst_ref, kwa_ref, vwt_ref,
                   *, tt, n_live):
    hd = NSA_HEAD_DIM
    dead = pl.program_id(1) >= n_live
    kcols = kc_ref[...]
    ones_rows = vr_ref[0]
    for g in range(NSA_GROUPS):
        lanes = slice(g * hd, (g + 1) * hd)
        live = lambda ref: jnp.where(dead, 0.0, ref[0][:, lanes])
        ksa_ref[0, g] = jnp.concatenate([live(ks_ref).astype(BF16), kcols], 1)
        kwa_ref[0, g] = jnp.concatenate([live(kw_ref).astype(BF16), kcols[:, 0:NSA_KW - hd]], 1)
        vs_t = jnp.concatenate([live(vs_ref).T.astype(BF16), ones_rows], 0)
        vw_t = jnp.concatenate([live(vw_ref).T.astype(BF16), ones_rows], 0)
        for c in range(tt // SEL_TILE):
            vst_ref[0, g, c] = vs_t[:, c * SEL_TILE:(c + 1) * SEL_TILE]
        for c in range(tt // WIN_TILE):
            vwt_ref[0, g, c] = vw_t[:, c * WIN_TILE:(c + 1) * WIN_TILE]


def _kvprep_call(proj3):
    B, S, _ = proj3.shape
    G, hd = NSA_GROUPS, NSA_HEAD_DIM
    tt = min(KV_PREP_ROWS, S)
    n_live = S // tt
    assert tt == WINDOW
    sp = S + tt
    col = lambda off: pl.BlockSpec((1, tt, LANES), lambda b, t: (b, jnp.minimum(t, n_live - 1), off // LANES))
    front = lambda t: (t + 1) % (n_live + 1)
    kcols, vrows = _kvprep_constants(S, tt)
    return pl.pallas_call(
        functools.partial(_kvprep_kernel, tt=tt, n_live=n_live),
        grid=(B, n_live + 1),
        in_specs=[col(OFF_KS), col(OFF_VS), col(OFF_KW), col(OFF_VW),
                  pl.BlockSpec((tt, kcols.shape[1]), lambda b, t: (t, 0)),
                  pl.BlockSpec((1,) + vrows.shape[1:], lambda b, t: (t // n_live, 0, 0))],
        out_specs=[pl.BlockSpec((1, G, tt, NSA_KW + SEL_ONEHOT), lambda b, t: (b, 0, t, 0)),
                   pl.BlockSpec((1, G, tt // SEL_TILE, NSA_VROWS, SEL_TILE), lambda b, t: (b, 0, t, 0, 0)),
                   pl.BlockSpec((1, G, tt, NSA_KW), lambda b, t: (b, 0, front(t), 0)),
                   pl.BlockSpec((1, G, tt // WIN_TILE, NSA_VROWS, WIN_TILE), lambda b, t: (b, 0, front(t), 0, 0))],
        out_shape=[jax.ShapeDtypeStruct((B, G, sp, NSA_KW + SEL_ONEHOT), BF16),
                   jax.ShapeDtypeStruct((B, G, sp // SEL_TILE, NSA_VROWS, SEL_TILE), BF16),
                   jax.ShapeDtypeStruct((B, G, sp, NSA_KW), BF16),
                   jax.ShapeDtypeStruct((B, G, sp // WIN_TILE, NSA_VROWS, WIN_TILE), BF16)],
        compiler_params=_params("parallel", "parallel"),
        name="nsa_kvprep",
    )(proj3, proj3, proj3, proj3, jnp.asarray(kcols, BF16), jnp.asarray(vrows, BF16))


def _importance_matrix(ns, nc):
    ratio = SEL_LEN // CMP_STRIDE
    a = np.zeros((ns, nc), np.float32)
    for j in range(ns):
        for n, wgt in ((ratio * j - 1, 0.5), (ratio * j, 1.0), (ratio * j + 1, 1.0),
                       (ratio * j + 2, 1.0), (ratio * j + 3, 0.5)):
            if 0 <= n < nc - 1:
                a[j, n] = wgt
    return a


def _plan_constants(ns):
    nt = ns // SEL_TILE_BLOCKS
    grp = (np.arange(ns)[None, :] // SEL_TILE_BLOCKS == np.arange(nt)[:, None]).astype(np.float32)
    k = np.arange(nt)
    w_lo = np.where(k < PLAN_HALF, 2.0 ** np.minimum(k, PLAN_HALF - 1), 0.0)
    w_hi = np.where(k >= PLAN_HALF, 2.0 ** np.maximum(k - PLAN_HALF, 0), 0.0)
    wts = np.stack([np.repeat(w_lo[:, None], LANES, 1), np.repeat(w_hi[:, None], LANES, 1)]).astype(np.float32)
    return grp, wts


def _select_kernel(q_ref, slope_ref, kc_ref, vct_ref, imp_ref, grp_ref, wts_ref, ocmp_ref, bias_ref, bits_ref,
                   *, ns, nc, qb):
    P, T = NSA_PER_GROUP, Q_BLOCK
    step = pl.program_id(2)
    nsteps = pl.num_programs(2)
    subs = range(qb)
    c0s = [(step * qb + u) * T for u in subs]
    qts, tlanes = zip(*[_query_t(q_ref, slope_ref, c0s[u], u * T) for u in subs])

    def body(rows, nblk):
        cend = lax.broadcasted_iota(jnp.int32, (rows, 1), 0) * CMP_STRIDE + (CMP_LEN - 1)
        cmask = [cend <= tlanes[u] for u in subs]
        s = [jnp.where(cmask[u], _dot(kc_ref[0, 0, 0:rows, :], qts[u]), NEG_BIG) for u in subs]
        m = [jnp.max(s[u], 0, keepdims=True) for u in subs]
        p_c = [jnp.where(cmask[u], jnp.exp(s[u] - m[u]), 0.0) for u in subs]
        p_c = [p_c[u] * (1.0 / jnp.maximum(jnp.sum(p_c[u], 0, keepdims=True), 1e-30)) for u in subs]
        for u in subs:
            ocmp_ref[0, 0, u] = _dot(vct_ref[0, 0][:, 0:rows], p_c[u].astype(BF16))

        psum = [p_c[u][:, 0:T] for u in subs]
        for p in range(1, P):
            psum = [psum[u] + p_c[u][:, p * T:(p + 1) * T] for u in subs]
        imp = [_dot_exact_lhs(imp_ref[0:nblk, 0:rows], psum[u]) for u in subs]
        blk = lax.broadcasted_iota(jnp.int32, (nblk, T), 0)
        val, chosen = [], []
        for u in subs:
            cur = (c0s[u] + lax.broadcasted_iota(jnp.int32, (nblk, T), 1)) // SEL_LEN
            forced = (blk == 0) | (blk == cur) | (blk == cur - 1)
            val.append(jnp.where(forced, 3e38, jnp.where(blk > cur, -1.0, imp[u])))
            chosen.append(jnp.zeros((nblk, T), F32))
        for _ in range(min(SEL_TOPK, nblk)):
            for u in subs:
                mx = jnp.max(val[u], 0, keepdims=True)
                first = jnp.min(jnp.where(val[u] == mx, blk, ns), 0, keepdims=True)
                pick = blk == first
                chosen[u] = jnp.where(pick, 1.0, chosen[u])
                val[u] = jnp.where(pick, -2.0, val[u])
        for u in subs:
            bias_ref[0, 0, u, 0:nblk, :] = jnp.where(chosen[u] > 0.5, 0.0, NEG_BIG)
            if nblk < ns:
                bias_ref[0, 0, u, nblk:ns, :] = jnp.full((ns - nblk, T), NEG_BIG, F32)
            ch = chosen[u]
            if nblk < ns:
                ch = jnp.concatenate([ch, jnp.zeros((ns - nblk, T), F32)], 0)
            cnt = _dot(grp_ref[...], ch.astype(BF16))
            act = jnp.where(jnp.max(cnt, 1, keepdims=True) > 0.5, 1.0, 0.0)
            lo = jnp.sum(act * wts_ref[0], 0, keepdims=True).astype(jnp.int32)
            hi = jnp.sum(act * wts_ref[1], 0, keepdims=True).astype(jnp.int32)
            bits_ref[u] = jnp.broadcast_to(lo | (hi << PLAN_HALF), (SUBLANES, LANES))

    parts = max(p for p in (SELECT_PARTS, 2, 1) if nc % (p * LANES) == 0 and ns % (p * 2 * SUBLANES) == 0)
    part = step * parts // nsteps
    for v in range(parts):
        pl.when(part == v)(functools.partial(body, nc * (v + 1) // parts, ns * (v + 1) // parts))


def _attend_kernel(bits_sref, q_ref, gt_ref, slope_ref, ocmp_ref, bias_ref, ks_ref, vst_ref, kw_ref, vwt_ref,
                   o_ref, list_ref, s_ref, m_ref, *, n_tiles_total, ns, qb):
    hd, P, T = NSA_HEAD_DIM, NSA_PER_GROUP, Q_BLOCK
    PT = P * T
    b, g, step = pl.program_id(0), pl.program_id(1), pl.program_id(2)
    subs = range(qb)
    iqs = [step * qb + u for u in subs]
    c0s = [iq * T for iq in iqs]
    qts, tlanes = zip(*[_query_t(q_ref, slope_ref, c0s[u], u * T) for u in subs])
    tl_row = lax.broadcasted_iota(jnp.int32, (1, PT), 1) % T

    gts = [jax.nn.sigmoid(gt_ref[0, u * T:(u + 1) * T, :]).T for u in subs]

    def gate_row(u, branch):
        return jnp.concatenate([gts[u][p * 3 + branch:p * 3 + branch + 1, :] for p in range(P)], 1)

    list_len = n_tiles_total + 1
    last_tiles = [(c0s[u] + T - 1) // SEL_TILE for u in subs]
    prev_bits = []
    for u in subs:
        bits = bits_sref[(b * pl.num_programs(1) + g) * (pl.num_programs(2) * qb) + iqs[u]]
        prev_bits.append(bits & (lax.shift_left(jnp.int32(1), last_tiles[u]) - 1))
    n_prevs = [jnp.int32(0) for _ in subs]
    for k in range(n_tiles_total):
        for u in subs:
            list_ref[u * list_len + n_prevs[u]] = k
            n_prevs[u] = n_prevs[u] + (lax.shift_right_logical(prev_bits[u], k) & 1)

    qt_sels = []
    for u in subs:
        sel_b = bias_ref[0, 0, u]
        if ns < SEL_ONEHOT:
            sel_b = jnp.concatenate([sel_b, jnp.zeros((SEL_ONEHOT - ns, T), F32)], 0)
        qt_sels.append(jnp.concatenate([qts[u], jnp.concatenate([sel_b.astype(BF16)] * P, 1)], 0))

    def tile_scores(u, kt):
        k0 = pl.multiple_of(kt * SEL_TILE, SEL_TILE)
        return _dot(ks_ref[0, 0, pl.ds(k0, SEL_TILE), :], qt_sels[u])

    def col_max(scores):
        m = jnp.max(scores[0], 0, keepdims=True)
        for sc in scores[1:]:
            m = jnp.maximum(m, jnp.max(sc, 0, keepdims=True))
        return m

    def softmax_update(carry, scores, vts, m_tile):
        m_run, acc = carry
        m_new = jnp.maximum(m_run, m_tile)
        acc = jnp.exp(m_run - m_new) * acc
        for sc, vt in zip(scores, vts):
            acc = acc + _dot(vt, jnp.exp(sc - m_new).astype(BF16))
        return m_new, acc

    init = (jnp.full((1, PT), NEG_BIG, F32), jnp.zeros((NSA_VROWS, PT), F32))
    dead_tile = n_tiles_total
    n_pairs = [jnp.maximum((n_prevs[u] + 1) // 2, 1) for u in subs]

    def pair_tiles(u, i):
        has_a, has_b = 2 * i < n_prevs[u], 2 * i + 1 < n_prevs[u]
        ka = jnp.where(has_a, list_ref[u * list_len + jnp.where(has_a, 2 * i, 0)], dead_tile)
        kb = jnp.where(has_b, list_ref[u * list_len + jnp.where(has_b, 2 * i + 1, 0)], dead_tile)
        return ka, kb

    def pair_scores(u, i):
        ka, kb = pair_tiles(u, i)
        keys = jnp.concatenate([ks_ref[0, 0, pl.ds(pl.multiple_of(k * SEL_TILE, SEL_TILE), SEL_TILE), :]
                                for k in (ka, kb)], 0)
        return _dot(keys, qt_sels[u])

    def put_scores(u, i, sc):
        slot = i % 2
        s_ref[u, slot] = sc
        m_ref[u, slot] = col_max([sc])

    def pair_probs(u, i, m_run):
        slot = i % 2
        m_new = jnp.maximum(m_run, m_ref[u, slot])
        return m_new, jnp.exp(s_ref[u, slot] - m_new).astype(BF16)

    def add_values(u, i, carry, m_new, pr):
        ka, kb = pair_tiles(u, i)
        m_run, acc = carry
        vt = jnp.concatenate([vst_ref[0, 0, ka], vst_ref[0, 0, kb]], 1)
        return m_new, jnp.exp(m_run - m_new) * acc + _dot(vt, pr)

    first = [pair_scores(u, 0) for u in subs]
    for u in subs:
        put_scores(u, 0, first[u])

    n_win = (WINDOW + T) // WIN_TILE
    roff = lax.broadcasted_iota(jnp.int32, (WIN_TILE, 1), 0)
    w_sc = [_dot(kw_ref[0, 0, pl.ds(pl.multiple_of(c0s[u], WIN_TILE), WINDOW + T), :], qts[u]) for u in subs]
    w_parts = [[jnp.where(roff > tl_row, w_sc[u][0:WIN_TILE], NEG_BIG),
                w_sc[u][WIN_TILE:WINDOW],
                jnp.where(roff <= tl_row, w_sc[u][WINDOW:WINDOW + T], NEG_BIG)]
               for u in subs]
    w_max = [col_max(w_parts[u]) for u in subs]
    out_cw = []
    for u in subs:
        vt = jnp.concatenate([vwt_ref[0, 0, iqs[u] + j] for j in range(n_win)], 1)
        _, acc_win = softmax_update(init, [jnp.concatenate(w_parts[u], 0)], [vt], w_max[u])
        out_cw.append(gate_row(u, 0) * ocmp_ref[0, 0, u] + gate_row(u, 2) * (acc_win[0:hd] / acc_win[hd:hd + 1]))

    def pipe_step(u, i, carry):
        m_new, pr = pair_probs(u, i, carry[0])
        sc_next = pair_scores(u, i + 1)
        carry = add_values(u, i, carry, m_new, pr)
        put_scores(u, i + 1, sc_next)
        return carry

    carries = [lax.fori_loop(0, n_pairs[u] - 1, functools.partial(pipe_step, u), init) for u in subs]
    probs = [pair_probs(u, n_pairs[u] - 1, carries[u][0]) for u in subs]
    s_last = []
    for u in subs:
        kpos = last_tiles[u] * SEL_TILE + lax.broadcasted_iota(jnp.int32, (SEL_TILE, 1), 0)
        s_last.append(jnp.where(kpos <= tlanes[u], tile_scores(u, last_tiles[u]), NEG_BIG))
    carries = [add_values(u, n_pairs[u] - 1, carries[u], *probs[u]) for u in subs]
    for u in subs:
        _, acc_sel = softmax_update(carries[u], [s_last[u]], [vst_ref[0, 0, last_tiles[u]]], col_max([s_last[u]]))
        out = out_cw[u] + gate_row(u, 1) * (acc_sel[0:hd] / acc_sel[hd:hd + 1])
        o_ref[0, u * T:(u + 1) * T, :] = jnp.concatenate(
            [out[:, p * T:(p + 1) * T] for p in range(P)], 0).T.astype(o_ref.dtype)


def _nsa_call(proj3, pe_k, w1_k, w2_k, pe_v, w1_v, w2_v):
    B, S, _ = proj3.shape
    G, hd, P, T = NSA_GROUPS, NSA_HEAD_DIM, NSA_PER_GROUP, Q_BLOCK
    PT = P * T
    nc = S // CMP_STRIDE
    ns = S // SEL_LEN
    nq = S // T
    nt = ns // SEL_TILE_BLOCKS
    assert nt <= 2 * PLAN_HALF and S % KV_PREP_ROWS == 0

    kc_a, vc_t = _compress_call(proj3, pe_k, w1_k, w2_k, pe_v, w1_v, w2_v)
    ks_a, vs_t, kw_a, vw_t = _kvprep_call(proj3)
    imp_m = jnp.asarray(_importance_matrix(ns, nc), BF16)
    grp, wts = _plan_constants(ns)
    head_slopes = 2.0 ** (-(8.0 / NSA_HEADS) * np.arange(1, NSA_HEADS + 1, dtype=np.float32))
    slopes = jnp.asarray(np.repeat(head_slopes.reshape(G, 1, P), T, axis=-1), F32)

    q_spec = lambda im: pl.BlockSpec((1, T, P * hd), im)
    bg = lambda shp: pl.BlockSpec((1, 1) + shp, lambda b, g, i: (b, g) + tuple(0 for _ in shp))
    const = lambda a: pl.BlockSpec(a.shape, lambda b, g, i: tuple(0 for _ in a.shape))
    qb = SELECT_QB if nq % (SELECT_QB * SELECT_PARTS) == 0 else 1
    nsteps = nq // qb
    ocmp, sel_bias, bits = pl.pallas_call(
        functools.partial(_select_kernel, ns=ns, nc=nc, qb=qb),
        grid=(B, G, nsteps),
        in_specs=[pl.BlockSpec((1, qb * T, P * hd), lambda b, g, i: (b, i, OFF_NQ // (P * hd) + g)),
                  pl.BlockSpec((1, 1, PT), lambda b, g, i: (g, 0, 0)),
                  bg((nc, NSA_KW)), bg((hd, nc)), const(imp_m),
                  pl.BlockSpec(grp.shape, lambda b, g, i: (0, 0)),
                  pl.BlockSpec(wts.shape, lambda b, g, i: (0, 0, 0))],
        out_specs=[pl.BlockSpec((1, 1, qb, hd, PT), lambda b, g, i: (b, g, i, 0, 0)),
                   pl.BlockSpec((1, 1, qb, ns, T), lambda b, g, i: (b, g, i, 0, 0)),
                   pl.BlockSpec((qb, SUBLANES, LANES), lambda b, g, i: ((b * G + g) * nsteps + i, 0, 0))],
        out_shape=[jax.ShapeDtypeStruct((B, G, nq, hd, PT), F32),
                   jax.ShapeDtypeStruct((B, G, nq, ns, T), F32),
                   jax.ShapeDtypeStruct((B * G * nq, SUBLANES, LANES), jnp.int32)],
        compiler_params=_params("parallel", "parallel", "parallel"),
        name="nsa_select",
    )(proj3, slopes, kc_a, vc_t, imp_m, jnp.asarray(grp, BF16), jnp.asarray(wts, F32))

    aq = ATTEND_QB if nq % ATTEND_QB == 0 else 1
    bg2 = lambda shp: pl.BlockSpec((1, 1) + shp, lambda b, g, i, s: (b, g) + tuple(0 for _ in shp))
    return pl.pallas_call(
        functools.partial(_attend_kernel, n_tiles_total=nt, ns=ns, qb=aq),
        grid_spec=pltpu.PrefetchScalarGridSpec(
            num_scalar_prefetch=1,
            grid=(B, G, nq // aq),
            in_specs=[pl.BlockSpec((1, aq * T, P * hd), lambda b, g, i, s: (b, i, OFF_NQ // (P * hd) + g)),
                      pl.BlockSpec((1, aq * T, LANES), lambda b, g, i, s: (b, i, OFF_NG // LANES + g)),
                      pl.BlockSpec((1, 1, PT), lambda b, g, i, s: (g, 0, 0)),
                      pl.BlockSpec((1, 1, aq, hd, PT), lambda b, g, i, s: (b, g, i, 0, 0)),
                      pl.BlockSpec((1, 1, aq, ns, T), lambda b, g, i, s: (b, g, i, 0, 0)),
                      bg2(ks_a.shape[2:]), bg2(vs_t.shape[2:]), bg2(kw_a.shape[2:]), bg2(vw_t.shape[2:])],
            out_specs=pl.BlockSpec((1, aq * T, P * hd), lambda b, g, i, s: (b, i, g)),
            scratch_shapes=[pltpu.SMEM((aq * (nt + 1),), jnp.int32),
                            pltpu.VMEM((aq, 2, 2 * SEL_TILE, PT), F32), pltpu.VMEM((aq, 2, 1, PT), F32)]),
        out_shape=jax.ShapeDtypeStruct((B, S, NSA_WIDTH), BF16),
        compiler_params=_params("parallel", "parallel", "arbitrary"),
        name="nsa_attend",
    )(bits[:, 0, 0], proj3, proj3, slopes, ocmp, sel_bias, ks_a, vs_t, kw_a, vw_t)


def _merge_kernel(x_ref, ya_ref, yb_ref, yc_ref, g0_ref, g1_ref, g2_ref, wa_ref, wb_ref, wc_ref, wo_ref,
                  lg_ref, lb_ref, o_ref):
    gate = lambda r: jax.nn.sigmoid(r[...].astype(F32))
    m = gate(g0_ref) * _dot(ya_ref[...], wa_ref[...])
    m = m + gate(g1_ref) * _dot(yb_ref[...], wb_ref[...])
    m = m + gate(g2_ref) * _dot(yc_ref[...], wc_ref[...])
    z = DEEPNORM_ALPHA * x_ref[...] + _dot(m.astype(BF16), wo_ref[...])
    o_ref[...] = _layer_norm_rows(z, lg_ref[...], lb_ref[...])


def _merge_call(x2, ya, yb, yc, gates, w_hg, w_nsa, w_lru, w_out, ln_g, ln_b, tm=512):
    T, D = x2.shape
    tm = min(tm, T)
    rows = lambda w: pl.BlockSpec((tm, w), lambda i: (i, 0))
    gate = lambda n: pl.BlockSpec((tm, D), lambda i: (i, n))
    full = lambda a: pl.BlockSpec(a.shape, lambda i: (0, 0))
    ws = [w.astype(BF16) for w in (w_hg, w_nsa, w_lru, w_out)]
    return pl.pallas_call(
        _merge_kernel,
        grid=(T // tm,),
        in_specs=[rows(D), rows(HG_WIDTH), rows(NSA_WIDTH), rows(LRU_WIDTH), gate(0), gate(1), gate(2)]
                 + [full(w) for w in ws] + [pl.BlockSpec((1, D), lambda i: (0, 0))] * 2,
        out_specs=rows(D),
        out_shape=jax.ShapeDtypeStruct((T, D), F32),
        compiler_params=_params("parallel"),
        name="merge_out",
    )(x2, ya, yb, yc, gates, gates, gates, *ws, ln_g.reshape(1, D), ln_b.reshape(1, D))


FFN_SPLIT = 1
FFN_PAD = SUBLANES
FFN_SUB = 256


def _ffn_kernel(x_ref, wu_ref, wv_ref, cw_ref, cb_ref, wd_ref, lg_ref, lb_ref, o_ref,
                xb_ref, acc_ref, upad_ref, carry_ref, *, tm, tiles_per_seq):
    i = pl.program_id(0)
    j = pl.program_id(1)

    @pl.when(j == 0)
    def _():
        xb_ref[...] = x_ref[...].astype(BF16)

    xb = xb_ref[...]
    fc = wu_ref.shape[1]
    first = (i % tiles_per_seq) == 0

    @pl.when(first)
    def _():
        upad_ref[0:FFN_PAD, :] = jnp.zeros((FFN_PAD, fc), F32)

    @pl.when(jnp.logical_not(first))
    def _():
        upad_ref[0:FFN_PAD, :] = carry_ref[j]

    @pl.when(j == 0)
    def _():
        acc_ref[...] = jnp.zeros_like(acc_ref)

    for c0 in range(0, fc, FFN_SUB):
        cs = slice(c0, min(c0 + FFN_SUB, fc))
        u = _dot(xb, wu_ref[:, cs])
        v = _dot(xb, wv_ref[:, cs])
        upad_ref[FFN_PAD:FFN_PAD + tm, cs] = u
        cw = cw_ref[:, cs]
        cv = cb_ref[:, cs] + cw[FFN_CONV - 1:FFN_CONV, :] * u
        for k in range(FFN_CONV - 1):
            off = FFN_PAD - (FFN_CONV - 1) + k
            cv = cv + cw[k:k + 1, :] * upad_ref[off:off + tm, cs]
        carry_ref[j, :, cs] = u[tm - FFN_PAD:tm, :]
        h = (jax.nn.gelu(cv) * v).astype(BF16)
        acc_ref[...] += _dot(h, wd_ref[cs, :])

    @pl.when(j == FFN_SPLIT - 1)
    def _():
        z = DEEPNORM_ALPHA * x_ref[...] + acc_ref[...]
        o_ref[...] = _layer_norm_rows(z, lg_ref[...], lb_ref[...])


def _ffn_call(x2, seq_len, w_up, conv_w, conv_b, w_down, ln_g, ln_b, tm=512):
    T, D = x2.shape
    tm = min(tm, seq_len)
    fc = FFN_DIM // FFN_SPLIT
    wu = w_up.astype(BF16)
    wmode = pl.Buffered(1) if FFN_SPLIT == 1 else None
    return pl.pallas_call(
        functools.partial(_ffn_kernel, tm=tm, tiles_per_seq=seq_len // tm),
        grid=(T // tm, FFN_SPLIT),
        in_specs=[pl.BlockSpec((tm, D), lambda i, j: (i, 0)),
                  pl.BlockSpec((D, fc), lambda i, j: (0, j), pipeline_mode=wmode),
                  pl.BlockSpec((D, fc), lambda i, j: (0, FFN_SPLIT + j), pipeline_mode=wmode),
                  pl.BlockSpec((FFN_CONV, fc), lambda i, j: (0, j)),
                  pl.BlockSpec((1, fc), lambda i, j: (0, j)),
                  pl.BlockSpec((fc, D), lambda i, j: (j, 0), pipeline_mode=wmode),
                  pl.BlockSpec((1, D), lambda i, j: (0, 0)),
                  pl.BlockSpec((1, D), lambda i, j: (0, 0))],
        out_specs=pl.BlockSpec((tm, D), lambda i, j: (i, 0)),
        out_shape=jax.ShapeDtypeStruct((T, D), F32),
        scratch_shapes=[pltpu.VMEM((tm, D), BF16), pltpu.VMEM((tm, D), F32),
                        pltpu.VMEM((tm + FFN_PAD, fc), F32), pltpu.VMEM((FFN_SPLIT, FFN_PAD, fc), F32)],
        compiler_params=_params("arbitrary", "arbitrary"),
        name="conv_ffn",
    )(x2, wu, wu, conv_w.astype(F32), conv_b.reshape(1, FFN_DIM).astype(F32), w_down.astype(BF16),
      ln_g.reshape(1, D), ln_b.reshape(1, D))


def _permute_in_proj(w, b):
    src = np.cumsum([0, HG_WIDTH, HG_WIDTH, HG_WIDTH, HG_WIDTH, NSA_WIDTH, NSA_KV, NSA_KV, NSA_KV, NSA_KV,
                     NSA_KV, NSA_KV, NSA_HEADS * 3, LRU_WIDTH, LRU_WIDTH, N_BRANCH * 1024])
    names = ["hq", "hf", "hi", "hg", "nq", "kc", "vc", "ks", "vs", "kw", "vw", "ng", "lx", "ly", "mg"]
    seg = {n: (int(src[k]), int(src[k + 1])) for k, n in enumerate(names)}
    order = ["mg", "hq", "hf", "hi", "hg", "nq", "lx", "ly", "kc", "vc", "ks", "vs", "kw", "vw"]
    per_group = NSA_PER_GROUP * 3
    ng0 = seg["ng"][0]
    w = w.astype(BF16)
    w_parts = [w[:, seg[n][0]:seg[n][1]] for n in order]
    b_parts = [b[seg[n][0]:seg[n][1]] for n in order]
    for gi in range(NSA_GROUPS):
        lo = ng0 + gi * per_group
        w_parts += [w[:, lo:lo + per_group], jnp.zeros((w.shape[0], LANES - per_group), w.dtype)]
        b_parts += [b[lo:lo + per_group], jnp.zeros((LANES - per_group,), b.dtype)]
    wp = jnp.concatenate(w_parts, 1)
    bp = jnp.concatenate(b_parts)
    assert wp.shape[1] == GATE_COLS + PROJ_COLS
    return wp.astype(BF16), bp.astype(F32)


def kernel(x, ln_emb_g, ln_emb_b, w_in, b_in, hg_lb_logits, hg_norm_g, cmp_pe_k, cmp_w1_k, cmp_w2_k, cmp_pe_v, cmp_w1_v, cmp_w2_v, lru_conv_w, lru_conv_b, lru_wa, lru_ba, lru_wx, lru_bx, lru_lambda, w_branch_hg, w_branch_nsa, w_branch_lru, w_out, ln1_g, ln1_b, ffn_w_up, ffn_conv_w, ffn_conv_b, ffn_w_down, ln2_g, ln2_b):
    B, S, D = x.shape
    T = B * S
    gam = jax.nn.softmax(hg_lb_logits.astype(F32), axis=0)
    lb_all = jnp.cumsum(gam, axis=0) - gam[0]
    h = x.reshape(T, D)
    for l in range(DEPTH):
        wp, bp = _permute_in_proj(w_in[l], b_in[l])
        if l == 0:
            gates, proj2, h = _inproj_call(h, wp, bp, ln=(ln_emb_g, ln_emb_b))
        else:
            gates, proj2 = _inproj_call(h, wp, bp)
        proj3 = proj2.reshape(B, S, PROJ_COLS)
        y_a = _hgrn_call(proj3, lb_all[l], hg_norm_g[l])
        y_b = _nsa_call(proj3, cmp_pe_k[l], cmp_w1_k[l], cmp_w2_k[l], cmp_pe_v[l], cmp_w1_v[l], cmp_w2_v[l])
        y_c = _lru_call(proj3, lru_conv_w[l], lru_conv_b[l], lru_wa[l], lru_ba[l], lru_wx[l], lru_bx[l],
                        lru_lambda[l])
        h = _merge_call(h, y_a.reshape(T, HG_WIDTH), y_b.reshape(T, NSA_WIDTH), y_c.reshape(T, LRU_WIDTH),
                        gates, w_branch_hg[l], w_branch_nsa[l], w_branch_lru[l], w_out[l], ln1_g[l], ln1_b[l])
        h = _ffn_call(h, S, ffn_w_up[l], ffn_conv_w[l], ffn_conv_b[l], ffn_w_down[l], ln2_g[l], ln2_b[l])
    return h.reshape(B, S, D)
```

```python
import functools

import numpy as np
import jax
import jax.numpy as jnp
from jax import lax
from jax.experimental import pallas as pl
from jax.experimental.pallas import tpu as pltpu

F32 = jnp.float32
BF16 = jnp.bfloat16

DEPTH = 2
HG_HEADS = 4
HG_HEAD_DIM = 128
HG_WIDTH = HG_HEADS * HG_HEAD_DIM
NSA_HEADS = 8
NSA_GROUPS = 2
NSA_PER_GROUP = NSA_HEADS // NSA_GROUPS
NSA_HEAD_DIM = 64
NSA_WIDTH = NSA_HEADS * NSA_HEAD_DIM
NSA_KV = NSA_GROUPS * NSA_HEAD_DIM
CMP_LEN = 32
CMP_STRIDE = 16
CMP_HIDDEN = 64
SEL_LEN = 64
SEL_TOPK = 16
WINDOW = 512
Q_BLOCK = 128
LRU_WIDTH = 512
LRU_BLOCKS = 4
LRU_CONV = 4
LRU_C = 8.0
FFN_DIM = 2816
FFN_CONV = 3
DEEPNORM_ALPHA = (2 * DEPTH) ** 0.25
LN_EPS = 1e-5
N_BRANCH = 3

LANES = 128
SUBLANES = 8
VMEM_LIMIT_BYTES = 52 * 1024 * 1024

GATE_COLS = N_BRANCH * 1024
OFF_HQ = 0
OFF_HF = OFF_HQ + HG_WIDTH
OFF_HI = OFF_HF + HG_WIDTH
OFF_HG = OFF_HI + HG_WIDTH
OFF_NQ = OFF_HG + HG_WIDTH
OFF_LX = OFF_NQ + NSA_WIDTH
OFF_LY = OFF_LX + LRU_WIDTH
OFF_KC = OFF_LY + LRU_WIDTH
OFF_VC = OFF_KC + NSA_KV
OFF_KS = OFF_VC + NSA_KV
OFF_VS = OFF_KS + NSA_KV
OFF_KW = OFF_VS + NSA_KV
OFF_VW = OFF_KW + NSA_KV
OFF_NG = OFF_VW + NSA_KV
PROJ_TILE_N = 1536
PROJ_COLS = OFF_NG + NSA_GROUPS * LANES

NEG_BIG = -1e30


def _split3(x):
    hi = x.astype(BF16)
    r1 = x - hi.astype(F32)
    mid = r1.astype(BF16)
    lo = (r1 - mid.astype(F32)).astype(BF16)
    return hi, mid, lo


def _dot(a, b):
    return jnp.dot(a, b, preferred_element_type=F32)


def _dot_nt(a, b):
    return lax.dot_general(a, b, (((1,), (1,)), ((), ())), preferred_element_type=F32)


def _dot_exact_lhs(m_bf16, x):
    hi, mid, lo = _split3(x)
    return _dot(m_bf16, hi) + _dot(m_bf16, mid) + _dot(m_bf16, lo)


def _dot_hilo(a, b):
    ah = a.astype(BF16)
    al = (a - ah.astype(F32)).astype(BF16)
    bh = b.astype(BF16)
    bl = (b - bh.astype(F32)).astype(BF16)
    return _dot(ah, bh) + _dot(ah, bl) + _dot(al, bh) + _dot(al, bl)


def _layer_norm_rows(z, g, b):
    mu = jnp.mean(z, -1, keepdims=True)
    zc = z - mu
    var = jnp.mean(zc * zc, -1, keepdims=True)
    return zc * lax.rsqrt(var + LN_EPS) * g + b


def _params(*sem):
    return pltpu.CompilerParams(dimension_semantics=sem, vmem_limit_bytes=VMEM_LIMIT_BYTES)


def _inproj_kernel(*refs, with_ln):
    if with_ln:
        x_ref, g_ref, be_ref, w_ref, b_ref, gate_ref, o_ref, xn_ref = refs
    else:
        x_ref, w_ref, b_ref, gate_ref, o_ref = refs
    x = x_ref[...]
    if with_ln:
        x = _layer_norm_rows(x, g_ref[...], be_ref[...])
        xn_ref[...] = x
    xb = x.astype(BF16)
    n = w_ref.shape[1]
    for c0 in range(0, n, PROJ_TILE_N):
        cs = slice(c0, c0 + PROJ_TILE_N)
        acc = _dot(xb, w_ref[:, cs]) + b_ref[:, cs]
        if c0 < GATE_COLS:
            gate_ref[:, cs] = acc.astype(BF16)
        else:
            o_ref[:, c0 - GATE_COLS:c0 - GATE_COLS + PROJ_TILE_N] = acc


def _inproj_call(x2, w_bf16, bias, ln=None, tm=512):
    T, D = x2.shape
    N = w_bf16.shape[1]
    tm = min(tm, T)
    assert GATE_COLS % PROJ_TILE_N == 0 and N % PROJ_TILE_N == 0
    vec = pl.BlockSpec((1, D), lambda i: (0, 0))
    in_specs = [pl.BlockSpec((tm, D), lambda i: (i, 0))] + ([vec, vec] if ln else []) + [
        pl.BlockSpec((D, N), lambda i: (0, 0), pipeline_mode=pl.Buffered(1)),
        pl.BlockSpec((1, N), lambda i: (0, 0))]
    out_specs = [pl.BlockSpec((tm, GATE_COLS), lambda i: (i, 0)),
                 pl.BlockSpec((tm, N - GATE_COLS), lambda i: (i, 0))]
    out_shape = [jax.ShapeDtypeStruct((T, GATE_COLS), BF16), jax.ShapeDtypeStruct((T, N - GATE_COLS), F32)]
    if ln:
        out_specs.append(pl.BlockSpec((tm, D), lambda i: (i, 0)))
        out_shape.append(jax.ShapeDtypeStruct((T, D), F32))
    ln_args = [ln[0].reshape(1, D), ln[1].reshape(1, D)] if ln else []
    return pl.pallas_call(
        functools.partial(_inproj_kernel, with_ln=bool(ln)),
        grid=(T // tm,),
        in_specs=in_specs, out_specs=out_specs, out_shape=out_shape,
        compiler_params=_params("parallel"),
        name="in_proj_ln" if ln else "in_proj",
    )(x2, *ln_args, w_bf16, bias.reshape(1, N))


HG_CHUNK = 64
HG_DIAG = SUBLANES
HG_HEADS_PER_STEP = 4


def _hgrn_constants(C):
    r = np.arange(C)
    tri = (r[None, :] <= r[:, None]).astype(np.float32)
    mats = [tri]
    masks = []
    w = HG_DIAG
    while 2 * w <= C:
        mid = (r // (2 * w)) * (2 * w) + w
        mats.append(tri - (r[None, :] <= mid[:, None]).astype(np.float32))
        same = (r[:, None] // (2 * w)) == (r[None, :] // (2 * w))
        masks.append((same & ((r[:, None] % (2 * w)) >= w) & ((r[None, :] % (2 * w)) < w)).astype(np.float32))
        w *= 2
    mats.append((r[None, :] > r[:, None]).astype(np.float32))
    return np.concatenate(mats, 0), np.stack(masks, 0)


def _hgrn_kernel(q_ref, f_ref, i_ref, g_ref, lb_ref, ng_ref, m_ref, mask_ref, o_ref, st_ref, *, C, nchunk, nh):
    d = HG_HEAD_DIM
    nlev = mask_ref.shape[0]

    @pl.when(pl.program_id(2) == 0)
    def _():
        st_ref[...] = jnp.zeros_like(st_ref)

    hs = range(nh)
    lane = [slice(h * d, (h + 1) * d) for h in hs]
    lb = [lb_ref[h] for h in hs]
    ng = [ng_ref[h] for h in hs]
    nb = C // HG_DIAG
    row3 = lax.broadcasted_iota(jnp.int32, (nb, HG_DIAG, d), 1)

    fl_all = f_ref[0]
    lf_all = [jnp.log(lb[h] + (1.0 - lb[h]) * jax.nn.sigmoid(fl_all[:, lane[h]])) for h in hs]
    kk_all = [(1.0 - lb[h]) * jax.nn.sigmoid(-fl_all[:, lane[h]]) for h in hs]
    allm_all = _dot_exact_lhs(m_ref[...], jnp.concatenate(
        [lf_all[h][c * C:(c + 1) * C] for h in hs for c in range(nchunk)], 1))

    for c in range(nchunk):
        sl = pl.ds(c * C, C)
        q = [q_ref[0, sl, lane[h]] for h in hs]
        iv = [i_ref[0, sl, lane[h]] for h in hs]
        kk = [kk_all[h][c * C:(c + 1) * C] for h in hs]
        allm = [allm_all[:, (h * nchunk + c) * d:(h * nchunk + c + 1) * d] for h in hs]
        bcs = [allm[h][0:C] for h in hs]
        rem = [allm[h][(nlev + 1) * C:(nlev + 2) * C] for h in hs]
        iv_b = [iv[h].astype(BF16) for h in hs]

        att = [jnp.zeros((C, C), F32) for _ in hs]
        for l in range(nlev):
            e = [jnp.exp(-jnp.abs(allm[h][(1 + l) * C:(2 + l) * C])) for h in hs]
            att = [att[h] + mask_ref[l] * _dot_nt((q[h] * e[h]).astype(BF16), (kk[h] * e[h]).astype(BF16))
                   for h in hs]
        o = [_dot(att[h].astype(BF16), iv_b[h]) for h in hs]

        q3 = [q[h].reshape(nb, HG_DIAG, d) for h in hs]
        k3 = [kk[h].reshape(nb, HG_DIAG, d) for h in hs]
        b3 = [bcs[h].reshape(nb, HG_DIAG, d) for h in hs]
        i3 = [iv[h].reshape(nb, HG_DIAG, d) for h in hs]
        acc = [jnp.zeros((nb, HG_DIAG, d), F32) for _ in hs]
        for s in range(HG_DIAG):
            for h in hs:
                dec = jnp.where(row3 >= s, jnp.exp(b3[h] - b3[h][:, s:s + 1, :]), 0.0)
                a = jnp.sum(q3[h] * dec * k3[h][:, s:s + 1, :], axis=-1, keepdims=True)
                acc[h] = acc[h] + a * i3[h][:, s:s + 1, :]

        for h in hs:
            st = st_ref[h]
            oh = o[h] + acc[h].reshape(C, d) + _dot_nt((q[h] * jnp.exp(bcs[h])).astype(BF16), st.astype(BF16))
            kdec = (kk[h] * jnp.exp(rem[h])).astype(BF16)
            st_ref[h] = st * jnp.exp(bcs[h][C - 1:C, :]) + _dot(iv[h].T.astype(BF16), kdec)
            oh = oh * lax.rsqrt(jnp.mean(oh * oh, -1, keepdims=True) + 1e-6) * ng[h]
            g = g_ref[0, sl, lane[h]]
            o_ref[0, sl, lane[h]] = (oh * (g * jax.nn.sigmoid(g))).astype(o_ref.dtype)


def _hgrn_call(proj3, lb, norm_g, tt=512):
    B, S, _ = proj3.shape
    d = HG_HEAD_DIM
    C = HG_CHUNK
    tt = min(tt, S)
    mstack, masks = _hgrn_constants(C)
    nm = mstack.shape[0]
    nh = HG_HEADS_PER_STEP
    wid = nh * d
    col = lambda off: (lambda b, h, t: (b, t, off // wid + h))
    blk = (1, tt, wid)
    return pl.pallas_call(
        functools.partial(_hgrn_kernel, C=C, nchunk=tt // C, nh=nh),
        grid=(B, HG_HEADS // nh, S // tt),
        in_specs=[pl.BlockSpec(blk, col(OFF_HQ)), pl.BlockSpec(blk, col(OFF_HF)),
                  pl.BlockSpec(blk, col(OFF_HI)), pl.BlockSpec(blk, col(OFF_HG)),
                  pl.BlockSpec((nh, 1, d), lambda b, h, t: (h, 0, 0)),
                  pl.BlockSpec((nh, 1, d), lambda b, h, t: (h, 0, 0)),
                  pl.BlockSpec((nm, C), lambda b, h, t: (0, 0)),
                  pl.BlockSpec(masks.shape, lambda b, h, t: (0, 0, 0))],
        out_specs=pl.BlockSpec(blk, lambda b, h, t: (b, t, h)),
        out_shape=jax.ShapeDtypeStruct((B, S, HG_WIDTH), BF16),
        scratch_shapes=[pltpu.VMEM((nh, d, d), F32)],
        compiler_params=_params("parallel", "parallel", "arbitrary"),
        name="hgrn2",
    )(proj3, proj3, proj3, proj3, lb.reshape(HG_HEADS, 1, d), norm_g.reshape(HG_HEADS, 1, d),
      jnp.asarray(mstack, BF16), jnp.asarray(masks, F32))


LRU_PAD = SUBLANES


def _lru_kernel(x_ref, y_ref, cw_ref, cb_ref, wa_ref, ba_ref, wx_ref, bx_ref, c_ref, ex_ref, o_ref,
                xpad_ref, h_ref, a_ref, u_ref, *, tt):
    W = LRU_WIDTH
    bw = W // LRU_BLOCKS

    @pl.when(pl.program_id(1) == 0)
    def _():
        xpad_ref[0:LRU_PAD, :] = jnp.zeros((LRU_PAD, W), F32)
        h_ref[...] = jnp.zeros_like(h_ref)

    x = x_ref[0]
    xpad_ref[LRU_PAD:LRU_PAD + tt, :] = x
    cw = cw_ref[...]
    xc = cb_ref[...] + cw[LRU_CONV - 1:LRU_CONV, :] * x
    for j in range(LRU_CONV - 1):
        off = LRU_PAD - (LRU_CONV - 1) + j
        xc = xc + cw[j:j + 1, :] * xpad_ref[off:off + tt, :]
    xpad_ref[0:LRU_PAD, :] = x[tt - LRU_PAD:tt, :]

    rs, is_ = [], []
    for gi in range(LRU_BLOCKS):
        xg = xc[:, gi * bw:(gi + 1) * bw].astype(BF16)
        rs.append(_dot(xg, wa_ref[gi]))
        is_.append(_dot(xg, wx_ref[gi]))
    r = jax.nn.sigmoid(jnp.concatenate(rs, -1) + ba_ref[...])
    ig = jax.nn.sigmoid(jnp.concatenate(is_, -1) + bx_ref[...])
    log_a = c_ref[...] * r
    a = jnp.exp(log_a)
    u = jnp.sqrt(-jnp.tanh(log_a) * (a * a + 1.0)) * (ig * xc)

    def scan_rows(a, u, n, group):
        idx = lax.broadcasted_iota(jnp.int32, (n, W), 0) % group
        dlt = 1
        while dlt < group:
            keep = idx >= dlt
            a_sh = jnp.where(keep, pltpu.roll(a, dlt, 0), 1.0)
            u_sh = jnp.where(keep, pltpu.roll(u, dlt, 0), 0.0)
            u = a * u_sh + u
            a = a * a_sh
            dlt *= 2
        return a, u

    a, u = scan_rows(a, u, tt, SUBLANES)
    ng = tt // SUBLANES
    last = pl.ds(SUBLANES - 1, ng, stride=SUBLANES)
    ends = []
    for ref, val in ((a_ref, a), (u_ref, u)):
        for cb in range(W // LANES):
            ref[cb] = val[:, cb * LANES:(cb + 1) * LANES]
        ends.append(jnp.concatenate([ref[cb, last, :] for cb in range(W // LANES)], 1))
    ae, ue = scan_rows(ends[0], ends[1], ng, ng)
    h0 = h_ref[...]
    h_end = ue + ae * h0
    gidx = lax.broadcasted_iota(jnp.int32, (ng, W), 0)
    h_in = jnp.where(gidx >= 1, pltpu.roll(h_end, 1, 0), h0)
    h = u + a * _dot_exact_lhs(ex_ref[...], h_in)
    o_ref[0] = (h * jax.nn.gelu(y_ref[0])).astype(o_ref.dtype)
    h_ref[...] = h_end[ng - 1:ng, :]


def _lru_call(proj3, conv_w, conv_b, wa, ba, wx, bx, lam, tt=512):
    B, S, _ = proj3.shape
    W = LRU_WIDTH
    bw = W // LRU_BLOCKS
    tt = min(tt, S)
    c = (-LRU_C * jax.nn.softplus(-lam.astype(F32))).reshape(1, W)
    expand = (np.arange(tt)[:, None] // SUBLANES == np.arange(tt // SUBLANES)[None, :]).astype(np.float32)
    vec = lambda: pl.BlockSpec((1, W), lambda b, t: (0, 0))
    return pl.pallas_call(
        functools.partial(_lru_kernel, tt=tt),
        grid=(B, S // tt),
        in_specs=[pl.BlockSpec((1, tt, W), lambda b, t: (b, t, OFF_LX // W)),
                  pl.BlockSpec((1, tt, W), lambda b, t: (b, t, OFF_LY // W)),
                  pl.BlockSpec((LRU_CONV, W), lambda b, t: (0, 0)), vec(),
                  pl.BlockSpec((LRU_BLOCKS, bw, bw), lambda b, t: (0, 0, 0)), vec(),
                  pl.BlockSpec((LRU_BLOCKS, bw, bw), lambda b, t: (0, 0, 0)), vec(), vec(),
                  pl.BlockSpec(expand.shape, lambda b, t: (0, 0))],
        out_specs=pl.BlockSpec((1, tt, W), lambda b, t: (b, t, 0)),
        out_shape=jax.ShapeDtypeStruct((B, S, W), BF16),
        scratch_shapes=[pltpu.VMEM((tt + LRU_PAD, W), F32), pltpu.VMEM((1, W), F32),
                        pltpu.VMEM((W // LANES, tt, LANES), F32), pltpu.VMEM((W // LANES, tt, LANES), F32)],
        compiler_params=_params("parallel", "arbitrary"),
        name="rglru",
    )(proj3, proj3, conv_w.astype(F32), conv_b.reshape(1, W).astype(F32), wa.astype(BF16),
      ba.reshape(1, W), wx.astype(BF16), bx.reshape(1, W), c, jnp.asarray(expand, BF16))


NSA_AUG = 4
NSA_KW = LANES
SEL_TILE = 256
SEL_TILE_BLOCKS = SEL_TILE // SEL_LEN
WIN_TILE = 128
POS_SPLIT = 128
SEL_ONEHOT = LANES
PLAN_HALF = 16
SELECT_PARTS = 4
SELECT_QB = 8
ATTEND_QB = 4
NSA_VROWS = NSA_HEAD_DIM + 16
KV_PREP_ROWS = 512


def _key_aug(pos, width):
    n = pos.shape[0]
    col = lax.broadcasted_iota(jnp.int32, (n, width), 1)
    hi = ((pos // POS_SPLIT) * POS_SPLIT).astype(F32)
    lo = (pos % POS_SPLIT).astype(F32)
    return jnp.where(col == 0, hi, jnp.where(col == 1, lo, jnp.where(col < NSA_AUG, 1.0, 0.0)))


def _query_t(q_ref, slope_ref, c0, row0=0):
    hd, P, T = NSA_HEAD_DIM, NSA_PER_GROUP, Q_BLOCK
    PT = P * T
    qT = (q_ref[0, row0:row0 + T, :] * (hd ** -0.5)).T
    qpart = jnp.concatenate([qT[p * hd:(p + 1) * hd, :] for p in range(P)], 1)
    tlane = c0 + lax.broadcasted_iota(jnp.int32, (1, PT), 1) % T
    slope = slope_ref[0]
    t_hi = ((tlane // POS_SPLIT) * POS_SPLIT).astype(F32)
    t_lo = (tlane % POS_SPLIT).astype(F32)
    rowi = lax.broadcasted_iota(jnp.int32, (NSA_KW - hd, PT), 0)
    aug = jnp.where(rowi < 2, slope,
                    jnp.where(rowi == 2, -slope * t_hi,
                              jnp.where(rowi == 3, -slope * t_lo, jnp.where(rowi == NSA_AUG, NEG_BIG, 0.0))))
    return jnp.concatenate([qpart, aug], 0).astype(BF16), tlane


def _compress_kernel(xk_ref, xv_ref, w1k_ref, pek_ref, w1fk_ref, w2k_ref, w1v_ref, pev_ref, w1fv_ref, w2v_ref,
                     kc_ref, vct_ref, *, nc):
    hd = NSA_HEAD_DIM
    pos = lax.broadcasted_iota(jnp.int32, (nc, 1), 0) * CMP_STRIDE + (CMP_LEN - 1)
    aug = _key_aug(pos, NSA_KW - hd)

    def one(x_ref, w1_ref, pe_ref, w1f_ref, w2_ref, g):
        uv = jnp.zeros((nc, 2 * CMP_HIDDEN), F32)
        for r in range(CMP_STRIDE):
            xr = x_ref[0, pl.ds(r, nc, stride=CMP_STRIDE), :][:, g * hd:(g + 1) * hd]
            uv = uv + _dot(xr.astype(BF16), w1_ref[r])
        cvec = _dot_hilo(pe_ref[...], w1f_ref[...])[0:1, :]
        hid = uv[:, 0:CMP_HIDDEN] + pltpu.roll(uv[:, CMP_HIDDEN:], nc - 1, 0) + cvec
        return _dot_hilo(jax.nn.gelu(hid), w2_ref[...])

    for g in range(NSA_GROUPS):
        kc = one(xk_ref, w1k_ref, pek_ref, w1fk_ref, w2k_ref, g)
        kc_ref[0, g] = jnp.concatenate([kc, aug], 1).astype(BF16)
        vct_ref[0, g] = one(xv_ref, w1v_ref, pev_ref, w1fv_ref, w2v_ref, g).T.astype(BF16)


def _compress_call(proj3, pe_k, w1_k, w2_k, pe_v, w1_v, w2_v):
    B, S, _ = proj3.shape
    G, hd = NSA_GROUPS, NSA_HEAD_DIM
    nc = S // CMP_STRIDE
    half = CMP_STRIDE * hd

    def prep(pe, w1):
        w1 = w1.astype(F32)
        wr = jnp.concatenate([w1[:half].reshape(CMP_STRIDE, hd, CMP_HIDDEN),
                              w1[half:].reshape(CMP_STRIDE, hd, CMP_HIDDEN)], -1)
        pe8 = jnp.broadcast_to(pe.reshape(1, CMP_LEN * hd).astype(F32), (SUBLANES, CMP_LEN * hd))
        return wr.astype(BF16), pe8, w1

    full = lambda a: pl.BlockSpec(a.shape, lambda b: tuple(0 for _ in a.shape))
    col = lambda off: pl.BlockSpec((1, S, LANES), lambda b: (b, 0, off // LANES))
    args = prep(pe_k, w1_k) + (w2_k.astype(F32),) + prep(pe_v, w1_v) + (w2_v.astype(F32),)
    return pl.pallas_call(
        functools.partial(_compress_kernel, nc=nc),
        grid=(B,),
        in_specs=[col(OFF_KC), col(OFF_VC)] + [full(a) for a in args],
        out_specs=[pl.BlockSpec((1, G, nc, NSA_KW), lambda b: (b, 0, 0, 0)),
                   pl.BlockSpec((1, G, hd, nc), lambda b: (b, 0, 0, 0))],
        out_shape=[jax.ShapeDtypeStruct((B, G, nc, NSA_KW), BF16),
                   jax.ShapeDtypeStruct((B, G, hd, nc), BF16)],
        compiler_params=_params("parallel"),
        name="nsa_compress",
    )(proj3, proj3, *args)


def _kvprep_constants(S, tt):
    hd = NSA_HEAD_DIM
    pos = np.arange(S + tt)
    live = pos < S
    kcols = np.zeros((S + tt, NSA_KW - hd + SEL_ONEHOT), np.float32)
    kcols[:, 0] = np.where(live, (pos // POS_SPLIT) * POS_SPLIT, 0)
    kcols[:, 1] = np.where(live, pos % POS_SPLIT, 0)
    kcols[:, 2:NSA_AUG] = live[:, None]
    kcols[:, NSA_AUG] = ~live
    kcols[pos[live], NSA_KW - hd + pos[live] // SEL_LEN] = 1.0
    vrows = np.zeros((2, NSA_VROWS - hd, tt), np.float32)
    vrows[0, 0] = 1.0
    return kcols, vrows


def _kvprep_kernel(ks_ref, vs_ref, kw_ref, vw_ref, kc_ref, vr_ref, ksa_ref, vst_ref, kwa_ref, vwt_ref,
                   *, tt, n_live):
    hd = NSA_HEAD_DIM
    dead = pl.program_id(1) >= n_live
    kcols = kc_ref[...]
    ones_rows = vr_ref[0]
    for g in range(NSA_GROUPS):
        lanes = slice(g * hd, (g + 1) * hd)
        live = lambda ref: jnp.where(dead, 0.0, ref[0][:, lanes])
        ksa_ref[0, g] = jnp.concatenate([live(ks_ref).astype(BF16), kcols], 1)
        kwa_ref[0, g] = jnp.concatenate([live(kw_ref).astype(BF16), kcols[:, 0:NSA_KW - hd]], 1)
        vs_t = jnp.concatenate([live(vs_ref).T.astype(BF16), ones_rows], 0)
        vw_t = jnp.concatenate([live(vw_ref).T.astype(BF16), ones_rows], 0)
        for c in range(tt // SEL_TILE):
            vst_ref[0, g, c] = vs_t[:, c * SEL_TILE:(c + 1) * SEL_TILE]
        for c in range(tt // WIN_TILE):
            vwt_ref[0, g, c] = vw_t[:, c * WIN_TILE:(c + 1) * WIN_TILE]


def _kvprep_call(proj3):
    B, S, _ = proj3.shape
    G, hd = NSA_GROUPS, NSA_HEAD_DIM
    tt = min(KV_PREP_ROWS, S)
    n_live = S // tt
    assert tt == WINDOW
    sp = S + tt
    col = lambda off: pl.BlockSpec((1, tt, LANES), lambda b, t: (b, jnp.minimum(t, n_live - 1), off // LANES))
    front = lambda t: (t + 1) % (n_live + 1)
    kcols, vrows = _kvprep_constants(S, tt)
    return pl.pallas_call(
        functools.partial(_kvprep_kernel, tt=tt, n_live=n_live),
        grid=(B, n_live + 1),
        in_specs=[col(OFF_KS), col(OFF_VS), col(OFF_KW), col(OFF_VW),
                  pl.BlockSpec((tt, kcols.shape[1]), lambda b, t: (t, 0)),
                  pl.BlockSpec((1,) + vrows.shape[1:], lambda b, t: (t // n_live, 0, 0))],
        out_specs=[pl.BlockSpec((1, G, tt, NSA_KW + SEL_ONEHOT), lambda b, t: (b, 0, t, 0)),
                   pl.BlockSpec((1, G, tt // SEL_TILE, NSA_VROWS, SEL_TILE), lambda b, t: (b, 0, t, 0, 0)),
                   pl.BlockSpec((1, G, tt, NSA_KW), lambda b, t: (b, 0, front(t), 0)),
                   pl.BlockSpec((1, G, tt // WIN_TILE, NSA_VROWS, WIN_TILE), lambda b, t: (b, 0, front(t), 0, 0))],
        out_shape=[jax.ShapeDtypeStruct((B, G, sp, NSA_KW + SEL_ONEHOT), BF16),
                   jax.ShapeDtypeStruct((B, G, sp // SEL_TILE, NSA_VROWS, SEL_TILE), BF16),
                   jax.ShapeDtypeStruct((B, G, sp, NSA_KW), BF16),
                   jax.ShapeDtypeStruct((B, G, sp // WIN_TILE, NSA_VROWS, WIN_TILE), BF16)],
        compiler_params=_params("parallel", "parallel"),
        name="nsa_kvprep",
    )(proj3, proj3, proj3, proj3, jnp.asarray(kcols, BF16), jnp.asarray(vrows, BF16))


def _importance_matrix(ns, nc):
    ratio = SEL_LEN // CMP_STRIDE
    a = np.zeros((ns, nc), np.float32)
    for j in range(ns):
        for n, wgt in ((ratio * j - 1, 0.5), (ratio * j, 1.0), (ratio * j + 1, 1.0),
                       (ratio * j + 2, 1.0), (ratio * j + 3, 0.5)):
            if 0 <= n < nc - 1:
                a[j, n] = wgt
    return a


def _plan_constants(ns):
    nt = ns // SEL_TILE_BLOCKS
    grp = (np.arange(ns)[None, :] // SEL_TILE_BLOCKS == np.arange(nt)[:, None]).astype(np.float32)
    k = np.arange(nt)
    w_lo = np.where(k < PLAN_HALF, 2.0 ** np.minimum(k, PLAN_HALF - 1), 0.0)
    w_hi = np.where(k >= PLAN_HALF, 2.0 ** np.maximum(k - PLAN_HALF, 0), 0.0)
    wts = np.stack([np.repeat(w_lo[:, None], LANES, 1), np.repeat(w_hi[:, None], LANES, 1)]).astype(np.float32)
    return grp, wts


def _select_kernel(q_ref, slope_ref, kc_ref, vct_ref, imp_ref, grp_ref, wts_ref, ocmp_ref, bias_ref, bits_ref,
                   *, ns, nc, qb):
    P, T = NSA_PER_GROUP, Q_BLOCK
    step = pl.program_id(2)
    nsteps = pl.num_programs(2)
    subs = range(qb)
    c0s = [(step * qb + u) * T for u in subs]
    qts, tlanes = zip(*[_query_t(q_ref, slope_ref, c0s[u], u * T) for u in subs])

    def body(rows, nblk):
        cend = lax.broadcasted_iota(jnp.int32, (rows, 1), 0) * CMP_STRIDE + (CMP_LEN - 1)
        cmask = [cend <= tlanes[u] for u in subs]
        s = [jnp.where(cmask[u], _dot(kc_ref[0, 0, 0:rows, :], qts[u]), NEG_BIG) for u in subs]
        m = [jnp.max(s[u], 0, keepdims=True) for u in subs]
        p_c = [jnp.where(cmask[u], jnp.exp(s[u] - m[u]), 0.0) for u in subs]
        p_c = [p_c[u] * (1.0 / jnp.maximum(jnp.sum(p_c[u], 0, keepdims=True), 1e-30)) for u in subs]
        for u in subs:
            ocmp_ref[0, 0, u] = _dot(vct_ref[0, 0][:, 0:rows], p_c[u].astype(BF16))

        psum = [p_c[u][:, 0:T] for u in subs]
        for p in range(1, P):
            psum = [psum[u] + p_c[u][:, p * T:(p + 1) * T] for u in subs]
        imp = [_dot_exact_lhs(imp_ref[0:nblk, 0:rows], psum[u]) for u in subs]
        blk = lax.broadcasted_iota(jnp.int32, (nblk, T), 0)
        val, chosen = [], []
        for u in subs:
            cur = (c0s[u] + lax.broadcasted_iota(jnp.int32, (nblk, T), 1)) // SEL_LEN
            forced = (blk == 0) | (blk == cur) | (blk == cur - 1)
            val.append(jnp.where(forced, 3e38, jnp.where(blk > cur, -1.0, imp[u])))
            chosen.append(jnp.zeros((nblk, T), F32))
        for _ in range(min(SEL_TOPK, nblk)):
            for u in subs:
                mx = jnp.max(val[u], 0, keepdims=True)
                first = jnp.min(jnp.where(val[u] == mx, blk, ns), 0, keepdims=True)
                pick = blk == first
                chosen[u] = jnp.where(pick, 1.0, chosen[u])
                val[u] = jnp.where(pick, -2.0, val[u])
        for u in subs:
            bias_ref[0, 0, u, 0:nblk, :] = jnp.where(chosen[u] > 0.5, 0.0, NEG_BIG)
            if nblk < ns:
                bias_ref[0, 0, u, nblk:ns, :] = jnp.full((ns - nblk, T), NEG_BIG, F32)
            ch = chosen[u]
            if nblk < ns:
                ch = jnp.concatenate([ch, jnp.zeros((ns - nblk, T), F32)], 0)
            cnt = _dot(grp_ref[...], ch.astype(BF16))
            act = jnp.where(jnp.max(cnt, 1, keepdims=True) > 0.5, 1.0, 0.0)
            lo = jnp.sum(act * wts_ref[0], 0, keepdims=True).astype(jnp.int32)
            hi = jnp.sum(act * wts_ref[1], 0, keepdims=True).astype(jnp.int32)
            bits_ref[u] = jnp.broadcast_to(lo | (hi << PLAN_HALF), (SUBLANES, LANES))

    parts = max(p for p in (SELECT_PARTS, 2, 1) if nc % (p * LANES) == 0 and ns % (p * 2 * SUBLANES) == 0)
    part = step * parts // nsteps
    for v in range(parts):
        pl.when(part == v)(functools.partial(body, nc * (v + 1) // parts, ns * (v + 1) // parts))


def _attend_kernel(bits_sref, q_ref, gt_ref, slope_ref, ocmp_ref, bias_ref, ks_ref, vst_ref, kw_ref, vwt_ref,
                   o_ref, list_ref, s_ref, m_ref, *, n_tiles_total, ns, qb):
    hd, P, T = NSA_HEAD_DIM, NSA_PER_GROUP, Q_BLOCK
    PT = P * T
    b, g, step = pl.program_id(0), pl.program_id(1), pl.program_id(2)
    subs = range(qb)
    iqs = [step * qb + u for u in subs]
    c0s = [iq * T for iq in iqs]
    qts, tlanes = zip(*[_query_t(q_ref, slope_ref, c0s[u], u * T) for u in subs])
    tl_row = lax.broadcasted_iota(jnp.int32, (1, PT), 1) % T

    gts = [jax.nn.sigmoid(gt_ref[0, u * T:(u + 1) * T, :]).T for u in subs]

    def gate_row(u, branch):
        return jnp.concatenate([gts[u][p * 3 + branch:p * 3 + branch + 1, :] for p in range(P)], 1)

    list_len = n_tiles_total + 1
    last_tiles = [(c0s[u] + T - 1) // SEL_TILE for u in subs]
    prev_bits = []
    for u in subs:
        bits = bits_sref[(b * pl.num_programs(1) + g) * (pl.num_programs(2) * qb) + iqs[u]]
        prev_bits.append(bits & (lax.shift_left(jnp.int32(1), last_tiles[u]) - 1))
    n_prevs = [jnp.int32(0) for _ in subs]
    for k in range(n_tiles_total):
        for u in subs:
            list_ref[u * list_len + n_prevs[u]] = k
            n_prevs[u] = n_prevs[u] + (lax.shift_right_logical(prev_bits[u], k) & 1)

    qt_sels = []
    for u in subs:
        sel_b = bias_ref[0, 0, u]
        if ns < SEL_ONEHOT:
            sel_b = jnp.concatenate([sel_b, jnp.zeros((SEL_ONEHOT - ns, T), F32)], 0)
        qt_sels.append(jnp.concatenate([qts[u], jnp.concatenate([sel_b.astype(BF16)] * P, 1)], 0))

    def tile_scores(u, kt):
        k0 = pl.multiple_of(kt * SEL_TILE, SEL_TILE)
        return _dot(ks_ref[0, 0, pl.ds(k0, SEL_TILE), :], qt_sels[u])

    def col_max(scores):
        m = jnp.max(scores[0], 0, keepdims=True)
        for sc in scores[1:]:
            m = jnp.maximum(m, jnp.max(sc, 0, keepdims=True))
        return m

    def softmax_update(carry, scores, vts, m_tile):
        m_run, acc = carry
        m_new = jnp.maximum(m_run, m_tile)
        acc = jnp.exp(m_run - m_new) * acc
        for sc, vt in zip(scores, vts):
            acc = acc + _dot(vt, jnp.exp(sc - m_new).astype(BF16))
        return m_new, acc

    init = (jnp.full((1, PT), NEG_BIG, F32), jnp.zeros((NSA_VROWS, PT), F32))
    dead_tile = n_tiles_total
    n_pairs = [jnp.maximum((n_prevs[u] + 1) // 2, 1) for u in subs]

    def pair_tiles(u, i):
        has_a, has_b = 2 * i < n_prevs[u], 2 * i + 1 < n_prevs[u]
        ka = jnp.where(has_a, list_ref[u * list_len + jnp.where(has_a, 2 * i, 0)], dead_tile)
        kb = jnp.where(has_b, list_ref[u * list_len + jnp.where(has_b, 2 * i + 1, 0)], dead_tile)
        return ka, kb

    def pair_scores(u, i):
        ka, kb = pair_tiles(u, i)
        keys = jnp.concatenate([ks_ref[0, 0, pl.ds(pl.multiple_of(k * SEL_TILE, SEL_TILE), SEL_TILE), :]
                                for k in (ka, kb)], 0)
        return _dot(keys, qt_sels[u])

    def put_scores(u, i, sc):
        slot = i % 2
        s_ref[u, slot] = sc
        m_ref[u, slot] = col_max([sc])

    def pair_probs(u, i, m_run):
        slot = i % 2
        m_new = jnp.maximum(m_run, m_ref[u, slot])
        return m_new, jnp.exp(s_ref[u, slot] - m_new).astype(BF16)

    def add_values(u, i, carry, m_new, pr):
        ka, kb = pair_tiles(u, i)
        m_run, acc = carry
        vt = jnp.concatenate([vst_ref[0, 0, ka], vst_ref[0, 0, kb]], 1)
        return m_new, jnp.exp(m_run - m_new) * acc + _dot(vt, pr)

    first = [pair_scores(u, 0) for u in subs]
    for u in subs:
        put_scores(u, 0, first[u])

    n_win = (WINDOW + T) // WIN_TILE
    roff = lax.broadcasted_iota(jnp.int32, (WIN_TILE, 1), 0)
    w_sc = [_dot(kw_ref[0, 0, pl.ds(pl.multiple_of(c0s[u], WIN_TILE), WINDOW + T), :], qts[u]) for u in subs]
    w_parts = [[jnp.where(roff > tl_row, w_sc[u][0:WIN_TILE], NEG_BIG),
                w_sc[u][WIN_TILE:WINDOW],
                jnp.where(roff <= tl_row, w_sc[u][WINDOW:WINDOW + T], NEG_BIG)]
               for u in subs]
    w_max = [col_max(w_parts[u]) for u in subs]
    out_cw = []
    for u in subs:
        vt = jnp.concatenate([vwt_ref[0, 0, iqs[u] + j] for j in range(n_win)], 1)
        _, acc_win = softmax_update(init, [jnp.concatenate(w_parts[u], 0)], [vt], w_max[u])
        out_cw.append(gate_row(u, 0) * ocmp_ref[0, 0, u] + gate_row(u, 2) * (acc_win[0:hd] / acc_win[hd:hd + 1]))

    def pipe_step(u, i, carry):
        m_new, pr = pair_probs(u, i, carry[0])
        sc_next = pair_scores(u, i + 1)
        carry = add_values(u, i, carry, m_new, pr)
        put_scores(u, i + 1, sc_next)
        return carry

    carries = [lax.fori_loop(0, n_pairs[u] - 1, functools.partial(pipe_step, u), init) for u in subs]
    probs = [pair_probs(u, n_pairs[u] - 1, carries[u][0]) for u in subs]
    s_last = []
    for u in subs:
        kpos = last_tiles[u] * SEL_TILE + lax.broadcasted_iota(jnp.int32, (SEL_TILE, 1), 0)
        s_last.append(jnp.where(kpos <= tlanes[u], tile_scores(u, last_tiles[u]), NEG_BIG))
    carries = [add_values(u, n_pairs[u] - 1, carries[u], *probs[u]) for u in subs]
    for u in subs:
        _, acc_sel = softmax_update(carries[u], [s_last[u]], [vst_ref[0, 0, last_tiles[u]]], col_max([s_last[u]]))
        out = out_cw[u] + gate_row(u, 1) * (acc_sel[0:hd] / acc_sel[hd:hd + 1])
        o_ref[0, u * T:(u + 1) * T, :] = jnp.concatenate(
            [out[:, p * T:(p + 1) * T] for p in range(P)], 0).T.astype(o_ref.dtype)


def _nsa_call(proj3, pe_k, w1_k, w2_k, pe_v, w1_v, w2_v):
    B, S, _ = proj3.shape
    G, hd, P, T = NSA_GROUPS, NSA_HEAD_DIM, NSA_PER_GROUP, Q_BLOCK
    PT = P * T
    nc = S // CMP_STRIDE
    ns = S // SEL_LEN
    nq = S // T
    nt = ns // SEL_TILE_BLOCKS
    assert nt <= 2 * PLAN_HALF and S % KV_PREP_ROWS == 0

    kc_a, vc_t = _compress_call(proj3, pe_k, w1_k, w2_k, pe_v, w1_v, w2_v)
    ks_a, vs_t, kw_a, vw_t = _kvprep_call(proj3)
    imp_m = jnp.asarray(_importance_matrix(ns, nc), BF16)
    grp, wts = _plan_constants(ns)
    head_slopes = 2.0 ** (-(8.0 / NSA_HEADS) * np.arange(1, NSA_HEADS + 1, dtype=np.float32))
    slopes = jnp.asarray(np.repeat(head_slopes.reshape(G, 1, P), T, axis=-1), F32)

    q_spec = lambda im: pl.BlockSpec((1, T, P * hd), im)
    bg = lambda shp: pl.BlockSpec((1, 1) + shp, lambda b, g, i: (b, g) + tuple(0 for _ in shp))
    const = lambda a: pl.BlockSpec(a.shape, lambda b, g, i: tuple(0 for _ in a.shape))
    qb = SELECT_QB if nq % (SELECT_QB * SELECT_PARTS) == 0 else 1
    nsteps = nq // qb
    ocmp, sel_bias, bits = pl.pallas_call(
        functools.partial(_select_kernel, ns=ns, nc=nc, qb=qb),
        grid=(B, G, nsteps),
        in_specs=[pl.BlockSpec((1, qb * T, P * hd), lambda b, g, i: (b, i, OFF_NQ // (P * hd) + g)),
                  pl.BlockSpec((1, 1, PT), lambda b, g, i: (g, 0, 0)),
                  bg((nc, NSA_KW)), bg((hd, nc)), const(imp_m),
                  pl.BlockSpec(grp.shape, lambda b, g, i: (0, 0)),
                  pl.BlockSpec(wts.shape, lambda b, g, i: (0, 0, 0))],
        out_specs=[pl.BlockSpec((1, 1, qb, hd, PT), lambda b, g, i: (b, g, i, 0, 0)),
                   pl.BlockSpec((1, 1, qb, ns, T), lambda b, g, i: (b, g, i, 0, 0)),
                   pl.BlockSpec((qb, SUBLANES, LANES), lambda b, g, i: ((b * G + g) * nsteps + i, 0, 0))],
        out_shape=[jax.ShapeDtypeStruct((B, G, nq, hd, PT), F32),
                   jax.ShapeDtypeStruct((B, G, nq, ns, T), F32),
                   jax.ShapeDtypeStruct((B * G * nq, SUBLANES, LANES), jnp.int32)],
        compiler_params=_params("parallel", "parallel", "parallel"),
        name="nsa_select",
    )(proj3, slopes, kc_a, vc_t, imp_m, jnp.asarray(grp, BF16), jnp.asarray(wts, F32))

    aq = ATTEND_QB if nq % ATTEND_QB == 0 else 1
    bg2 = lambda shp: pl.BlockSpec((1, 1) + shp, lambda b, g, i, s: (b, g) + tuple(0 for _ in shp))
    return pl.pallas_call(
        functools.partial(_attend_kernel, n_tiles_total=nt, ns=ns, qb=aq),
        grid_spec=pltpu.PrefetchScalarGridSpec(
            num_scalar_prefetch=1,
            grid=(B, G, nq // aq),
            in_specs=[pl.BlockSpec((1, aq * T, P * hd), lambda b, g, i, s: (b, i, OFF_NQ // (P * hd) + g)),
                      pl.BlockSpec((1, aq * T, LANES), lambda b, g, i, s: (b, i, OFF_NG // LANES + g)),
                      pl.BlockSpec((1, 1, PT), lambda b, g, i, s: (g, 0, 0)),
                      pl.BlockSpec((1, 1, aq, hd, PT), lambda b, g, i, s: (b, g, i, 0, 0)),
                      pl.BlockSpec((1, 1, aq, ns, T), lambda b, g, i, s: (b, g, i, 0, 0)),
                      bg2(ks_a.shape[2:]), bg2(vs_t.shape[2:]), bg2(kw_a.shape[2:]), bg2(vw_t.shape[2:])],
            out_specs=pl.BlockSpec((1, aq * T, P * hd), lambda b, g, i, s: (b, i, g)),
            scratch_shapes=[pltpu.SMEM((aq * (nt + 1),), jnp.int32),
                            pltpu.VMEM((aq, 2, 2 * SEL_TILE, PT), F32), pltpu.VMEM((aq, 2, 1, PT), F32)]),
        out_shape=jax.ShapeDtypeStruct((B, S, NSA_WIDTH), BF16),
        compiler_params=_params("parallel", "parallel", "arbitrary"),
        name="nsa_attend",
    )(bits[:, 0, 0], proj3, proj3, slopes, ocmp, sel_bias, ks_a, vs_t, kw_a, vw_t)


def _merge_kernel(x_ref, ya_ref, yb_ref, yc_ref, g0_ref, g1_ref, g2_ref, wa_ref, wb_ref, wc_ref, wo_ref,
                  lg_ref, lb_ref, o_ref):
    gate = lambda r: jax.nn.sigmoid(r[...].astype(F32))
    m = gate(g0_ref) * _dot(ya_ref[...], wa_ref[...])
    m = m + gate(g1_ref) * _dot(yb_ref[...], wb_ref[...])
    m = m + gate(g2_ref) * _dot(yc_ref[...], wc_ref[...])
    z = DEEPNORM_ALPHA * x_ref[...] + _dot(m.astype(BF16), wo_ref[...])
    o_ref[...] = _layer_norm_rows(z, lg_ref[...], lb_ref[...])


def _merge_call(x2, ya, yb, yc, gates, w_hg, w_nsa, w_lru, w_out, ln_g, ln_b, tm=512):
    T, D = x2.shape
    tm = min(tm, T)
    rows = lambda w: pl.BlockSpec((tm, w), lambda i: (i, 0))
    gate = lambda n: pl.BlockSpec((tm, D), lambda i: (i, n))
    full = lambda a: pl.BlockSpec(a.shape, lambda i: (0, 0))
    ws = [w.astype(BF16) for w in (w_hg, w_nsa, w_lru, w_out)]
    return pl.pallas_call(
        _merge_kernel,
        grid=(T // tm,),
        in_specs=[rows(D), rows(HG_WIDTH), rows(NSA_WIDTH), rows(LRU_WIDTH), gate(0), gate(1), gate(2)]
                 + [full(w) for w in ws] + [pl.BlockSpec((1, D), lambda i: (0, 0))] * 2,
        out_specs=rows(D),
        out_shape=jax.ShapeDtypeStruct((T, D), F32),
        compiler_params=_params("parallel"),
        name="merge_out",
    )(x2, ya, yb, yc, gates, gates, gates, *ws, ln_g.reshape(1, D), ln_b.reshape(1, D))


FFN_SPLIT = 1
FFN_PAD = SUBLANES
FFN_SUB = 256


def _ffn_kernel(x_ref, wu_ref, wv_ref, cw_ref, cb_ref, wd_ref, lg_ref, lb_ref, o_ref,
                xb_ref, acc_ref, upad_ref, carry_ref, *, tm, tiles_per_seq):
    i = pl.program_id(0)
    j = pl.program_id(1)

    @pl.when(j == 0)
    def _():
        xb_ref[...] = x_ref[...].astype(BF16)

    xb = xb_ref[...]
    fc = wu_ref.shape[1]
    first = (i % tiles_per_seq) == 0

    @pl.when(first)
    def _():
        upad_ref[0:FFN_PAD, :] = jnp.zeros((FFN_PAD, fc), F32)

    @pl.when(jnp.logical_not(first))
    def _():
        upad_ref[0:FFN_PAD, :] = carry_ref[j]

    @pl.when(j == 0)
    def _():
        acc_ref[...] = jnp.zeros_like(acc_ref)

    for c0 in range(0, fc, FFN_SUB):
        cs = slice(c0, min(c0 + FFN_SUB, fc))
        u = _dot(xb, wu_ref[:, cs])
        v = _dot(xb, wv_ref[:, cs])
        upad_ref[FFN_PAD:FFN_PAD + tm, cs] = u
        cw = cw_ref[:, cs]
        cv = cb_ref[:, cs] + cw[FFN_CONV - 1:FFN_CONV, :] * u
        for k in range(FFN_CONV - 1):
            off = FFN_PAD - (FFN_CONV - 1) + k
            cv = cv + cw[k:k + 1, :] * upad_ref[off:off + tm, cs]
        carry_ref[j, :, cs] = u[tm - FFN_PAD:tm, :]
        h = (jax.nn.gelu(cv) * v).astype(BF16)
        acc_ref[...] += _dot(h, wd_ref[cs, :])

    @pl.when(j == FFN_SPLIT - 1)
    def _():
        z = DEEPNORM_ALPHA * x_ref[...] + acc_ref[...]
        o_ref[...] = _layer_norm_rows(z, lg_ref[...], lb_ref[...])


def _ffn_call(x2, seq_len, w_up, conv_w, conv_b, w_down, ln_g, ln_b, tm=512):
    T, D = x2.shape
    tm = min(tm, seq_len)
    fc = FFN_DIM // FFN_SPLIT
    wu = w_up.astype(BF16)
    wmode = pl.Buffered(1) if FFN_SPLIT == 1 else None
    return pl.pallas_call(
        functools.partial(_ffn_kernel, tm=tm, tiles_per_seq=seq_len // tm),
        grid=(T // tm, FFN_SPLIT),
        in_specs=[pl.BlockSpec((tm, D), lambda i, j: (i, 0)),
                  pl.BlockSpec((D, fc), lambda i, j: (0, j), pipeline_mode=wmode),
                  pl.BlockSpec((D, fc), lambda i, j: (0, FFN_SPLIT + j), pipeline_mode=wmode),
                  pl.BlockSpec((FFN_CONV, fc), lambda i, j: (0, j)),
                  pl.BlockSpec((1, fc), lambda i, j: (0, j)),
                  pl.BlockSpec((fc, D), lambda i, j: (j, 0), pipeline_mode=wmode),
                  pl.BlockSpec((1, D), lambda i, j: (0, 0)),
                  pl.BlockSpec((1, D), lambda i, j: (0, 0))],
        out_specs=pl.BlockSpec((tm, D), lambda i, j: (i, 0)),
        out_shape=jax.ShapeDtypeStruct((T, D), F32),
        scratch_shapes=[pltpu.VMEM((tm, D), BF16), pltpu.VMEM((tm, D), F32),
                        pltpu.VMEM((tm + FFN_PAD, fc), F32), pltpu.VMEM((FFN_SPLIT, FFN_PAD, fc), F32)],
        compiler_params=_params("arbitrary", "arbitrary"),
        name="conv_ffn",
    )(x2, wu, wu, conv_w.astype(F32), conv_b.reshape(1, FFN_DIM).astype(F32), w_down.astype(BF16),
      ln_g.reshape(1, D), ln_b.reshape(1, D))


def _permute_in_proj(w, b):
    src = np.cumsum([0, HG_WIDTH, HG_WIDTH, HG_WIDTH, HG_WIDTH, NSA_WIDTH, NSA_KV, NSA_KV, NSA_KV, NSA_KV,
                     NSA_KV, NSA_KV, NSA_HEADS * 3, LRU_WIDTH, LRU_WIDTH, N_BRANCH * 1024])
    names = ["hq", "hf", "hi", "hg", "nq", "kc", "vc", "ks", "vs", "kw", "vw", "ng", "lx", "ly", "mg"]
    seg = {n: (int(src[k]), int(src[k + 1])) for k, n in enumerate(names)}
    order = ["mg", "hq", "hf", "hi", "hg", "nq", "lx", "ly", "kc", "vc", "ks", "vs", "kw", "vw"]
    per_group = NSA_PER_GROUP * 3
    ng0 = seg["ng"][0]
    w = w.astype(BF16)
    w_parts = [w[:, seg[n][0]:seg[n][1]] for n in order]
    b_parts = [b[seg[n][0]:seg[n][1]] for n in order]
    for gi in range(NSA_GROUPS):
        lo = ng0 + gi * per_group
        w_parts += [w[:, lo:lo + per_group], jnp.zeros((w.shape[0], LANES - per_group), w.dtype)]
        b_parts += [b[lo:lo + per_group], jnp.zeros((LANES - per_group,), b.dtype)]
    wp = jnp.concatenate(w_parts, 1)
    bp = jnp.concatenate(b_parts)
    assert wp.shape[1] == GATE_COLS + PROJ_COLS
    return wp.astype(BF16), bp.astype(F32)


def kernel(x, ln_emb_g, ln_emb_b, w_in, b_in, hg_lb_logits, hg_norm_g, cmp_pe_k, cmp_w1_k, cmp_w2_k, cmp_pe_v, cmp_w1_v, cmp_w2_v, lru_conv_w, lru_conv_b, lru_wa, lru_ba, lru_wx, lru_bx, lru_lambda, w_branch_hg, w_branch_nsa, w_branch_lru, w_out, ln1_g, ln1_b, ffn_w_up, ffn_conv_w, ffn_conv_b, ffn_w_down, ln2_g, ln2_b):
    B, S, D = x.shape
    T = B * S
    gam = jax.nn.softmax(hg_lb_logits.astype(F32), axis=0)
    lb_all = jnp.cumsum(gam, axis=0) - gam[0]
    h = x.reshape(T, D)
    for l in range(DEPTH):
        wp, bp = _permute_in_proj(w_in[l], b_in[l])
        if l == 0:
            gates, proj2, h = _inproj_call(h, wp, bp, ln=(ln_emb_g, ln_emb_b))
        else:
            gates, proj2 = _inproj_call(h, wp, bp)
        proj3 = proj2.reshape(B, S, PROJ_COLS)
        y_a = _hgrn_call(proj3, lb_all[l], hg_norm_g[l])
        y_b = _nsa_call(proj3, cmp_pe_k[l], cmp_w1_k[l], cmp_w2_k[l], cmp_pe_v[l], cmp_w1_v[l], cmp_w2_v[l])
        y_c = _lru_call(proj3, lru_conv_w[l], lru_conv_b[l], lru_wa[l], lru_ba[l], lru_wx[l], lru_bx[l],
                        lru_lambda[l])
        h = _merge_call(h, y_a.reshape(T, HG_WIDTH), y_b.reshape(T, NSA_WIDTH), y_c.reshape(T, LRU_WIDTH),
                        gates, w_branch_hg[l], w_branch_nsa[l], w_branch_lru[l], w_out[l], ln1_g[l], ln1_b[l])
        h = _ffn_call(h, S, ffn_w_up[l], ffn_conv_w[l], ffn_conv_b[l], ffn_w_down[l], ln2_g[l], ln2_b[l])
    return h.reshape(B, S, D)
```

```python
import functools

import numpy as np
import jax
import jax.numpy as jnp
from jax import lax
from jax.experimental import pallas as pl
from jax.experimental.pallas import tpu as pltpu

F32 = jnp.float32
BF16 = jnp.bfloat16

DEPTH = 2
HG_HEADS = 4
HG_HEAD_DIM = 128
HG_WIDTH = HG_HEADS * HG_HEAD_DIM
NSA_HEADS = 8
NSA_GROUPS = 2
NSA_PER_GROUP = NSA_HEADS // NSA_GROUPS
NSA_HEAD_DIM = 64
NSA_WIDTH = NSA_HEADS * NSA_HEAD_DIM
NSA_KV = NSA_GROUPS * NSA_HEAD_DIM
CMP_LEN = 32
CMP_STRIDE = 16
CMP_HIDDEN = 64
SEL_LEN = 64
SEL_TOPK = 16
WINDOW = 512
Q_BLOCK = 128
LRU_WIDTH = 512
LRU_BLOCKS = 4
LRU_CONV = 4
LRU_C = 8.0
FFN_DIM = 2816
FFN_CONV = 3
DEEPNORM_ALPHA = (2 * DEPTH) ** 0.25
LN_EPS = 1e-5
N_BRANCH = 3

LANES = 128
SUBLANES = 8
VMEM_LIMIT_BYTES = 52 * 1024 * 1024

GATE_COLS = N_BRANCH * 1024
OFF_HQ = 0
OFF_HF = OFF_HQ + HG_WIDTH
OFF_HI = OFF_HF + HG_WIDTH
OFF_HG = OFF_HI + HG_WIDTH
OFF_NQ = OFF_HG + HG_WIDTH
OFF_LX = OFF_NQ + NSA_WIDTH
OFF_LY = OFF_LX + LRU_WIDTH
OFF_KC = OFF_LY + LRU_WIDTH
OFF_VC = OFF_KC + NSA_KV
OFF_KS = OFF_VC + NSA_KV
OFF_VS = OFF_KS + NSA_KV
OFF_KW = OFF_VS + NSA_KV
OFF_VW = OFF_KW + NSA_KV
OFF_NG = OFF_VW + NSA_KV
PROJ_TILE_N = 1536
PROJ_COLS = OFF_NG + NSA_GROUPS * LANES

NEG_BIG = -1e30


def _split3(x):
    hi = x.astype(BF16)
    r1 = x - hi.astype(F32)
    mid = r1.astype(BF16)
    lo = (r1 - mid.astype(F32)).astype(BF16)
    return hi, mid, lo


def _dot(a, b):
    return jnp.dot(a, b, preferred_element_type=F32)


def _dot_nt(a, b):
    return lax.dot_general(a, b, (((1,), (1,)), ((), ())), preferred_element_type=F32)


def _dot_exact_lhs(m_bf16, x):
    hi, mid, lo = _split3(x)
    return _dot(m_bf16, hi) + _dot(m_bf16, mid) + _dot(m_bf16, lo)


def _dot_hilo(a, b):
    ah = a.astype(BF16)
    al = (a - ah.astype(F32)).astype(BF16)
    bh = b.astype(BF16)
    bl = (b - bh.astype(F32)).astype(BF16)
    return _dot(ah, bh) + _dot(ah, bl) + _dot(al, bh) + _dot(al, bl)


def _layer_norm_rows(z, g, b):
    mu = jnp.mean(z, -1, keepdims=True)
    zc = z - mu
    var = jnp.mean(zc * zc, -1, keepdims=True)
    return zc * lax.rsqrt(var + LN_EPS) * g + b


def _params(*sem):
    return pltpu.CompilerParams(dimension_semantics=sem, vmem_limit_bytes=VMEM_LIMIT_BYTES)


def _inproj_kernel(*refs, with_ln):
    if with_ln:
        x_ref, g_ref, be_ref, w_ref, b_ref, gate_ref, o_ref, xn_ref = refs
    else:
        x_ref, w_ref, b_ref, gate_ref, o_ref = refs
    x = x_ref[...]
    if with_ln:
        x = _layer_norm_rows(x, g_ref[...], be_ref[...])
        xn_ref[...] = x
    xb = x.astype(BF16)
    n = w_ref.shape[1]
    for c0 in range(0, n, PROJ_TILE_N):
        cs = slice(c0, c0 + PROJ_TILE_N)
        acc = _dot(xb, w_ref[:, cs]) + b_ref[:, cs]
        if c0 < GATE_COLS:
            gate_ref[:, cs] = acc.astype(BF16)
        else:
            o_ref[:, c0 - GATE_COLS:c0 - GATE_COLS + PROJ_TILE_N] = acc


def _inproj_call(x2, w_all, bias_all, layer, ln=None, tm=512):
    T, D = x2.shape
    N = w_all.shape[-1]
    tm = min(tm, T)
    assert GATE_COLS % PROJ_TILE_N == 0 and N % PROJ_TILE_N == 0
    vec = pl.BlockSpec((1, D), lambda i: (0, 0))
    in_specs = [pl.BlockSpec((tm, D), lambda i: (i, 0))] + ([vec, vec] if ln else []) + [
        pl.BlockSpec((None, D, N), lambda i: (layer, 0, 0), pipeline_mode=pl.Buffered(1)),
        pl.BlockSpec((None, 1, N), lambda i: (layer, 0, 0))]
    out_specs = [pl.BlockSpec((tm, GATE_COLS), lambda i: (i, 0)),
                 pl.BlockSpec((tm, N - GATE_COLS), lambda i: (i, 0))]
    out_shape = [jax.ShapeDtypeStruct((T, GATE_COLS), BF16), jax.ShapeDtypeStruct((T, N - GATE_COLS), F32)]
    if ln:
        out_specs.append(pl.BlockSpec((tm, D), lambda i: (i, 0)))
        out_shape.append(jax.ShapeDtypeStruct((T, D), F32))
    ln_args = [ln[0].reshape(1, D), ln[1].reshape(1, D)] if ln else []
    return pl.pallas_call(
        functools.partial(_inproj_kernel, with_ln=bool(ln)),
        grid=(T // tm,),
        in_specs=in_specs, out_specs=out_specs, out_shape=out_shape,
        compiler_params=_params("parallel"),
        name="in_proj_ln" if ln else "in_proj",
    )(x2, *ln_args, w_all, bias_all)


HG_CHUNK = 64
HG_DIAG = SUBLANES
HG_HEADS_PER_STEP = 4


def _hgrn_constants(C):
    r = np.arange(C)
    tri = (r[None, :] <= r[:, None]).astype(np.float32)
    mats = [tri]
    masks = []
    w = HG_DIAG
    while 2 * w <= C:
        mid = (r // (2 * w)) * (2 * w) + w
        mats.append(tri - (r[None, :] <= mid[:, None]).astype(np.float32))
        same = (r[:, None] // (2 * w)) == (r[None, :] // (2 * w))
        masks.append((same & ((r[:, None] % (2 * w)) >= w) & ((r[None, :] % (2 * w)) < w)).astype(np.float32))
        w *= 2
    mats.append((r[None, :] > r[:, None]).astype(np.float32))
    return np.concatenate(mats, 0), np.stack(masks, 0)


def _hgrn_kernel(q_ref, f_ref, i_ref, g_ref, lb_ref, ng_ref, m_ref, mask_ref, o_ref, st_ref, *, C, nchunk, nh):
    d = HG_HEAD_DIM
    nlev = mask_ref.shape[0]

    @pl.when(pl.program_id(2) == 0)
    def _():
        st_ref[...] = jnp.zeros_like(st_ref)

    hs = range(nh)
    lane = [slice(h * d, (h + 1) * d) for h in hs]
    lb = [lb_ref[h] for h in hs]
    ng = [ng_ref[h] for h in hs]
    nb = C // HG_DIAG
    row3 = lax.broadcasted_iota(jnp.int32, (nb, HG_DIAG, d), 1)

    fl_all = f_ref[0]
    lf_all = [jnp.log(lb[h] + (1.0 - lb[h]) * jax.nn.sigmoid(fl_all[:, lane[h]])) for h in hs]
    kk_all = [(1.0 - lb[h]) * jax.nn.sigmoid(-fl_all[:, lane[h]]) for h in hs]
    allm_all = _dot_exact_lhs(m_ref[...], jnp.concatenate(
        [lf_all[h][c * C:(c + 1) * C] for h in hs for c in range(nchunk)], 1))

    for c in range(nchunk):
        sl = pl.ds(c * C, C)
        q = [q_ref[0, sl, lane[h]] for h in hs]
        iv = [i_ref[0, sl, lane[h]] for h in hs]
        kk = [kk_all[h][c * C:(c + 1) * C] for h in hs]
        allm = [allm_all[:, (h * nchunk + c) * d:(h * nchunk + c + 1) * d] for h in hs]
        bcs = [allm[h][0:C] for h in hs]
        rem = [allm[h][(nlev + 1) * C:(nlev + 2) * C] for h in hs]
        iv_b = [iv[h].astype(BF16) for h in hs]

        att = [jnp.zeros((C, C), F32) for _ in hs]
        for l in range(nlev):
            e = [jnp.exp(-jnp.abs(allm[h][(1 + l) * C:(2 + l) * C])) for h in hs]
            att = [att[h] + mask_ref[l] * _dot_nt((q[h] * e[h]).astype(BF16), (kk[h] * e[h]).astype(BF16))
                   for h in hs]
        o = [_dot(att[h].astype(BF16), iv_b[h]) for h in hs]

        q3 = [q[h].reshape(nb, HG_DIAG, d) for h in hs]
        k3 = [kk[h].reshape(nb, HG_DIAG, d) for h in hs]
        b3 = [bcs[h].reshape(nb, HG_DIAG, d) for h in hs]
        i3 = [iv[h].reshape(nb, HG_DIAG, d) for h in hs]
        acc = [jnp.zeros((nb, HG_DIAG, d), F32) for _ in hs]
        for s in range(HG_DIAG):
            for h in hs:
                dec = jnp.where(row3 >= s, jnp.exp(b3[h] - b3[h][:, s:s + 1, :]), 0.0)
                a = jnp.sum(q3[h] * dec * k3[h][:, s:s + 1, :], axis=-1, keepdims=True)
                acc[h] = acc[h] + a * i3[h][:, s:s + 1, :]

        for h in hs:
            st = st_ref[h]
            oh = o[h] + acc[h].reshape(C, d) + _dot_nt((q[h] * jnp.exp(bcs[h])).astype(BF16), st.astype(BF16))
            kdec = (kk[h] * jnp.exp(rem[h])).astype(BF16)
            st_ref[h] = st * jnp.exp(bcs[h][C - 1:C, :]) + _dot(iv[h].T.astype(BF16), kdec)
            oh = oh * lax.rsqrt(jnp.mean(oh * oh, -1, keepdims=True) + 1e-6) * ng[h]
            g = g_ref[0, sl, lane[h]]
            o_ref[0, sl, lane[h]] = (oh * (g * jax.nn.sigmoid(g))).astype(o_ref.dtype)


def _hgrn_call(proj3, lb, norm_g, tt=512):
    B, S, _ = proj3.shape
    d = HG_HEAD_DIM
    C = HG_CHUNK
    tt = min(tt, S)
    mstack, masks = _hgrn_constants(C)
    nm = mstack.shape[0]
    nh = HG_HEADS_PER_STEP
    wid = nh * d
    col = lambda off: (lambda b, h, t: (b, t, off // wid + h))
    blk = (1, tt, wid)
    return pl.pallas_call(
        functools.partial(_hgrn_kernel, C=C, nchunk=tt // C, nh=nh),
        grid=(B, HG_HEADS // nh, S // tt),
        in_specs=[pl.BlockSpec(blk, col(OFF_HQ)), pl.BlockSpec(blk, col(OFF_HF)),
                  pl.BlockSpec(blk, col(OFF_HI)), pl.BlockSpec(blk, col(OFF_HG)),
                  pl.BlockSpec((nh, 1, d), lambda b, h, t: (h, 0, 0)),
                  pl.BlockSpec((nh, 1, d), lambda b, h, t: (h, 0, 0)),
                  pl.BlockSpec((nm, C), lambda b, h, t: (0, 0)),
                  pl.BlockSpec(masks.shape, lambda b, h, t: (0, 0, 0))],
        out_specs=pl.BlockSpec(blk, lambda b, h, t: (b, t, h)),
        out_shape=jax.ShapeDtypeStruct((B, S, HG_WIDTH), BF16),
        scratch_shapes=[pltpu.VMEM((nh, d, d), F32)],
        compiler_params=_params("parallel", "parallel", "arbitrary"),
        name="hgrn2",
    )(proj3, proj3, proj3, proj3, lb.reshape(HG_HEADS, 1, d), norm_g.reshape(HG_HEADS, 1, d),
      jnp.asarray(mstack, BF16), jnp.asarray(masks, F32))


LRU_PAD = SUBLANES


def _lru_kernel(x_ref, y_ref, cw_ref, cb_ref, wa_ref, ba_ref, wx_ref, bx_ref, c_ref, ex_ref, o_ref,
                xpad_ref, h_ref, a_ref, u_ref, *, tt):
    W = LRU_WIDTH
    bw = W // LRU_BLOCKS

    @pl.when(pl.program_id(1) == 0)
    def _():
        xpad_ref[0:LRU_PAD, :] = jnp.zeros((LRU_PAD, W), F32)
        h_ref[...] = jnp.zeros_like(h_ref)

    x = x_ref[0]
    xpad_ref[LRU_PAD:LRU_PAD + tt, :] = x
    cw = cw_ref[...]
    xc = cb_ref[...] + cw[LRU_CONV - 1:LRU_CONV, :] * x
    for j in range(LRU_CONV - 1):
        off = LRU_PAD - (LRU_CONV - 1) + j
        xc = xc + cw[j:j + 1, :] * xpad_ref[off:off + tt, :]
    xpad_ref[0:LRU_PAD, :] = x[tt - LRU_PAD:tt, :]

    rs, is_ = [], []
    for gi in range(LRU_BLOCKS):
        xg = xc[:, gi * bw:(gi + 1) * bw].astype(BF16)
        rs.append(_dot(xg, wa_ref[gi]))
        is_.append(_dot(xg, wx_ref[gi]))
    r = jax.nn.sigmoid(jnp.concatenate(rs, -1) + ba_ref[...])
    ig = jax.nn.sigmoid(jnp.concatenate(is_, -1) + bx_ref[...])
    log_a = c_ref[...] * r
    a = jnp.exp(log_a)
    u = jnp.sqrt(-jnp.tanh(log_a) * (a * a + 1.0)) * (ig * xc)

    def scan_rows(a, u, n, group):
        idx = lax.broadcasted_iota(jnp.int32, (n, W), 0) % group
        dlt = 1
        while dlt < group:
            keep = idx >= dlt
            a_sh = jnp.where(keep, pltpu.roll(a, dlt, 0), 1.0)
            u_sh = jnp.where(keep, pltpu.roll(u, dlt, 0), 0.0)
            u = a * u_sh + u
            a = a * a_sh
            dlt *= 2
        return a, u

    a, u = scan_rows(a, u, tt, SUBLANES)
    ng = tt // SUBLANES
    last = pl.ds(SUBLANES - 1, ng, stride=SUBLANES)
    ends = []
    for ref, val in ((a_ref, a), (u_ref, u)):
        for cb in range(W // LANES):
            ref[cb] = val[:, cb * LANES:(cb + 1) * LANES]
        ends.append(jnp.concatenate([ref[cb, last, :] for cb in range(W // LANES)], 1))
    ae, ue = scan_rows(ends[0], ends[1], ng, ng)
    h0 = h_ref[...]
    h_end = ue + ae * h0
    gidx = lax.broadcasted_iota(jnp.int32, (ng, W), 0)
    h_in = jnp.where(gidx >= 1, pltpu.roll(h_end, 1, 0), h0)
    h = u + a * _dot_exact_lhs(ex_ref[...], h_in)
    o_ref[0] = (h * jax.nn.gelu(y_ref[0])).astype(o_ref.dtype)
    h_ref[...] = h_end[ng - 1:ng, :]


def _lru_call(proj3, conv_w, conv_b, wa, ba, wx, bx, lam, tt=512):
    B, S, _ = proj3.shape
    W = LRU_WIDTH
    bw = W // LRU_BLOCKS
    tt = min(tt, S)
    c = (-LRU_C * jax.nn.softplus(-lam.astype(F32))).reshape(1, W)
    expand = (np.arange(tt)[:, None] // SUBLANES == np.arange(tt // SUBLANES)[None, :]).astype(np.float32)
    vec = lambda: pl.BlockSpec((1, W), lambda b, t: (0, 0))
    return pl.pallas_call(
        functools.partial(_lru_kernel, tt=tt),
        grid=(B, S // tt),
        in_specs=[pl.BlockSpec((1, tt, W), lambda b, t: (b, t, OFF_LX // W)),
                  pl.BlockSpec((1, tt, W), lambda b, t: (b, t, OFF_LY // W)),
                  pl.BlockSpec((LRU_CONV, W), lambda b, t: (0, 0)), vec(),
                  pl.BlockSpec((LRU_BLOCKS, bw, bw), lambda b, t: (0, 0, 0)), vec(),
                  pl.BlockSpec((LRU_BLOCKS, bw, bw), lambda b, t: (0, 0, 0)), vec(), vec(),
                  pl.BlockSpec(expand.shape, lambda b, t: (0, 0))],
        out_specs=pl.BlockSpec((1, tt, W), lambda b, t: (b, t, 0)),
        out_shape=jax.ShapeDtypeStruct((B, S, W), BF16),
        scratch_shapes=[pltpu.VMEM((tt + LRU_PAD, W), F32), pltpu.VMEM((1, W), F32),
                        pltpu.VMEM((W // LANES, tt, LANES), F32), pltpu.VMEM((W // LANES, tt, LANES), F32)],
        compiler_params=_params("parallel", "arbitrary"),
        name="rglru",
    )(proj3, proj3, conv_w.astype(F32), conv_b.reshape(1, W).astype(F32), wa.astype(BF16),
      ba.reshape(1, W), wx.astype(BF16), bx.reshape(1, W), c, jnp.asarray(expand, BF16))


NSA_AUG = 4
NSA_KW = LANES
SEL_TILE = 256
SEL_TILE_BLOCKS = SEL_TILE // SEL_LEN
WIN_TILE = 128
POS_SPLIT = 128
SEL_ONEHOT = LANES
PLAN_HALF = 16
SELECT_PARTS = 4
SELECT_QB = 8
ATTEND_QB = 4
NSA_VROWS = NSA_HEAD_DIM + 16
KV_PREP_ROWS = 512


def _key_aug(pos, width):
    n = pos.shape[0]
    col = lax.broadcasted_iota(jnp.int32, (n, width), 1)
    hi = ((pos // POS_SPLIT) * POS_SPLIT).astype(F32)
    lo = (pos % POS_SPLIT).astype(F32)
    return jnp.where(col == 0, hi, jnp.where(col == 1, lo, jnp.where(col < NSA_AUG, 1.0, 0.0)))


def _query_t(q_ref, slope_ref, c0, row0=0):
    hd, P, T = NSA_HEAD_DIM, NSA_PER_GROUP, Q_BLOCK
    PT = P * T
    qT = (q_ref[0, row0:row0 + T, :] * (hd ** -0.5)).T
    qpart = jnp.concatenate([qT[p * hd:(p + 1) * hd, :] for p in range(P)], 1)
    tlane = c0 + lax.broadcasted_iota(jnp.int32, (1, PT), 1) % T
    slope = slope_ref[0]
    t_hi = ((tlane // POS_SPLIT) * POS_SPLIT).astype(F32)
    t_lo = (tlane % POS_SPLIT).astype(F32)
    rowi = lax.broadcasted_iota(jnp.int32, (NSA_KW - hd, PT), 0)
    aug = jnp.where(rowi < 2, slope,
                    jnp.where(rowi == 2, -slope * t_hi,
                              jnp.where(rowi == 3, -slope * t_lo, jnp.where(rowi == NSA_AUG, NEG_BIG, 0.0))))
    return jnp.concatenate([qpart, aug], 0).astype(BF16), tlane


def _compress_kernel(xk_ref, xv_ref, w1k_ref, pek_ref, w1fk_ref, w2k_ref, w1v_ref, pev_ref, w1fv_ref, w2v_ref,
                     kc_ref, vct_ref, *, nc):
    hd = NSA_HEAD_DIM
    pos = lax.broadcasted_iota(jnp.int32, (nc, 1), 0) * CMP_STRIDE + (CMP_LEN - 1)
    aug = _key_aug(pos, NSA_KW - hd)

    def one(x_ref, w1_ref, pe_ref, w1f_ref, w2_ref, g):
        uv = jnp.zeros((nc, 2 * CMP_HIDDEN), F32)
        for r in range(CMP_STRIDE):
            xr = x_ref[0, pl.ds(r, nc, stride=CMP_STRIDE), :][:, g * hd:(g + 1) * hd]
            uv = uv + _dot(xr.astype(BF16), w1_ref[r])
        cvec = _dot_hilo(pe_ref[...], w1f_ref[...])[0:1, :]
        hid = uv[:, 0:CMP_HIDDEN] + pltpu.roll(uv[:, CMP_HIDDEN:], nc - 1, 0) + cvec
        return _dot_hilo(jax.nn.gelu(hid), w2_ref[...])

    for g in range(NSA_GROUPS):
        kc = one(xk_ref, w1k_ref, pek_ref, w1fk_ref, w2k_ref, g)
        kc_ref[0, g] = jnp.concatenate([kc, aug], 1).astype(BF16)
        vct_ref[0, g] = one(xv_ref, w1v_ref, pev_ref, w1fv_ref, w2v_ref, g).T.astype(BF16)


def _compress_call(proj3, pe_k, w1_k, w2_k, pe_v, w1_v, w2_v):
    B, S, _ = proj3.shape
    G, hd = NSA_GROUPS, NSA_HEAD_DIM
    nc = S // CMP_STRIDE
    half = CMP_STRIDE * hd

    def prep(pe, w1):
        w1 = w1.astype(F32)
        wr = jnp.concatenate([w1[:half].reshape(CMP_STRIDE, hd, CMP_HIDDEN),
                              w1[half:].reshape(CMP_STRIDE, hd, CMP_HIDDEN)], -1)
        pe8 = jnp.broadcast_to(pe.reshape(1, CMP_LEN * hd).astype(F32), (SUBLANES, CMP_LEN * hd))
        return wr.astype(BF16), pe8, w1

    full = lambda a: pl.BlockSpec(a.shape, lambda b: tuple(0 for _ in a.shape))
    col = lambda off: pl.BlockSpec((1, S, LANES), lambda b: (b, 0, off // LANES))
    args = prep(pe_k, w1_k) + (w2_k.astype(F32),) + prep(pe_v, w1_v) + (w2_v.astype(F32),)
    return pl.pallas_call(
        functools.partial(_compress_kernel, nc=nc),
        grid=(B,),
        in_specs=[col(OFF_KC), col(OFF_VC)] + [full(a) for a in args],
        out_specs=[pl.BlockSpec((1, G, nc, NSA_KW), lambda b: (b, 0, 0, 0)),
                   pl.BlockSpec((1, G, hd, nc), lambda b: (b, 0, 0, 0))],
        out_shape=[jax.ShapeDtypeStruct((B, G, nc, NSA_KW), BF16),
                   jax.ShapeDtypeStruct((B, G, hd, nc), BF16)],
        compiler_params=_params("parallel"),
        name="nsa_compress",
    )(proj3, proj3, *args)


def _kvprep_constants(S, tt):
    hd = NSA_HEAD_DIM
    pos = np.arange(S + tt)
    live = pos < S
    kcols = np.zeros((S + tt, NSA_KW - hd + SEL_ONEHOT), np.float32)
    kcols[:, 0] = np.where(live, (pos // POS_SPLIT) * POS_SPLIT, 0)
    kcols[:, 1] = np.where(live, pos % POS_SPLIT, 0)
    kcols[:, 2:NSA_AUG] = live[:, None]
    kcols[:, NSA_AUG] = ~live
    kcols[pos[live], NSA_KW - hd + pos[live] // SEL_LEN] = 1.0
    vrows = np.zeros((2, NSA_VROWS - hd, tt), np.float32)
    vrows[0, 0] = 1.0
    return kcols, vrows


def _kvprep_kernel(ks_ref, vs_ref, kw_ref, vw_ref, kc_ref, vr_ref, ksa_ref, vst_ref, kwa_ref, vwt_ref,
                   *, tt, n_live):
    hd = NSA_HEAD_DIM
    dead = pl.program_id(1) >= n_live
    kcols = kc_ref[...]
    ones_rows = vr_ref[0]
    for g in range(NSA_GROUPS):
        lanes = slice(g * hd, (g + 1) * hd)
        live = lambda ref: jnp.where(dead, 0.0, ref[0][:, lanes])
        ksa_ref[0, g] = jnp.concatenate([live(ks_ref).astype(BF16), kcols], 1)
        kwa_ref[0, g] = jnp.concatenate([live(kw_ref).astype(BF16), kcols[:, 0:NSA_KW - hd]], 1)
        vs_t = jnp.concatenate([live(vs_ref).T.astype(BF16), ones_rows], 0)
        vw_t = jnp.concatenate([live(vw_ref).T.astype(BF16), ones_rows], 0)
        for c in range(tt // SEL_TILE):
            vst_ref[0, g, c] = vs_t[:, c * SEL_TILE:(c + 1) * SEL_TILE]
        for c in range(tt // WIN_TILE):
            vwt_ref[0, g, c] = vw_t[:, c * WIN_TILE:(c + 1) * WIN_TILE]


def _kvprep_call(proj3):
    B, S, _ = proj3.shape
    G, hd = NSA_GROUPS, NSA_HEAD_DIM
    tt = min(KV_PREP_ROWS, S)
    n_live = S // tt
    assert tt == WINDOW
    sp = S + tt
    col = lambda off: pl.BlockSpec((1, tt, LANES), lambda b, t: (b, jnp.minimum(t, n_live - 1), off // LANES))
    front = lambda t: (t + 1) % (n_live + 1)
    kcols, vrows = _kvprep_constants(S, tt)
    return pl.pallas_call(
        functools.partial(_kvprep_kernel, tt=tt, n_live=n_live),
        grid=(B, n_live + 1),
        in_specs=[col(OFF_KS), col(OFF_VS), col(OFF_KW), col(OFF_VW),
                  pl.BlockSpec((tt, kcols.shape[1]), lambda b, t: (t, 0)),
                  pl.BlockSpec((1,) + vrows.shape[1:], lambda b, t: (t // n_live, 0, 0))],
        out_specs=[pl.BlockSpec((1, G, tt, NSA_KW + SEL_ONEHOT), lambda b, t: (b, 0, t, 0)),
                   pl.BlockSpec((1, G, tt // SEL_TILE, NSA_VROWS, SEL_TILE), lambda b, t: (b, 0, t, 0, 0)),
                   pl.BlockSpec((1, G, tt, NSA_KW), lambda b, t: (b, 0, front(t), 0)),
                   pl.BlockSpec((1, G, tt // WIN_TILE, NSA_VROWS, WIN_TILE), lambda b, t: (b, 0, front(t), 0, 0))],
        out_shape=[jax.ShapeDtypeStruct((B, G, sp, NSA_KW + SEL_ONEHOT), BF16),
                   jax.ShapeDtypeStruct((B, G, sp // SEL_TILE, NSA_VROWS, SEL_TILE), BF16),
                   jax.ShapeDtypeStruct((B, G, sp, NSA_KW), BF16),
                   jax.ShapeDtypeStruct((B, G, sp // WIN_TILE, NSA_VROWS, WIN_TILE), BF16)],
        compiler_params=_params("parallel", "parallel"),
        name="nsa_kvprep",
    )(proj3, proj3, proj3, proj3, jnp.asarray(kcols, BF16), jnp.asarray(vrows, BF16))


def _importance_matrix(ns, nc):
    ratio = SEL_LEN // CMP_STRIDE
    a = np.zeros((ns, nc), np.float32)
    for j in range(ns):
        for n, wgt in ((ratio * j - 1, 0.5), (ratio * j, 1.0), (ratio * j + 1, 1.0),
                       (ratio * j + 2, 1.0), (ratio * j + 3, 0.5)):
            if 0 <= n < nc - 1:
                a[j, n] = wgt
    return a


def _plan_constants(ns):
    nt = ns // SEL_TILE_BLOCKS
    grp = (np.arange(ns)[None, :] // SEL_TILE_BLOCKS == np.arange(nt)[:, None]).astype(np.float32)
    k = np.arange(nt)
    w_lo = np.where(k < PLAN_HALF, 2.0 ** np.minimum(k, PLAN_HALF - 1), 0.0)
    w_hi = np.where(k >= PLAN_HALF, 2.0 ** np.maximum(k - PLAN_HALF, 0), 0.0)
    wts = np.stack([np.repeat(w_lo[:, None], LANES, 1), np.repeat(w_hi[:, None], LANES, 1)]).astype(np.float32)
    return grp, wts


def _select_kernel(q_ref, slope_ref, kc_ref, vct_ref, imp_ref, grp_ref, wts_ref, ocmp_ref, bias_ref, bits_ref,
                   *, ns, nc, qb):
    P, T = NSA_PER_GROUP, Q_BLOCK
    step = pl.program_id(2)
    nsteps = pl.num_programs(2)
    subs = range(qb)
    c0s = [(step * qb + u) * T for u in subs]
    qts, tlanes = zip(*[_query_t(q_ref, slope_ref, c0s[u], u * T) for u in subs])

    def body(rows, nblk):
        cend = lax.broadcasted_iota(jnp.int32, (rows, 1), 0) * CMP_STRIDE + (CMP_LEN - 1)
        cmask = [cend <= tlanes[u] for u in subs]
        s = [jnp.where(cmask[u], _dot(kc_ref[0, 0, 0:rows, :], qts[u]), NEG_BIG) for u in subs]
        m = [jnp.max(s[u], 0, keepdims=True) for u in subs]
        p_c = [jnp.where(cmask[u], jnp.exp(s[u] - m[u]), 0.0) for u in subs]
        p_c = [p_c[u] * (1.0 / jnp.maximum(jnp.sum(p_c[u], 0, keepdims=True), 1e-30)) for u in subs]
        for u in subs:
            ocmp_ref[0, 0, u] = _dot(vct_ref[0, 0][:, 0:rows], p_c[u].astype(BF16))

        psum = [p_c[u][:, 0:T] for u in subs]
        for p in range(1, P):
            psum = [psum[u] + p_c[u][:, p * T:(p + 1) * T] for u in subs]
        imp = [_dot_exact_lhs(imp_ref[0:nblk, 0:rows], psum[u]) for u in subs]
        blk = lax.broadcasted_iota(jnp.int32, (nblk, T), 0)
        val, chosen = [], []
        for u in subs:
            cur = (c0s[u] + lax.broadcasted_iota(jnp.int32, (nblk, T), 1)) // SEL_LEN
            forced = (blk == 0) | (blk == cur) | (blk == cur - 1)
            val.append(jnp.where(forced, 3e38, jnp.where(blk > cur, -1.0, imp[u])))
            chosen.append(jnp.zeros((nblk, T), F32))
        for _ in range(min(SEL_TOPK, nblk)):
            for u in subs:
                mx = jnp.max(val[u], 0, keepdims=True)
                first = jnp.min(jnp.where(val[u] == mx, blk, ns), 0, keepdims=True)
                pick = blk == first
                chosen[u] = jnp.where(pick, 1.0, chosen[u])
                val[u] = jnp.where(pick, -2.0, val[u])
        for u in subs:
            bias_ref[0, 0, u, 0:nblk, :] = jnp.where(chosen[u] > 0.5, 0.0, NEG_BIG)
            if nblk < ns:
                bias_ref[0, 0, u, nblk:ns, :] = jnp.full((ns - nblk, T), NEG_BIG, F32)
            ch = chosen[u]
            if nblk < ns:
                ch = jnp.concatenate([ch, jnp.zeros((ns - nblk, T), F32)], 0)
            cnt = _dot(grp_ref[...], ch.astype(BF16))
            act = jnp.where(jnp.max(cnt, 1, keepdims=True) > 0.5, 1.0, 0.0)
            lo = jnp.sum(act * wts_ref[0], 0, keepdims=True).astype(jnp.int32)
            hi = jnp.sum(act * wts_ref[1], 0, keepdims=True).astype(jnp.int32)
            bits_ref[u] = jnp.broadcast_to(lo | (hi << PLAN_HALF), (SUBLANES, LANES))

    parts = max(p for p in (SELECT_PARTS, 2, 1) if nc % (p * LANES) == 0 and ns % (p * 2 * SUBLANES) == 0)
    part = step * parts // nsteps
    for v in range(parts):
        pl.when(part == v)(functools.partial(body, nc * (v + 1) // parts, ns * (v + 1) // parts))


def _attend_kernel(bits_sref, q_ref, gt_ref, slope_ref, ocmp_ref, bias_ref, ks_ref, vst_ref, kw_ref, vwt_ref,
                   o_ref, list_ref, s_ref, m_ref, *, n_tiles_total, ns, qb):
    hd, P, T = NSA_HEAD_DIM, NSA_PER_GROUP, Q_BLOCK
    PT = P * T
    b, g, step = pl.program_id(0), pl.program_id(1), pl.program_id(2)
    subs = range(qb)
    iqs = [step * qb + u for u in subs]
    c0s = [iq * T for iq in iqs]
    qts, tlanes = zip(*[_query_t(q_ref, slope_ref, c0s[u], u * T) for u in subs])
    tl_row = lax.broadcasted_iota(jnp.int32, (1, PT), 1) % T

    gts = [jax.nn.sigmoid(gt_ref[0, u * T:(u + 1) * T, :]).T for u in subs]

    def gate_row(u, branch):
        return jnp.concatenate([gts[u][p * 3 + branch:p * 3 + branch + 1, :] for p in range(P)], 1)

    list_len = n_tiles_total + 1
    last_tiles = [(c0s[u] + T - 1) // SEL_TILE for u in subs]
    prev_bits = []
    for u in subs:
        bits = bits_sref[(b * pl.num_programs(1) + g) * (pl.num_programs(2) * qb) + iqs[u]]
        prev_bits.append(bits & (lax.shift_left(jnp.int32(1), last_tiles[u]) - 1))
    n_prevs = [jnp.int32(0) for _ in subs]
    for k in range(n_tiles_total):
        for u in subs:
            list_ref[u * list_len + n_prevs[u]] = k
            n_prevs[u] = n_prevs[u] + (lax.shift_right_logical(prev_bits[u], k) & 1)

    qt_sels = []
    for u in subs:
        sel_b = bias_ref[0, 0, u]
        if ns < SEL_ONEHOT:
            sel_b = jnp.concatenate([sel_b, jnp.zeros((SEL_ONEHOT - ns, T), F32)], 0)
        qt_sels.append(jnp.concatenate([qts[u], jnp.concatenate([sel_b.astype(BF16)] * P, 1)], 0))

    def tile_scores(u, kt):
        k0 = pl.multiple_of(kt * SEL_TILE, SEL_TILE)
        return _dot(ks_ref[0, 0, pl.ds(k0, SEL_TILE), :], qt_sels[u])

    def col_max(scores):
        m = jnp.max(scores[0], 0, keepdims=True)
        for sc in scores[1:]:
            m = jnp.maximum(m, jnp.max(sc, 0, keepdims=True))
        return m

    def softmax_update(carry, scores, vts, m_tile):
        m_run, acc = carry
        m_new = jnp.maximum(m_run, m_tile)
        acc = jnp.exp(m_run - m_new) * acc
        for sc, vt in zip(scores, vts):
            acc = acc + _dot(vt, jnp.exp(sc - m_new).astype(BF16))
        return m_new, acc

    init = (jnp.full((1, PT), NEG_BIG, F32), jnp.zeros((NSA_VROWS, PT), F32))
    dead_tile = n_tiles_total
    n_pairs = [jnp.maximum((n_prevs[u] + 1) // 2, 1) for u in subs]

    def pair_tiles(u, i):
        has_a, has_b = 2 * i < n_prevs[u], 2 * i + 1 < n_prevs[u]
        ka = jnp.where(has_a, list_ref[u * list_len + jnp.where(has_a, 2 * i, 0)], dead_tile)
        kb = jnp.where(has_b, list_ref[u * list_len + jnp.where(has_b, 2 * i + 1, 0)], dead_tile)
        return ka, kb

    def pair_scores(u, i):
        ka, kb = pair_tiles(u, i)
        keys = jnp.concatenate([ks_ref[0, 0, pl.ds(pl.multiple_of(k * SEL_TILE, SEL_TILE), SEL_TILE), :]
                                for k in (ka, kb)], 0)
        return _dot(keys, qt_sels[u])

    def put_scores(u, i, sc):
        slot = i % 2
        s_ref[u, slot] = sc
        m_ref[u, slot] = col_max([sc])

    def pair_probs(u, i, m_run):
        slot = i % 2
        m_new = jnp.maximum(m_run, m_ref[u, slot])
        return m_new, jnp.exp(s_ref[u, slot] - m_new).astype(BF16)

    def add_values(u, i, carry, m_new, pr):
        ka, kb = pair_tiles(u, i)
        m_run, acc = carry
        vt = jnp.concatenate([vst_ref[0, 0, ka], vst_ref[0, 0, kb]], 1)
        return m_new, jnp.exp(m_run - m_new) * acc + _dot(vt, pr)

    first = [pair_scores(u, 0) for u in subs]
    for u in subs:
        put_scores(u, 0, first[u])

    n_win = (WINDOW + T) // WIN_TILE
    roff = lax.broadcasted_iota(jnp.int32, (WIN_TILE, 1), 0)
    w_sc = [_dot(kw_ref[0, 0, pl.ds(pl.multiple_of(c0s[u], WIN_TILE), WINDOW + T), :], qts[u]) for u in subs]
    w_parts = [[jnp.where(roff > tl_row, w_sc[u][0:WIN_TILE], NEG_BIG),
                w_sc[u][WIN_TILE:WINDOW],
                jnp.where(roff <= tl_row, w_sc[u][WINDOW:WINDOW + T], NEG_BIG)]
               for u in subs]
    w_max = [col_max(w_parts[u]) for u in subs]
    out_cw = []
    for u in subs:
        vt = jnp.concatenate([vwt_ref[0, 0, iqs[u] + j] for j in range(n_win)], 1)
        _, acc_win = softmax_update(init, [jnp.concatenate(w_parts[u], 0)], [vt], w_max[u])
        out_cw.append(gate_row(u, 0) * ocmp_ref[0, 0, u] + gate_row(u, 2) * (acc_win[0:hd] / acc_win[hd:hd + 1]))

    def pipe_step(u, i, carry):
        m_new, pr = pair_probs(u, i, carry[0])
        sc_next = pair_scores(u, i + 1)
        carry = add_values(u, i, carry, m_new, pr)
        put_scores(u, i + 1, sc_next)
        return carry

    carries = [lax.fori_loop(0, n_pairs[u] - 1, functools.partial(pipe_step, u), init) for u in subs]
    probs = [pair_probs(u, n_pairs[u] - 1, carries[u][0]) for u in subs]
    s_last = []
    for u in subs:
        kpos = last_tiles[u] * SEL_TILE + lax.broadcasted_iota(jnp.int32, (SEL_TILE, 1), 0)
        s_last.append(jnp.where(kpos <= tlanes[u], tile_scores(u, last_tiles[u]), NEG_BIG))
    carries = [add_values(u, n_pairs[u] - 1, carries[u], *probs[u]) for u in subs]
    for u in subs:
        _, acc_sel = softmax_update(carries[u], [s_last[u]], [vst_ref[0, 0, last_tiles[u]]], col_max([s_last[u]]))
        out = out_cw[u] + gate_row(u, 1) * (acc_sel[0:hd] / acc_sel[hd:hd + 1])
        o_ref[0, u * T:(u + 1) * T, :] = jnp.concatenate(
            [out[:, p * T:(p + 1) * T] for p in range(P)], 0).T.astype(o_ref.dtype)


def _nsa_call(proj3, pe_k, w1_k, w2_k, pe_v, w1_v, w2_v):
    B, S, _ = proj3.shape
    G, hd, P, T = NSA_GROUPS, NSA_HEAD_DIM, NSA_PER_GROUP, Q_BLOCK
    PT = P * T
    nc = S // CMP_STRIDE
    ns = S // SEL_LEN
    nq = S // T
    nt = ns // SEL_TILE_BLOCKS
    assert nt <= 2 * PLAN_HALF and S % KV_PREP_ROWS == 0

    kc_a, vc_t = _compress_call(proj3, pe_k, w1_k, w2_k, pe_v, w1_v, w2_v)
    ks_a, vs_t, kw_a, vw_t = _kvprep_call(proj3)
    imp_m = jnp.asarray(_importance_matrix(ns, nc), BF16)
    grp, wts = _plan_constants(ns)
    head_slopes = 2.0 ** (-(8.0 / NSA_HEADS) * np.arange(1, NSA_HEADS + 1, dtype=np.float32))
    slopes = jnp.asarray(np.repeat(head_slopes.reshape(G, 1, P), T, axis=-1), F32)

    q_spec = lambda im: pl.BlockSpec((1, T, P * hd), im)
    bg = lambda shp: pl.BlockSpec((1, 1) + shp, lambda b, g, i: (b, g) + tuple(0 for _ in shp))
    const = lambda a: pl.BlockSpec(a.shape, lambda b, g, i: tuple(0 for _ in a.shape))
    qb = SELECT_QB if nq % (SELECT_QB * SELECT_PARTS) == 0 else 1
    nsteps = nq // qb
    ocmp, sel_bias, bits = pl.pallas_call(
        functools.partial(_select_kernel, ns=ns, nc=nc, qb=qb),
        grid=(B, G, nsteps),
        in_specs=[pl.BlockSpec((1, qb * T, P * hd), lambda b, g, i: (b, i, OFF_NQ // (P * hd) + g)),
                  pl.BlockSpec((1, 1, PT), lambda b, g, i: (g, 0, 0)),
                  bg((nc, NSA_KW)), bg((hd, nc)), const(imp_m),
                  pl.BlockSpec(grp.shape, lambda b, g, i: (0, 0)),
                  pl.BlockSpec(wts.shape, lambda b, g, i: (0, 0, 0))],
        out_specs=[pl.BlockSpec((1, 1, qb, hd, PT), lambda b, g, i: (b, g, i, 0, 0)),
                   pl.BlockSpec((1, 1, qb, ns, T), lambda b, g, i: (b, g, i, 0, 0)),
                   pl.BlockSpec((qb, SUBLANES, LANES), lambda b, g, i: ((b * G + g) * nsteps + i, 0, 0))],
        out_shape=[jax.ShapeDtypeStruct((B, G, nq, hd, PT), F32),
                   jax.ShapeDtypeStruct((B, G, nq, ns, T), F32),
                   jax.ShapeDtypeStruct((B * G * nq, SUBLANES, LANES), jnp.int32)],
        compiler_params=_params("parallel", "parallel", "parallel"),
        name="nsa_select",
    )(proj3, slopes, kc_a, vc_t, imp_m, jnp.asarray(grp, BF16), jnp.asarray(wts, F32))

    aq = ATTEND_QB if nq % ATTEND_QB == 0 else 1
    bg2 = lambda shp: pl.BlockSpec((1, 1) + shp, lambda b, g, i, s: (b, g) + tuple(0 for _ in shp))
    return pl.pallas_call(
        functools.partial(_attend_kernel, n_tiles_total=nt, ns=ns, qb=aq),
        grid_spec=pltpu.PrefetchScalarGridSpec(
            num_scalar_prefetch=1,
            grid=(B, G, nq // aq),
            in_specs=[pl.BlockSpec((1, aq * T, P * hd), lambda b, g, i, s: (b, i, OFF_NQ // (P * hd) + g)),
                      pl.BlockSpec((1, aq * T, LANES), lambda b, g, i, s: (b, i, OFF_NG // LANES + g)),
                      pl.BlockSpec((1, 1, PT), lambda b, g, i, s: (g, 0, 0)),
                      pl.BlockSpec((1, 1, aq, hd, PT), lambda b, g, i, s: (b, g, i, 0, 0)),
                      pl.BlockSpec((1, 1, aq, ns, T), lambda b, g, i, s: (b, g, i, 0, 0)),
                      bg2(ks_a.shape[2:]), bg2(vs_t.shape[2:]), bg2(kw_a.shape[2:]), bg2(vw_t.shape[2:])],
            out_specs=pl.BlockSpec((1, aq * T, P * hd), lambda b, g, i, s: (b, i, g)),
            scratch_shapes=[pltpu.SMEM((aq * (nt + 1),), jnp.int32),
                            pltpu.VMEM((aq, 2, 2 * SEL_TILE, PT), F32), pltpu.VMEM((aq, 2, 1, PT), F32)]),
        out_shape=jax.ShapeDtypeStruct((B, S, NSA_WIDTH), BF16),
        compiler_params=_params("parallel", "parallel", "arbitrary"),
        name="nsa_attend",
    )(bits[:, 0, 0], proj3, proj3, slopes, ocmp, sel_bias, ks_a, vs_t, kw_a, vw_t)


def _merge_kernel(x_ref, ya_ref, yb_ref, yc_ref, g0_ref, g1_ref, g2_ref, wa_ref, wb_ref, wc_ref, wo_ref,
                  lg_ref, lb_ref, o_ref):
    gate = lambda r: jax.nn.sigmoid(r[...].astype(F32))
    m = gate(g0_ref) * _dot(ya_ref[...], wa_ref[...])
    m = m + gate(g1_ref) * _dot(yb_ref[...], wb_ref[...])
    m = m + gate(g2_ref) * _dot(yc_ref[...], wc_ref[...])
    z = DEEPNORM_ALPHA * x_ref[...] + _dot(m.astype(BF16), wo_ref[...])
    o_ref[...] = _layer_norm_rows(z, lg_ref[...], lb_ref[...])


def _merge_call(x2, ya, yb, yc, gates, w_hg, w_nsa, w_lru, w_out, layer, ln_g, ln_b, tm=512):
    T, D = x2.shape
    tm = min(tm, T)
    rows = lambda w: pl.BlockSpec((tm, w), lambda i: (i, 0))
    gate = lambda n: pl.BlockSpec((tm, D), lambda i: (i, n))
    full = lambda a: pl.BlockSpec((None,) + a.shape[1:], lambda i: (layer, 0, 0))
    ws = [w.astype(BF16) for w in (w_hg, w_nsa, w_lru, w_out)]
    return pl.pallas_call(
        _merge_kernel,
        grid=(T // tm,),
        in_specs=[rows(D), rows(HG_WIDTH), rows(NSA_WIDTH), rows(LRU_WIDTH), gate(0), gate(1), gate(2)]
                 + [full(w) for w in ws] + [pl.BlockSpec((1, D), lambda i: (0, 0))] * 2,
        out_specs=rows(D),
        out_shape=jax.ShapeDtypeStruct((T, D), F32),
        compiler_params=_params("parallel"),
        name="merge_out",
    )(x2, ya, yb, yc, gates, gates, gates, *ws, ln_g.reshape(1, D), ln_b.reshape(1, D))


FFN_SPLIT = 1
FFN_PAD = SUBLANES
FFN_SUB = 256


def _ffn_kernel(x_ref, wu_ref, wv_ref, cw_ref, cb_ref, wd_ref, lg_ref, lb_ref, o_ref,
                xb_ref, acc_ref, upad_ref, carry_ref, *, tm, tiles_per_seq):
    i = pl.program_id(0)
    j = pl.program_id(1)

    @pl.when(j == 0)
    def _():
        xb_ref[...] = x_ref[...].astype(BF16)

    xb = xb_ref[...]
    fc = wu_ref.shape[1]
    first = (i % tiles_per_seq) == 0

    @pl.when(first)
    def _():
        upad_ref[0:FFN_PAD, :] = jnp.zeros((FFN_PAD, fc), F32)

    @pl.when(jnp.logical_not(first))
    def _():
        upad_ref[0:FFN_PAD, :] = carry_ref[j]

    @pl.when(j == 0)
    def _():
        acc_ref[...] = jnp.zeros_like(acc_ref)

    for c0 in range(0, fc, FFN_SUB):
        cs = slice(c0, min(c0 + FFN_SUB, fc))
        u = _dot(xb, wu_ref[:, cs])
        v = _dot(xb, wv_ref[:, cs])
        upad_ref[FFN_PAD:FFN_PAD + tm, cs] = u
        cw = cw_ref[:, cs]
        cv = cb_ref[:, cs] + cw[FFN_CONV - 1:FFN_CONV, :] * u
        for k in range(FFN_CONV - 1):
            off = FFN_PAD - (FFN_CONV - 1) + k
            cv = cv + cw[k:k + 1, :] * upad_ref[off:off + tm, cs]
        carry_ref[j, :, cs] = u[tm - FFN_PAD:tm, :]
        h = (jax.nn.gelu(cv) * v).astype(BF16)
        acc_ref[...] += _dot(h, wd_ref[cs, :])

    @pl.when(j == FFN_SPLIT - 1)
    def _():
        z = DEEPNORM_ALPHA * x_ref[...] + acc_ref[...]
        o_ref[...] = _layer_norm_rows(z, lg_ref[...], lb_ref[...])


def _ffn_call(x2, seq_len, w_up, conv_w, conv_b, w_down, layer, ln_g, ln_b, tm=512):
    T, D = x2.shape
    tm = min(tm, seq_len)
    fc = FFN_DIM // FFN_SPLIT
    wu = w_up.astype(BF16)
    wmode = pl.Buffered(1) if FFN_SPLIT == 1 else None
    return pl.pallas_call(
        functools.partial(_ffn_kernel, tm=tm, tiles_per_seq=seq_len // tm),
        grid=(T // tm, FFN_SPLIT),
        in_specs=[pl.BlockSpec((tm, D), lambda i, j: (i, 0)),
                  pl.BlockSpec((None, D, fc), lambda i, j: (layer, 0, j), pipeline_mode=wmode),
                  pl.BlockSpec((None, D, fc), lambda i, j: (layer, 0, FFN_SPLIT + j), pipeline_mode=wmode),
                  pl.BlockSpec((FFN_CONV, fc), lambda i, j: (0, j)),
                  pl.BlockSpec((1, fc), lambda i, j: (0, j)),
                  pl.BlockSpec((None, fc, D), lambda i, j: (layer, j, 0), pipeline_mode=wmode),
                  pl.BlockSpec((1, D), lambda i, j: (0, 0)),
                  pl.BlockSpec((1, D), lambda i, j: (0, 0))],
        out_specs=pl.BlockSpec((tm, D), lambda i, j: (i, 0)),
        out_shape=jax.ShapeDtypeStruct((T, D), F32),
        scratch_shapes=[pltpu.VMEM((tm, D), BF16), pltpu.VMEM((tm, D), F32),
                        pltpu.VMEM((tm + FFN_PAD, fc), F32), pltpu.VMEM((FFN_SPLIT, FFN_PAD, fc), F32)],
        compiler_params=_params("arbitrary", "arbitrary"),
        name="conv_ffn",
    )(x2, wu, wu, conv_w.astype(F32), conv_b.reshape(1, FFN_DIM).astype(F32), w_down.astype(BF16),
      ln_g.reshape(1, D), ln_b.reshape(1, D))


def _permute_in_proj(w, b):
    src = np.cumsum([0, HG_WIDTH, HG_WIDTH, HG_WIDTH, HG_WIDTH, NSA_WIDTH, NSA_KV, NSA_KV, NSA_KV, NSA_KV,
                     NSA_KV, NSA_KV, NSA_HEADS * 3, LRU_WIDTH, LRU_WIDTH, N_BRANCH * 1024])
    names = ["hq", "hf", "hi", "hg", "nq", "kc", "vc", "ks", "vs", "kw", "vw", "ng", "lx", "ly", "mg"]
    seg = {n: (int(src[k]), int(src[k + 1])) for k, n in enumerate(names)}
    order = ["mg", "hq", "hf", "hi", "hg", "nq", "lx", "ly", "kc", "vc", "ks", "vs", "kw", "vw"]
    per_group = NSA_PER_GROUP * 3
    ng0 = seg["ng"][0]
    w = w.astype(BF16)
    w_parts = [w[..., seg[n][0]:seg[n][1]] for n in order]
    b_parts = [b[..., seg[n][0]:seg[n][1]] for n in order]
    for gi in range(NSA_GROUPS):
        lo = ng0 + gi * per_group
        w_parts += [w[..., lo:lo + per_group], jnp.zeros(w.shape[:-1] + (LANES - per_group,), w.dtype)]
        b_parts += [b[..., lo:lo + per_group], jnp.zeros(b.shape[:-1] + (LANES - per_group,), b.dtype)]
    wp = jnp.concatenate(w_parts, -1)
    bp = jnp.concatenate(b_parts, -1)
    assert wp.shape[-1] == GATE_COLS + PROJ_COLS
    return wp, bp.astype(F32)[..., None, :]


def kernel(x, ln_emb_g, ln_emb_b, w_in, b_in, hg_lb_logits, hg_norm_g, cmp_pe_k, cmp_w1_k, cmp_w2_k, cmp_pe_v, cmp_w1_v, cmp_w2_v, lru_conv_w, lru_conv_b, lru_wa, lru_ba, lru_wx, lru_bx, lru_lambda, w_branch_hg, w_branch_nsa, w_branch_lru, w_out, ln1_g, ln1_b, ffn_w_up, ffn_conv_w, ffn_conv_b, ffn_w_down, ln2_g, ln2_b):
    B, S, D = x.shape
    T = B * S
    gam = jax.nn.softmax(hg_lb_logits.astype(F32), axis=0)
    lb_all = jnp.cumsum(gam, axis=0) - gam[0]
    h = x.reshape(T, D)
    wp, bp = _permute_in_proj(w_in, b_in)
    for l in range(DEPTH):
        if l == 0:
            gates, proj2, h = _inproj_call(h, wp, bp, l, ln=(ln_emb_g, ln_emb_b))
        else:
            gates, proj2 = _inproj_call(h, wp, bp, l)
        proj3 = proj2.reshape(B, S, PROJ_COLS)
        y_a = _hgrn_call(proj3, lb_all[l], hg_norm_g[l])
        y_b = _nsa_call(proj3, cmp_pe_k[l], cmp_w1_k[l], cmp_w2_k[l], cmp_pe_v[l], cmp_w1_v[l], cmp_w2_v[l])
        y_c = _lru_call(proj3, lru_conv_w[l], lru_conv_b[l], lru_wa[l], lru_ba[l], lru_wx[l], lru_bx[l],
                        lru_lambda[l])
        h = _merge_call(h, y_a.reshape(T, HG_WIDTH), y_b.reshape(T, NSA_WIDTH), y_c.reshape(T, LRU_WIDTH),
                        gates, w_branch_hg, w_branch_nsa, w_branch_lru, w_out, l, ln1_g[l], ln1_b[l])
        h = _ffn_call(h, S, ffn_w_up, ffn_conv_w[l], ffn_conv_b[l], ffn_w_down, l, ln2_g[l], ln2_b[l])
    return h.reshape(B, S, D)
```

```python
import functools

import numpy as np
import jax
import jax.numpy as jnp
from jax import lax
from jax.experimental import pallas as pl
from jax.experimental.pallas import tpu as pltpu

F32 = jnp.float32
BF16 = jnp.bfloat16

DEPTH = 2
HG_HEADS = 4
HG_HEAD_DIM = 128
HG_WIDTH = HG_HEADS * HG_HEAD_DIM
NSA_HEADS = 8
NSA_GROUPS = 2
NSA_PER_GROUP = NSA_HEADS // NSA_GROUPS
NSA_HEAD_DIM = 64
NSA_WIDTH = NSA_HEADS * NSA_HEAD_DIM
NSA_KV = NSA_GROUPS * NSA_HEAD_DIM
CMP_LEN = 32
CMP_STRIDE = 16
CMP_HIDDEN = 64
SEL_LEN = 64
SEL_TOPK = 16
WINDOW = 512
Q_BLOCK = 128
LRU_WIDTH = 512
LRU_BLOCKS = 4
LRU_CONV = 4
LRU_C = 8.0
FFN_DIM = 2816
FFN_CONV = 3
DEEPNORM_ALPHA = (2 * DEPTH) ** 0.25
LN_EPS = 1e-5
N_BRANCH = 3

LANES = 128
SUBLANES = 8
VMEM_LIMIT_BYTES = 52 * 1024 * 1024

GATE_COLS = N_BRANCH * 1024
OFF_HQ = 0
OFF_HF = OFF_HQ + HG_WIDTH
OFF_HI = OFF_HF + HG_WIDTH
OFF_HG = OFF_HI + HG_WIDTH
OFF_NQ = OFF_HG + HG_WIDTH
OFF_LX = OFF_NQ + NSA_WIDTH
OFF_LY = OFF_LX + LRU_WIDTH
OFF_KC = OFF_LY + LRU_WIDTH
OFF_VC = OFF_KC + NSA_KV
OFF_KS = OFF_VC + NSA_KV
OFF_VS = OFF_KS + NSA_KV
OFF_KW = OFF_VS + NSA_KV
OFF_VW = OFF_KW + NSA_KV
OFF_NG = OFF_VW + NSA_KV
PROJ_TILE_N = 1536
PROJ_COLS = OFF_NG + NSA_GROUPS * LANES

NEG_BIG = -1e30


def _split3(x):
    hi = x.astype(BF16)
    r1 = x - hi.astype(F32)
    mid = r1.astype(BF16)
    lo = (r1 - mid.astype(F32)).astype(BF16)
    return hi, mid, lo


def _dot(a, b):
    return jnp.dot(a, b, preferred_element_type=F32)


def _dot_nt(a, b):
    return lax.dot_general(a, b, (((1,), (1,)), ((), ())), preferred_element_type=F32)


def _dot_exact_lhs(m_bf16, x):
    hi, mid, lo = _split3(x)
    return _dot(m_bf16, hi) + _dot(m_bf16, mid) + _dot(m_bf16, lo)


def _dot_hilo(a, b):
    ah = a.astype(BF16)
    al = (a - ah.astype(F32)).astype(BF16)
    bh = b.astype(BF16)
    bl = (b - bh.astype(F32)).astype(BF16)
    return _dot(ah, bh) + _dot(ah, bl) + _dot(al, bh) + _dot(al, bl)


def _layer_norm_rows(z, g, b):
    mu = jnp.mean(z, -1, keepdims=True)
    zc = z - mu
    var = jnp.mean(zc * zc, -1, keepdims=True)
    return zc * lax.rsqrt(var + LN_EPS) * g + b


def _params(*sem):
    return pltpu.CompilerParams(dimension_semantics=sem, vmem_limit_bytes=VMEM_LIMIT_BYTES)


def _inproj_kernel(*refs, with_ln):
    if with_ln:
        x_ref, g_ref, be_ref, w_ref, b_ref, gate_ref, o_ref, xn_ref = refs
    else:
        x_ref, w_ref, b_ref, gate_ref, o_ref = refs
    x = x_ref[...]
    if with_ln:
        x = _layer_norm_rows(x, g_ref[...], be_ref[...])
        xn_ref[...] = x
    xb = x.astype(BF16)
    n = w_ref.shape[1]
    for c0 in range(0, n, PROJ_TILE_N):
        cs = slice(c0, c0 + PROJ_TILE_N)
        acc = _dot(xb, w_ref[:, cs]) + b_ref[:, cs]
        if c0 < GATE_COLS:
            gate_ref[:, cs] = acc.astype(BF16)
        else:
            o_ref[:, c0 - GATE_COLS:c0 - GATE_COLS + PROJ_TILE_N] = acc


def _inproj_call(x2, w_all, bias_all, layer, ln=None, tm=512):
    T, D = x2.shape
    N = w_all.shape[-1]
    tm = min(tm, T)
    assert GATE_COLS % PROJ_TILE_N == 0 and N % PROJ_TILE_N == 0
    vec = pl.BlockSpec((1, D), lambda i: (0, 0))
    in_specs = [pl.BlockSpec((tm, D), lambda i: (i, 0))] + ([vec, vec] if ln else []) + [
        pl.BlockSpec((None, D, N), lambda i: (layer, 0, 0), pipeline_mode=pl.Buffered(1)),
        pl.BlockSpec((None, 1, N), lambda i: (layer, 0, 0))]
    out_specs = [pl.BlockSpec((tm, GATE_COLS), lambda i: (i, 0)),
                 pl.BlockSpec((tm, N - GATE_COLS), lambda i: (i, 0))]
    out_shape = [jax.ShapeDtypeStruct((T, GATE_COLS), BF16), jax.ShapeDtypeStruct((T, N - GATE_COLS), F32)]
    if ln:
        out_specs.append(pl.BlockSpec((tm, D), lambda i: (i, 0)))
        out_shape.append(jax.ShapeDtypeStruct((T, D), F32))
    ln_args = [ln[0].reshape(1, D), ln[1].reshape(1, D)] if ln else []
    return pl.pallas_call(
        functools.partial(_inproj_kernel, with_ln=bool(ln)),
        grid=(T // tm,),
        in_specs=in_specs, out_specs=out_specs, out_shape=out_shape,
        compiler_params=_params("parallel"),
        name="in_proj_ln" if ln else "in_proj",
    )(x2, *ln_args, w_all, bias_all)


HG_CHUNK = 64
HG_DIAG = SUBLANES
HG_HEADS_PER_STEP = 4


def _hgrn_constants(C):
    r = np.arange(C)
    tri = (r[None, :] <= r[:, None]).astype(np.float32)
    mats = [tri]
    masks = []
    w = HG_DIAG
    while 2 * w <= C:
        mid = (r // (2 * w)) * (2 * w) + w
        mats.append(tri - (r[None, :] <= mid[:, None]).astype(np.float32))
        same = (r[:, None] // (2 * w)) == (r[None, :] // (2 * w))
        masks.append((same & ((r[:, None] % (2 * w)) >= w) & ((r[None, :] % (2 * w)) < w)).astype(np.float32))
        w *= 2
    mats.append((r[None, :] > r[:, None]).astype(np.float32))
    return np.concatenate(mats, 0), np.stack(masks, 0)


def _hgrn_kernel(q_ref, f_ref, i_ref, g_ref, lb_ref, ng_ref, m_ref, mask_ref, o_ref, st_ref, *, C, nchunk, nh):
    d = HG_HEAD_DIM
    nlev = mask_ref.shape[0]

    @pl.when(pl.program_id(2) == 0)
    def _():
        st_ref[...] = jnp.zeros_like(st_ref)

    hs = range(nh)
    lane = [slice(h * d, (h + 1) * d) for h in hs]
    lb = [lb_ref[h] for h in hs]
    ng = [ng_ref[h] for h in hs]
    nb = C // HG_DIAG
    row3 = lax.broadcasted_iota(jnp.int32, (nb, HG_DIAG, d), 1)

    fl_all = f_ref[0]
    lf_all = [jnp.log(lb[h] + (1.0 - lb[h]) * jax.nn.sigmoid(fl_all[:, lane[h]])) for h in hs]
    kk_all = [(1.0 - lb[h]) * jax.nn.sigmoid(-fl_all[:, lane[h]]) for h in hs]
    allm_all = _dot_exact_lhs(m_ref[...], jnp.concatenate(
        [lf_all[h][c * C:(c + 1) * C] for h in hs for c in range(nchunk)], 1))

    for c in range(nchunk):
        sl = pl.ds(c * C, C)
        q = [q_ref[0, sl, lane[h]] for h in hs]
        iv = [i_ref[0, sl, lane[h]] for h in hs]
        kk = [kk_all[h][c * C:(c + 1) * C] for h in hs]
        allm = [allm_all[:, (h * nchunk + c) * d:(h * nchunk + c + 1) * d] for h in hs]
        bcs = [allm[h][0:C] for h in hs]
        rem = [allm[h][(nlev + 1) * C:(nlev + 2) * C] for h in hs]
        iv_b = [iv[h].astype(BF16) for h in hs]

        att = [jnp.zeros((C, C), F32) for _ in hs]
        for l in range(nlev):
            e = [jnp.exp(-jnp.abs(allm[h][(1 + l) * C:(2 + l) * C])) for h in hs]
            att = [att[h] + mask_ref[l] * _dot_nt((q[h] * e[h]).astype(BF16), (kk[h] * e[h]).astype(BF16))
                   for h in hs]
        o = [_dot(att[h].astype(BF16), iv_b[h]) for h in hs]

        q3 = [q[h].reshape(nb, HG_DIAG, d) for h in hs]
        k3 = [kk[h].reshape(nb, HG_DIAG, d) for h in hs]
        b3 = [bcs[h].reshape(nb, HG_DIAG, d) for h in hs]
        i3 = [iv[h].reshape(nb, HG_DIAG, d) for h in hs]
        acc = [jnp.zeros((nb, HG_DIAG, d), F32) for _ in hs]
        for s in range(HG_DIAG):
            for h in hs:
                dec = jnp.where(row3 >= s, jnp.exp(b3[h] - b3[h][:, s:s + 1, :]), 0.0)
                a = jnp.sum(q3[h] * dec * k3[h][:, s:s + 1, :], axis=-1, keepdims=True)
                acc[h] = acc[h] + a * i3[h][:, s:s + 1, :]

        for h in hs:
            st = st_ref[h]
            oh = o[h] + acc[h].reshape(C, d) + _dot_nt((q[h] * jnp.exp(bcs[h])).astype(BF16), st.astype(BF16))
            kdec = (kk[h] * jnp.exp(rem[h])).astype(BF16)
            st_ref[h] = st * jnp.exp(bcs[h][C - 1:C, :]) + _dot(iv[h].T.astype(BF16), kdec)
            oh = oh * lax.rsqrt(jnp.mean(oh * oh, -1, keepdims=True) + 1e-6) * ng[h]
            g = g_ref[0, sl, lane[h]]
            o_ref[0, sl, lane[h]] = (oh * (g * jax.nn.sigmoid(g))).astype(o_ref.dtype)


def _hgrn_call(proj3, lb, norm_g, tt=512):
    B, S, _ = proj3.shape
    d = HG_HEAD_DIM
    C = HG_CHUNK
    tt = min(tt, S)
    mstack, masks = _hgrn_constants(C)
    nm = mstack.shape[0]
    nh = HG_HEADS_PER_STEP
    wid = nh * d
    col = lambda off: (lambda b, h, t: (b, t, off // wid + h))
    blk = (1, tt, wid)
    return pl.pallas_call(
        functools.partial(_hgrn_kernel, C=C, nchunk=tt // C, nh=nh),
        grid=(B, HG_HEADS // nh, S // tt),
        in_specs=[pl.BlockSpec(blk, col(OFF_HQ)), pl.BlockSpec(blk, col(OFF_HF)),
                  pl.BlockSpec(blk, col(OFF_HI)), pl.BlockSpec(blk, col(OFF_HG)),
                  pl.BlockSpec((nh, 1, d), lambda b, h, t: (h, 0, 0)),
                  pl.BlockSpec((nh, 1, d), lambda b, h, t: (h, 0, 0)),
                  pl.BlockSpec((nm, C), lambda b, h, t: (0, 0)),
                  pl.BlockSpec(masks.shape, lambda b, h, t: (0, 0, 0))],
        out_specs=pl.BlockSpec(blk, lambda b, h, t: (b, t, h)),
        out_shape=jax.ShapeDtypeStruct((B, S, HG_WIDTH), BF16),
        scratch_shapes=[pltpu.VMEM((nh, d, d), F32)],
        compiler_params=_params("parallel", "parallel", "arbitrary"),
        name="hgrn2",
    )(proj3, proj3, proj3, proj3, lb.reshape(HG_HEADS, 1, d), norm_g.reshape(HG_HEADS, 1, d),
      jnp.asarray(mstack, BF16), jnp.asarray(masks, F32))


LRU_PAD = SUBLANES


def _lru_kernel(x_ref, y_ref, cw_ref, cb_ref, wa_ref, ba_ref, wx_ref, bx_ref, c_ref, ex_ref, o_ref,
                xpad_ref, h_ref, a_ref, u_ref, *, tt):
    W = LRU_WIDTH
    bw = W // LRU_BLOCKS

    @pl.when(pl.program_id(1) == 0)
    def _():
        xpad_ref[0:LRU_PAD, :] = jnp.zeros((LRU_PAD, W), F32)
        h_ref[...] = jnp.zeros_like(h_ref)

    x = x_ref[0]
    xpad_ref[LRU_PAD:LRU_PAD + tt, :] = x
    cw = cw_ref[...]
    xc = cb_ref[...] + cw[LRU_CONV - 1:LRU_CONV, :] * x
    for j in range(LRU_CONV - 1):
        off = LRU_PAD - (LRU_CONV - 1) + j
        xc = xc + cw[j:j + 1, :] * xpad_ref[off:off + tt, :]
    xpad_ref[0:LRU_PAD, :] = x[tt - LRU_PAD:tt, :]

    rs, is_ = [], []
    for gi in range(LRU_BLOCKS):
        xg = xc[:, gi * bw:(gi + 1) * bw].astype(BF16)
        rs.append(_dot(xg, wa_ref[gi]))
        is_.append(_dot(xg, wx_ref[gi]))
    r = jax.nn.sigmoid(jnp.concatenate(rs, -1) + ba_ref[...])
    ig = jax.nn.sigmoid(jnp.concatenate(is_, -1) + bx_ref[...])
    log_a = c_ref[...] * r
    a = jnp.exp(log_a)
    u = jnp.sqrt(-jnp.tanh(log_a) * (a * a + 1.0)) * (ig * xc)

    def scan_rows(a, u, n, group):
        idx = lax.broadcasted_iota(jnp.int32, (n, W), 0) % group
        dlt = 1
        while dlt < group:
            keep = idx >= dlt
            a_sh = jnp.where(keep, pltpu.roll(a, dlt, 0), 1.0)
            u_sh = jnp.where(keep, pltpu.roll(u, dlt, 0), 0.0)
            u = a * u_sh + u
            a = a * a_sh
            dlt *= 2
        return a, u

    a, u = scan_rows(a, u, tt, SUBLANES)
    ng = tt // SUBLANES
    last = pl.ds(SUBLANES - 1, ng, stride=SUBLANES)
    ends = []
    for ref, val in ((a_ref, a), (u_ref, u)):
        for cb in range(W // LANES):
            ref[cb] = val[:, cb * LANES:(cb + 1) * LANES]
        ends.append(jnp.concatenate([ref[cb, last, :] for cb in range(W // LANES)], 1))
    ae, ue = scan_rows(ends[0], ends[1], ng, ng)
    h0 = h_ref[...]
    h_end = ue + ae * h0
    gidx = lax.broadcasted_iota(jnp.int32, (ng, W), 0)
    h_in = jnp.where(gidx >= 1, pltpu.roll(h_end, 1, 0), h0)
    h = u + a * _dot_exact_lhs(ex_ref[...], h_in)
    o_ref[0] = (h * jax.nn.gelu(y_ref[0])).astype(o_ref.dtype)
    h_ref[...] = h_end[ng - 1:ng, :]


def _lru_call(proj3, conv_w, conv_b, wa, ba, wx, bx, lam, tt=512):
    B, S, _ = proj3.shape
    W = LRU_WIDTH
    bw = W // LRU_BLOCKS
    tt = min(tt, S)
    c = (-LRU_C * jax.nn.softplus(-lam.astype(F32))).reshape(1, W)
    expand = (np.arange(tt)[:, None] // SUBLANES == np.arange(tt // SUBLANES)[None, :]).astype(np.float32)
    vec = lambda: pl.BlockSpec((1, W), lambda b, t: (0, 0))
    return pl.pallas_call(
        functools.partial(_lru_kernel, tt=tt),
        grid=(B, S // tt),
        in_specs=[pl.BlockSpec((1, tt, W), lambda b, t: (b, t, OFF_LX // W)),
                  pl.BlockSpec((1, tt, W), lambda b, t: (b, t, OFF_LY // W)),
                  pl.BlockSpec((LRU_CONV, W), lambda b, t: (0, 0)), vec(),
                  pl.BlockSpec((LRU_BLOCKS, bw, bw), lambda b, t: (0, 0, 0)), vec(),
                  pl.BlockSpec((LRU_BLOCKS, bw, bw), lambda b, t: (0, 0, 0)), vec(), vec(),
                  pl.BlockSpec(expand.shape, lambda b, t: (0, 0))],
        out_specs=pl.BlockSpec((1, tt, W), lambda b, t: (b, t, 0)),
        out_shape=jax.ShapeDtypeStruct((B, S, W), BF16),
        scratch_shapes=[pltpu.VMEM((tt + LRU_PAD, W), F32), pltpu.VMEM((1, W), F32),
                        pltpu.VMEM((W // LANES, tt, LANES), F32), pltpu.VMEM((W // LANES, tt, LANES), F32)],
        compiler_params=_params("parallel", "arbitrary"),
        name="rglru",
    )(proj3, proj3, conv_w.astype(F32), conv_b.reshape(1, W).astype(F32), wa.astype(BF16),
      ba.reshape(1, W), wx.astype(BF16), bx.reshape(1, W), c, jnp.asarray(expand, BF16))


NSA_AUG = 4
NSA_KW = LANES
SEL_TILE = 256
SEL_TILE_BLOCKS = SEL_TILE // SEL_LEN
WIN_TILE = 128
POS_SPLIT = 128
SEL_ONEHOT = LANES
PLAN_HALF = 16
SELECT_PARTS = 4
SELECT_QB = 8
ATTEND_QB = 4
NSA_VROWS = NSA_HEAD_DIM + 16
KV_PREP_ROWS = 512


def _key_aug(pos, width):
    n = pos.shape[0]
    col = lax.broadcasted_iota(jnp.int32, (n, width), 1)
    hi = ((pos // POS_SPLIT) * POS_SPLIT).astype(F32)
    lo = (pos % POS_SPLIT).astype(F32)
    return jnp.where(col == 0, hi, jnp.where(col == 1, lo, jnp.where(col < NSA_AUG, 1.0, 0.0)))


def _query_t(q_ref, slope_ref, c0, row0=0):
    hd, P, T = NSA_HEAD_DIM, NSA_PER_GROUP, Q_BLOCK
    PT = P * T
    qT = (q_ref[0, row0:row0 + T, :] * (hd ** -0.5)).T
    qpart = jnp.concatenate([qT[p * hd:(p + 1) * hd, :] for p in range(P)], 1)
    tlane = c0 + lax.broadcasted_iota(jnp.int32, (1, PT), 1) % T
    slope = slope_ref[0]
    t_hi = ((tlane // POS_SPLIT) * POS_SPLIT).astype(F32)
    t_lo = (tlane % POS_SPLIT).astype(F32)
    rowi = lax.broadcasted_iota(jnp.int32, (NSA_KW - hd, PT), 0)
    aug = jnp.where(rowi < 2, slope,
                    jnp.where(rowi == 2, -slope * t_hi,
                              jnp.where(rowi == 3, -slope * t_lo, jnp.where(rowi == NSA_AUG, NEG_BIG, 0.0))))
    return jnp.concatenate([qpart, aug], 0).astype(BF16), tlane


def _compress_kernel(xk_ref, xv_ref, w1k_ref, pek_ref, w1fk_ref, w2k_ref, w1v_ref, pev_ref, w1fv_ref, w2v_ref,
                     kc_ref, vct_ref, *, nc):
    hd = NSA_HEAD_DIM
    pos = lax.broadcasted_iota(jnp.int32, (nc, 1), 0) * CMP_STRIDE + (CMP_LEN - 1)
    aug = _key_aug(pos, NSA_KW - hd)

    def one(x_ref, w1_ref, pe_ref, w1f_ref, w2_ref, g):
        uv = jnp.zeros((nc, 2 * CMP_HIDDEN), F32)
        for r in range(CMP_STRIDE):
            xr = x_ref[0, pl.ds(r, nc, stride=CMP_STRIDE), :][:, g * hd:(g + 1) * hd]
            uv = uv + _dot(xr.astype(BF16), w1_ref[r])
        cvec = _dot_hilo(pe_ref[...], w1f_ref[...])[0:1, :]
        hid = uv[:, 0:CMP_HIDDEN] + pltpu.roll(uv[:, CMP_HIDDEN:], nc - 1, 0) + cvec
        return _dot_hilo(jax.nn.gelu(hid), w2_ref[...])

    for g in range(NSA_GROUPS):
        kc = one(xk_ref, w1k_ref, pek_ref, w1fk_ref, w2k_ref, g)
        kc_ref[0, g] = jnp.concatenate([kc, aug], 1).astype(BF16)
        vct_ref[0, g] = one(xv_ref, w1v_ref, pev_ref, w1fv_ref, w2v_ref, g).T.astype(BF16)


def _compress_call(proj3, pe_k, w1_k, w2_k, pe_v, w1_v, w2_v):
    B, S, _ = proj3.shape
    G, hd = NSA_GROUPS, NSA_HEAD_DIM
    nc = S // CMP_STRIDE
    half = CMP_STRIDE * hd

    def prep(pe, w1):
        w1 = w1.astype(F32)
        wr = jnp.concatenate([w1[:half].reshape(CMP_STRIDE, hd, CMP_HIDDEN),
                              w1[half:].reshape(CMP_STRIDE, hd, CMP_HIDDEN)], -1)
        pe8 = jnp.broadcast_to(pe.reshape(1, CMP_LEN * hd).astype(F32), (SUBLANES, CMP_LEN * hd))
        return wr.astype(BF16), pe8, w1

    full = lambda a: pl.BlockSpec(a.shape, lambda b: tuple(0 for _ in a.shape))
    col = lambda off: pl.BlockSpec((1, S, LANES), lambda b: (b, 0, off // LANES))
    args = prep(pe_k, w1_k) + (w2_k.astype(F32),) + prep(pe_v, w1_v) + (w2_v.astype(F32),)
    return pl.pallas_call(
        functools.partial(_compress_kernel, nc=nc),
        grid=(B,),
        in_specs=[col(OFF_KC), col(OFF_VC)] + [full(a) for a in args],
        out_specs=[pl.BlockSpec((1, G, nc, NSA_KW), lambda b: (b, 0, 0, 0)),
                   pl.BlockSpec((1, G, hd, nc), lambda b: (b, 0, 0, 0))],
        out_shape=[jax.ShapeDtypeStruct((B, G, nc, NSA_KW), BF16),
                   jax.ShapeDtypeStruct((B, G, hd, nc), BF16)],
        compiler_params=_params("parallel"),
        name="nsa_compress",
    )(proj3, proj3, *args)


def _kvprep_constants(S, tt):
    hd = NSA_HEAD_DIM
    pos = np.arange(S + tt)
    live = pos < S
    kcols = np.zeros((S + tt, NSA_KW - hd + SEL_ONEHOT), np.float32)
    kcols[:, 0] = np.where(live, (pos // POS_SPLIT) * POS_SPLIT, 0)
    kcols[:, 1] = np.where(live, pos % POS_SPLIT, 0)
    kcols[:, 2:NSA_AUG] = live[:, None]
    kcols[:, NSA_AUG] = ~live
    kcols[pos[live], NSA_KW - hd + pos[live] // SEL_LEN] = 1.0
    vrows = np.zeros((2, NSA_VROWS - hd, tt), np.float32)
    vrows[0, 0] = 1.0
    return kcols, vrows


def _kvprep_kernel(ks_ref, vs_ref, kw_ref, vw_ref, kc_ref, vr_ref, ksa_ref, vst_ref, kwa_ref, vwt_ref,
                   *, tt, n_live):
    hd = NSA_HEAD_DIM
    dead = pl.program_id(1) >= n_live
    kcols = kc_ref[...]
    ones_rows = vr_ref[0]
    for g in range(NSA_GROUPS):
        lanes = slice(g * hd, (g + 1) * hd)
        live = lambda ref: jnp.where(dead, 0.0, ref[0][:, lanes])
        ksa_ref[0, g] = jnp.concatenate([live(ks_ref).astype(BF16), kcols], 1)
        kwa_ref[0, g] = jnp.concatenate([live(kw_ref).astype(BF16), kcols[:, 0:NSA_KW - hd]], 1)
        vs_t = jnp.concatenate([live(vs_ref).T.astype(BF16), ones_rows], 0)
        vw_t = jnp.concatenate([live(vw_ref).T.astype(BF16), ones_rows], 0)
        for c in range(tt // SEL_TILE):
            vst_ref[0, g, c] = vs_t[:, c * SEL_TILE:(c + 1) * SEL_TILE]
        for c in range(tt // WIN_TILE):
            vwt_ref[0, g, c] = vw_t[:, c * WIN_TILE:(c + 1) * WIN_TILE]


def _kvprep_call(proj3):
    B, S, _ = proj3.shape
    G, hd = NSA_GROUPS, NSA_HEAD_DIM
    tt = min(KV_PREP_ROWS, S)
    n_live = S // tt
    assert tt == WINDOW
    sp = S + tt
    col = lambda off: pl.BlockSpec((1, tt, LANES), lambda b, t: (b, jnp.minimum(t, n_live - 1), off // LANES))
    front = lambda t: (t + 1) % (n_live + 1)
    kcols, vrows = _kvprep_constants(S, tt)
    return pl.pallas_call(
        functools.partial(_kvprep_kernel, tt=tt, n_live=n_live),
        grid=(B, n_live + 1),
        in_specs=[col(OFF_KS), col(OFF_VS), col(OFF_KW), col(OFF_VW),
                  pl.BlockSpec((tt, kcols.shape[1]), lambda b, t: (t, 0)),
                  pl.BlockSpec((1,) + vrows.shape[1:], lambda b, t: (t // n_live, 0, 0))],
        out_specs=[pl.BlockSpec((1, G, tt, NSA_KW + SEL_ONEHOT), lambda b, t: (b, 0, t, 0)),
                   pl.BlockSpec((1, G, tt // SEL_TILE, NSA_VROWS, SEL_TILE), lambda b, t: (b, 0, t, 0, 0)),
                   pl.BlockSpec((1, G, tt, NSA_KW), lambda b, t: (b, 0, front(t), 0)),
                   pl.BlockSpec((1, G, tt // WIN_TILE, NSA_VROWS, WIN_TILE), lambda b, t: (b, 0, front(t), 0, 0))],
        out_shape=[jax.ShapeDtypeStruct((B, G, sp, NSA_KW + SEL_ONEHOT), BF16),
                   jax.ShapeDtypeStruct((B, G, sp // SEL_TILE, NSA_VROWS, SEL_TILE), BF16),
                   jax.ShapeDtypeStruct((B, G, sp, NSA_KW), BF16),
                   jax.ShapeDtypeStruct((B, G, sp // WIN_TILE, NSA_VROWS, WIN_TILE), BF16)],
        compiler_params=_params("parallel", "parallel"),
        name="nsa_kvprep",
    )(proj3, proj3, proj3, proj3, jnp.asarray(kcols, BF16), jnp.asarray(vrows, BF16))


def _importance_matrix(ns, nc):
    ratio = SEL_LEN // CMP_STRIDE
    a = np.zeros((ns, nc), np.float32)
    for j in range(ns):
        for n, wgt in ((ratio * j - 1, 0.5), (ratio * j, 1.0), (ratio * j + 1, 1.0),
                       (ratio * j + 2, 1.0), (ratio * j + 3, 0.5)):
            if 0 <= n < nc - 1:
                a[j, n] = wgt
    return a


def _plan_constants(ns):
    nt = ns // SEL_TILE_BLOCKS
    grp = (np.arange(ns)[None, :] // SEL_TILE_BLOCKS == np.arange(nt)[:, None]).astype(np.float32)
    k = np.arange(nt)
    w_lo = np.where(k < PLAN_HALF, 2.0 ** np.minimum(k, PLAN_HALF - 1), 0.0)
    w_hi = np.where(k >= PLAN_HALF, 2.0 ** np.maximum(k - PLAN_HALF, 0), 0.0)
    wts = np.stack([np.repeat(w_lo[:, None], LANES, 1), np.repeat(w_hi[:, None], LANES, 1)]).astype(np.float32)
    return grp, wts


def _select_kernel(q_ref, slope_ref, kc_ref, vct_ref, imp_ref, grp_ref, wts_ref, ocmp_ref, bias_ref, bits_ref,
                   *, ns, nc, qb):
    P, T = NSA_PER_GROUP, Q_BLOCK
    step = pl.program_id(2)
    nsteps = pl.num_programs(2)
    subs = range(qb)
    c0s = [(step * qb + u) * T for u in subs]
    qts, tlanes = zip(*[_query_t(q_ref, slope_ref, c0s[u], u * T) for u in subs])

    def body(rows, nblk):
        cend = lax.broadcasted_iota(jnp.int32, (rows, 1), 0) * CMP_STRIDE + (CMP_LEN - 1)
        cmask = [cend <= tlanes[u] for u in subs]
        s = [jnp.where(cmask[u], _dot(kc_ref[0, 0, 0:rows, :], qts[u]), NEG_BIG) for u in subs]
        m = [jnp.max(s[u], 0, keepdims=True) for u in subs]
        p_c = [jnp.where(cmask[u], jnp.exp(s[u] - m[u]), 0.0) for u in subs]
        p_c = [p_c[u] * (1.0 / jnp.maximum(jnp.sum(p_c[u], 0, keepdims=True), 1e-30)) for u in subs]
        for u in subs:
            ocmp_ref[0, 0, u] = _dot(vct_ref[0, 0][:, 0:rows], p_c[u].astype(BF16))

        psum = [p_c[u][:, 0:T] for u in subs]
        for p in range(1, P):
            psum = [psum[u] + p_c[u][:, p * T:(p + 1) * T] for u in subs]
        imp = [_dot_exact_lhs(imp_ref[0:nblk, 0:rows], psum[u]) for u in subs]
        blk = lax.broadcasted_iota(jnp.int32, (nblk, T), 0)
        val, chosen = [], []
        for u in subs:
            cur = (c0s[u] + lax.broadcasted_iota(jnp.int32, (nblk, T), 1)) // SEL_LEN
            forced = (blk == 0) | (blk == cur) | (blk == cur - 1)
            val.append(jnp.where(forced, 3e38, jnp.where(blk > cur, -1.0, imp[u])))
            chosen.append(jnp.zeros((nblk, T), F32))
        for _ in range(min(SEL_TOPK, nblk)):
            for u in subs:
                mx = jnp.max(val[u], 0, keepdims=True)
                first = jnp.min(jnp.where(val[u] == mx, blk, ns), 0, keepdims=True)
                pick = blk == first
                chosen[u] = jnp.where(pick, 1.0, chosen[u])
                val[u] = jnp.where(pick, -2.0, val[u])
        for u in subs:
            bias_ref[0, 0, u, 0:nblk, :] = jnp.where(chosen[u] > 0.5, 0.0, NEG_BIG)
            if nblk < ns:
                bias_ref[0, 0, u, nblk:ns, :] = jnp.full((ns - nblk, T), NEG_BIG, F32)
            ch = chosen[u]
            if nblk < ns:
                ch = jnp.concatenate([ch, jnp.zeros((ns - nblk, T), F32)], 0)
            cnt = _dot(grp_ref[...], ch.astype(BF16))
            act = jnp.where(jnp.max(cnt, 1, keepdims=True) > 0.5, 1.0, 0.0)
            lo = jnp.sum(act * wts_ref[0], 0, keepdims=True).astype(jnp.int32)
            hi = jnp.sum(act * wts_ref[1], 0, keepdims=True).astype(jnp.int32)
            bits_ref[u] = jnp.broadcast_to(lo | (hi << PLAN_HALF), (SUBLANES, LANES))

    parts = max(p for p in (SELECT_PARTS, 2, 1) if nc % (p * LANES) == 0 and ns % (p * 2 * SUBLANES) == 0)
    part = step * parts // nsteps
    for v in range(parts):
        pl.when(part == v)(functools.partial(body, nc * (v + 1) // parts, ns * (v + 1) // parts))


def _attend_kernel(bits_sref, q_ref, gt_ref, slope_ref, ocmp_ref, bias_ref, ks_ref, vst_ref, kw_ref, vwt_ref,
                   o_ref, list_ref, s_ref, m_ref, *, n_tiles_total, ns, qb):
    hd, P, T = NSA_HEAD_DIM, NSA_PER_GROUP, Q_BLOCK
    PT = P * T
    b, g, step = pl.program_id(0), pl.program_id(1), pl.program_id(2)
    subs = range(qb)
    iqs = [step * qb + u for u in subs]
    c0s = [iq * T for iq in iqs]
    qts, tlanes = zip(*[_query_t(q_ref, slope_ref, c0s[u], u * T) for u in subs])
    tl_row = lax.broadcasted_iota(jnp.int32, (1, PT), 1) % T

    gts = [jax.nn.sigmoid(gt_ref[0, u * T:(u + 1) * T, :]).T for u in subs]

    def gate_row(u, branch):
        return jnp.concatenate([gts[u][p * 3 + branch:p * 3 + branch + 1, :] for p in range(P)], 1)

    list_len = n_tiles_total + 1
    last_tiles = [(c0s[u] + T - 1) // SEL_TILE for u in subs]
    prev_bits = []
    for u in subs:
        bits = bits_sref[(b * pl.num_programs(1) + g) * (pl.num_programs(2) * qb) + iqs[u]]
        prev_bits.append(bits & (lax.shift_left(jnp.int32(1), last_tiles[u]) - 1))
    n_prevs = [jnp.int32(0) for _ in subs]
    for k in range(n_tiles_total):
        for u in subs:
            list_ref[u * list_len + n_prevs[u]] = k
            n_prevs[u] = n_prevs[u] + (lax.shift_right_logical(prev_bits[u], k) & 1)

    qt_sels = []
    for u in subs:
        sel_b = bias_ref[0, 0, u]
        if ns < SEL_ONEHOT:
            sel_b = jnp.concatenate([sel_b, jnp.zeros((SEL_ONEHOT - ns, T), F32)], 0)
        qt_sels.append(jnp.concatenate([qts[u], jnp.concatenate([sel_b.astype(BF16)] * P, 1)], 0))

    def tile_scores(u, kt):
        k0 = pl.multiple_of(kt * SEL_TILE, SEL_TILE)
        return _dot(ks_ref[0, 0, pl.ds(k0, SEL_TILE), :], qt_sels[u])

    def col_max(scores):
        m = jnp.max(scores[0], 0, keepdims=True)
        for sc in scores[1:]:
            m = jnp.maximum(m, jnp.max(sc, 0, keepdims=True))
        return m

    def softmax_update(carry, scores, vts, m_tile):
        m_run, acc = carry
        m_new = jnp.maximum(m_run, m_tile)
        acc = jnp.exp(m_run - m_new) * acc
        for sc, vt in zip(scores, vts):
            acc = acc + _dot(vt, jnp.exp(sc - m_new).astype(BF16))
        return m_new, acc

    init = (jnp.full((1, PT), NEG_BIG, F32), jnp.zeros((NSA_VROWS, PT), F32))
    dead_tile = n_tiles_total
    n_pairs = [jnp.maximum((n_prevs[u] + 1) // 2, 1) for u in subs]

    def pair_tiles(u, i):
        has_a, has_b = 2 * i < n_prevs[u], 2 * i + 1 < n_prevs[u]
        ka = jnp.where(has_a, list_ref[u * list_len + jnp.where(has_a, 2 * i, 0)], dead_tile)
        kb = jnp.where(has_b, list_ref[u * list_len + jnp.where(has_b, 2 * i + 1, 0)], dead_tile)
        return ka, kb

    def pair_scores(u, i):
        ka, kb = pair_tiles(u, i)
        keys = jnp.concatenate([ks_ref[0, 0, pl.ds(pl.multiple_of(k * SEL_TILE, SEL_TILE), SEL_TILE), :]
                                for k in (ka, kb)], 0)
        return _dot(keys, qt_sels[u])

    def put_scores(u, i, sc):
        slot = i % 2
        s_ref[u, slot] = sc
        m_ref[u, slot] = col_max([sc])

    def pair_probs(u, i, m_run):
        slot = i % 2
        m_new = jnp.maximum(m_run, m_ref[u, slot])
        return m_new, jnp.exp(s_ref[u, slot] - m_new).astype(BF16)

    def add_values(u, i, carry, m_new, pr):
        ka, kb = pair_tiles(u, i)
        m_run, acc = carry
        vt = jnp.concatenate([vst_ref[0, 0, ka], vst_ref[0, 0, kb]], 1)
        return m_new, jnp.exp(m_run - m_new) * acc + _dot(vt, pr)

    first = [pair_scores(u, 0) for u in subs]
    for u in subs:
        put_scores(u, 0, first[u])

    n_win = (WINDOW + T) // WIN_TILE
    roff = lax.broadcasted_iota(jnp.int32, (WIN_TILE, 1), 0)
    w_sc = [_dot(kw_ref[0, 0, pl.ds(pl.multiple_of(c0s[u], WIN_TILE), WINDOW + T), :], qts[u]) for u in subs]
    w_parts = [[jnp.where(roff > tl_row, w_sc[u][0:WIN_TILE], NEG_BIG),
                w_sc[u][WIN_TILE:WINDOW],
                jnp.where(roff <= tl_row, w_sc[u][WINDOW:WINDOW + T], NEG_BIG)]
               for u in subs]
    w_max = [col_max(w_parts[u]) for u in subs]
    out_cw = []
    for u in subs:
        vt = jnp.concatenate([vwt_ref[0, 0, iqs[u] + j] for j in range(n_win)], 1)
        _, acc_win = softmax_update(init, [jnp.concatenate(w_parts[u], 0)], [vt], w_max[u])
        out_cw.append(gate_row(u, 0) * ocmp_ref[0, 0, u] + gate_row(u, 2) * (acc_win[0:hd] / acc_win[hd:hd + 1]))

    def pipe_step(u, i, carry):
        m_new, pr = pair_probs(u, i, carry[0])
        sc_next = pair_scores(u, i + 1)
        carry = add_values(u, i, carry, m_new, pr)
        put_scores(u, i + 1, sc_next)
        return carry

    carries = [lax.fori_loop(0, n_pairs[u] - 1, functools.partial(pipe_step, u), init) for u in subs]
    probs = [pair_probs(u, n_pairs[u] - 1, carries[u][0]) for u in subs]
    s_last = []
    for u in subs:
        kpos = last_tiles[u] * SEL_TILE + lax.broadcasted_iota(jnp.int32, (SEL_TILE, 1), 0)
        s_last.append(jnp.where(kpos <= tlanes[u], tile_scores(u, last_tiles[u]), NEG_BIG))
    carries = [add_values(u, n_pairs[u] - 1, carries[u], *probs[u]) for u in subs]
    for u in subs:
        _, acc_sel = softmax_update(carries[u], [s_last[u]], [vst_ref[0, 0, last_tiles[u]]], col_max([s_last[u]]))
        out = out_cw[u] + gate_row(u, 1) * (acc_sel[0:hd] / acc_sel[hd:hd + 1])
        o_ref[0, u * T:(u + 1) * T, :] = jnp.concatenate(
            [out[:, p * T:(p + 1) * T] for p in range(P)], 0).T.astype(o_ref.dtype)


def _nsa_call(proj3, pe_k, w1_k, w2_k, pe_v, w1_v, w2_v):
    B, S, _ = proj3.shape
    G, hd, P, T = NSA_GROUPS, NSA_HEAD_DIM, NSA_PER_GROUP, Q_BLOCK
    PT = P * T
    nc = S // CMP_STRIDE
    ns = S // SEL_LEN
    nq = S // T
    nt = ns // SEL_TILE_BLOCKS
    assert nt <= 2 * PLAN_HALF and S % KV_PREP_ROWS == 0

    kc_a, vc_t = _compress_call(proj3, pe_k, w1_k, w2_k, pe_v, w1_v, w2_v)
    ks_a, vs_t, kw_a, vw_t = _kvprep_call(proj3)
    imp_m = jnp.asarray(_importance_matrix(ns, nc), BF16)
    grp, wts = _plan_constants(ns)
    head_slopes = 2.0 ** (-(8.0 / NSA_HEADS) * np.arange(1, NSA_HEADS + 1, dtype=np.float32))
    slopes = jnp.asarray(np.repeat(head_slopes.reshape(G, 1, P), T, axis=-1), F32)

    q_spec = lambda im: pl.BlockSpec((1, T, P * hd), im)
    bg = lambda shp: pl.BlockSpec((1, 1) + shp, lambda b, g, i: (b, g) + tuple(0 for _ in shp))
    const = lambda a: pl.BlockSpec(a.shape, lambda b, g, i: tuple(0 for _ in a.shape))
    qb = SELECT_QB if nq % (SELECT_QB * SELECT_PARTS) == 0 else 1
    nsteps = nq // qb
    ocmp, sel_bias, bits = pl.pallas_call(
        functools.partial(_select_kernel, ns=ns, nc=nc, qb=qb),
        grid=(B, G, nsteps),
        in_specs=[pl.BlockSpec((1, qb * T, P * hd), lambda b, g, i: (b, i, OFF_NQ // (P * hd) + g)),
                  pl.BlockSpec((1, 1, PT), lambda b, g, i: (g, 0, 0)),
                  bg((nc, NSA_KW)), bg((hd, nc)), const(imp_m),
                  pl.BlockSpec(grp.shape, lambda b, g, i: (0, 0)),
                  pl.BlockSpec(wts.shape, lambda b, g, i: (0, 0, 0))],
        out_specs=[pl.BlockSpec((1, 1, qb, hd, PT), lambda b, g, i: (b, g, i, 0, 0)),
                   pl.BlockSpec((1, 1, qb, ns, T), lambda b, g, i: (b, g, i, 0, 0)),
                   pl.BlockSpec((qb, SUBLANES, LANES), lambda b, g, i: ((b * G + g) * nsteps + i, 0, 0))],
        out_shape=[jax.ShapeDtypeStruct((B, G, nq, hd, PT), F32),
                   jax.ShapeDtypeStruct((B, G, nq, ns, T), F32),
                   jax.ShapeDtypeStruct((B * G * nq, SUBLANES, LANES), jnp.int32)],
        compiler_params=_params("parallel", "parallel", "parallel"),
        name="nsa_select",
    )(proj3, slopes, kc_a, vc_t, imp_m, jnp.asarray(grp, BF16), jnp.asarray(wts, F32))

    aq = ATTEND_QB if nq % ATTEND_QB == 0 else 1
    bg2 = lambda shp: pl.BlockSpec((1, 1) + shp, lambda b, g, i, s: (b, g) + tuple(0 for _ in shp))
    return pl.pallas_call(
        functools.partial(_attend_kernel, n_tiles_total=nt, ns=ns, qb=aq),
        grid_spec=pltpu.PrefetchScalarGridSpec(
            num_scalar_prefetch=1,
            grid=(B, G, nq // aq),
            in_specs=[pl.BlockSpec((1, aq * T, P * hd), lambda b, g, i, s: (b, i, OFF_NQ // (P * hd) + g)),
                      pl.BlockSpec((1, aq * T, LANES), lambda b, g, i, s: (b, i, OFF_NG // LANES + g)),
                      pl.BlockSpec((1, 1, PT), lambda b, g, i, s: (g, 0, 0)),
                      pl.BlockSpec((1, 1, aq, hd, PT), lambda b, g, i, s: (b, g, i, 0, 0)),
                      pl.BlockSpec((1, 1, aq, ns, T), lambda b, g, i, s: (b, g, i, 0, 0)),
                      bg2(ks_a.shape[2:]), bg2(vs_t.shape[2:]), bg2(kw_a.shape[2:]), bg2(vw_t.shape[2:])],
            out_specs=pl.BlockSpec((1, aq * T, P * hd), lambda b, g, i, s: (b, i, g)),
            scratch_shapes=[pltpu.SMEM((aq * (nt + 1),), jnp.int32),
                            pltpu.VMEM((aq, 2, 2 * SEL_TILE, PT), F32), pltpu.VMEM((aq, 2, 1, PT), F32)]),
        out_shape=jax.ShapeDtypeStruct((B, S, NSA_WIDTH), BF16),
        compiler_params=_params("parallel", "parallel", "arbitrary"),
        name="nsa_attend",
    )(bits[:, 0, 0], proj3, proj3, slopes, ocmp, sel_bias, ks_a, vs_t, kw_a, vw_t)


FFN_PAD = SUBLANES
FFN_SUB = 256


def _merge_ffn_kernel(x_ref, ya_ref, yb_ref, yc_ref, g0_ref, g1_ref, g2_ref, wa_ref, wb_ref, wc_ref, wo_ref,
                      l1g_ref, l1b_ref, wu_ref, wv_ref, cw_ref, cb_ref, wd_ref, l2g_ref, l2b_ref, o_ref,
                      h1_ref, acc_ref, upad_ref, carry_ref, *, tm, tiles_per_seq):
    gate = lambda r: jax.nn.sigmoid(r[...].astype(F32))
    m = gate(g0_ref) * _dot(ya_ref[...], wa_ref[...])
    m = m + gate(g1_ref) * _dot(yb_ref[...], wb_ref[...])
    m = m + gate(g2_ref) * _dot(yc_ref[...], wc_ref[...])
    z = DEEPNORM_ALPHA * x_ref[...] + _dot(m.astype(BF16), wo_ref[...])
    h1 = _layer_norm_rows(z, l1g_ref[...], l1b_ref[...])
    h1_ref[...] = h1
    xb = h1.astype(BF16)

    fc = wu_ref.shape[1]

    @pl.when((pl.program_id(0) % tiles_per_seq) == 0)
    def _():
        carry_ref[...] = jnp.zeros_like(carry_ref)

    acc_ref[...] = jnp.zeros_like(acc_ref)
    for c0 in range(0, fc, FFN_SUB):
        cs = slice(c0, min(c0 + FFN_SUB, fc))
        w = cs.stop - cs.start
        u = _dot(xb, wu_ref[:, cs])
        v = _dot(xb, wv_ref[:, cs])
        upad_ref[0:FFN_PAD, 0:w] = carry_ref[:, cs]
        upad_ref[FFN_PAD:FFN_PAD + tm, 0:w] = u
        cw = cw_ref[:, cs]
        cv = cb_ref[:, cs] + cw[FFN_CONV - 1:FFN_CONV, :] * u
        for k in range(FFN_CONV - 1):
            off = FFN_PAD - (FFN_CONV - 1) + k
            cv = cv + cw[k:k + 1, :] * upad_ref[off:off + tm, 0:w]
        carry_ref[:, cs] = u[tm - FFN_PAD:tm, :]
        h = (jax.nn.gelu(cv) * v).astype(BF16)
        acc_ref[...] += _dot(h, wd_ref[cs, :])

    z2 = DEEPNORM_ALPHA * h1_ref[...] + acc_ref[...]
    o_ref[...] = _layer_norm_rows(z2, l2g_ref[...], l2b_ref[...])


def _merge_ffn_call(x2, seq_len, ya, yb, yc, gates, w_hg, w_nsa, w_lru, w_out, ln1_g, ln1_b,
                    w_up, conv_w, conv_b, w_down, ln2_g, ln2_b, layer, tm=512):
    T, D = x2.shape
    tm = min(tm, seq_len)
    fc = FFN_DIM
    once = pl.Buffered(1)
    rows = lambda wid: pl.BlockSpec((tm, wid), lambda i: (i, 0))
    gate = lambda n: pl.BlockSpec((tm, D), lambda i: (i, n))
    lw = lambda a: pl.BlockSpec((None,) + a.shape[1:], lambda i: (layer, 0, 0), pipeline_mode=once)
    vec = lambda n: pl.BlockSpec((1, n), lambda i: (0, 0))
    ws = [w.astype(BF16) for w in (w_hg, w_nsa, w_lru, w_out)]
    wu = w_up.astype(BF16)
    wd = w_down.astype(BF16)
    return pl.pallas_call(
        functools.partial(_merge_ffn_kernel, tm=tm, tiles_per_seq=seq_len // tm),
        grid=(T // tm,),
        in_specs=[rows(D), rows(HG_WIDTH), rows(NSA_WIDTH), rows(LRU_WIDTH), gate(0), gate(1), gate(2)]
                 + [lw(w) for w in ws] + [vec(D), vec(D),
                    pl.BlockSpec((None, D, fc), lambda i: (layer, 0, 0), pipeline_mode=once),
                    pl.BlockSpec((None, D, fc), lambda i: (layer, 0, 1), pipeline_mode=once),
                    pl.BlockSpec((FFN_CONV, fc), lambda i: (0, 0)), vec(fc), lw(wd), vec(D), vec(D)],
        out_specs=rows(D),
        out_shape=jax.ShapeDtypeStruct((T, D), F32),
        scratch_shapes=[pltpu.VMEM((tm, D), F32), pltpu.VMEM((tm, D), F32),
                        pltpu.VMEM((tm + FFN_PAD, FFN_SUB), F32), pltpu.VMEM((FFN_PAD, fc), F32)],
        compiler_params=_params("arbitrary"),
        name="merge_ffn",
    )(x2, ya, yb, yc, gates, gates, gates, *ws, ln1_g.reshape(1, D), ln1_b.reshape(1, D),
      wu, wu, conv_w.astype(F32), conv_b.reshape(1, FFN_DIM).astype(F32), wd,
      ln2_g.reshape(1, D), ln2_b.reshape(1, D))


def _permute_in_proj(w, b):
    src = np.cumsum([0, HG_WIDTH, HG_WIDTH, HG_WIDTH, HG_WIDTH, NSA_WIDTH, NSA_KV, NSA_KV, NSA_KV, NSA_KV,
                     NSA_KV, NSA_KV, NSA_HEADS * 3, LRU_WIDTH, LRU_WIDTH, N_BRANCH * 1024])
    names = ["hq", "hf", "hi", "hg", "nq", "kc", "vc", "ks", "vs", "kw", "vw", "ng", "lx", "ly", "mg"]
    seg = {n: (int(src[k]), int(src[k + 1])) for k, n in enumerate(names)}
    order = ["mg", "hq", "hf", "hi", "hg", "nq", "lx", "ly", "kc", "vc", "ks", "vs", "kw", "vw"]
    per_group = NSA_PER_GROUP * 3
    ng0 = seg["ng"][0]
    w = w.astype(BF16)
    w_parts = [w[..., seg[n][0]:seg[n][1]] for n in order]
    b_parts = [b[..., seg[n][0]:seg[n][1]] for n in order]
    for gi in range(NSA_GROUPS):
        lo = ng0 + gi * per_group
        w_parts += [w[..., lo:lo + per_group], jnp.zeros(w.shape[:-1] + (LANES - per_group,), w.dtype)]
        b_parts += [b[..., lo:lo + per_group], jnp.zeros(b.shape[:-1] + (LANES - per_group,), b.dtype)]
    wp = jnp.concatenate(w_parts, -1)
    bp = jnp.concatenate(b_parts, -1)
    assert wp.shape[-1] == GATE_COLS + PROJ_COLS
    return wp, bp.astype(F32)[..., None, :]


def kernel(x, ln_emb_g, ln_emb_b, w_in, b_in, hg_lb_logits, hg_norm_g, cmp_pe_k, cmp_w1_k, cmp_w2_k, cmp_pe_v, cmp_w1_v, cmp_w2_v, lru_conv_w, lru_conv_b, lru_wa, lru_ba, lru_wx, lru_bx, lru_lambda, w_branch_hg, w_branch_nsa, w_branch_lru, w_out, ln1_g, ln1_b, ffn_w_up, ffn_conv_w, ffn_conv_b, ffn_w_down, ln2_g, ln2_b):
    B, S, D = x.shape
    T = B * S
    gam = jax.nn.softmax(hg_lb_logits.astype(F32), axis=0)
    lb_all = jnp.cumsum(gam, axis=0) - gam[0]
    h = x.reshape(T, D)
    wp, bp = _permute_in_proj(w_in, b_in)
    for l in range(DEPTH):
        if l == 0:
            gates, proj2, h = _inproj_call(h, wp, bp, l, ln=(ln_emb_g, ln_emb_b))
        else:
            gates, proj2 = _inproj_call(h, wp, bp, l)
        proj3 = proj2.reshape(B, S, PROJ_COLS)
        y_a = _hgrn_call(proj3, lb_all[l], hg_norm_g[l])
        y_b = _nsa_call(proj3, cmp_pe_k[l], cmp_w1_k[l], cmp_w2_k[l], cmp_pe_v[l], cmp_w1_v[l], cmp_w2_v[l])
        y_c = _lru_call(proj3, lru_conv_w[l], lru_conv_b[l], lru_wa[l], lru_ba[l], lru_wx[l], lru_bx[l],
                        lru_lambda[l])
        h = _merge_ffn_call(h, S, y_a.reshape(T, HG_WIDTH), y_b.reshape(T, NSA_WIDTH), y_c.reshape(T, LRU_WIDTH),
                            gates, w_branch_hg, w_branch_nsa, w_branch_lru, w_out, ln1_g[l], ln1_b[l],
                            ffn_w_up, ffn_conv_w[l], ffn_conv_b[l], ffn_w_down, ln2_g[l], ln2_b[l], l)
    return h.reshape(B, S, D)
```

```python
import functools

import numpy as np
import jax
import jax.numpy as jnp
from jax import lax
from jax.experimental import pallas as pl
from jax.experimental.pallas import tpu as pltpu

F32 = jnp.float32
BF16 = jnp.bfloat16

DEPTH = 2
HG_HEADS = 4
HG_HEAD_DIM = 128
HG_WIDTH = HG_HEADS * HG_HEAD_DIM
NSA_HEADS = 8
NSA_GROUPS = 2
NSA_PER_GROUP = NSA_HEADS // NSA_GROUPS
NSA_HEAD_DIM = 64
NSA_WIDTH = NSA_HEADS * NSA_HEAD_DIM
NSA_KV = NSA_GROUPS * NSA_HEAD_DIM
CMP_LEN = 32
CMP_STRIDE = 16
CMP_HIDDEN = 64
SEL_LEN = 64
SEL_TOPK = 16
WINDOW = 512
Q_BLOCK = 128
LRU_WIDTH = 512
LRU_BLOCKS = 4
LRU_CONV = 4
LRU_C = 8.0
FFN_DIM = 2816
FFN_CONV = 3
DEEPNORM_ALPHA = (2 * DEPTH) ** 0.25
LN_EPS = 1e-5
N_BRANCH = 3

LANES = 128
SUBLANES = 8
VMEM_LIMIT_BYTES = 52 * 1024 * 1024

GATE_COLS = N_BRANCH * 1024
OFF_HQ = 0
OFF_HF = OFF_HQ + HG_WIDTH
OFF_HI = OFF_HF + HG_WIDTH
OFF_HG = OFF_HI + HG_WIDTH
OFF_NQ = OFF_HG + HG_WIDTH
OFF_LX = OFF_NQ + NSA_WIDTH
OFF_LY = OFF_LX + LRU_WIDTH
OFF_KC = OFF_LY + LRU_WIDTH
OFF_VC = OFF_KC + NSA_KV
OFF_KS = OFF_VC + NSA_KV
OFF_VS = OFF_KS + NSA_KV
OFF_KW = OFF_VS + NSA_KV
OFF_VW = OFF_KW + NSA_KV
OFF_NG = OFF_VW + NSA_KV
PROJ_TILE_N = 1536
PROJ_COLS = OFF_NG + NSA_GROUPS * LANES

NEG_BIG = -1e30


def _split3(x):
    hi = x.astype(BF16)
    r1 = x - hi.astype(F32)
    mid = r1.astype(BF16)
    lo = (r1 - mid.astype(F32)).astype(BF16)
    return hi, mid, lo


def _dot(a, b):
    return jnp.dot(a, b, preferred_element_type=F32)


def _dot_nt(a, b):
    return lax.dot_general(a, b, (((1,), (1,)), ((), ())), preferred_element_type=F32)


def _dot_exact_lhs(m_bf16, x):
    hi, mid, lo = _split3(x)
    return _dot(m_bf16, hi) + _dot(m_bf16, mid) + _dot(m_bf16, lo)


def _dot_hilo(a, b):
    ah = a.astype(BF16)
    al = (a - ah.astype(F32)).astype(BF16)
    bh = b.astype(BF16)
    bl = (b - bh.astype(F32)).astype(BF16)
    return _dot(ah, bh) + _dot(ah, bl) + _dot(al, bh) + _dot(al, bl)


def _layer_norm_rows(z, g, b):
    mu = jnp.mean(z, -1, keepdims=True)
    zc = z - mu
    var = jnp.mean(zc * zc, -1, keepdims=True)
    return zc * lax.rsqrt(var + LN_EPS) * g + b


def _params(*sem):
    return pltpu.CompilerParams(dimension_semantics=sem, vmem_limit_bytes=VMEM_LIMIT_BYTES)


def _inproj_kernel(*refs, with_ln):
    if with_ln:
        x_ref, g_ref, be_ref, w_ref, b_ref, gate_ref, o_ref, xn_ref = refs
    else:
        x_ref, w_ref, b_ref, gate_ref, o_ref = refs
    x = x_ref[...]
    if with_ln:
        x = _layer_norm_rows(x, g_ref[...], be_ref[...])
        xn_ref[...] = x
    xb = x.astype(BF16)
    n = w_ref.shape[1]
    for c0 in range(0, n, PROJ_TILE_N):
        cs = slice(c0, c0 + PROJ_TILE_N)
        acc = _dot(xb, w_ref[:, cs]) + b_ref[:, cs]
        if c0 < GATE_COLS:
            gate_ref[:, cs] = acc.astype(BF16)
        else:
            o_ref[:, c0 - GATE_COLS:c0 - GATE_COLS + PROJ_TILE_N] = acc


def _inproj_call(x2, w_all, bias_all, layer, ln=None, tm=512):
    T, D = x2.shape
    N = w_all.shape[-1]
    tm = min(tm, T)
    assert GATE_COLS % PROJ_TILE_N == 0 and N % PROJ_TILE_N == 0
    vec = pl.BlockSpec((1, D), lambda i: (0, 0))
    in_specs = [pl.BlockSpec((tm, D), lambda i: (i, 0))] + ([vec, vec] if ln else []) + [
        pl.BlockSpec((None, D, N), lambda i: (layer, 0, 0), pipeline_mode=pl.Buffered(1)),
        pl.BlockSpec((None, 1, N), lambda i: (layer, 0, 0))]
    out_specs = [pl.BlockSpec((tm, GATE_COLS), lambda i: (i, 0)),
                 pl.BlockSpec((tm, N - GATE_COLS), lambda i: (i, 0))]
    out_shape = [jax.ShapeDtypeStruct((T, GATE_COLS), BF16), jax.ShapeDtypeStruct((T, N - GATE_COLS), F32)]
    if ln:
        out_specs.append(pl.BlockSpec((tm, D), lambda i: (i, 0)))
        out_shape.append(jax.ShapeDtypeStruct((T, D), F32))
    ln_args = [ln[0].reshape(1, D), ln[1].reshape(1, D)] if ln else []
    return pl.pallas_call(
        functools.partial(_inproj_kernel, with_ln=bool(ln)),
        grid=(T // tm,),
        in_specs=in_specs, out_specs=out_specs, out_shape=out_shape,
        compiler_params=_params("parallel"),
        name="in_proj_ln" if ln else "in_proj",
    )(x2, *ln_args, w_all, bias_all)


HG_CHUNK = 64
HG_DIAG = SUBLANES
HG_HEADS_PER_STEP = 4


def _hgrn_constants(C):
    r = np.arange(C)
    tri = (r[None, :] <= r[:, None]).astype(np.float32)
    mats = [tri]
    masks = []
    w = HG_DIAG
    while 2 * w <= C:
        mid = (r // (2 * w)) * (2 * w) + w
        mats.append(tri - (r[None, :] <= mid[:, None]).astype(np.float32))
        same = (r[:, None] // (2 * w)) == (r[None, :] // (2 * w))
        masks.append((same & ((r[:, None] % (2 * w)) >= w) & ((r[None, :] % (2 * w)) < w)).astype(np.float32))
        w *= 2
    mats.append((r[None, :] > r[:, None]).astype(np.float32))
    return np.concatenate(mats, 0), np.stack(masks, 0)


def _hgrn_kernel(q_ref, f_ref, i_ref, g_ref, lb_ref, ng_ref, m_ref, mask_ref, o_ref, st_ref, *, C, nchunk, nh):
    d = HG_HEAD_DIM
    nlev = mask_ref.shape[0]

    @pl.when(pl.program_id(2) == 0)
    def _():
        st_ref[...] = jnp.zeros_like(st_ref)

    hs = range(nh)
    lane = [slice(h * d, (h + 1) * d) for h in hs]
    lb = [lb_ref[h] for h in hs]
    ng = [ng_ref[h] for h in hs]
    nb = C // HG_DIAG
    row3 = lax.broadcasted_iota(jnp.int32, (nb, HG_DIAG, d), 1)

    fl_all = f_ref[0]
    lf_all = [jnp.log(lb[h] + (1.0 - lb[h]) * jax.nn.sigmoid(fl_all[:, lane[h]])) for h in hs]
    kk_all = [(1.0 - lb[h]) * jax.nn.sigmoid(-fl_all[:, lane[h]]) for h in hs]
    allm_all = _dot_exact_lhs(m_ref[...], jnp.concatenate(
        [lf_all[h][c * C:(c + 1) * C] for h in hs for c in range(nchunk)], 1))

    for c in range(nchunk):
        sl = pl.ds(c * C, C)
        q = [q_ref[0, sl, lane[h]] for h in hs]
        iv = [i_ref[0, sl, lane[h]] for h in hs]
        kk = [kk_all[h][c * C:(c + 1) * C] for h in hs]
        allm = [allm_all[:, (h * nchunk + c) * d:(h * nchunk + c + 1) * d] for h in hs]
        bcs = [allm[h][0:C] for h in hs]
        rem = [allm[h][(nlev + 1) * C:(nlev + 2) * C] for h in hs]
        iv_b = [iv[h].astype(BF16) for h in hs]

        att = [jnp.zeros((C, C), F32) for _ in hs]
        for l in range(nlev):
            e = [jnp.exp(-jnp.abs(allm[h][(1 + l) * C:(2 + l) * C])) for h in hs]
            att = [att[h] + mask_ref[l] * _dot_nt((q[h] * e[h]).astype(BF16), (kk[h] * e[h]).astype(BF16))
                   for h in hs]
        o = [_dot(att[h].astype(BF16), iv_b[h]) for h in hs]

        q3 = [q[h].reshape(nb, HG_DIAG, d) for h in hs]
        k3 = [kk[h].reshape(nb, HG_DIAG, d) for h in hs]
        b3 = [bcs[h].reshape(nb, HG_DIAG, d) for h in hs]
        i3 = [iv[h].reshape(nb, HG_DIAG, d) for h in hs]
        acc = [jnp.zeros((nb, HG_DIAG, d), F32) for _ in hs]
        for s in range(HG_DIAG):
            for h in hs:
                dec = jnp.where(row3 >= s, jnp.exp(b3[h] - b3[h][:, s:s + 1, :]), 0.0)
                a = jnp.sum(q3[h] * dec * k3[h][:, s:s + 1, :], axis=-1, keepdims=True)
                acc[h] = acc[h] + a * i3[h][:, s:s + 1, :]

        for h in hs:
            st = st_ref[h]
            oh = o[h] + acc[h].reshape(C, d) + _dot_nt((q[h] * jnp.exp(bcs[h])).astype(BF16), st.astype(BF16))
            kdec = (kk[h] * jnp.exp(rem[h])).astype(BF16)
            st_ref[h] = st * jnp.exp(bcs[h][C - 1:C, :]) + _dot(iv[h].T.astype(BF16), kdec)
            oh = oh * lax.rsqrt(jnp.mean(oh * oh, -1, keepdims=True) + 1e-6) * ng[h]
            g = g_ref[0, sl, lane[h]]
            o_ref[0, sl, lane[h]] = (oh * (g * jax.nn.sigmoid(g))).astype(o_ref.dtype)


def _hgrn_call(proj3, lb, norm_g, tt=512):
    B, S, _ = proj3.shape
    d = HG_HEAD_DIM
    C = HG_CHUNK
    tt = min(tt, S)
    mstack, masks = _hgrn_constants(C)
    nm = mstack.shape[0]
    nh = HG_HEADS_PER_STEP
    wid = nh * d
    col = lambda off: (lambda b, h, t: (b, t, off // wid + h))
    blk = (1, tt, wid)
    return pl.pallas_call(
        functools.partial(_hgrn_kernel, C=C, nchunk=tt // C, nh=nh),
        grid=(B, HG_HEADS // nh, S // tt),
        in_specs=[pl.BlockSpec(blk, col(OFF_HQ)), pl.BlockSpec(blk, col(OFF_HF)),
                  pl.BlockSpec(blk, col(OFF_HI)), pl.BlockSpec(blk, col(OFF_HG)),
                  pl.BlockSpec((nh, 1, d), lambda b, h, t: (h, 0, 0)),
                  pl.BlockSpec((nh, 1, d), lambda b, h, t: (h, 0, 0)),
                  pl.BlockSpec((nm, C), lambda b, h, t: (0, 0)),
                  pl.BlockSpec(masks.shape, lambda b, h, t: (0, 0, 0))],
        out_specs=pl.BlockSpec(blk, lambda b, h, t: (b, t, h)),
        out_shape=jax.ShapeDtypeStruct((B, S, HG_WIDTH), BF16),
        scratch_shapes=[pltpu.VMEM((nh, d, d), F32)],
        compiler_params=_params("parallel", "parallel", "arbitrary"),
        name="hgrn2",
    )(proj3, proj3, proj3, proj3, lb.reshape(HG_HEADS, 1, d), norm_g.reshape(HG_HEADS, 1, d),
      jnp.asarray(mstack, BF16), jnp.asarray(masks, F32))


LRU_PAD = SUBLANES


def _lru_kernel(x_ref, y_ref, cw_ref, cb_ref, wa_ref, ba_ref, wx_ref, bx_ref, c_ref, ex_ref, o_ref,
                xpad_ref, h_ref, a_ref, u_ref, *, tt):
    W = LRU_WIDTH
    bw = W // LRU_BLOCKS

    @pl.when(pl.program_id(1) == 0)
    def _():
        xpad_ref[0:LRU_PAD, :] = jnp.zeros((LRU_PAD, W), F32)
        h_ref[...] = jnp.zeros_like(h_ref)

    x = x_ref[0]
    xpad_ref[LRU_PAD:LRU_PAD + tt, :] = x
    cw = cw_ref[...]
    xc = cb_ref[...] + cw[LRU_CONV - 1:LRU_CONV, :] * x
    for j in range(LRU_CONV - 1):
        off = LRU_PAD - (LRU_CONV - 1) + j
        xc = xc + cw[j:j + 1, :] * xpad_ref[off:off + tt, :]
    xpad_ref[0:LRU_PAD, :] = x[tt - LRU_PAD:tt, :]

    rs, is_ = [], []
    for gi in range(LRU_BLOCKS):
        xg = xc[:, gi * bw:(gi + 1) * bw].astype(BF16)
        rs.append(_dot(xg, wa_ref[gi]))
        is_.append(_dot(xg, wx_ref[gi]))
    r = jax.nn.sigmoid(jnp.concatenate(rs, -1) + ba_ref[...])
    ig = jax.nn.sigmoid(jnp.concatenate(is_, -1) + bx_ref[...])
    log_a = c_ref[...] * r
    a = jnp.exp(log_a)
    u = jnp.sqrt(-jnp.tanh(log_a) * (a * a + 1.0)) * (ig * xc)

    def scan_rows(a, u, n, group):
        idx = lax.broadcasted_iota(jnp.int32, (n, W), 0) % group
        dlt = 1
        while dlt < group:
            keep = idx >= dlt
            a_sh = jnp.where(keep, pltpu.roll(a, dlt, 0), 1.0)
            u_sh = jnp.where(keep, pltpu.roll(u, dlt, 0), 0.0)
            u = a * u_sh + u
            a = a * a_sh
            dlt *= 2
        return a, u

    a, u = scan_rows(a, u, tt, SUBLANES)
    ng = tt // SUBLANES
    last = pl.ds(SUBLANES - 1, ng, stride=SUBLANES)
    ends = []
    for ref, val in ((a_ref, a), (u_ref, u)):
        for cb in range(W // LANES):
            ref[cb] = val[:, cb * LANES:(cb + 1) * LANES]
        ends.append(jnp.concatenate([ref[cb, last, :] for cb in range(W // LANES)], 1))
    ae, ue = scan_rows(ends[0], ends[1], ng, ng)
    h0 = h_ref[...]
    h_end = ue + ae * h0
    gidx = lax.broadcasted_iota(jnp.int32, (ng, W), 0)
    h_in = jnp.where(gidx >= 1, pltpu.roll(h_end, 1, 0), h0)
    h = u + a * _dot_exact_lhs(ex_ref[...], h_in)
    o_ref[0] = (h * jax.nn.gelu(y_ref[0])).astype(o_ref.dtype)
    h_ref[...] = h_end[ng - 1:ng, :]


def _lru_call(proj3, conv_w, conv_b, wa, ba, wx, bx, lam, tt=512):
    B, S, _ = proj3.shape
    W = LRU_WIDTH
    bw = W // LRU_BLOCKS
    tt = min(tt, S)
    c = (-LRU_C * jax.nn.softplus(-lam.astype(F32))).reshape(1, W)
    expand = (np.arange(tt)[:, None] // SUBLANES == np.arange(tt // SUBLANES)[None, :]).astype(np.float32)
    vec = lambda: pl.BlockSpec((1, W), lambda b, t: (0, 0))
    return pl.pallas_call(
        functools.partial(_lru_kernel, tt=tt),
        grid=(B, S // tt),
        in_specs=[pl.BlockSpec((1, tt, W), lambda b, t: (b, t, OFF_LX // W)),
                  pl.BlockSpec((1, tt, W), lambda b, t: (b, t, OFF_LY // W)),
                  pl.BlockSpec((LRU_CONV, W), lambda b, t: (0, 0)), vec(),
                  pl.BlockSpec((LRU_BLOCKS, bw, bw), lambda b, t: (0, 0, 0)), vec(),
                  pl.BlockSpec((LRU_BLOCKS, bw, bw), lambda b, t: (0, 0, 0)), vec(), vec(),
                  pl.BlockSpec(expand.shape, lambda b, t: (0, 0))],
        out_specs=pl.BlockSpec((1, tt, W), lambda b, t: (b, t, 0)),
        out_shape=jax.ShapeDtypeStruct((B, S, W), BF16),
        scratch_shapes=[pltpu.VMEM((tt + LRU_PAD, W), F32), pltpu.VMEM((1, W), F32),
                        pltpu.VMEM((W // LANES, tt, LANES), F32), pltpu.VMEM((W // LANES, tt, LANES), F32)],
        compiler_params=_params("parallel", "arbitrary"),
        name="rglru",
    )(proj3, proj3, conv_w.astype(F32), conv_b.reshape(1, W).astype(F32), wa.astype(BF16),
      ba.reshape(1, W), wx.astype(BF16), bx.reshape(1, W), c, jnp.asarray(expand, BF16))


NSA_AUG = 4
NSA_KW = LANES
SEL_TILE = 256
SEL_TILE_BLOCKS = SEL_TILE // SEL_LEN
WIN_TILE = 128
POS_SPLIT = 128
SEL_ONEHOT = LANES
PLAN_HALF = 16
SELECT_PARTS = 4
SELECT_QB = 8
ATTEND_QB = 4
NSA_VROWS = NSA_HEAD_DIM + 16
KV_PREP_ROWS = 512
KV_READ = 512


def _key_aug(pos, width):
    n = pos.shape[0]
    col = lax.broadcasted_iota(jnp.int32, (n, width), 1)
    hi = ((pos // POS_SPLIT) * POS_SPLIT).astype(F32)
    lo = (pos % POS_SPLIT).astype(F32)
    return jnp.where(col == 0, hi, jnp.where(col == 1, lo, jnp.where(col < NSA_AUG, 1.0, 0.0)))


def _query_t(q_ref, slope_ref, c0, row0=0):
    hd, P, T = NSA_HEAD_DIM, NSA_PER_GROUP, Q_BLOCK
    PT = P * T
    qT = (q_ref[0, row0:row0 + T, :] * (hd ** -0.5)).T
    qpart = jnp.concatenate([qT[p * hd:(p + 1) * hd, :] for p in range(P)], 1)
    tlane = c0 + lax.broadcasted_iota(jnp.int32, (1, PT), 1) % T
    slope = slope_ref[0]
    t_hi = ((tlane // POS_SPLIT) * POS_SPLIT).astype(F32)
    t_lo = (tlane % POS_SPLIT).astype(F32)
    rowi = lax.broadcasted_iota(jnp.int32, (NSA_KW - hd, PT), 0)
    aug = jnp.where(rowi < 2, slope,
                    jnp.where(rowi == 2, -slope * t_hi,
                              jnp.where(rowi == 3, -slope * t_lo, jnp.where(rowi == NSA_AUG, NEG_BIG, 0.0))))
    return jnp.concatenate([qpart, aug], 0).astype(BF16), tlane


def _compress_kernel(xk_ref, xv_ref, w1k_ref, pek_ref, w1fk_ref, w2k_ref, w1v_ref, pev_ref, w1fv_ref, w2v_ref,
                     kc_ref, vct_ref, *, nc):
    hd = NSA_HEAD_DIM
    pos = lax.broadcasted_iota(jnp.int32, (nc, 1), 0) * CMP_STRIDE + (CMP_LEN - 1)
    aug = _key_aug(pos, NSA_KW - hd)

    def one(x_ref, w1_ref, pe_ref, w1f_ref, w2_ref, g):
        uv = jnp.zeros((nc, 2 * CMP_HIDDEN), F32)
        for r in range(CMP_STRIDE):
            xr = x_ref[0, pl.ds(r, nc, stride=CMP_STRIDE), :][:, g * hd:(g + 1) * hd]
            uv = uv + _dot(xr.astype(BF16), w1_ref[r])
        cvec = _dot_hilo(pe_ref[...], w1f_ref[...])[0:1, :]
        hid = uv[:, 0:CMP_HIDDEN] + pltpu.roll(uv[:, CMP_HIDDEN:], nc - 1, 0) + cvec
        return _dot_hilo(jax.nn.gelu(hid), w2_ref[...])

    for g in range(NSA_GROUPS):
        kc = one(xk_ref, w1k_ref, pek_ref, w1fk_ref, w2k_ref, g)
        kc_ref[0, g] = jnp.concatenate([kc, aug], 1).astype(BF16)
        vct_ref[0, g] = one(xv_ref, w1v_ref, pev_ref, w1fv_ref, w2v_ref, g).T.astype(BF16)


def _compress_call(proj3, pe_k, w1_k, w2_k, pe_v, w1_v, w2_v):
    B, S, _ = proj3.shape
    G, hd = NSA_GROUPS, NSA_HEAD_DIM
    nc = S // CMP_STRIDE
    half = CMP_STRIDE * hd

    def prep(pe, w1):
        w1 = w1.astype(F32)
        wr = jnp.concatenate([w1[:half].reshape(CMP_STRIDE, hd, CMP_HIDDEN),
                              w1[half:].reshape(CMP_STRIDE, hd, CMP_HIDDEN)], -1)
        pe8 = jnp.broadcast_to(pe.reshape(1, CMP_LEN * hd).astype(F32), (SUBLANES, CMP_LEN * hd))
        return wr.astype(BF16), pe8, w1

    full = lambda a: pl.BlockSpec(a.shape, lambda b: tuple(0 for _ in a.shape))
    col = lambda off: pl.BlockSpec((1, S, LANES), lambda b: (b, 0, off // LANES))
    args = prep(pe_k, w1_k) + (w2_k.astype(F32),) + prep(pe_v, w1_v) + (w2_v.astype(F32),)
    return pl.pallas_call(
        functools.partial(_compress_kernel, nc=nc),
        grid=(B,),
        in_specs=[col(OFF_KC), col(OFF_VC)] + [full(a) for a in args],
        out_specs=[pl.BlockSpec((1, G, nc, NSA_KW), lambda b: (b, 0, 0, 0)),
                   pl.BlockSpec((1, G, hd, nc), lambda b: (b, 0, 0, 0))],
        out_shape=[jax.ShapeDtypeStruct((B, G, nc, NSA_KW), BF16),
                   jax.ShapeDtypeStruct((B, G, hd, nc), BF16)],
        compiler_params=_params("parallel"),
        name="nsa_compress",
    )(proj3, proj3, *args)


def _kvprep_constants(S, tt):
    hd = NSA_HEAD_DIM
    pos = np.arange(S + tt)
    live = pos < S
    kcols = np.zeros((S + tt, NSA_KW - hd + SEL_ONEHOT), np.float32)
    kcols[:, 0] = np.where(live, (pos // POS_SPLIT) * POS_SPLIT, 0)
    kcols[:, 1] = np.where(live, pos % POS_SPLIT, 0)
    kcols[:, 2:NSA_AUG] = live[:, None]
    kcols[:, NSA_AUG] = ~live
    kcols[pos[live], NSA_KW - hd + pos[live] // SEL_LEN] = 1.0
    vrows = np.zeros((2, NSA_VROWS - hd, tt), np.float32)
    vrows[0, 0] = 1.0
    return kcols, vrows


def _kvprep_kernel(sel_ref, win_ref, kc_ref, vr_ref, ksa_ref, vst_ref, kwa_ref, vwt_ref, *, tt, n_live):
    hd = NSA_HEAD_DIM
    dead = pl.program_id(1) >= n_live
    kcols = kc_ref[...]
    ones_rows = vr_ref[0]
    for g in range(NSA_GROUPS):
        live = lambda ref, off: jnp.where(dead, 0.0, ref[0][:, off + g * hd:off + (g + 1) * hd])
        ksa_ref[0, g] = jnp.concatenate([live(sel_ref, OFF_KS - OFF_KC).astype(BF16), kcols], 1)
        kwa_ref[0, g] = jnp.concatenate([live(win_ref, 0).astype(BF16), kcols[:, 0:NSA_KW - hd]], 1)
        vs_t = jnp.concatenate([live(sel_ref, OFF_VS - OFF_KC).T.astype(BF16), ones_rows], 0)
        vw_t = jnp.concatenate([live(win_ref, OFF_VW - OFF_KW).T.astype(BF16), ones_rows], 0)
        for c in range(tt // SEL_TILE):
            vst_ref[0, g, c] = vs_t[:, c * SEL_TILE:(c + 1) * SEL_TILE]
        for c in range(tt // WIN_TILE):
            vwt_ref[0, g, c] = vw_t[:, c * WIN_TILE:(c + 1) * WIN_TILE]


def _kvprep_call(proj3):
    B, S, _ = proj3.shape
    G, hd = NSA_GROUPS, NSA_HEAD_DIM
    tt = min(KV_PREP_ROWS, S)
    n_live = S // tt
    assert tt == WINDOW
    sp = S + tt
    assert OFF_KC % KV_READ == 0 and OFF_KW == OFF_KC + KV_READ and OFF_VW + NSA_KV <= OFF_KW + KV_READ
    col = lambda off: pl.BlockSpec((1, tt, KV_READ), lambda b, t: (b, jnp.minimum(t, n_live - 1), off // KV_READ))
    front = lambda t: (t + 1) % (n_live + 1)
    kcols, vrows = _kvprep_constants(S, tt)
    return pl.pallas_call(
        functools.partial(_kvprep_kernel, tt=tt, n_live=n_live),
        grid=(B, n_live + 1),
        in_specs=[col(OFF_KC), col(OFF_KW),
                  pl.BlockSpec((tt, kcols.shape[1]), lambda b, t: (t, 0)),
                  pl.BlockSpec((1,) + vrows.shape[1:], lambda b, t: (t // n_live, 0, 0))],
        out_specs=[pl.BlockSpec((1, G, tt, NSA_KW + SEL_ONEHOT), lambda b, t: (b, 0, t, 0)),
                   pl.BlockSpec((1, G, tt // SEL_TILE, NSA_VROWS, SEL_TILE), lambda b, t: (b, 0, t, 0, 0)),
                   pl.BlockSpec((1, G, tt, NSA_KW), lambda b, t: (b, 0, front(t), 0)),
                   pl.BlockSpec((1, G, tt // WIN_TILE, NSA_VROWS, WIN_TILE), lambda b, t: (b, 0, front(t), 0, 0))],
        out_shape=[jax.ShapeDtypeStruct((B, G, sp, NSA_KW + SEL_ONEHOT), BF16),
                   jax.ShapeDtypeStruct((B, G, sp // SEL_TILE, NSA_VROWS, SEL_TILE), BF16),
                   jax.ShapeDtypeStruct((B, G, sp, NSA_KW), BF16),
                   jax.ShapeDtypeStruct((B, G, sp // WIN_TILE, NSA_VROWS, WIN_TILE), BF16)],
        compiler_params=_params("parallel", "parallel"),
        name="nsa_kvprep",
    )(proj3, proj3, jnp.asarray(kcols, BF16), jnp.asarray(vrows, BF16))


def _importance_matrix(ns, nc):
    ratio = SEL_LEN // CMP_STRIDE
    a = np.zeros((ns, nc), np.float32)
    for j in range(ns):
        for n, wgt in ((ratio * j - 1, 0.5), (ratio * j, 1.0), (ratio * j + 1, 1.0),
                       (ratio * j + 2, 1.0), (ratio * j + 3, 0.5)):
            if 0 <= n < nc - 1:
                a[j, n] = wgt
    return a


def _plan_constants(ns):
    nt = ns // SEL_TILE_BLOCKS
    grp = (np.arange(ns)[None, :] // SEL_TILE_BLOCKS == np.arange(nt)[:, None]).astype(np.float32)
    k = np.arange(nt)
    w_lo = np.where(k < PLAN_HALF, 2.0 ** np.minimum(k, PLAN_HALF - 1), 0.0)
    w_hi = np.where(k >= PLAN_HALF, 2.0 ** np.maximum(k - PLAN_HALF, 0), 0.0)
    wts = np.stack([np.repeat(w_lo[:, None], LANES, 1), np.repeat(w_hi[:, None], LANES, 1)]).astype(np.float32)
    return grp, wts


def _select_kernel(q_ref, slope_ref, kc_ref, vct_ref, imp_ref, grp_ref, wts_ref, ocmp_ref, bias_ref, bits_ref,
                   *, ns, nc, qb):
    P, T = NSA_PER_GROUP, Q_BLOCK
    step = pl.program_id(2)
    nsteps = pl.num_programs(2)
    subs = range(qb)
    c0s = [(step * qb + u) * T for u in subs]
    qts, tlanes = zip(*[_query_t(q_ref, slope_ref, c0s[u], u * T) for u in subs])

    def body(rows, nblk):
        cend = lax.broadcasted_iota(jnp.int32, (rows, 1), 0) * CMP_STRIDE + (CMP_LEN - 1)
        cmask = [cend <= tlanes[u] for u in subs]
        s = [jnp.where(cmask[u], _dot(kc_ref[0, 0, 0:rows, :], qts[u]), NEG_BIG) for u in subs]
        m = [jnp.max(s[u], 0, keepdims=True) for u in subs]
        p_c = [jnp.where(cmask[u], jnp.exp(s[u] - m[u]), 0.0) for u in subs]
        p_c = [p_c[u] * (1.0 / jnp.maximum(jnp.sum(p_c[u], 0, keepdims=True), 1e-30)) for u in subs]
        for u in subs:
            ocmp_ref[0, 0, u] = _dot(vct_ref[0, 0][:, 0:rows], p_c[u].astype(BF16))

        psum = [p_c[u][:, 0:T] for u in subs]
        for p in range(1, P):
            psum = [psum[u] + p_c[u][:, p * T:(p + 1) * T] for u in subs]
        imp = [_dot_exact_lhs(imp_ref[0:nblk, 0:rows], psum[u]) for u in subs]
        blk = lax.broadcasted_iota(jnp.int32, (nblk, T), 0)
        val, chosen = [], []
        for u in subs:
            cur = (c0s[u] + lax.broadcasted_iota(jnp.int32, (nblk, T), 1)) // SEL_LEN
            forced = (blk == 0) | (blk == cur) | (blk == cur - 1)
            val.append(jnp.where(forced, 3e38, jnp.where(blk > cur, -1.0, imp[u])))
            chosen.append(jnp.zeros((nblk, T), F32))
        for _ in range(min(SEL_TOPK, nblk)):
            for u in subs:
                mx = jnp.max(val[u], 0, keepdims=True)
                first = jnp.min(jnp.where(val[u] == mx, blk, ns), 0, keepdims=True)
                pick = blk == first
                chosen[u] = jnp.where(pick, 1.0, chosen[u])
                val[u] = jnp.where(pick, -2.0, val[u])
        for u in subs:
            bias_ref[0, 0, u, 0:nblk, :] = jnp.where(chosen[u] > 0.5, 0.0, NEG_BIG)
            if nblk < ns:
                bias_ref[0, 0, u, nblk:ns, :] = jnp.full((ns - nblk, T), NEG_BIG, F32)
            ch = chosen[u]
            if nblk < ns:
                ch = jnp.concatenate([ch, jnp.zeros((ns - nblk, T), F32)], 0)
            cnt = _dot(grp_ref[...], ch.astype(BF16))
            act = jnp.where(jnp.max(cnt, 1, keepdims=True) > 0.5, 1.0, 0.0)
            lo = jnp.sum(act * wts_ref[0], 0, keepdims=True).astype(jnp.int32)
            hi = jnp.sum(act * wts_ref[1], 0, keepdims=True).astype(jnp.int32)
            bits_ref[u] = jnp.broadcast_to(lo | (hi << PLAN_HALF), (SUBLANES, LANES))

    parts = max(p for p in (SELECT_PARTS, 2, 1) if nc % (p * LANES) == 0 and ns % (p * 2 * SUBLANES) == 0)
    part = step * parts // nsteps
    for v in range(parts):
        pl.when(part == v)(functools.partial(body, nc * (v + 1) // parts, ns * (v + 1) // parts))


def _attend_kernel(bits_sref, q_ref, gt_ref, slope_ref, ocmp_ref, bias_ref, ks_ref, vst_ref, kw_ref, vwt_ref,
                   o_ref, list_ref, s_ref, m_ref, *, n_tiles_total, ns, qb):
    hd, P, T = NSA_HEAD_DIM, NSA_PER_GROUP, Q_BLOCK
    PT = P * T
    b, g, step = pl.program_id(0), pl.program_id(1), pl.program_id(2)
    subs = range(qb)
    iqs = [step * qb + u for u in subs]
    c0s = [iq * T for iq in iqs]
    qts, tlanes = zip(*[_query_t(q_ref, slope_ref, c0s[u], u * T) for u in subs])
    tl_row = lax.broadcasted_iota(jnp.int32, (1, PT), 1) % T

    gts = [jax.nn.sigmoid(gt_ref[0, u * T:(u + 1) * T, :]).T for u in subs]

    def gate_row(u, branch):
        return jnp.concatenate([gts[u][p * 3 + branch:p * 3 + branch + 1, :] for p in range(P)], 1)

    list_len = n_tiles_total + 1
    last_tiles = [(c0s[u] + T - 1) // SEL_TILE for u in subs]
    prev_bits = []
    for u in subs:
        bits = bits_sref[(b * pl.num_programs(1) + g) * (pl.num_programs(2) * qb) + iqs[u]]
        prev_bits.append(bits & (lax.shift_left(jnp.int32(1), last_tiles[u]) - 1))
    n_prevs = [jnp.int32(0) for _ in subs]
    for k in range(n_tiles_total):
        for u in subs:
            list_ref[u * list_len + n_prevs[u]] = k
            n_prevs[u] = n_prevs[u] + (lax.shift_right_logical(prev_bits[u], k) & 1)

    qt_sels = []
    for u in subs:
        sel_b = bias_ref[0, 0, u]
        if ns < SEL_ONEHOT:
            sel_b = jnp.concatenate([sel_b, jnp.zeros((SEL_ONEHOT - ns, T), F32)], 0)
        qt_sels.append(jnp.concatenate([qts[u], jnp.concatenate([sel_b.astype(BF16)] * P, 1)], 0))

    def tile_scores(u, kt):
        k0 = pl.multiple_of(kt * SEL_TILE, SEL_TILE)
        return _dot(ks_ref[0, 0, pl.ds(k0, SEL_TILE), :], qt_sels[u])

    def col_max(scores):
        m = jnp.max(scores[0], 0, keepdims=True)
        for sc in scores[1:]:
            m = jnp.maximum(m, jnp.max(sc, 0, keepdims=True))
        return m

    def softmax_update(carry, scores, vts, m_tile):
        m_run, acc = carry
        m_new = jnp.maximum(m_run, m_tile)
        acc = jnp.exp(m_run - m_new) * acc
        for sc, vt in zip(scores, vts):
            acc = acc + _dot(vt, jnp.exp(sc - m_new).astype(BF16))
        return m_new, acc

    init = (jnp.full((1, PT), NEG_BIG, F32), jnp.zeros((NSA_VROWS, PT), F32))
    dead_tile = n_tiles_total
    n_pairs = [jnp.maximum((n_prevs[u] + 1) // 2, 1) for u in subs]

    def pair_tiles(u, i):
        has_a, has_b = 2 * i < n_prevs[u], 2 * i + 1 < n_prevs[u]
        ka = jnp.where(has_a, list_ref[u * list_len + jnp.where(has_a, 2 * i, 0)], dead_tile)
        kb = jnp.where(has_b, list_ref[u * list_len + jnp.where(has_b, 2 * i + 1, 0)], dead_tile)
        return ka, kb

    def pair_scores(u, i):
        ka, kb = pair_tiles(u, i)
        keys = jnp.concatenate([ks_ref[0, 0, pl.ds(pl.multiple_of(k * SEL_TILE, SEL_TILE), SEL_TILE), :]
                                for k in (ka, kb)], 0)
        return _dot(keys, qt_sels[u])

    def put_scores(u, i, sc):
        slot = i % 2
        s_ref[u, slot] = sc
        m_ref[u, slot] = col_max([sc])

    def pair_probs(u, i, m_run):
        slot = i % 2
        m_new = jnp.maximum(m_run, m_ref[u, slot])
        return m_new, jnp.exp(s_ref[u, slot] - m_new).astype(BF16)

    def add_values(u, i, carry, m_new, pr):
        ka, kb = pair_tiles(u, i)
        m_run, acc = carry
        vt = jnp.concatenate([vst_ref[0, 0, ka], vst_ref[0, 0, kb]], 1)
        return m_new, jnp.exp(m_run - m_new) * acc + _dot(vt, pr)

    first = [pair_scores(u, 0) for u in subs]
    for u in subs:
        put_scores(u, 0, first[u])

    n_win = (WINDOW + T) // WIN_TILE
    roff = lax.broadcasted_iota(jnp.int32, (WIN_TILE, 1), 0)
    w_sc = [_dot(kw_ref[0, 0, pl.ds(pl.multiple_of(c0s[u], WIN_TILE), WINDOW + T), :], qts[u]) for u in subs]
    w_parts = [[jnp.where(roff > tl_row, w_sc[u][0:WIN_TILE], NEG_BIG),
                w_sc[u][WIN_TILE:WINDOW],
                jnp.where(roff <= tl_row, w_sc[u][WINDOW:WINDOW + T], NEG_BIG)]
               for u in subs]
    w_max = [col_max(w_parts[u]) for u in subs]
    out_cw = []
    for u in subs:
        vt = jnp.concatenate([vwt_ref[0, 0, iqs[u] + j] for j in range(n_win)], 1)
        _, acc_win = softmax_update(init, [jnp.concatenate(w_parts[u], 0)], [vt], w_max[u])
        out_cw.append(gate_row(u, 0) * ocmp_ref[0, 0, u] + gate_row(u, 2) * (acc_win[0:hd] / acc_win[hd:hd + 1]))

    def pipe_step(u, i, carry):
        m_new, pr = pair_probs(u, i, carry[0])
        sc_next = pair_scores(u, i + 1)
        carry = add_values(u, i, carry, m_new, pr)
        put_scores(u, i + 1, sc_next)
        return carry

    carries = [lax.fori_loop(0, n_pairs[u] - 1, functools.partial(pipe_step, u), init) for u in subs]
    probs = [pair_probs(u, n_pairs[u] - 1, carries[u][0]) for u in subs]
    s_last = []
    for u in subs:
        kpos = last_tiles[u] * SEL_TILE + lax.broadcasted_iota(jnp.int32, (SEL_TILE, 1), 0)
        s_last.append(jnp.where(kpos <= tlanes[u], tile_scores(u, last_tiles[u]), NEG_BIG))
    carries = [add_values(u, n_pairs[u] - 1, carries[u], *probs[u]) for u in subs]
    for u in subs:
        _, acc_sel = softmax_update(carries[u], [s_last[u]], [vst_ref[0, 0, last_tiles[u]]], col_max([s_last[u]]))
        out = out_cw[u] + gate_row(u, 1) * (acc_sel[0:hd] / acc_sel[hd:hd + 1])
        o_ref[0, u * T:(u + 1) * T, :] = jnp.concatenate(
            [out[:, p * T:(p + 1) * T] for p in range(P)], 0).T.astype(o_ref.dtype)


def _nsa_call(proj3, pe_k, w1_k, w2_k, pe_v, w1_v, w2_v):
    B, S, _ = proj3.shape
    G, hd, P, T = NSA_GROUPS, NSA_HEAD_DIM, NSA_PER_GROUP, Q_BLOCK
    PT = P * T
    nc = S // CMP_STRIDE
    ns = S // SEL_LEN
    nq = S // T
    nt = ns // SEL_TILE_BLOCKS
    assert nt <= 2 * PLAN_HALF and S % KV_PREP_ROWS == 0

    kc_a, vc_t = _compress_call(proj3, pe_k, w1_k, w2_k, pe_v, w1_v, w2_v)
    ks_a, vs_t, kw_a, vw_t = _kvprep_call(proj3)
    imp_m = jnp.asarray(_importance_matrix(ns, nc), BF16)
    grp, wts = _plan_constants(ns)
    head_slopes = 2.0 ** (-(8.0 / NSA_HEADS) * np.arange(1, NSA_HEADS + 1, dtype=np.float32))
    slopes = jnp.asarray(np.repeat(head_slopes.reshape(G, 1, P), T, axis=-1), F32)

    q_spec = lambda im: pl.BlockSpec((1, T, P * hd), im)
    bg = lambda shp: pl.BlockSpec((1, 1) + shp, lambda b, g, i: (b, g) + tuple(0 for _ in shp))
    const = lambda a: pl.BlockSpec(a.shape, lambda b, g, i: tuple(0 for _ in a.shape))
    qb = SELECT_QB if nq % (SELECT_QB * SELECT_PARTS) == 0 else 1
    nsteps = nq // qb
    ocmp, sel_bias, bits = pl.pallas_call(
        functools.partial(_select_kernel, ns=ns, nc=nc, qb=qb),
        grid=(B, G, nsteps),
        in_specs=[pl.BlockSpec((1, qb * T, P * hd), lambda b, g, i: (b, i, OFF_NQ // (P * hd) + g)),
                  pl.BlockSpec((1, 1, PT), lambda b, g, i: (g, 0, 0)),
                  bg((nc, NSA_KW)), bg((hd, nc)), const(imp_m),
                  pl.BlockSpec(grp.shape, lambda b, g, i: (0, 0)),
                  pl.BlockSpec(wts.shape, lambda b, g, i: (0, 0, 0))],
        out_specs=[pl.BlockSpec((1, 1, qb, hd, PT), lambda b, g, i: (b, g, i, 0, 0)),
                   pl.BlockSpec((1, 1, qb, ns, T), lambda b, g, i: (b, g, i, 0, 0)),
                   pl.BlockSpec((qb, SUBLANES, LANES), lambda b, g, i: ((b * G + g) * nsteps + i, 0, 0))],
        out_shape=[jax.ShapeDtypeStruct((B, G, nq, hd, PT), F32),
                   jax.ShapeDtypeStruct((B, G, nq, ns, T), F32),
                   jax.ShapeDtypeStruct((B * G * nq, SUBLANES, LANES), jnp.int32)],
        compiler_params=_params("parallel", "parallel", "parallel"),
        name="nsa_select",
    )(proj3, slopes, kc_a, vc_t, imp_m, jnp.asarray(grp, BF16), jnp.asarray(wts, F32))

    aq = ATTEND_QB if nq % ATTEND_QB == 0 else 1
    bg2 = lambda shp: pl.BlockSpec((1, 1) + shp, lambda b, g, i, s: (b, g) + tuple(0 for _ in shp))
    return pl.pallas_call(
        functools.partial(_attend_kernel, n_tiles_total=nt, ns=ns, qb=aq),
        grid_spec=pltpu.PrefetchScalarGridSpec(
            num_scalar_prefetch=1,
            grid=(B, G, nq // aq),
            in_specs=[pl.BlockSpec((1, aq * T, P * hd), lambda b, g, i, s: (b, i, OFF_NQ // (P * hd) + g)),
                      pl.BlockSpec((1, aq * T, LANES), lambda b, g, i, s: (b, i, OFF_NG // LANES + g)),
                      pl.BlockSpec((1, 1, PT), lambda b, g, i, s: (g, 0, 0)),
                      pl.BlockSpec((1, 1, aq, hd, PT), lambda b, g, i, s: (b, g, i, 0, 0)),
                      pl.BlockSpec((1, 1, aq, ns, T), lambda b, g, i, s: (b, g, i, 0, 0)),
                      bg2(ks_a.shape[2:]), bg2(vs_t.shape[2:]), bg2(kw_a.shape[2:]), bg2(vw_t.shape[2:])],
            out_specs=pl.BlockSpec((1, aq * T, P * hd), lambda b, g, i, s: (b, i, g)),
            scratch_shapes=[pltpu.SMEM((aq * (nt + 1),), jnp.int32),
                            pltpu.VMEM((aq, 2, 2 * SEL_TILE, PT), F32), pltpu.VMEM((aq, 2, 1, PT), F32)]),
        out_shape=jax.ShapeDtypeStruct((B, S, NSA_WIDTH), BF16),
        compiler_params=_params("parallel", "parallel", "arbitrary"),
        name="nsa_attend",
    )(bits[:, 0, 0], proj3, proj3, slopes, ocmp, sel_bias, ks_a, vs_t, kw_a, vw_t)


FFN_PAD = SUBLANES
FFN_SUB = 256


def _merge_ffn_kernel(x_ref, ya_ref, yb_ref, yc_ref, g0_ref, g1_ref, g2_ref, wa_ref, wb_ref, wc_ref, wo_ref,
                      l1g_ref, l1b_ref, wu_ref, wv_ref, cw_ref, cb_ref, wd_ref, l2g_ref, l2b_ref, o_ref,
                      h1_ref, acc_ref, upad_ref, carry_ref, *, tm, tiles_per_seq):
    gate = lambda r: jax.nn.sigmoid(r[...].astype(F32))
    m = gate(g0_ref) * _dot(ya_ref[...], wa_ref[...])
    m = m + gate(g1_ref) * _dot(yb_ref[...], wb_ref[...])
    m = m + gate(g2_ref) * _dot(yc_ref[...], wc_ref[...])
    z = DEEPNORM_ALPHA * x_ref[...] + _dot(m.astype(BF16), wo_ref[...])
    h1 = _layer_norm_rows(z, l1g_ref[...], l1b_ref[...])
    h1_ref[...] = h1
    xb = h1.astype(BF16)

    fc = wu_ref.shape[1]

    @pl.when((pl.program_id(0) % tiles_per_seq) == 0)
    def _():
        carry_ref[...] = jnp.zeros_like(carry_ref)

    acc_ref[...] = jnp.zeros_like(acc_ref)
    for c0 in range(0, fc, FFN_SUB):
        cs = slice(c0, min(c0 + FFN_SUB, fc))
        w = cs.stop - cs.start
        u = _dot(xb, wu_ref[:, cs])
        v = _dot(xb, wv_ref[:, cs])
        upad_ref[0:FFN_PAD, 0:w] = carry_ref[:, cs]
        upad_ref[FFN_PAD:FFN_PAD + tm, 0:w] = u
        cw = cw_ref[:, cs]
        cv = cb_ref[:, cs] + cw[FFN_CONV - 1:FFN_CONV, :] * u
        for k in range(FFN_CONV - 1):
            off = FFN_PAD - (FFN_CONV - 1) + k
            cv = cv + cw[k:k + 1, :] * upad_ref[off:off + tm, 0:w]
        carry_ref[:, cs] = u[tm - FFN_PAD:tm, :]
        h = (jax.nn.gelu(cv) * v).astype(BF16)
        acc_ref[...] += _dot(h, wd_ref[cs, :])

    z2 = DEEPNORM_ALPHA * h1_ref[...] + acc_ref[...]
    o_ref[...] = _layer_norm_rows(z2, l2g_ref[...], l2b_ref[...])


def _merge_ffn_call(x2, seq_len, ya, yb, yc, gates, w_hg, w_nsa, w_lru, w_out, ln1_g, ln1_b,
                    w_up, conv_w, conv_b, w_down, ln2_g, ln2_b, layer, tm=512):
    T, D = x2.shape
    tm = min(tm, seq_len)
    fc = FFN_DIM
    once = pl.Buffered(1)
    rows = lambda wid: pl.BlockSpec((tm, wid), lambda i: (i, 0))
    gate = lambda n: pl.BlockSpec((tm, D), lambda i: (i, n))
    lw = lambda a: pl.BlockSpec((None,) + a.shape[1:], lambda i: (layer, 0, 0), pipeline_mode=once)
    vec = lambda n: pl.BlockSpec((1, n), lambda i: (0, 0))
    ws = [w.astype(BF16) for w in (w_hg, w_nsa, w_lru, w_out)]
    wu = w_up.astype(BF16)
    wd = w_down.astype(BF16)
    return pl.pallas_call(
        functools.partial(_merge_ffn_kernel, tm=tm, tiles_per_seq=seq_len // tm),
        grid=(T // tm,),
        in_specs=[rows(D), rows(HG_WIDTH), rows(NSA_WIDTH), rows(LRU_WIDTH), gate(0), gate(1), gate(2)]
                 + [lw(w) for w in ws] + [vec(D), vec(D),
                    pl.BlockSpec((None, D, fc), lambda i: (layer, 0, 0), pipeline_mode=once),
                    pl.BlockSpec((None, D, fc), lambda i: (layer, 0, 1), pipeline_mode=once),
                    pl.BlockSpec((FFN_CONV, fc), lambda i: (0, 0)), vec(fc), lw(wd), vec(D), vec(D)],
        out_specs=rows(D),
        out_shape=jax.ShapeDtypeStruct((T, D), F32),
        scratch_shapes=[pltpu.VMEM((tm, D), F32), pltpu.VMEM((tm, D), F32),
                        pltpu.VMEM((tm + FFN_PAD, FFN_SUB), F32), pltpu.VMEM((FFN_PAD, fc), F32)],
        compiler_params=_params("arbitrary"),
        name="merge_ffn",
    )(x2, ya, yb, yc, gates, gates, gates, *ws, ln1_g.reshape(1, D), ln1_b.reshape(1, D),
      wu, wu, conv_w.astype(F32), conv_b.reshape(1, FFN_DIM).astype(F32), wd,
      ln2_g.reshape(1, D), ln2_b.reshape(1, D))


def _permute_in_proj(w, b):
    src = np.cumsum([0, HG_WIDTH, HG_WIDTH, HG_WIDTH, HG_WIDTH, NSA_WIDTH, NSA_KV, NSA_KV, NSA_KV, NSA_KV,
                     NSA_KV, NSA_KV, NSA_HEADS * 3, LRU_WIDTH, LRU_WIDTH, N_BRANCH * 1024])
    names = ["hq", "hf", "hi", "hg", "nq", "kc", "vc", "ks", "vs", "kw", "vw", "ng", "lx", "ly", "mg"]
    seg = {n: (int(src[k]), int(src[k + 1])) for k, n in enumerate(names)}
    order = ["mg", "hq", "hf", "hi", "hg", "nq", "lx", "ly", "kc", "vc", "ks", "vs", "kw", "vw"]
    per_group = NSA_PER_GROUP * 3
    ng0 = seg["ng"][0]
    w = w.astype(BF16)
    w_parts = [w[..., seg[n][0]:seg[n][1]] for n in order]
    b_parts = [b[..., seg[n][0]:seg[n][1]] for n in order]
    for gi in range(NSA_GROUPS):
        lo = ng0 + gi * per_group
        w_parts += [w[..., lo:lo + per_group], jnp.zeros(w.shape[:-1] + (LANES - per_group,), w.dtype)]
        b_parts += [b[..., lo:lo + per_group], jnp.zeros(b.shape[:-1] + (LANES - per_group,), b.dtype)]
    wp = jnp.concatenate(w_parts, -1)
    bp = jnp.concatenate(b_parts, -1)
    assert wp.shape[-1] == GATE_COLS + PROJ_COLS
    return wp, bp.astype(F32)[..., None, :]


def kernel(x, ln_emb_g, ln_emb_b, w_in, b_in, hg_lb_logits, hg_norm_g, cmp_pe_k, cmp_w1_k, cmp_w2_k, cmp_pe_v, cmp_w1_v, cmp_w2_v, lru_conv_w, lru_conv_b, lru_wa, lru_ba, lru_wx, lru_bx, lru_lambda, w_branch_hg, w_branch_nsa, w_branch_lru, w_out, ln1_g, ln1_b, ffn_w_up, ffn_conv_w, ffn_conv_b, ffn_w_down, ln2_g, ln2_b):
    B, S, D = x.shape
    T = B * S
    gam = jax.nn.softmax(hg_lb_logits.astype(F32), axis=0)
    lb_all = jnp.cumsum(gam, axis=0) - gam[0]
    h = x.reshape(T, D)
    wp, bp = _permute_in_proj(w_in, b_in)
    for l in range(DEPTH):
        if l == 0:
            gates, proj2, h = _inproj_call(h, wp, bp, l, ln=(ln_emb_g, ln_emb_b))
        else:
            gates, proj2 = _inproj_call(h, wp, bp, l)
        proj3 = proj2.reshape(B, S, PROJ_COLS)
        y_a = _hgrn_call(proj3, lb_all[l], hg_norm_g[l])
        y_b = _nsa_call(proj3, cmp_pe_k[l], cmp_w1_k[l], cmp_w2_k[l], cmp_pe_v[l], cmp_w1_v[l], cmp_w2_v[l])
        y_c = _lru_call(proj3, lru_conv_w[l], lru_conv_b[l], lru_wa[l], lru_ba[l], lru_wx[l], lru_bx[l],
                        lru_lambda[l])
        h = _merge_ffn_call(h, S, y_a.reshape(T, HG_WIDTH), y_b.reshape(T, NSA_WIDTH), y_c.reshape(T, LRU_WIDTH),
                            gates, w_branch_hg, w_branch_nsa, w_branch_lru, w_out, ln1_g[l], ln1_b[l],
                            ffn_w_up, ffn_conv_w[l], ffn_conv_b[l], ffn_w_down, ln2_g[l], ln2_b[l], l)
    return h.reshape(B, S, D)
```
